```python
import math
import jax, jax.numpy as jnp
from jax import lax
import numpy as np

D_MODEL = 1024
BATCH = 8
SEQ = 2048
DEPTH = 2
DEC_BATCH = 128
DEC_SEQ = 1
PAST_LEN = 8192
PAGE_SIZE = 128

F32 = jnp.float32
EPS = 1e-6
NEG = -1e30
N_EVEN = (DEPTH + 1) // 2
N_ODD = DEPTH // 2

SSD_HEADS = 16
SSD_HEADDIM = 64
SSD_INNER = SSD_HEADS * SSD_HEADDIM
SSD_GROUPS = 2
SSD_STATE = 128
SSD_CONV = 4
SSD_CHUNK = 128
SSD_CONV_CH = SSD_INNER + 2 * SSD_GROUPS * SSD_STATE

NSA_HEADS = 16
NSA_KV_HEADS = 2
NSA_GROUP = NSA_HEADS // NSA_KV_HEADS
NSA_HEAD_DIM = 64
NSA_KV_WIDTH = NSA_KV_HEADS * NSA_HEAD_DIM
NSA_CMP_BLOCK = 32
NSA_CMP_STRIDE = 16
NSA_SEL_BLOCK = 64
NSA_TOPK = 16
NSA_WINDOW = 512
NSA_QBLOCK = 128
NSA_SCALE = NSA_HEAD_DIM ** -0.5
NSA_FORCE = 1e4

REL_BUCKETS = 32
REL_MAX_DIST = 128

GLA_HEADS = 4
GLA_DK = 128
GLA_DV = 256
GLA_GATE_RANK = 16
GLA_TAU = 16.0
GLA_CHUNK = 16

MLA_HEADS = 8
MLA_Q_RANK = 384
MLA_KV_RANK = 256
MLA_NOPE = 64
MLA_ROPE = 32
MLA_V = 64
MLA_QBLOCK = 128
MLA_SCALE = (MLA_NOPE + MLA_ROPE) ** -0.5
ROPE_THETA = 10000.0

FFN_HIDDEN = -(-(8 * D_MODEL) // (3 * 256)) * 256

EV_SPLITS = (SSD_INNER, SSD_CONV_CH, SSD_HEADS, NSA_HEADS * NSA_HEAD_DIM, 6 * NSA_KV_WIDTH, 3 * NSA_HEADS)
EV_IN = sum(EV_SPLITS)
EV_MIX = SSD_INNER + NSA_HEADS * NSA_HEAD_DIM
OD_SPLITS = (GLA_HEADS * GLA_DK, GLA_HEADS * GLA_DK, GLA_HEADS * GLA_DV, GLA_GATE_RANK, GLA_HEADS * GLA_DV,
             MLA_Q_RANK, MLA_KV_RANK, MLA_ROPE)
OD_IN = sum(OD_SPLITS)
OD_MIX = GLA_HEADS * GLA_DV + MLA_HEADS * MLA_V

kernel_name = 'hybrid_ssd_nsa_gla_mla_step'


def split_cols(h, sizes):
    return jnp.split(h, [int(i) for i in np.cumsum(sizes)[:-1]], axis=-1)


def rmsnorm(x, g):
    xf = x.astype(F32)
    y = xf * lax.rsqrt(jnp.mean(xf * xf, axis=-1, keepdims=True) + EPS)
    return (y * g.astype(F32)).astype(x.dtype)


def rope(x, pos):
    half = x.shape[-1] // 2
    freqs = ROPE_THETA ** (-jnp.arange(half, dtype=F32) / half)
    ang = pos.astype(F32)[:, None] * freqs
    shape = (pos.shape[0],) + (1,) * (x.ndim - 3) + (half,)
    cos, sin = jnp.cos(ang).reshape(shape), jnp.sin(ang).reshape(shape)
    xf = x.astype(F32)
    x1, x2 = xf[..., :half], xf[..., half:]
    return jnp.concatenate([x1 * cos - x2 * sin, x2 * cos + x1 * sin], axis=-1).astype(x.dtype)


def t5_bucket(rel):
    exact = REL_BUCKETS // 2
    n = jnp.maximum(rel, 0)
    nf = jnp.maximum(n, exact).astype(F32)
    large = exact + (jnp.log(nf / exact) / math.log(REL_MAX_DIST / exact) * (REL_BUCKETS - exact)).astype(jnp.int32)
    return jnp.where(n < exact, n, jnp.minimum(large, REL_BUCKETS - 1))


def shared_bias(rel, table):
    lq, nk = rel.shape
    return table.astype(F32)[t5_bucket(rel)].reshape(lq, nk, NSA_KV_HEADS, NSA_GROUP).transpose(0, 2, 3, 1)


def masked_softmax(s, mask):
    return jax.nn.softmax(jnp.where(mask, s, NEG), axis=-1) * mask


def causal_conv(xbc, conv_state, w, b):
    full = jnp.concatenate([conv_state.astype(xbc.dtype), xbc], axis=1)
    ch = xbc.shape[-1]
    y = lax.conv_general_dilated(full, w[:, None, :].astype(xbc.dtype), (1,), 'VALID',
                                 dimension_numbers=('NWC', 'WIO', 'NWC'), feature_group_count=ch)
    return jax.nn.silu(y + b.astype(y.dtype)), full[:, -(SSD_CONV - 1):]


def segsum(x):
    t = x.shape[-1]
    cs = jnp.cumsum(x, axis=-1)
    diff = cs[..., :, None] - cs[..., None, :]
    return jnp.where(jnp.tril(jnp.ones((t, t), bool)), diff, -jnp.inf)


def ssd_scan(x, dt, a, bm, cm, s0, chunk):
    b, l, h, p = x.shape
    g, n = bm.shape[2], bm.shape[3]
    e = h // g
    nc = l // chunk
    xd = (x * dt[..., None]).reshape(b, nc, chunk, g, e, p)
    da = (dt * a).reshape(b, nc, chunk, g, e).transpose(0, 3, 4, 1, 2)
    bm = bm.reshape(b, nc, chunk, g, n)
    cm = cm.reshape(b, nc, chunk, g, n)
    a_cs = jnp.cumsum(da, axis=-1)
    lmat = jnp.exp(segsum(da))
    cb = jnp.einsum('bclgn,bcsgn->bcgls', cm, bm)
    y_diag = jnp.einsum('bcgls,bgecls,bcsgep->bclgep', cb, lmat, xd)
    decay_st = jnp.exp(a_cs[..., -1:] - a_cs)
    states = jnp.einsum('bclgn,bgecl,bclgep->bcgepn', bm, decay_st, xd)
    states = jnp.concatenate([s0.reshape(b, 1, g, e, p, n), states], axis=1)
    chunk_decay = jnp.exp(segsum(jnp.pad(a_cs[..., -1], ((0, 0), (0, 0), (0, 0), (1, 0)))))
    new_states = jnp.einsum('bgezc,bcgepn->bzgepn', chunk_decay, states)
    y_off = jnp.einsum('bclgn,bcgepn,bgecl->bclgep', cm, new_states[:, :-1], jnp.exp(a_cs))
    return (y_diag + y_off).reshape(b, l, h, p), new_states[:, -1].reshape(b, h, p, n)


def ssd_mixer(z, xbc, dt_raw, conv_state, ssm_state, conv_w, conv_b, dt_bias, a_log, d_skip, norm_g):
    b, l = z.shape[:2]
    xbc, new_conv = causal_conv(xbc, conv_state, conv_w, conv_b)
    xs, bm, cm = split_cols(xbc.astype(F32), (SSD_INNER, SSD_GROUPS * SSD_STATE, SSD_GROUPS * SSD_STATE))
    x = xs.reshape(b, l, SSD_HEADS, SSD_HEADDIM)
    bm = bm.reshape(b, l, SSD_GROUPS, SSD_STATE)
    cm = cm.reshape(b, l, SSD_GROUPS, SSD_STATE)
    dt = jax.nn.softplus(dt_raw.astype(F32) + dt_bias.astype(F32))
    a = -jnp.exp(a_log.astype(F32))
    chunk = SSD_CHUNK if l % SSD_CHUNK == 0 else l
    y, new_ssm = ssd_scan(x, dt, a, bm, cm, ssm_state.astype(F32), chunk)
    y = y + x * d_skip.astype(F32)[:, None]
    y = (y.reshape(b, l, SSD_INNER) * jax.nn.silu(z.astype(F32))).reshape(b, l, SSD_GROUPS, SSD_INNER // SSD_GROUPS)
    y = rmsnorm(y, norm_g.reshape(SSD_GROUPS, -1)).reshape(b, l, SSD_INNER)
    return y.astype(z.dtype), new_conv, new_ssm


def nsa_compress(k, w):
    b, t = k.shape[:2]
    nseg = t // NSA_CMP_STRIDE
    seg = k[:, :nseg * NSA_CMP_STRIDE].reshape(b, nseg, NSA_CMP_STRIDE, NSA_KV_HEADS, NSA_HEAD_DIM)
    first = jnp.einsum('bsrhd,rd->bshd', seg, w[:NSA_CMP_STRIDE])
    second = jnp.einsum('bsrhd,rd->bshd', seg, w[NSA_CMP_STRIDE:])
    return first[:, :-1] + second[:, 1:]


def sel_overlap(n_sel, n_cmp):
    j = jnp.arange(n_sel)[:, None]
    i = jnp.arange(n_cmp)[None, :]
    lo = jnp.maximum(j * NSA_SEL_BLOCK, i * NSA_CMP_STRIDE)
    hi = jnp.minimum(j * NSA_SEL_BLOCK + NSA_SEL_BLOCK, i * NSA_CMP_STRIDE + NSA_CMP_BLOCK)
    return (jnp.maximum(hi - lo, 0) / NSA_CMP_STRIDE).astype(F32)


def nsa_cmp_branch(q, qpos, kc, vc, table):
    nc = kc.shape[1]
    cend = jnp.arange(nc) * NSA_CMP_STRIDE + NSA_CMP_BLOCK - 1
    rel = qpos[:, None] - cend[None, :]
    s = jnp.einsum('bqhgd,bchd->bqhgc', q, kc).astype(F32) * NSA_SCALE + shared_bias(rel, table)
    p = masked_softmax(s, (rel >= 0)[:, None, None, :])
    return jnp.einsum('bqhgc,bchd->bqhgd', p, vc), p


def nsa_select(p_cmp, qpos, n_sel):
    imp = jnp.einsum('bqhgc,jc->bqhj', p_cmp, sel_overlap(n_sel, p_cmp.shape[-1]))
    j = jnp.arange(n_sel)[None, :]
    cur = (qpos // NSA_SEL_BLOCK)[:, None]
    valid = (j <= cur)[:, None, :]
    forced = ((j == 0) | (j == cur) | (j == cur - 1))[:, None, :]
    score = jnp.where(valid, jnp.where(forced, NSA_FORCE, imp), -1.0)
    if n_sel < NSA_TOPK:
        score = jnp.pad(score, ((0, 0), (0, 0), (0, 0), (0, NSA_TOPK - n_sel)), constant_values=-1.0)
    vals, idx = lax.top_k(score, NSA_TOPK)
    kpos = (idx[..., None] * NSA_SEL_BLOCK + jnp.arange(NSA_SEL_BLOCK)).reshape(*idx.shape[:-1], NSA_TOPK * NSA_SEL_BLOCK)
    kok = jnp.repeat(vals >= 0, NSA_SEL_BLOCK, axis=-1)
    return kpos, kok


def nsa_slc_branch(q, qpos, ks, vs, kpos, kok, table):
    rel = qpos[None, :, None, None] - kpos
    tb = table.astype(F32).reshape(REL_BUCKETS, NSA_KV_HEADS, NSA_GROUP)
    bias = jnp.moveaxis(tb[t5_bucket(rel), jnp.arange(NSA_KV_HEADS)[:, None]], -1, -2)
    s = jnp.einsum('bqhgd,bqhkd->bqhgk', q, ks).astype(F32) * NSA_SCALE + bias
    p = masked_softmax(s, (kok & (rel >= 0))[..., None, :])
    return jnp.einsum('bqhgk,bqhkd->bqhgd', p, vs)


def nsa_win_branch(q, qpos, kw, vw, wpos, table):
    rel = qpos[:, None] - wpos[None, :]
    mask = (rel >= 0) & (rel < NSA_WINDOW) & (wpos >= 0)[None, :]
    s = jnp.einsum('bqhgd,bkhd->bqhgk', q, kw).astype(F32) * NSA_SCALE + shared_bias(rel, table)
    p = masked_softmax(s, mask[:, None, None, :])
    return jnp.einsum('bqhgk,bkhd->bqhgd', p, vw)


def nsa_gate(g, o_c, o_s, o_w):
    return g[..., 0:1] * o_c + g[..., 1:2] * o_s + g[..., 2:3] * o_w


def nsa_prompt(q, kvs, gates, w_ck, w_cv, table):
    kc_raw, vc_raw, ks, vs, kw, vw = kvs
    b, l = q.shape[:2]
    kc = nsa_compress(kc_raw, w_ck)
    vc = nsa_compress(vc_raw, w_cv)
    n_sel = -(-l // NSA_SEL_BLOCK)
    kw_pad = jnp.pad(kw, ((0, 0), (NSA_WINDOW, 0), (0, 0), (0, 0)))
    vw_pad = jnp.pad(vw, ((0, 0), (NSA_WINDOW, 0), (0, 0), (0, 0)))
    nqb = l // NSA_QBLOCK
    bi = jnp.arange(b)[:, None, None, None]
    hi = jnp.arange(NSA_KV_HEADS)[None, None, :, None]

    def block(args):
        i, qb, gb = args
        start = i * NSA_QBLOCK
        qpos = start + jnp.arange(NSA_QBLOCK)
        o_c, p_c = nsa_cmp_branch(qb, qpos, kc, vc, table)
        kpos, kok = nsa_select(p_c, qpos, n_sel)
        kp = jnp.minimum(kpos, l - 1)
        o_s = nsa_slc_branch(qb, qpos, ks[bi, kp, hi], vs[bi, kp, hi], kpos, kok, table)
        kwb = lax.dynamic_slice_in_dim(kw_pad, start, NSA_QBLOCK + NSA_WINDOW, axis=1)
        vwb = lax.dynamic_slice_in_dim(vw_pad, start, NSA_QBLOCK + NSA_WINDOW, axis=1)
        wpos = start - NSA_WINDOW + jnp.arange(NSA_QBLOCK + NSA_WINDOW)
        o_w = nsa_win_branch(qb, qpos, kwb, vwb, wpos, table)
        return nsa_gate(gb, o_c, o_s, o_w)

    qs = q.reshape(b, nqb, NSA_QBLOCK, NSA_KV_HEADS, NSA_GROUP, NSA_HEAD_DIM).swapaxes(0, 1)
    gs = gates.reshape(b, nqb, NSA_QBLOCK, NSA_KV_HEADS, NSA_GROUP, 3).swapaxes(0, 1)
    out = lax.map(block, (jnp.arange(nqb), qs, gs))
    return out.swapaxes(0, 1).reshape(b, l, NSA_HEADS * NSA_HEAD_DIM)


def gather_rows(pool, e, new, page_table, kpos):
    db, s = new.shape[:2]
    p0 = page_table.shape[1] * PAGE_SIZE
    bi = jnp.arange(db)[:, None, None, None]
    hi = jnp.arange(NSA_KV_HEADS)[None, None, :, None]
    pp = jnp.minimum(kpos, p0 - 1)
    past = pool[e, page_table[bi, pp // PAGE_SIZE], pp % PAGE_SIZE, hi]
    cur = new[bi, jnp.clip(kpos - p0, 0, s - 1), hi]
    return jnp.where((kpos < p0)[..., None], past, cur)


def nsa_sample(q, kvs, gates, e, k_cmp_pool, v_cmp_pool, k_slc_pool, v_slc_pool, k_win_buf, v_win_buf,
               page_table, w_ck, w_cv, table):
    kc_new, vc_new, ks_new, vs_new, kw_new, vw_new = kvs
    db, s = q.shape[:2]
    p0 = page_table.shape[1] * PAGE_SIZE
    qpos = p0 + jnp.arange(s)
    past = lambda pool: pool[e, page_table].reshape(db, p0, NSA_KV_HEADS, NSA_HEAD_DIM)
    kc = nsa_compress(jnp.concatenate([past(k_cmp_pool), kc_new], axis=1), w_ck)
    vc = nsa_compress(jnp.concatenate([past(v_cmp_pool), vc_new], axis=1), w_cv)
    o_c, p_c = nsa_cmp_branch(q, qpos, kc, vc, table)
    kpos, kok = nsa_select(p_c, qpos, -(-(p0 + s) // NSA_SEL_BLOCK))
    ksel = gather_rows(k_slc_pool, e, ks_new, page_table, kpos)
    vsel = gather_rows(v_slc_pool, e, vs_new, page_table, kpos)
    o_s = nsa_slc_branch(q, qpos, ksel, vsel, kpos, kok, table)
    nbuf = k_win_buf.shape[1]
    kw = jnp.concatenate([k_win_buf.astype(kw_new.dtype), kw_new], axis=1)
    vw = jnp.concatenate([v_win_buf.astype(vw_new.dtype), vw_new], axis=1)
    wpos = p0 - nbuf + jnp.arange(nbuf + s)
    o_w = nsa_win_branch(q, qpos, kw, vw, wpos, table)
    out = nsa_gate(gates, o_c, o_s, o_w).reshape(db, s, NSA_HEADS * NSA_HEAD_DIM)
    return out, kw[:, -nbuf:], vw[:, -nbuf:]


def even_project(x, norm_g, w_in):
    b, l = x.shape[:2]
    z, xbc, dtr, q, kvs, g = split_cols(rmsnorm(x, norm_g) @ w_in, EV_SPLITS)
    q = q.reshape(b, l, NSA_KV_HEADS, NSA_GROUP, NSA_HEAD_DIM)
    kvs = tuple(t.reshape(b, l, NSA_KV_HEADS, NSA_HEAD_DIM) for t in split_cols(kvs, (NSA_KV_WIDTH,) * 6))
    g = jax.nn.sigmoid(g.astype(F32)).reshape(b, l, 3, NSA_KV_HEADS, NSA_GROUP).transpose(0, 1, 3, 4, 2)
    return z, xbc, dtr, q, kvs, g


def gla_scan(q, k, v, log_a, s0, chunk):
    b, l, h, dk = q.shape
    dv = v.shape[-1]
    nc = l // chunk
    q, k, v, log_a = [t.reshape(b, nc, chunk, h, t.shape[-1]) for t in (q, k, v, log_a)]
    bc = jnp.cumsum(log_a, axis=2)
    blast = bc[:, :, -1:]
    qt = q * jnp.exp(bc)
    kt = k * jnp.exp(-bc)
    kd = k * jnp.exp(blast - bc)
    causal = jnp.tril(jnp.ones((chunk, chunk), bool))
    att = jnp.where(causal, jnp.einsum('bclhk,bcshk->bchls', qt, kt), 0.0)
    o_intra = jnp.einsum('bchls,bcshv->bclhv', att, v)

    def step(st, inp):
        qt_c, kd_c, v_c, dec_c = inp
        o_c = jnp.einsum('blhk,bhkv->blhv', qt_c, st)
        st = jnp.exp(dec_c)[..., None] * st + jnp.einsum('blhk,blhv->bhkv', kd_c, v_c)
        return st, o_c

    xs = tuple(jnp.swapaxes(t, 0, 1) for t in (qt, kd, v, blast[:, :, 0]))
    s_fin, o_inter = lax.scan(step, s0, xs)
    return (o_intra + jnp.swapaxes(o_inter, 0, 1)).reshape(b, l, h, dv), s_fin


def gla_mixer(q, k, v, g1, r, state, w_g2, b_g, norm_g):
    b, l = q.shape[:2]
    q = q.astype(F32).reshape(b, l, GLA_HEADS, GLA_DK) * GLA_DK ** -0.5
    k = k.astype(F32).reshape(b, l, GLA_HEADS, GLA_DK)
    v = v.astype(F32).reshape(b, l, GLA_HEADS, GLA_DV)
    log_a = jax.nn.log_sigmoid(g1.astype(F32) @ w_g2.astype(F32) + b_g.astype(F32)).reshape(b, l, GLA_HEADS, GLA_DK) / GLA_TAU
    chunk = GLA_CHUNK if l % GLA_CHUNK == 0 else l
    o, new_state = gla_scan(q, k, v, log_a, state.astype(F32), chunk)
    o = rmsnorm(o, norm_g).reshape(b, l, GLA_HEADS * GLA_DV) * jax.nn.silu(r.astype(F32))
    return o.astype(r.dtype), new_state


def odd_project(x, norm_g, w_in, pos, q_norm, w_qb, kv_norm):
    b, l = x.shape[:2]
    gq, gk, gv, g1, r, qa, kva, kr = split_cols(rmsnorm(x, norm_g) @ w_in, OD_SPLITS)
    q = (rmsnorm(qa, q_norm) @ w_qb).reshape(b, l, MLA_HEADS, MLA_NOPE + MLA_ROPE)
    q_nope, q_rope = q[..., :MLA_NOPE], rope(q[..., MLA_NOPE:], pos)
    return (gq, gk, gv, g1, r), (q_nope, q_rope, rmsnorm(kva, kv_norm), rope(kr, pos))


def mla_prompt(q_nope, q_rope, c, k_rope, w_kvb):
    b, l = c.shape[:2]
    kv = (c @ w_kvb).reshape(b, l, MLA_HEADS, MLA_NOPE + MLA_V)
    k_nope, v = kv[..., :MLA_NOPE], kv[..., MLA_NOPE:]
    nqb = l // MLA_QBLOCK
    kpos = jnp.arange(l)

    def block(args):
        i, qn, qr = args
        qpos = i * MLA_QBLOCK + jnp.arange(MLA_QBLOCK)
        s = (jnp.einsum('bqhd,bkhd->bhqk', qn, k_nope) + jnp.einsum('bqhr,bkr->bhqk', qr, k_rope)).astype(F32) * MLA_SCALE
        p = jax.nn.softmax(jnp.where(kpos[None, :] <= qpos[:, None], s, NEG), axis=-1)
        return jnp.einsum('bhqk,bkhd->bqhd', p, v)

    blk = lambda t: t.reshape(b, nqb, MLA_QBLOCK, *t.shape[2:]).swapaxes(0, 1)
    o = lax.map(block, (jnp.arange(nqb), blk(q_nope), blk(q_rope)))
    return o.swapaxes(0, 1).reshape(b, l, MLA_HEADS * MLA_V)


def mla_sample(q_nope, q_rope, c, k_rope, lat_pool, kr_pool, o_idx, page_table, w_kvb):
    db, s = c.shape[:2]
    w = w_kvb.reshape(MLA_KV_RANK, MLA_HEADS, MLA_NOPE + MLA_V)
    q_lat = jnp.einsum('bshd,chd->bshc', q_nope, w[..., :MLA_NOPE])
    c_past = lat_pool[o_idx, page_table].reshape(db, -1, MLA_KV_RANK)
    kr_past = kr_pool[o_idx, page_table].reshape(db, -1, MLA_ROPE)
    s_past = (jnp.einsum('bshc,btc->bsht', q_lat, c_past) + jnp.einsum('bshr,btr->bsht', q_rope, kr_past)).astype(F32) * MLA_SCALE
    s_new = (jnp.einsum('bshc,btc->bsht', q_lat, c) + jnp.einsum('bshr,btr->bsht', q_rope, k_rope)).astype(F32) * MLA_SCALE
    causal = jnp.arange(s)[None, :] <= jnp.arange(s)[:, None]
    s_new = jnp.where(causal[:, None, :], s_new, NEG)
    p = jax.nn.softmax(jnp.concatenate([s_past, s_new], axis=-1), axis=-1)
    n_past = c_past.shape[1]
    o_lat = jnp.einsum('bsht,btc->bshc', p[..., :n_past], c_past) + jnp.einsum('bsht,btc->bshc', p[..., n_past:], c)
    return jnp.einsum('bshc,chd->bshd', o_lat, w[..., MLA_NOPE:]).reshape(db, s, MLA_HEADS * MLA_V)


def swiglu_ffn(x, g, w_gate, w_up, w_down):
    h = rmsnorm(x, g)
    return x + (jax.nn.silu(h @ w_gate) * (h @ w_up)) @ w_down


def setup_inputs(seed: int = 0) -> dict:
    key = jax.random.key(seed)
    ks = iter(jax.random.split(key, 64))
    nrm = lambda shape, scale: jax.random.normal(next(ks), shape, F32) * scale
    gain = lambda shape: 1.0 + 0.01 * jax.random.normal(next(ks), shape, F32)
    n_pages = PAST_LEN // PAGE_SIZE
    used = DEC_BATCH * n_pages
    n_pool = used + max(1, used // 4)
    win_buf = min(NSA_WINDOW, PAST_LEN)
    page_table = jax.random.permutation(next(ks), n_pool)[:used].reshape(DEC_BATCH, n_pages).astype(jnp.int32)
    dt = jnp.exp(jax.random.uniform(next(ks), (N_EVEN, SSD_HEADS), F32, math.log(1e-3), math.log(1e-1)))
    kv_pool = (N_EVEN, n_pool, PAGE_SIZE, NSA_KV_HEADS, NSA_HEAD_DIM)
    win = (N_EVEN, DEC_BATCH, win_buf, NSA_KV_HEADS, NSA_HEAD_DIM)
    return {
        'x_prompt': nrm((BATCH, SEQ, D_MODEL), 1.0),
        'x_sample': nrm((DEC_BATCH, DEC_SEQ, D_MODEL), 1.0),
        'state_ssm': nrm((N_EVEN, DEC_BATCH, SSD_HEADS, SSD_HEADDIM, SSD_STATE), 0.5),
        'state_conv': nrm((N_EVEN, DEC_BATCH, SSD_CONV - 1, SSD_CONV_CH), 1.0),
        'cache_nsa_k_cmp': nrm(kv_pool, 1.0),
        'cache_nsa_v_cmp': nrm(kv_pool, 1.0),
        'cache_nsa_k_slc': nrm(kv_pool, 1.0),
        'cache_nsa_v_slc': nrm(kv_pool, 1.0),
        'cache_nsa_k_win': nrm(win, 1.0),
        'cache_nsa_v_win': nrm(win, 1.0),
        'state_gla': nrm((N_ODD, DEC_BATCH, GLA_HEADS, GLA_DK, GLA_DV), 0.5),
        'cache_mla_latent': nrm((N_ODD, n_pool, PAGE_SIZE, MLA_KV_RANK), 1.0),
        'cache_mla_krope': nrm((N_ODD, n_pool, PAGE_SIZE, MLA_ROPE), 1.0),
        'page_table': page_table,
        'rel_bias': nrm((REL_BUCKETS, NSA_HEADS), 0.2),
        'ev_norm': gain((N_EVEN, D_MODEL)),
        'ev_w_in': nrm((N_EVEN, D_MODEL, EV_IN), D_MODEL ** -0.5),
        'ssd_conv_w': nrm((N_EVEN, SSD_CONV, SSD_CONV_CH), SSD_CONV ** -0.5),
        'ssd_conv_b': nrm((N_EVEN, SSD_CONV_CH), 0.01),
        'ssd_dt_bias': dt + jnp.log(-jnp.expm1(-dt)),
        'ssd_a_log': jnp.log(jax.random.uniform(next(ks), (N_EVEN, SSD_HEADS), F32, 1.0, 16.0)),
        'ssd_d': gain((N_EVEN, SSD_HEADS)),
        'ssd_norm': gain((N_EVEN, SSD_INNER)),
        'nsa_w_cmp_k': (1.0 + nrm((N_EVEN, NSA_CMP_BLOCK, NSA_HEAD_DIM), 0.1)) / NSA_CMP_BLOCK,
        'nsa_w_cmp_v': (1.0 + nrm((N_EVEN, NSA_CMP_BLOCK, NSA_HEAD_DIM), 0.1)) / NSA_CMP_BLOCK,
        'ev_w_out': nrm((N_EVEN, EV_MIX, D_MODEL), EV_MIX ** -0.5),
        'od_norm': gain((N_ODD, D_MODEL)),
        'od_w_in': nrm((N_ODD, D_MODEL, OD_IN), D_MODEL ** -0.5),
        'gla_w_gate2': nrm((N_ODD, GLA_GATE_RANK, GLA_HEADS * GLA_DK), GLA_GATE_RANK ** -0.5),
        'gla_b_gate': nrm((N_ODD, GLA_HEADS * GLA_DK), 0.01),
        'gla_norm': gain((N_ODD, GLA_DV)),
        'mla_q_norm': gain((N_ODD, MLA_Q_RANK)),
        'mla_w_qb': nrm((N_ODD, MLA_Q_RANK, MLA_HEADS * (MLA_NOPE + MLA_ROPE)), MLA_Q_RANK ** -0.5),
        'mla_kv_norm': gain((N_ODD, MLA_KV_RANK)),
        'mla_w_kvb': nrm((N_ODD, MLA_KV_RANK, MLA_HEADS * (MLA_NOPE + MLA_V)), MLA_KV_RANK ** -0.5),
        'od_w_out': nrm((N_ODD, OD_MIX, D_MODEL), OD_MIX ** -0.5),
        'ffn_norm': gain((DEPTH, D_MODEL)),
        'ffn_w_gate': nrm((DEPTH, D_MODEL, FFN_HIDDEN), D_MODEL ** -0.5),
        'ffn_w_up': nrm((DEPTH, D_MODEL, FFN_HIDDEN), D_MODEL ** -0.5),
        'ffn_w_down': nrm((DEPTH, FFN_HIDDEN, D_MODEL), FFN_HIDDEN ** -0.5),
        'final_norm': gain((D_MODEL,)),
    }


def reference(x_prompt, x_sample, state_ssm, state_conv, cache_nsa_k_cmp, cache_nsa_v_cmp, cache_nsa_k_slc,
              cache_nsa_v_slc, cache_nsa_k_win, cache_nsa_v_win, state_gla, cache_mla_latent, cache_mla_krope,
              page_table, rel_bias, ev_norm, ev_w_in, ssd_conv_w, ssd_conv_b, ssd_dt_bias, ssd_a_log, ssd_d,
              ssd_norm, nsa_w_cmp_k, nsa_w_cmp_v, ev_w_out, od_norm, od_w_in, gla_w_gate2, gla_b_gate, gla_norm,
              mla_q_norm, mla_w_qb, mla_kv_norm, mla_w_kvb, od_w_out, ffn_norm, ffn_w_gate, ffn_w_up, ffn_w_down,
              final_norm):
    b, l = x_prompt.shape[:2]
    db, s = x_sample.shape[:2]
    p0 = page_table.shape[1] * PAGE_SIZE
    pos_p = jnp.arange(l)
    pos_s = p0 + jnp.arange(s)
    names = ('ssm', 'conv', 'k_cmp', 'v_cmp', 'k_slc', 'v_slc', 'k_win', 'v_win', 'gla', 'lat', 'krope')
    newp = {n: [] for n in names}
    news = {n: [] for n in names}
    xp, xs = x_prompt, x_sample
    for li in range(DEPTH):
        if li % 2 == 0:
            e = li // 2
            ssd_w = (ssd_conv_w[e], ssd_conv_b[e], ssd_dt_bias[e], ssd_a_log[e], ssd_d[e], ssd_norm[e])
            z, xbc, dtr, q, kvs, g = even_project(xp, ev_norm[e], ev_w_in[e])
            y_ssd, conv_n, ssm_n = ssd_mixer(z, xbc, dtr, jnp.zeros((b, SSD_CONV - 1, SSD_CONV_CH), xp.dtype),
                                             jnp.zeros((b, SSD_HEADS, SSD_HEADDIM, SSD_STATE), F32), *ssd_w)
            y_nsa = nsa_prompt(q, kvs, g, nsa_w_cmp_k[e], nsa_w_cmp_v[e], rel_bias)
            xp = xp + jnp.concatenate([y_ssd, y_nsa.astype(xp.dtype)], axis=-1) @ ev_w_out[e]
            nw = min(NSA_WINDOW, l)
            for n, t in zip(names[:8], (ssm_n.astype(state_ssm.dtype), conv_n, kvs[0], kvs[1], kvs[2], kvs[3],
                                        kvs[4][:, -nw:], kvs[5][:, -nw:])):
                newp[n].append(t)
            z, xbc, dtr, q, kvs, g = even_project(xs, ev_norm[e], ev_w_in[e])
            y_ssd, conv_n, ssm_n = ssd_mixer(z, xbc, dtr, state_conv[e], state_ssm[e], *ssd_w)
            y_nsa, kw_n, vw_n = nsa_sample(q, kvs, g, e, cache_nsa_k_cmp, cache_nsa_v_cmp, cache_nsa_k_slc,
                                           cache_nsa_v_slc, cache_nsa_k_win[e], cache_nsa_v_win[e], page_table,
                                           nsa_w_cmp_k[e], nsa_w_cmp_v[e], rel_bias)
            xs = xs + jnp.concatenate([y_ssd, y_nsa.astype(xs.dtype)], axis=-1) @ ev_w_out[e]
            for n, t in zip(names[:8], (ssm_n.astype(state_ssm.dtype), conv_n, kvs[0], kvs[1], kvs[2], kvs[3],
                                        kw_n, vw_n)):
                news[n].append(t)
        else:
            o = li // 2
            gla_w = (gla_w_gate2[o], gla_b_gate[o], gla_norm[o])
            gla_in, mla_in = odd_project(xp, od_norm[o], od_w_in[o], pos_p, mla_q_norm[o], mla_w_qb[o], mla_kv_norm[o])
            y_gla, gla_n = gla_mixer(*gla_in, jnp.zeros((b, GLA_HEADS, GLA_DK, GLA_DV), F32), *gla_w)
            y_mla = mla_prompt(*mla_in, mla_w_kvb[o])
            xp = xp + jnp.concatenate([y_gla, y_mla.astype(xp.dtype)], axis=-1) @ od_w_out[o]
            newp['gla'].append(gla_n.astype(state_gla.dtype))
            newp['lat'].append(mla_in[2])
            newp['krope'].append(mla_in[3])
            gla_in, mla_in = odd_project(xs, od_norm[o], od_w_in[o], pos_s, mla_q_norm[o], mla_w_qb[o], mla_kv_norm[o])
            y_gla, gla_n = gla_mixer(*gla_in, state_gla[o], *gla_w)
            y_mla = mla_sample(*mla_in, cache_mla_latent, cache_mla_krope, o, page_table, mla_w_kvb[o])
            xs = xs + jnp.concatenate([y_gla, y_mla.astype(xs.dtype)], axis=-1) @ od_w_out[o]
            news['gla'].append(gla_n.astype(state_gla.dtype))
            news['lat'].append(mla_in[2])
            news['krope'].append(mla_in[3])
        xp = swiglu_ffn(xp, ffn_norm[li], ffn_w_gate[li], ffn_w_up[li], ffn_w_down[li])
        xs = swiglu_ffn(xs, ffn_norm[li], ffn_w_gate[li], ffn_w_up[li], ffn_w_down[li])
    y_prompt = rmsnorm(xp, final_norm)
    y_sample = rmsnorm(xs, final_norm)
    st = lambda d, n: jnp.stack(d[n])
    return (y_prompt, y_sample,
            st(newp, 'ssm'), st(news, 'ssm'), st(newp, 'conv'), st(news, 'conv'),
            st(newp, 'k_cmp'), st(news, 'k_cmp'), st(newp, 'v_cmp'), st(news, 'v_cmp'),
            st(newp, 'k_slc'), st(news, 'k_slc'), st(newp, 'v_slc'), st(news, 'v_slc'),
            st(newp, 'k_win'), st(news, 'k_win'), st(newp, 'v_win'), st(news, 'v_win'),
            st(newp, 'gla'), st(news, 'gla'), st(newp, 'lat'), st(news, 'lat'),
            st(newp, 'krope'), st(news, 'krope'))
```

```python
import functools
import math

import jax
import jax.numpy as jnp
import numpy as np
from jax import lax
from jax.experimental import pallas as pl
from jax.experimental.pallas import tpu as pltpu

F32 = jnp.float32
BF16 = jnp.bfloat16
EPS = 1e-6
NEG = -1e30

D_MODEL = 1024
PAGE_SIZE = 128

SSD_HEADS = 16
SSD_HEADDIM = 64
SSD_INNER = SSD_HEADS * SSD_HEADDIM
SSD_GROUPS = 2
SSD_STATE = 128
SSD_CONV = 4
SSD_CHUNK = 128
SSD_CONV_CH = SSD_INNER + 2 * SSD_GROUPS * SSD_STATE

NSA_HEADS = 16
NSA_KV_HEADS = 2
NSA_GROUP = NSA_HEADS // NSA_KV_HEADS
NSA_HEAD_DIM = 64
NSA_KV_WIDTH = NSA_KV_HEADS * NSA_HEAD_DIM
NSA_CMP_BLOCK = 32
NSA_CMP_STRIDE = 16
NSA_SEL_BLOCK = 64
NSA_TOPK = 16
NSA_WINDOW = 512
NSA_QBLOCK = 128
NSA_SCALE = NSA_HEAD_DIM ** -0.5
NSA_FORCE = 1e4

REL_BUCKETS = 32
REL_MAX_DIST = 128

GLA_HEADS = 4
GLA_DK = 128
GLA_DV = 256
GLA_GATE_RANK = 16
GLA_TAU = 16.0
GLA_CHUNK = 16

MLA_HEADS = 8
MLA_Q_RANK = 384
MLA_KV_RANK = 256
MLA_NOPE = 64
MLA_ROPE = 32
MLA_V = 64
MLA_QBLOCK = 128
MLA_SCALE = (MLA_NOPE + MLA_ROPE) ** -0.5
ROPE_THETA = 10000.0

EV_SPLITS = (SSD_INNER, SSD_CONV_CH, SSD_HEADS, NSA_HEADS * NSA_HEAD_DIM, 6 * NSA_KV_WIDTH, 3 * NSA_HEADS)
OD_SPLITS = (GLA_HEADS * GLA_DK, GLA_HEADS * GLA_DK, GLA_HEADS * GLA_DV, GLA_GATE_RANK, GLA_HEADS * GLA_DV,
             MLA_Q_RANK, MLA_KV_RANK, MLA_ROPE)

VMEM_LIMIT_BYTES = 56 * 1024 * 1024
LANE = 128


def _row_tile(m):
    for t in (512, 384, 256, 128):
        if m % t == 0:
            return t
    return m


def _col_chunks(n, width=512):
    out, c = [], 0
    while c < n:
        w = min(width, n - c)
        out.append((c, w))
        c += w
    return out


def _mm_body(*refs, norm, residual, chunks):
    it = iter(refs)
    x_ref = next(it)
    g_ref = next(it) if norm else None
    w_ref = next(it)
    r_ref = next(it) if residual else None
    o_ref = next(it)
    x = x_ref[...]
    if norm:
        x = x * lax.rsqrt(jnp.mean(x * x, axis=-1, keepdims=True) + EPS) * g_ref[...]
    xb = x.astype(BF16)
    for c0, cw in chunks:
        acc = jnp.dot(xb, w_ref[:, c0:c0 + cw], preferred_element_type=F32)
        if residual:
            acc = acc + r_ref[:, c0:c0 + cw]
        o_ref[:, c0:c0 + cw] = acc


def fused_matmul(x, w, norm_g=None, residual=None):
    m, k = x.shape
    n = w.shape[1]
    tm = _row_tile(m)
    norm = norm_g is not None
    res = residual is not None
    args = [x]
    specs = [pl.BlockSpec((tm, k), lambda i: (i, 0))]
    if norm:
        args.append(norm_g.reshape(1, k).astype(F32))
        specs.append(pl.BlockSpec((1, k), lambda i: (0, 0)))
    args.append(w.astype(BF16))
    specs.append(pl.BlockSpec((k, n), lambda i: (0, 0)))
    if res:
        args.append(residual)
        specs.append(pl.BlockSpec((tm, n), lambda i: (i, 0)))
    return pl.pallas_call(
        functools.partial(_mm_body, norm=norm, residual=res, chunks=_col_chunks(n)),
        grid=(m // tm,),
        in_specs=specs,
        out_specs=pl.BlockSpec((tm, n), lambda i: (i, 0)),
        out_shape=jax.ShapeDtypeStruct((m, n), F32),
        compiler_params=pltpu.CompilerParams(dimension_semantics=("arbitrary",),
                                             vmem_limit_bytes=VMEM_LIMIT_BYTES),
        name="fused_matmul",
    )(*args)


def _ffn_body(x_ref, g_ref, wg_ref, wu_ref, wd_ref, fg_ref, o_ref, *, chunks, final):
    x = x_ref[...]
    h = (x * lax.rsqrt(jnp.mean(x * x, axis=-1, keepdims=True) + EPS) * g_ref[...]).astype(BF16)
    acc = x
    for c0, cw in chunks:
        a = jnp.dot(h, wg_ref[:, c0:c0 + cw], preferred_element_type=F32)
        u = jnp.dot(h, wu_ref[:, c0:c0 + cw], preferred_element_type=F32)
        act = (a * jax.nn.sigmoid(a) * u).astype(BF16)
        acc = acc + jnp.dot(act, wd_ref[c0:c0 + cw, :], preferred_element_type=F32)
    if final:
        acc = acc * lax.rsqrt(jnp.mean(acc * acc, axis=-1, keepdims=True) + EPS) * fg_ref[...]
    o_ref[...] = acc


def swiglu_ffn(x, g, w_gate, w_up, w_down, final_g=None):
    m, d = x.shape
    hdim = w_gate.shape[1]
    tm = _row_tile(m)
    final = final_g is not None
    fg = (final_g if final else g).reshape(1, d).astype(F32)
    wspec = lambda shape: pl.BlockSpec(shape, lambda i: (0, 0), pipeline_mode=pl.Buffered(1))
    return pl.pallas_call(
        functools.partial(_ffn_body, chunks=_col_chunks(hdim, 256), final=final),
        grid=(m // tm,),
        in_specs=[pl.BlockSpec((tm, d), lambda i: (i, 0)),
                  pl.BlockSpec((1, d), lambda i: (0, 0)),
                  wspec((d, hdim)), wspec((d, hdim)), wspec((hdim, d)),
                  pl.BlockSpec((1, d), lambda i: (0, 0))],
        out_specs=pl.BlockSpec((tm, d), lambda i: (i, 0)),
        out_shape=jax.ShapeDtypeStruct((m, d), F32),
        compiler_params=pltpu.CompilerParams(dimension_semantics=("arbitrary",),
                                             vmem_limit_bytes=VMEM_LIMIT_BYTES),
        name="swiglu_ffn",
    )(x, g.reshape(1, d).astype(F32), w_gate.astype(BF16), w_up.astype(BF16), w_down.astype(BF16), fg)


def split_cols(h, sizes):
    return jnp.split(h, [int(i) for i in np.cumsum(sizes)[:-1]], axis=-1)


def rmsnorm(x, g):
    xf = x.astype(F32)
    y = xf * lax.rsqrt(jnp.mean(xf * xf, axis=-1, keepdims=True) + EPS)
    return (y * g.astype(F32)).astype(x.dtype)


def rope(x, pos):
    half = x.shape[-1] // 2
    freqs = ROPE_THETA ** (-jnp.arange(half, dtype=F32) / half)
    ang = pos.astype(F32)[:, None] * freqs
    shape = (pos.shape[0],) + (1,) * (x.ndim - 3) + (half,)
    cos, sin = jnp.cos(ang).reshape(shape), jnp.sin(ang).reshape(shape)
    xf = x.astype(F32)
    x1, x2 = xf[..., :half], xf[..., half:]
    return jnp.concatenate([x1 * cos - x2 * sin, x2 * cos + x1 * sin], axis=-1).astype(x.dtype)


def t5_bucket(rel):
    exact = REL_BUCKETS // 2
    n = jnp.maximum(rel, 0)
    nf = jnp.maximum(n, exact).astype(F32)
    large = exact + (jnp.log(nf / exact) / math.log(REL_MAX_DIST / exact) * (REL_BUCKETS - exact)).astype(jnp.int32)
    return jnp.where(n < exact, n, jnp.minimum(large, REL_BUCKETS - 1))


def shared_bias(rel, table):
    lq, nk = rel.shape
    return table.astype(F32)[t5_bucket(rel)].reshape(lq, nk, NSA_KV_HEADS, NSA_GROUP).transpose(0, 2, 3, 1)


def masked_softmax(s, mask):
    return jax.nn.softmax(jnp.where(mask, s, NEG), axis=-1) * mask


def causal_conv(xbc, conv_state, w, b):
    full = jnp.concatenate([conv_state.astype(xbc.dtype), xbc], axis=1)
    ch = xbc.shape[-1]
    y = lax.conv_general_dilated(full, w[:, None, :].astype(xbc.dtype), (1,), 'VALID',
                                 dimension_numbers=('NWC', 'WIO', 'NWC'), feature_group_count=ch)
    return jax.nn.silu(y + b.astype(y.dtype)), full[:, -(SSD_CONV - 1):]


def segsum(x):
    t = x.shape[-1]
    cs = jnp.cumsum(x, axis=-1)
    diff = cs[..., :, None] - cs[..., None, :]
    return jnp.where(jnp.tril(jnp.ones((t, t), bool)), diff, -jnp.inf)


def ssd_scan(x, dt, a, bm, cm, s0, chunk):
    b, l, h, p = x.shape
    g, n = bm.shape[2], bm.shape[3]
    e = h // g
    nc = l // chunk
    xd = (x * dt[..., None]).reshape(b, nc, chunk, g, e, p)
    da = (dt * a).reshape(b, nc, chunk, g, e).transpose(0, 3, 4, 1, 2)
    bm = bm.reshape(b, nc, chunk, g, n)
    cm = cm.reshape(b, nc, chunk, g, n)
    a_cs = jnp.cumsum(da, axis=-1)
    lmat = jnp.exp(segsum(da))
    cb = jnp.einsum('bclgn,bcsgn->bcgls', cm, bm)
    y_diag = jnp.einsum('bcgls,bgecls,bcsgep->bclgep', cb, lmat, xd)
    decay_st = jnp.exp(a_cs[..., -1:] - a_cs)
    states = jnp.einsum('bclgn,bgecl,bclgep->bcgepn', bm, decay_st, xd)
    states = jnp.concatenate([s0.reshape(b, 1, g, e, p, n), states], axis=1)
    chunk_decay = jnp.exp(segsum(jnp.pad(a_cs[..., -1], ((0, 0), (0, 0), (0, 0), (1, 0)))))
    new_states = jnp.einsum('bgezc,bcgepn->bzgepn', chunk_decay, states)
    y_off = jnp.einsum('bclgn,bcgepn,bgecl->bclgep', cm, new_states[:, :-1], jnp.exp(a_cs))
    return (y_diag + y_off).reshape(b, l, h, p), new_states[:, -1].reshape(b, h, p, n)


def ssd_mixer(z, xbc, dt_raw, conv_state, ssm_state, conv_w, conv_b, dt_bias, a_log, d_skip, norm_g):
    b, l = z.shape[:2]
    xbc, new_conv = causal_conv(xbc, conv_state, conv_w, conv_b)
    xs, bm, cm = split_cols(xbc.astype(F32), (SSD_INNER, SSD_GROUPS * SSD_STATE, SSD_GROUPS * SSD_STATE))
    x = xs.reshape(b, l, SSD_HEADS, SSD_HEADDIM)
    bm = bm.reshape(b, l, SSD_GROUPS, SSD_STATE)
    cm = cm.reshape(b, l, SSD_GROUPS, SSD_STATE)
    dt = jax.nn.softplus(dt_raw.astype(F32) + dt_bias.astype(F32))
    a = -jnp.exp(a_log.astype(F32))
    chunk = SSD_CHUNK if l % SSD_CHUNK == 0 else l
    y, new_ssm = ssd_scan(x, dt, a, bm, cm, ssm_state.astype(F32), chunk)
    y = y + x * d_skip.astype(F32)[:, None]
    y = (y.reshape(b, l, SSD_INNER) * jax.nn.silu(z.astype(F32))).reshape(b, l, SSD_GROUPS, SSD_INNER // SSD_GROUPS)
    y = rmsnorm(y, norm_g.reshape(SSD_GROUPS, -1)).reshape(b, l, SSD_INNER)
    return y.astype(z.dtype), new_conv, new_ssm


def nsa_compress(k, w):
    b, t = k.shape[:2]
    nseg = t // NSA_CMP_STRIDE
    seg = k[:, :nseg * NSA_CMP_STRIDE].reshape(b, nseg, NSA_CMP_STRIDE, NSA_KV_HEADS, NSA_HEAD_DIM)
    first = jnp.einsum('bsrhd,rd->bshd', seg, w[:NSA_CMP_STRIDE])
    second = jnp.einsum('bsrhd,rd->bshd', seg, w[NSA_CMP_STRIDE:])
    return first[:, :-1] + second[:, 1:]


def sel_overlap(n_sel, n_cmp):
    j = jnp.arange(n_sel)[:, None]
    i = jnp.arange(n_cmp)[None, :]
    lo = jnp.maximum(j * NSA_SEL_BLOCK, i * NSA_CMP_STRIDE)
    hi = jnp.minimum(j * NSA_SEL_BLOCK + NSA_SEL_BLOCK, i * NSA_CMP_STRIDE + NSA_CMP_BLOCK)
    return (jnp.maximum(hi - lo, 0) / NSA_CMP_STRIDE).astype(F32)


def nsa_cmp_branch(q, qpos, kc, vc, table):
    nc = kc.shape[1]
    cend = jnp.arange(nc) * NSA_CMP_STRIDE + NSA_CMP_BLOCK - 1
    rel = qpos[:, None] - cend[None, :]
    s = jnp.einsum('bqhgd,bchd->bqhgc', q, kc).astype(F32) * NSA_SCALE + shared_bias(rel, table)
    p = masked_softmax(s, (rel >= 0)[:, None, None, :])
    return jnp.einsum('bqhgc,bchd->bqhgd', p, vc), p


def nsa_select(p_cmp, qpos, n_sel):
    imp = jnp.einsum('bqhgc,jc->bqhj', p_cmp, sel_overlap(n_sel, p_cmp.shape[-1]))
    j = jnp.arange(n_sel)[None, :]
    cur = (qpos // NSA_SEL_BLOCK)[:, None]
    valid = (j <= cur)[:, None, :]
    forced = ((j == 0) | (j == cur) | (j == cur - 1))[:, None, :]
    score = jnp.where(valid, jnp.where(forced, NSA_FORCE, imp), -1.0)
    if n_sel < NSA_TOPK:
        score = jnp.pad(score, ((0, 0), (0, 0), (0, 0), (0, NSA_TOPK - n_sel)), constant_values=-1.0)
    vals, idx = lax.top_k(score, NSA_TOPK)
    kpos = (idx[..., None] * NSA_SEL_BLOCK + jnp.arange(NSA_SEL_BLOCK)).reshape(*idx.shape[:-1], NSA_TOPK * NSA_SEL_BLOCK)
    kok = jnp.repeat(vals >= 0, NSA_SEL_BLOCK, axis=-1)
    return kpos, kok


def nsa_slc_branch(q, qpos, ks, vs, kpos, kok, table):
    rel = qpos[None, :, None, None] - kpos
    tb = table.astype(F32).reshape(REL_BUCKETS, NSA_KV_HEADS, NSA_GROUP)
    bias = jnp.moveaxis(tb[t5_bucket(rel), jnp.arange(NSA_KV_HEADS)[:, None]], -1, -2)
    s = jnp.einsum('bqhgd,bqhkd->bqhgk', q, ks).astype(F32) * NSA_SCALE + bias
    p = masked_softmax(s, (kok & (rel >= 0))[..., None, :])
    return jnp.einsum('bqhgk,bqhkd->bqhgd', p, vs)


def nsa_win_branch(q, qpos, kw, vw, wpos, table):
    rel = qpos[:, None] - wpos[None, :]
    mask = (rel >= 0) & (rel < NSA_WINDOW) & (wpos >= 0)[None, :]
    s = jnp.einsum('bqhgd,bkhd->bqhgk', q, kw).astype(F32) * NSA_SCALE + shared_bias(rel, table)
    p = masked_softmax(s, mask[:, None, None, :])
    return jnp.einsum('bqhgk,bkhd->bqhgd', p, vw)


def nsa_gate(g, o_c, o_s, o_w):
    return g[..., 0:1] * o_c + g[..., 1:2] * o_s + g[..., 2:3] * o_w


def nsa_prompt(q, kvs, gates, w_ck, w_cv, table):
    kc_raw, vc_raw, ks, vs, kw, vw = kvs
    b, l = q.shape[:2]
    kc = nsa_compress(kc_raw, w_ck)
    vc = nsa_compress(vc_raw, w_cv)
    n_sel = -(-l // NSA_SEL_BLOCK)
    kw_pad = jnp.pad(kw, ((0, 0), (NSA_WINDOW, 0), (0, 0), (0, 0)))
    vw_pad = jnp.pad(vw, ((0, 0), (NSA_WINDOW, 0), (0, 0), (0, 0)))
    nqb = l // NSA_QBLOCK
    bi = jnp.arange(b)[:, None, None, None]
    hi = jnp.arange(NSA_KV_HEADS)[None, None, :, None]

    def block(args):
        i, qb, gb = args
        start = i * NSA_QBLOCK
        qpos = start + jnp.arange(NSA_QBLOCK)
        o_c, p_c = nsa_cmp_branch(qb, qpos, kc, vc, table)
        kpos, kok = nsa_select(p_c, qpos, n_sel)
        kp = jnp.minimum(kpos, l - 1)
        o_s = nsa_slc_branch(qb, qpos, ks[bi, kp, hi], vs[bi, kp, hi], kpos, kok, table)
        kwb = lax.dynamic_slice_in_dim(kw_pad, start, NSA_QBLOCK + NSA_WINDOW, axis=1)
        vwb = lax.dynamic_slice_in_dim(vw_pad, start, NSA_QBLOCK + NSA_WINDOW, axis=1)
        wpos = start - NSA_WINDOW + jnp.arange(NSA_QBLOCK + NSA_WINDOW)
        o_w = nsa_win_branch(qb, qpos, kwb, vwb, wpos, table)
        return nsa_gate(gb, o_c, o_s, o_w)

    qs = q.reshape(b, nqb, NSA_QBLOCK, NSA_KV_HEADS, NSA_GROUP, NSA_HEAD_DIM).swapaxes(0, 1)
    gs = gates.reshape(b, nqb, NSA_QBLOCK, NSA_KV_HEADS, NSA_GROUP, 3).swapaxes(0, 1)
    out = lax.map(block, (jnp.arange(nqb), qs, gs))
    return out.swapaxes(0, 1).reshape(b, l, NSA_HEADS * NSA_HEAD_DIM)


def gather_rows(pool, e, new, page_table, kpos):
    db, s = new.shape[:2]
    p0 = page_table.shape[1] * PAGE_SIZE
    bi = jnp.arange(db)[:, None, None, None]
    hi = jnp.arange(NSA_KV_HEADS)[None, None, :, None]
    pp = jnp.minimum(kpos, p0 - 1)
    past = pool[e, page_table[bi, pp // PAGE_SIZE], pp % PAGE_SIZE, hi]
    cur = new[bi, jnp.clip(kpos - p0, 0, s - 1), hi]
    return jnp.where((kpos < p0)[..., None], past, cur)


def nsa_sample(q, kvs, gates, e, k_cmp_pool, v_cmp_pool, k_slc_pool, v_slc_pool, k_win_buf, v_win_buf,
               page_table, w_ck, w_cv, table):
    kc_new, vc_new, ks_new, vs_new, kw_new, vw_new = kvs
    db, s = q.shape[:2]
    p0 = page_table.shape[1] * PAGE_SIZE
    qpos = p0 + jnp.arange(s)
    past = lambda pool: pool[e, page_table].reshape(db, p0, NSA_KV_HEADS, NSA_HEAD_DIM)
    kc = nsa_compress(jnp.concatenate([past(k_cmp_pool), kc_new], axis=1), w_ck)
    vc = nsa_compress(jnp.concatenate([past(v_cmp_pool), vc_new], axis=1), w_cv)
    o_c, p_c = nsa_cmp_branch(q, qpos, kc, vc, table)
    kpos, kok = nsa_select(p_c, qpos, -(-(p0 + s) // NSA_SEL_BLOCK))
    ksel = gather_rows(k_slc_pool, e, ks_new, page_table, kpos)
    vsel = gather_rows(v_slc_pool, e, vs_new, page_table, kpos)
    o_s = nsa_slc_branch(q, qpos, ksel, vsel, kpos, kok, table)
    nbuf = k_win_buf.shape[1]
    kw = jnp.concatenate([k_win_buf.astype(kw_new.dtype), kw_new], axis=1)
    vw = jnp.concatenate([v_win_buf.astype(vw_new.dtype), vw_new], axis=1)
    wpos = p0 - nbuf + jnp.arange(nbuf + s)
    o_w = nsa_win_branch(q, qpos, kw, vw, wpos, table)
    out = nsa_gate(gates, o_c, o_s, o_w).reshape(db, s, NSA_HEADS * NSA_HEAD_DIM)
    return out, kw[:, -nbuf:], vw[:, -nbuf:]


def even_split(h, b, l):
    z, xbc, dtr, q, kvs, g = split_cols(h.reshape(b, l, -1), EV_SPLITS)
    q = q.reshape(b, l, NSA_KV_HEADS, NSA_GROUP, NSA_HEAD_DIM)
    kvs = tuple(t.reshape(b, l, NSA_KV_HEADS, NSA_HEAD_DIM) for t in split_cols(kvs, (NSA_KV_WIDTH,) * 6))
    g = jax.nn.sigmoid(g.astype(F32)).reshape(b, l, 3, NSA_KV_HEADS, NSA_GROUP).transpose(0, 1, 3, 4, 2)
    return z, xbc, dtr, q, kvs, g


def gla_scan(q, k, v, log_a, s0, chunk):
    b, l, h, dk = q.shape
    dv = v.shape[-1]
    nc = l // chunk
    q, k, v, log_a = [t.reshape(b, nc, chunk, h, t.shape[-1]) for t in (q, k, v, log_a)]
    bc = jnp.cumsum(log_a, axis=2)
    blast = bc[:, :, -1:]
    qt = q * jnp.exp(bc)
    kt = k * jnp.exp(-bc)
    kd = k * jnp.exp(blast - bc)
    causal = jnp.tril(jnp.ones((chunk, chunk), bool))
    att = jnp.where(causal, jnp.einsum('bclhk,bcshk->bchls', qt, kt), 0.0)
    o_intra = jnp.einsum('bchls,bcshv->bclhv', att, v)

    def step(st, inp):
        qt_c, kd_c, v_c, dec_c = inp
        o_c = jnp.einsum('blhk,bhkv->blhv', qt_c, st)
        st = jnp.exp(dec_c)[..., None] * st + jnp.einsum('blhk,blhv->bhkv', kd_c, v_c)
        return st, o_c

    xs = tuple(jnp.swapaxes(t, 0, 1) for t in (qt, kd, v, blast[:, :, 0]))
    s_fin, o_inter = lax.scan(step, s0, xs)
    return (o_intra + jnp.swapaxes(o_inter, 0, 1)).reshape(b, l, h, dv), s_fin


def gla_mixer(q, k, v, g1, r, state, w_g2, b_g, norm_g):
    b, l = q.shape[:2]
    q = q.astype(F32).reshape(b, l, GLA_HEADS, GLA_DK) * GLA_DK ** -0.5
    k = k.astype(F32).reshape(b, l, GLA_HEADS, GLA_DK)
    v = v.astype(F32).reshape(b, l, GLA_HEADS, GLA_DV)
    log_a = jax.nn.log_sigmoid(g1.astype(F32) @ w_g2.astype(F32) + b_g.astype(F32)).reshape(b, l, GLA_HEADS, GLA_DK) / GLA_TAU
    chunk = GLA_CHUNK if l % GLA_CHUNK == 0 else l
    o, new_state = gla_scan(q, k, v, log_a, state.astype(F32), chunk)
    o = rmsnorm(o, norm_g).reshape(b, l, GLA_HEADS * GLA_DV) * jax.nn.silu(r.astype(F32))
    return o.astype(r.dtype), new_state


def odd_split(h, b, l, pos, q_norm, w_qb, kv_norm):
    gq, gk, gv, g1, r, qa, kva, kr = split_cols(h.reshape(b, l, -1), OD_SPLITS)
    q = fused_matmul(qa.reshape(b * l, -1), w_qb, norm_g=q_norm).reshape(b, l, MLA_HEADS, MLA_NOPE + MLA_ROPE)
    q_nope, q_rope = q[..., :MLA_NOPE], rope(q[..., MLA_NOPE:], pos)
    return (gq, gk, gv, g1, r), (q_nope, q_rope, rmsnorm(kva, kv_norm), rope(kr, pos))


def mla_prompt(q_nope, q_rope, c, k_rope, w_kvb):
    b, l = c.shape[:2]
    kv = fused_matmul(c.reshape(b * l, -1), w_kvb).reshape(b, l, MLA_HEADS, MLA_NOPE + MLA_V)
    k_nope, v = kv[..., :MLA_NOPE], kv[..., MLA_NOPE:]
    nqb = l // MLA_QBLOCK
    kpos = jnp.arange(l)

    def block(args):
        i, qn, qr = args
        qpos = i * MLA_QBLOCK + jnp.arange(MLA_QBLOCK)
        s = (jnp.einsum('bqhd,bkhd->bhqk', qn, k_nope) + jnp.einsum('bqhr,bkr->bhqk', qr, k_rope)).astype(F32) * MLA_SCALE
        p = jax.nn.softmax(jnp.where(kpos[None, :] <= qpos[:, None], s, NEG), axis=-1)
        return jnp.einsum('bhqk,bkhd->bqhd', p, v)

    blk = lambda t: t.reshape(b, nqb, MLA_QBLOCK, *t.shape[2:]).swapaxes(0, 1)
    o = lax.map(block, (jnp.arange(nqb), blk(q_nope), blk(q_rope)))
    return o.swapaxes(0, 1).reshape(b, l, MLA_HEADS * MLA_V)


def mla_sample(q_nope, q_rope, c, k_rope, lat_pool, kr_pool, o_idx, page_table, w_kvb):
    db, s = c.shape[:2]
    w = w_kvb.reshape(MLA_KV_RANK, MLA_HEADS, MLA_NOPE + MLA_V)
    q_lat = jnp.einsum('bshd,chd->bshc', q_nope, w[..., :MLA_NOPE])
    c_past = lat_pool[o_idx, page_table].reshape(db, -1, MLA_KV_RANK)
    kr_past = kr_pool[o_idx, page_table].reshape(db, -1, MLA_ROPE)
    s_past = (jnp.einsum('bshc,btc->bsht', q_lat, c_past) + jnp.einsum('bshr,btr->bsht', q_rope, kr_past)).astype(F32) * MLA_SCALE
    s_new = (jnp.einsum('bshc,btc->bsht', q_lat, c) + jnp.einsum('bshr,btr->bsht', q_rope, k_rope)).astype(F32) * MLA_SCALE
    causal = jnp.arange(s)[None, :] <= jnp.arange(s)[:, None]
    s_new = jnp.where(causal[:, None, :], s_new, NEG)
    p = jax.nn.softmax(jnp.concatenate([s_past, s_new], axis=-1), axis=-1)
    n_past = c_past.shape[1]
    o_lat = jnp.einsum('bsht,btc->bshc', p[..., :n_past], c_past) + jnp.einsum('bsht,btc->bshc', p[..., n_past:], c)
    return jnp.einsum('bshc,chd->bshd', o_lat, w[..., MLA_NOPE:]).reshape(db, s, MLA_HEADS * MLA_V)


def kernel(x_prompt, x_sample, state_ssm, state_conv, cache_nsa_k_cmp, cache_nsa_v_cmp, cache_nsa_k_slc, cache_nsa_v_slc, cache_nsa_k_win, cache_nsa_v_win, state_gla, cache_mla_latent, cache_mla_krope, page_table, rel_bias, ev_norm, ev_w_in, ssd_conv_w, ssd_conv_b, ssd_dt_bias, ssd_a_log, ssd_d, ssd_norm, nsa_w_cmp_k, nsa_w_cmp_v, ev_w_out, od_norm, od_w_in, gla_w_gate2, gla_b_gate, gla_norm, mla_q_norm, mla_w_qb, mla_kv_norm, mla_w_kvb, od_w_out, ffn_norm, ffn_w_gate, ffn_w_up, ffn_w_down, final_norm):
    b, l = x_prompt.shape[:2]
    db, s = x_sample.shape[:2]
    depth = ffn_norm.shape[0]
    p0 = page_table.shape[1] * PAGE_SIZE
    pos_p = jnp.arange(l)
    pos_s = p0 + jnp.arange(s)
    names = ('ssm', 'conv', 'k_cmp', 'v_cmp', 'k_slc', 'v_slc', 'k_win', 'v_win', 'gla', 'lat', 'krope')
    newp = {n: [] for n in names}
    news = {n: [] for n in names}
    np_rows = b * l
    x = jnp.concatenate([x_prompt.reshape(np_rows, D_MODEL), x_sample.reshape(db * s, D_MODEL)], axis=0)
    for li in range(depth):
        if li % 2 == 0:
            e = li // 2
            ssd_w = (ssd_conv_w[e], ssd_conv_b[e], ssd_dt_bias[e], ssd_a_log[e], ssd_d[e], ssd_norm[e])
            h = fused_matmul(x, ev_w_in[e], norm_g=ev_norm[e])
            z, xbc, dtr, q, kvs, g = even_split(h[:np_rows], b, l)
            y_ssd, conv_n, ssm_n = ssd_mixer(z, xbc, dtr, jnp.zeros((b, SSD_CONV - 1, SSD_CONV_CH), F32),
                                             jnp.zeros((b, SSD_HEADS, SSD_HEADDIM, SSD_STATE), F32), *ssd_w)
            y_nsa = nsa_prompt(q, kvs, g, nsa_w_cmp_k[e], nsa_w_cmp_v[e], rel_bias)
            mix_p = jnp.concatenate([y_ssd, y_nsa], axis=-1).reshape(np_rows, -1)
            nw = min(NSA_WINDOW, l)
            for n, t in zip(names[:8], (ssm_n, conv_n, kvs[0], kvs[1], kvs[2], kvs[3],
                                        kvs[4][:, -nw:], kvs[5][:, -nw:])):
                newp[n].append(t)
            z, xbc, dtr, q, kvs, g = even_split(h[np_rows:], db, s)
            y_ssd, conv_n, ssm_n = ssd_mixer(z, xbc, dtr, state_conv[e], state_ssm[e], *ssd_w)
            y_nsa, kw_n, vw_n = nsa_sample(q, kvs, g, e, cache_nsa_k_cmp, cache_nsa_v_cmp, cache_nsa_k_slc,
                                           cache_nsa_v_slc, cache_nsa_k_win[e], cache_nsa_v_win[e], page_table,
                                           nsa_w_cmp_k[e], nsa_w_cmp_v[e], rel_bias)
            mix_s = jnp.concatenate([y_ssd, y_nsa], axis=-1).reshape(db * s, -1)
            for n, t in zip(names[:8], (ssm_n, conv_n, kvs[0], kvs[1], kvs[2], kvs[3], kw_n, vw_n)):
                news[n].append(t)
            x = fused_matmul(jnp.concatenate([mix_p, mix_s], axis=0), ev_w_out[e], residual=x)
        else:
            o = li // 2
            gla_w = (gla_w_gate2[o], gla_b_gate[o], gla_norm[o])
            h = fused_matmul(x, od_w_in[o], norm_g=od_norm[o])
            gla_in, mla_in = odd_split(h[:np_rows], b, l, pos_p, mla_q_norm[o], mla_w_qb[o], mla_kv_norm[o])
            y_gla, gla_n = gla_mixer(*gla_in, jnp.zeros((b, GLA_HEADS, GLA_DK, GLA_DV), F32), *gla_w)
            y_mla = mla_prompt(*mla_in, mla_w_kvb[o])
            mix_p = jnp.concatenate([y_gla, y_mla], axis=-1).reshape(np_rows, -1)
            newp['gla'].append(gla_n)
            newp['lat'].append(mla_in[2])
            newp['krope'].append(mla_in[3])
            gla_in, mla_in = odd_split(h[np_rows:], db, s, pos_s, mla_q_norm[o], mla_w_qb[o], mla_kv_norm[o])
            y_gla, gla_n = gla_mixer(*gla_in, state_gla[o], *gla_w)
            y_mla = mla_sample(*mla_in, cache_mla_latent, cache_mla_krope, o, page_table, mla_w_kvb[o])
            mix_s = jnp.concatenate([y_gla, y_mla], axis=-1).reshape(db * s, -1)
            news['gla'].append(gla_n)
            news['lat'].append(mla_in[2])
            news['krope'].append(mla_in[3])
            x = fused_matmul(jnp.concatenate([mix_p, mix_s], axis=0), od_w_out[o], residual=x)
        x = swiglu_ffn(x, ffn_norm[li], ffn_w_gate[li], ffn_w_up[li], ffn_w_down[li],
                       final_g=final_norm if li == depth - 1 else None)
    y_prompt = x[:np_rows].reshape(b, l, D_MODEL)
    y_sample = x[np_rows:].reshape(db, s, D_MODEL)
    st = lambda d, n: jnp.stack(d[n])
    return (y_prompt, y_sample,
            st(newp, 'ssm'), st(news, 'ssm'), st(newp, 'conv'), st(news, 'conv'),
            st(newp, 'k_cmp'), st(news, 'k_cmp'), st(newp, 'v_cmp'), st(news, 'v_cmp'),
            st(newp, 'k_slc'), st(news, 'k_slc'), st(newp, 'v_slc'), st(news, 'v_slc'),
            st(newp, 'k_win'), st(news, 'k_win'), st(newp, 'v_win'), st(news, 'v_win'),
            st(newp, 'gla'), st(news, 'gla'), st(newp, 'lat'), st(news, 'lat'),
            st(newp, 'krope'), st(news, 'krope'))
```

```python
import functools
import math

import jax
import jax.numpy as jnp
import numpy as np
from jax import lax
from jax.experimental import pallas as pl
from jax.experimental.pallas import tpu as pltpu

F32 = jnp.float32
BF16 = jnp.bfloat16
EPS = 1e-6
NEG = -1e30

D_MODEL = 1024
PAGE_SIZE = 128

SSD_HEADS = 16
SSD_HEADDIM = 64
SSD_INNER = SSD_HEADS * SSD_HEADDIM
SSD_GROUPS = 2
SSD_STATE = 128
SSD_CONV = 4
SSD_CHUNK = 128
SSD_CONV_CH = SSD_INNER + 2 * SSD_GROUPS * SSD_STATE

NSA_HEADS = 16
NSA_KV_HEADS = 2
NSA_GROUP = NSA_HEADS // NSA_KV_HEADS
NSA_HEAD_DIM = 64
NSA_KV_WIDTH = NSA_KV_HEADS * NSA_HEAD_DIM
NSA_CMP_BLOCK = 32
NSA_CMP_STRIDE = 16
NSA_SEL_BLOCK = 64
NSA_TOPK = 16
NSA_WINDOW = 512
NSA_QBLOCK = 128
NSA_SCALE = NSA_HEAD_DIM ** -0.5
NSA_FORCE = 1e4

REL_BUCKETS = 32
REL_MAX_DIST = 128

GLA_HEADS = 4
GLA_DK = 128
GLA_DV = 256
GLA_GATE_RANK = 16
GLA_TAU = 16.0
GLA_CHUNK = 16

MLA_HEADS = 8
MLA_Q_RANK = 384
MLA_KV_RANK = 256
MLA_NOPE = 64
MLA_ROPE = 32
MLA_V = 64
MLA_QBLOCK = 128
MLA_SCALE = (MLA_NOPE + MLA_ROPE) ** -0.5
ROPE_THETA = 10000.0

EV_SPLITS = (SSD_INNER, SSD_CONV_CH, SSD_HEADS, NSA_HEADS * NSA_HEAD_DIM, 6 * NSA_KV_WIDTH, 3 * NSA_HEADS)
OD_SPLITS = (GLA_HEADS * GLA_DK, GLA_HEADS * GLA_DK, GLA_HEADS * GLA_DV, GLA_GATE_RANK, GLA_HEADS * GLA_DV,
             MLA_Q_RANK, MLA_KV_RANK, MLA_ROPE)

VMEM_LIMIT_BYTES = 56 * 1024 * 1024
LANE = 128


def _row_tile(m):
    for t in (512, 384, 256, 128):
        if m % t == 0:
            return t
    return m


def _col_chunks(n, width=512):
    out, c = [], 0
    while c < n:
        w = min(width, n - c)
        out.append((c, w))
        c += w
    return out


def _mm_body(*refs, norm, residual, groups):
    it = iter(refs)
    x_ref = next(it)
    g_ref = next(it) if norm else None
    w_ref = next(it)
    r_ref = next(it) if residual else None
    o_refs = list(it)
    x = x_ref[...]
    if norm:
        x = x * lax.rsqrt(jnp.mean(x * x, axis=-1, keepdims=True) + EPS) * g_ref[...]
    xb = x.astype(BF16)
    off = 0
    for o_ref, gw in zip(o_refs, groups):
        for c0, cw in _col_chunks(gw):
            acc = jnp.dot(xb, w_ref[:, off + c0:off + c0 + cw], preferred_element_type=F32)
            if residual:
                acc = acc + r_ref[:, off + c0:off + c0 + cw]
            o_ref[:, c0:c0 + cw] = acc
        off += gw


def fused_matmul(x, w, norm_g=None, residual=None, groups=None):
    m, k = x.shape
    n = w.shape[1]
    single = groups is None
    groups = (n,) if single else tuple(groups)
    assert sum(groups) == n
    tm = _row_tile(m)
    norm = norm_g is not None
    res = residual is not None
    args = [x]
    specs = [pl.BlockSpec((tm, k), lambda i: (i, 0))]
    if norm:
        args.append(norm_g.reshape(1, k).astype(F32))
        specs.append(pl.BlockSpec((1, k), lambda i: (0, 0)))
    args.append(w.astype(BF16))
    specs.append(pl.BlockSpec((k, n), lambda i: (0, 0)))
    if res:
        args.append(residual)
        specs.append(pl.BlockSpec((tm, n), lambda i: (i, 0)))
    outs = pl.pallas_call(
        functools.partial(_mm_body, norm=norm, residual=res, groups=groups),
        grid=(m // tm,),
        in_specs=specs,
        out_specs=[pl.BlockSpec((tm, gw), lambda i: (i, 0)) for gw in groups],
        out_shape=[jax.ShapeDtypeStruct((m, gw), F32) for gw in groups],
        compiler_params=pltpu.CompilerParams(dimension_semantics=("arbitrary",),
                                             vmem_limit_bytes=VMEM_LIMIT_BYTES),
        name="fused_matmul",
    )(*args)
    return outs[0] if single else outs


EV_GROUPS = (SSD_INNER, SSD_CONV_CH, NSA_HEADS * NSA_HEAD_DIM, 6 * NSA_KV_WIDTH, LANE)
SMALL_DT = 0
SMALL_GATE = SSD_HEADS
QB = NSA_QBLOCK
ROWS = NSA_GROUP * QB


def even_weight(w_in):
    z, xbc, dtr, q, kvs, g = split_cols(w_in, EV_SPLITS)
    pad = jnp.zeros((w_in.shape[0], LANE - SSD_HEADS - 3 * NSA_HEADS), w_in.dtype)
    return jnp.concatenate([z, xbc, q, kvs, dtr, g, pad], axis=1)


def _bucket_tile(rel):
    exact = REL_BUCKETS // 2
    n = jnp.maximum(rel, 0)
    nf = jnp.maximum(n, exact).astype(F32)
    large = exact + (jnp.log(nf / exact) / math.log(REL_MAX_DIST / exact) * (REL_BUCKETS - exact)).astype(jnp.int32)
    return jnp.where(n < exact, n, jnp.minimum(large, REL_BUCKETS - 1))


def _bias_tiles_body(table_ref, o_ref, *, base_step, base_off, kmul):
    t = pl.program_id(0)
    qi = lax.broadcasted_iota(jnp.int32, (QB, LANE), 0)
    ki = lax.broadcasted_iota(jnp.int32, (QB, LANE), 1)
    bucket = _bucket_tile(t * base_step + base_off + qi - kmul * ki)
    for hd in range(NSA_HEADS):
        acc = jnp.zeros((QB, LANE), F32)
        for bk in range(REL_BUCKETS):
            acc = jnp.where(bucket == bk, table_ref[bk, hd], acc)
        o_ref[0, hd] = acc


def bias_tiles(table, n_tiles, base_step, base_off, kmul):
    return pl.pallas_call(
        functools.partial(_bias_tiles_body, base_step=base_step, base_off=base_off, kmul=kmul),
        grid=(n_tiles,),
        in_specs=[pl.BlockSpec(memory_space=pltpu.SMEM)],
        out_specs=pl.BlockSpec((1, NSA_HEADS, QB, LANE), lambda t: (t, 0, 0, 0)),
        out_shape=jax.ShapeDtypeStruct((n_tiles, NSA_HEADS, QB, LANE), F32),
        compiler_params=pltpu.CompilerParams(dimension_semantics=("arbitrary",)),
        name="t5_bias_tiles",
    )(table.astype(F32))


def _compress_body(kv_ref, w_ref, o_ref):
    nseg = o_ref.shape[1]
    first = jnp.zeros(o_ref.shape[1:], F32)
    second = jnp.zeros(o_ref.shape[1:], F32)
    for r in range(NSA_CMP_STRIDE):
        rows = kv_ref[pl.ds(r, nseg, stride=NSA_CMP_STRIDE), :]
        first = first + rows * w_ref[r:r + 1, :]
        second = second + rows * w_ref[NSA_CMP_STRIDE + r:NSA_CMP_STRIDE + r + 1, :]
    out = first + pltpu.roll(second, nseg - 1, axis=0)
    row = lax.broadcasted_iota(jnp.int32, out.shape, 0)
    o_ref[0] = jnp.where(row < nseg - 1, out, 0.0)


def nsa_compress_prompt(kv, w_ck, w_cv, b, l):
    nseg = l // NSA_CMP_STRIDE
    w = jnp.concatenate([w_ck, w_ck, w_cv, w_cv], axis=1).astype(F32)
    return pl.pallas_call(
        _compress_body,
        grid=(b, 2),
        in_specs=[pl.BlockSpec((l, NSA_KV_WIDTH), lambda i, j: (i, j)),
                  pl.BlockSpec((NSA_CMP_BLOCK, NSA_KV_WIDTH), lambda i, j: (0, j))],
        out_specs=pl.BlockSpec((1, nseg, NSA_KV_WIDTH), lambda i, j: (i, 0, j)),
        out_shape=jax.ShapeDtypeStruct((b, nseg, 2 * NSA_KV_WIDTH), F32),
        compiler_params=pltpu.CompilerParams(dimension_semantics=("arbitrary", "arbitrary")),
        name="nsa_compress",
    )(kv, w)


def _nt_dot(a, b):
    return lax.dot_general(a, b, (((1,), (1,)), ((), ())), preferred_element_type=F32)


def _nsa_prompt_body(q_ref, kv_ref, sm_ref, cmp_ref, cmpb_ref, toeb_ref, ovt_ref, exp_ref, o_ref,
                     selm_ref, acc_ref, *, n_kt):
    i = pl.program_id(1)
    row = lax.broadcasted_iota(jnp.int32, (QB, LANE), 0)
    lane = lax.broadcasted_iota(jnp.int32, (QB, LANE), 1)
    low = lane < NSA_HEAD_DIM
    gates = jax.nn.sigmoid(sm_ref[...])
    qpos = i * QB + row

    def attend(qp, h, kcol, vcol, n_tiles, tile_of, use_sel, window):
        acc_ref[...] = jnp.zeros_like(acc_ref)

        def body(t, carry):
            m, l = carry
            kt = tile_of(t)
            d = i - kt
            k0 = pl.multiple_of(kt * QB, QB)
            k_t = kv_ref[pl.ds(k0, QB), kcol:kcol + LANE].astype(BF16)
            v_t = kv_ref[pl.ds(k0, QB), vcol:vcol + LANE].astype(BF16)
            s = _nt_dot(qp, k_t).reshape(NSA_GROUP, QB, LANE)
            s = s + toeb_ref[jnp.minimum(d, 2), pl.ds(h * NSA_GROUP, NSA_GROUP)]
            rel = d * QB + row - lane
            msk = rel >= 0
            if window:
                msk = msk & (rel < NSA_WINDOW)
            if use_sel:
                msk = msk & (selm_ref[kt] > 0.5)
            s = jnp.where(msk[None], s, NEG)
            m_new = jnp.maximum(m, jnp.max(s, axis=-1, keepdims=True))
            alpha = jnp.exp(m - m_new)
            p = jnp.exp(s - m_new)
            l_new = alpha * l + jnp.sum(p, axis=-1, keepdims=True)
            pv = jnp.dot(p.reshape(ROWS, LANE).astype(BF16), v_t, preferred_element_type=F32)
            acc_ref[...] = acc_ref[...] * alpha.reshape(ROWS, 1) + pv
            return m_new, l_new

        m0 = jnp.full((NSA_GROUP, QB, 1), NEG, F32)
        l0 = jnp.zeros((NSA_GROUP, QB, 1), F32)
        _, l = lax.fori_loop(0, n_tiles, body, (m0, l0))
        return acc_ref[...] / l.reshape(ROWS, 1)

    for h in range(NSA_KV_HEADS):
        in_half = low if h == 0 else jnp.logical_not(low)
        parts = []
        for g in range(NSA_GROUP):
            hd = h * NSA_GROUP + g
            t = q_ref[:, (hd // 2) * LANE:(hd // 2) * LANE + LANE]
            if hd % 2 != h:
                t = pltpu.roll(t, NSA_HEAD_DIM, axis=1)
            parts.append((jnp.where(in_half, t, 0.0) * NSA_SCALE).astype(BF16))
        qp = jnp.concatenate(parts, axis=0)

        kc = cmp_ref[0, :, 0:LANE].astype(BF16)
        vc = cmp_ref[0, :, LANE:2 * LANE].astype(BF16)
        s = _nt_dot(qp, kc).reshape(NSA_GROUP, QB, LANE) + cmpb_ref[0, pl.ds(h * NSA_GROUP, NSA_GROUP)]
        mc = (qpos - (NSA_CMP_STRIDE * lane + NSA_CMP_BLOCK - 1)) >= 0
        s = jnp.where(mc[None], s, NEG)
        e = jnp.exp(s - jnp.max(s, axis=-1, keepdims=True))
        p = e / jnp.sum(e, axis=-1, keepdims=True) * mc[None].astype(F32)
        pb = p.reshape(ROWS, LANE).astype(BF16)
        o_c = jnp.dot(pb, vc, preferred_element_type=F32)
        imp = jnp.sum(jnp.dot(pb, ovt_ref[...], preferred_element_type=F32).reshape(NSA_GROUP, QB, LANE), axis=0)

        cur = qpos // NSA_SEL_BLOCK
        valid = lane <= cur
        forced = (lane == 0) | (lane == cur) | (lane == cur - 1)
        score = jnp.where(valid, jnp.where(forced, NSA_FORCE, imp), -1.0)
        cnt = jnp.zeros((QB, LANE), jnp.int32)
        for jp in range(n_kt * 2):
            col = score[:, jp:jp + 1]
            beats = (col > score) | ((col == score) & (lane > jp))
            cnt = cnt + beats.astype(jnp.int32)
        sel = ((cnt < NSA_TOPK) & valid & (lane < n_kt * 2)).astype(BF16)
        for kt in range(n_kt):
            selm_ref[kt] = jnp.dot(sel, exp_ref[:, kt * QB:(kt + 1) * QB], preferred_element_type=F32)

        o_s = attend(qp, h, 2 * LANE, 3 * LANE, i + 1, lambda t: t, True, False)
        o_w = attend(qp, h, 4 * LANE, 5 * LANE, jnp.minimum(i, NSA_WINDOW // QB) + 1, lambda t: i - t, False, True)

        outs = []
        for g in range(NSA_GROUP):
            c = SMALL_GATE + h * NSA_GROUP + g
            r0 = g * QB
            outs.append(gates[:, c:c + 1] * o_c[r0:r0 + QB]
                        + gates[:, c + NSA_HEADS:c + NSA_HEADS + 1] * o_s[r0:r0 + QB]
                        + gates[:, c + 2 * NSA_HEADS:c + 2 * NSA_HEADS + 1] * o_w[r0:r0 + QB])
        for pr in range(NSA_GROUP // 2):
            a, b2 = outs[2 * pr], outs[2 * pr + 1]
            if h == 0:
                b2 = pltpu.roll(b2, NSA_HEAD_DIM, axis=1)
            else:
                a = pltpu.roll(a, NSA_HEAD_DIM, axis=1)
            c0 = (h * NSA_GROUP + 2 * pr) * NSA_HEAD_DIM
            o_ref[:, c0:c0 + LANE] = jnp.where(low, a, b2)


def nsa_prompt_attention(q, kv, small, w_ck, w_cv, table, b, l):
    nqb = l // QB
    n_sel = l // NSA_SEL_BLOCK
    nseg = l // NSA_CMP_STRIDE
    assert nseg == LANE and n_sel <= LANE and l % QB == 0
    cmp = nsa_compress_prompt(kv, w_ck, w_cv, b, l)
    cmp_bias = bias_tiles(table, nqb, QB, -(NSA_CMP_BLOCK - 1), NSA_CMP_STRIDE)
    toe_bias = bias_tiles(table, 3, QB, 0, 1)
    jj = np.arange(LANE)[None, :]
    cc = np.arange(LANE)[:, None]
    lo = np.maximum(jj * NSA_SEL_BLOCK, cc * NSA_CMP_STRIDE)
    hi = np.minimum(jj * NSA_SEL_BLOCK + NSA_SEL_BLOCK, cc * NSA_CMP_STRIDE + NSA_CMP_BLOCK)
    ovt = np.where((jj < n_sel) & (cc < nseg - 1), np.maximum(hi - lo, 0) / NSA_CMP_STRIDE, 0.0)
    expand = (np.arange(l)[None, :] // NSA_SEL_BLOCK == np.arange(LANE)[:, None]).astype(np.float32)
    return pl.pallas_call(
        functools.partial(_nsa_prompt_body, n_kt=nqb),
        grid=(b, nqb),
        in_specs=[pl.BlockSpec((QB, NSA_HEADS * NSA_HEAD_DIM), lambda bi, i: (bi * nqb + i, 0)),
                  pl.BlockSpec((l, 6 * NSA_KV_WIDTH), lambda bi, i: (bi, 0)),
                  pl.BlockSpec((QB, LANE), lambda bi, i: (bi * nqb + i, 0)),
                  pl.BlockSpec((1, nseg, 2 * NSA_KV_WIDTH), lambda bi, i: (bi, 0, 0)),
                  pl.BlockSpec((1, NSA_HEADS, QB, LANE), lambda bi, i: (i, 0, 0, 0)),
                  pl.BlockSpec((3, NSA_HEADS, QB, LANE), lambda bi, i: (0, 0, 0, 0)),
                  pl.BlockSpec((LANE, LANE), lambda bi, i: (0, 0)),
                  pl.BlockSpec((LANE, l), lambda bi, i: (0, 0))],
        out_specs=pl.BlockSpec((QB, NSA_HEADS * NSA_HEAD_DIM), lambda bi, i: (bi * nqb + i, 0)),
        out_shape=jax.ShapeDtypeStruct((b * l, NSA_HEADS * NSA_HEAD_DIM), F32),
        scratch_shapes=[pltpu.VMEM((nqb, QB, QB), F32), pltpu.VMEM((ROWS, LANE), F32)],
        compiler_params=pltpu.CompilerParams(dimension_semantics=("arbitrary", "arbitrary"),
                                             vmem_limit_bytes=VMEM_LIMIT_BYTES),
        name="nsa_prompt",
    )(q, kv, small, cmp, cmp_bias, toe_bias, jnp.asarray(ovt, BF16), jnp.asarray(expand, BF16))


def _ffn_body(x_ref, g_ref, wg_ref, wu_ref, wd_ref, fg_ref, o_ref, *, chunks, final):
    x = x_ref[...]
    h = (x * lax.rsqrt(jnp.mean(x * x, axis=-1, keepdims=True) + EPS) * g_ref[...]).astype(BF16)
    acc = x
    for c0, cw in chunks:
        a = jnp.dot(h, wg_ref[:, c0:c0 + cw], preferred_element_type=F32)
        u = jnp.dot(h, wu_ref[:, c0:c0 + cw], preferred_element_type=F32)
        act = (a * jax.nn.sigmoid(a) * u).astype(BF16)
        acc = acc + jnp.dot(act, wd_ref[c0:c0 + cw, :], preferred_element_type=F32)
    if final:
        acc = acc * lax.rsqrt(jnp.mean(acc * acc, axis=-1, keepdims=True) + EPS) * fg_ref[...]
    o_ref[...] = acc


def swiglu_ffn(x, g, w_gate, w_up, w_down, final_g=None):
    m, d = x.shape
    hdim = w_gate.shape[1]
    tm = _row_tile(m)
    final = final_g is not None
    fg = (final_g if final else g).reshape(1, d).astype(F32)
    wspec = lambda shape: pl.BlockSpec(shape, lambda i: (0, 0), pipeline_mode=pl.Buffered(1))
    return pl.pallas_call(
        functools.partial(_ffn_body, chunks=_col_chunks(hdim, 256), final=final),
        grid=(m // tm,),
        in_specs=[pl.BlockSpec((tm, d), lambda i: (i, 0)),
                  pl.BlockSpec((1, d), lambda i: (0, 0)),
                  wspec((d, hdim)), wspec((d, hdim)), wspec((hdim, d)),
                  pl.BlockSpec((1, d), lambda i: (0, 0))],
        out_specs=pl.BlockSpec((tm, d), lambda i: (i, 0)),
        out_shape=jax.ShapeDtypeStruct((m, d), F32),
        compiler_params=pltpu.CompilerParams(dimension_semantics=("arbitrary",),
                                             vmem_limit_bytes=VMEM_LIMIT_BYTES),
        name="swiglu_ffn",
    )(x, g.reshape(1, d).astype(F32), w_gate.astype(BF16), w_up.astype(BF16), w_down.astype(BF16), fg)


def split_cols(h, sizes):
    return jnp.split(h, [int(i) for i in np.cumsum(sizes)[:-1]], axis=-1)


def rmsnorm(x, g):
    xf = x.astype(F32)
    y = xf * lax.rsqrt(jnp.mean(xf * xf, axis=-1, keepdims=True) + EPS)
    return (y * g.astype(F32)).astype(x.dtype)


def rope(x, pos):
    half = x.shape[-1] // 2
    freqs = ROPE_THETA ** (-jnp.arange(half, dtype=F32) / half)
    ang = pos.astype(F32)[:, None] * freqs
    shape = (pos.shape[0],) + (1,) * (x.ndim - 3) + (half,)
    cos, sin = jnp.cos(ang).reshape(shape), jnp.sin(ang).reshape(shape)
    xf = x.astype(F32)
    x1, x2 = xf[..., :half], xf[..., half:]
    return jnp.concatenate([x1 * cos - x2 * sin, x2 * cos + x1 * sin], axis=-1).astype(x.dtype)


def t5_bucket(rel):
    exact = REL_BUCKETS // 2
    n = jnp.maximum(rel, 0)
    nf = jnp.maximum(n, exact).astype(F32)
    large = exact + (jnp.log(nf / exact) / math.log(REL_MAX_DIST / exact) * (REL_BUCKETS - exact)).astype(jnp.int32)
    return jnp.where(n < exact, n, jnp.minimum(large, REL_BUCKETS - 1))


def shared_bias(rel, table):
    lq, nk = rel.shape
    return table.astype(F32)[t5_bucket(rel)].reshape(lq, nk, NSA_KV_HEADS, NSA_GROUP).transpose(0, 2, 3, 1)


def masked_softmax(s, mask):
    return jax.nn.softmax(jnp.where(mask, s, NEG), axis=-1) * mask


def causal_conv(xbc, conv_state, w, b):
    full = jnp.concatenate([conv_state.astype(xbc.dtype), xbc], axis=1)
    ch = xbc.shape[-1]
    y = lax.conv_general_dilated(full, w[:, None, :].astype(xbc.dtype), (1,), 'VALID',
                                 dimension_numbers=('NWC', 'WIO', 'NWC'), feature_group_count=ch)
    return jax.nn.silu(y + b.astype(y.dtype)), full[:, -(SSD_CONV - 1):]


def segsum(x):
    t = x.shape[-1]
    cs = jnp.cumsum(x, axis=-1)
    diff = cs[..., :, None] - cs[..., None, :]
    return jnp.where(jnp.tril(jnp.ones((t, t), bool)), diff, -jnp.inf)


def ssd_scan(x, dt, a, bm, cm, s0, chunk):
    b, l, h, p = x.shape
    g, n = bm.shape[2], bm.shape[3]
    e = h // g
    nc = l // chunk
    xd = (x * dt[..., None]).reshape(b, nc, chunk, g, e, p)
    da = (dt * a).reshape(b, nc, chunk, g, e).transpose(0, 3, 4, 1, 2)
    bm = bm.reshape(b, nc, chunk, g, n)
    cm = cm.reshape(b, nc, chunk, g, n)
    a_cs = jnp.cumsum(da, axis=-1)
    lmat = jnp.exp(segsum(da))
    cb = jnp.einsum('bclgn,bcsgn->bcgls', cm, bm)
    y_diag = jnp.einsum('bcgls,bgecls,bcsgep->bclgep', cb, lmat, xd)
    decay_st = jnp.exp(a_cs[..., -1:] - a_cs)
    states = jnp.einsum('bclgn,bgecl,bclgep->bcgepn', bm, decay_st, xd)
    states = jnp.concatenate([s0.reshape(b, 1, g, e, p, n), states], axis=1)
    chunk_decay = jnp.exp(segsum(jnp.pad(a_cs[..., -1], ((0, 0), (0, 0), (0, 0), (1, 0)))))
    new_states = jnp.einsum('bgezc,bcgepn->bzgepn', chunk_decay, states)
    y_off = jnp.einsum('bclgn,bcgepn,bgecl->bclgep', cm, new_states[:, :-1], jnp.exp(a_cs))
    return (y_diag + y_off).reshape(b, l, h, p), new_states[:, -1].reshape(b, h, p, n)


def ssd_mixer(z, xbc, dt_raw, conv_state, ssm_state, conv_w, conv_b, dt_bias, a_log, d_skip, norm_g):
    b, l = z.shape[:2]
    xbc, new_conv = causal_conv(xbc, conv_state, conv_w, conv_b)
    xs, bm, cm = split_cols(xbc.astype(F32), (SSD_INNER, SSD_GROUPS * SSD_STATE, SSD_GROUPS * SSD_STATE))
    x = xs.reshape(b, l, SSD_HEADS, SSD_HEADDIM)
    bm = bm.reshape(b, l, SSD_GROUPS, SSD_STATE)
    cm = cm.reshape(b, l, SSD_GROUPS, SSD_STATE)
    dt = jax.nn.softplus(dt_raw.astype(F32) + dt_bias.astype(F32))
    a = -jnp.exp(a_log.astype(F32))
    chunk = SSD_CHUNK if l % SSD_CHUNK == 0 else l
    y, new_ssm = ssd_scan(x, dt, a, bm, cm, ssm_state.astype(F32), chunk)
    y = y + x * d_skip.astype(F32)[:, None]
    y = (y.reshape(b, l, SSD_INNER) * jax.nn.silu(z.astype(F32))).reshape(b, l, SSD_GROUPS, SSD_INNER // SSD_GROUPS)
    y = rmsnorm(y, norm_g.reshape(SSD_GROUPS, -1)).reshape(b, l, SSD_INNER)
    return y.astype(z.dtype), new_conv, new_ssm


def nsa_compress(k, w):
    b, t = k.shape[:2]
    nseg = t // NSA_CMP_STRIDE
    seg = k[:, :nseg * NSA_CMP_STRIDE].reshape(b, nseg, NSA_CMP_STRIDE, NSA_KV_HEADS, NSA_HEAD_DIM)
    first = jnp.einsum('bsrhd,rd->bshd', seg, w[:NSA_CMP_STRIDE])
    second = jnp.einsum('bsrhd,rd->bshd', seg, w[NSA_CMP_STRIDE:])
    return first[:, :-1] + second[:, 1:]


def sel_overlap(n_sel, n_cmp):
    j = jnp.arange(n_sel)[:, None]
    i = jnp.arange(n_cmp)[None, :]
    lo = jnp.maximum(j * NSA_SEL_BLOCK, i * NSA_CMP_STRIDE)
    hi = jnp.minimum(j * NSA_SEL_BLOCK + NSA_SEL_BLOCK, i * NSA_CMP_STRIDE + NSA_CMP_BLOCK)
    return (jnp.maximum(hi - lo, 0) / NSA_CMP_STRIDE).astype(F32)


def nsa_cmp_branch(q, qpos, kc, vc, table):
    nc = kc.shape[1]
    cend = jnp.arange(nc) * NSA_CMP_STRIDE + NSA_CMP_BLOCK - 1
    rel = qpos[:, None] - cend[None, :]
    s = jnp.einsum('bqhgd,bchd->bqhgc', q, kc).astype(F32) * NSA_SCALE + shared_bias(rel, table)
    p = masked_softmax(s, (rel >= 0)[:, None, None, :])
    return jnp.einsum('bqhgc,bchd->bqhgd', p, vc), p


def nsa_select(p_cmp, qpos, n_sel):
    imp = jnp.einsum('bqhgc,jc->bqhj', p_cmp, sel_overlap(n_sel, p_cmp.shape[-1]))
    j = jnp.arange(n_sel)[None, :]
    cur = (qpos // NSA_SEL_BLOCK)[:, None]
    valid = (j <= cur)[:, None, :]
    forced = ((j == 0) | (j == cur) | (j == cur - 1))[:, None, :]
    score = jnp.where(valid, jnp.where(forced, NSA_FORCE, imp), -1.0)
    if n_sel < NSA_TOPK:
        score = jnp.pad(score, ((0, 0), (0, 0), (0, 0), (0, NSA_TOPK - n_sel)), constant_values=-1.0)
    vals, idx = lax.top_k(score, NSA_TOPK)
    kpos = (idx[..., None] * NSA_SEL_BLOCK + jnp.arange(NSA_SEL_BLOCK)).reshape(*idx.shape[:-1], NSA_TOPK * NSA_SEL_BLOCK)
    kok = jnp.repeat(vals >= 0, NSA_SEL_BLOCK, axis=-1)
    return kpos, kok


def nsa_slc_branch(q, qpos, ks, vs, kpos, kok, table):
    rel = qpos[None, :, None, None] - kpos
    tb = table.astype(F32).reshape(REL_BUCKETS, NSA_KV_HEADS, NSA_GROUP)
    bias = jnp.moveaxis(tb[t5_bucket(rel), jnp.arange(NSA_KV_HEADS)[:, None]], -1, -2)
    s = jnp.einsum('bqhgd,bqhkd->bqhgk', q, ks).astype(F32) * NSA_SCALE + bias
    p = masked_softmax(s, (kok & (rel >= 0))[..., None, :])
    return jnp.einsum('bqhgk,bqhkd->bqhgd', p, vs)


def nsa_win_branch(q, qpos, kw, vw, wpos, table):
    rel = qpos[:, None] - wpos[None, :]
    mask = (rel >= 0) & (rel < NSA_WINDOW) & (wpos >= 0)[None, :]
    s = jnp.einsum('bqhgd,bkhd->bqhgk', q, kw).astype(F32) * NSA_SCALE + shared_bias(rel, table)
    p = masked_softmax(s, mask[:, None, None, :])
    return jnp.einsum('bqhgk,bkhd->bqhgd', p, vw)


def nsa_gate(g, o_c, o_s, o_w):
    return g[..., 0:1] * o_c + g[..., 1:2] * o_s + g[..., 2:3] * o_w


def gather_rows(pool, e, new, page_table, kpos):
    db, s = new.shape[:2]
    p0 = page_table.shape[1] * PAGE_SIZE
    bi = jnp.arange(db)[:, None, None, None]
    hi = jnp.arange(NSA_KV_HEADS)[None, None, :, None]
    pp = jnp.minimum(kpos, p0 - 1)
    past = pool[e, page_table[bi, pp // PAGE_SIZE], pp % PAGE_SIZE, hi]
    cur = new[bi, jnp.clip(kpos - p0, 0, s - 1), hi]
    return jnp.where((kpos < p0)[..., None], past, cur)


def nsa_sample(q, kvs, gates, e, k_cmp_pool, v_cmp_pool, k_slc_pool, v_slc_pool, k_win_buf, v_win_buf,
               page_table, w_ck, w_cv, table):
    kc_new, vc_new, ks_new, vs_new, kw_new, vw_new = kvs
    db, s = q.shape[:2]
    p0 = page_table.shape[1] * PAGE_SIZE
    qpos = p0 + jnp.arange(s)
    past = lambda pool: pool[e, page_table].reshape(db, p0, NSA_KV_HEADS, NSA_HEAD_DIM)
    kc = nsa_compress(jnp.concatenate([past(k_cmp_pool), kc_new], axis=1), w_ck)
    vc = nsa_compress(jnp.concatenate([past(v_cmp_pool), vc_new], axis=1), w_cv)
    o_c, p_c = nsa_cmp_branch(q, qpos, kc, vc, table)
    kpos, kok = nsa_select(p_c, qpos, -(-(p0 + s) // NSA_SEL_BLOCK))
    ksel = gather_rows(k_slc_pool, e, ks_new, page_table, kpos)
    vsel = gather_rows(v_slc_pool, e, vs_new, page_table, kpos)
    o_s = nsa_slc_branch(q, qpos, ksel, vsel, kpos, kok, table)
    nbuf = k_win_buf.shape[1]
    kw = jnp.concatenate([k_win_buf.astype(kw_new.dtype), kw_new], axis=1)
    vw = jnp.concatenate([v_win_buf.astype(vw_new.dtype), vw_new], axis=1)
    wpos = p0 - nbuf + jnp.arange(nbuf + s)
    o_w = nsa_win_branch(q, qpos, kw, vw, wpos, table)
    out = nsa_gate(gates, o_c, o_s, o_w).reshape(db, s, NSA_HEADS * NSA_HEAD_DIM)
    return out, kw[:, -nbuf:], vw[:, -nbuf:]


def even_split(parts, r0, b, l):
    z, xbc, q, kv, small = (t[r0:r0 + b * l].reshape(b, l, -1) for t in parts)
    dtr = small[..., SMALL_DT:SMALL_DT + SSD_HEADS]
    g = small[..., SMALL_GATE:SMALL_GATE + 3 * NSA_HEADS]
    q = q.reshape(b, l, NSA_KV_HEADS, NSA_GROUP, NSA_HEAD_DIM)
    kvs = tuple(t.reshape(b, l, NSA_KV_HEADS, NSA_HEAD_DIM) for t in split_cols(kv, (NSA_KV_WIDTH,) * 6))
    g = jax.nn.sigmoid(g.astype(F32)).reshape(b, l, 3, NSA_KV_HEADS, NSA_GROUP).transpose(0, 1, 3, 4, 2)
    return z, xbc, dtr, q, kvs, g


def gla_scan(q, k, v, log_a, s0, chunk):
    b, l, h, dk = q.shape
    dv = v.shape[-1]
    nc = l // chunk
    q, k, v, log_a = [t.reshape(b, nc, chunk, h, t.shape[-1]) for t in (q, k, v, log_a)]
    bc = jnp.cumsum(log_a, axis=2)
    blast = bc[:, :, -1:]
    qt = q * jnp.exp(bc)
    kt = k * jnp.exp(-bc)
    kd = k * jnp.exp(blast - bc)
    causal = jnp.tril(jnp.ones((chunk, chunk), bool))
    att = jnp.where(causal, jnp.einsum('bclhk,bcshk->bchls', qt, kt), 0.0)
    o_intra = jnp.einsum('bchls,bcshv->bclhv', att, v)

    def step(st, inp):
        qt_c, kd_c, v_c, dec_c = inp
        o_c = jnp.einsum('blhk,bhkv->blhv', qt_c, st)
        st = jnp.exp(dec_c)[..., None] * st + jnp.einsum('blhk,blhv->bhkv', kd_c, v_c)
        return st, o_c

    xs = tuple(jnp.swapaxes(t, 0, 1) for t in (qt, kd, v, blast[:, :, 0]))
    s_fin, o_inter = lax.scan(step, s0, xs)
    return (o_intra + jnp.swapaxes(o_inter, 0, 1)).reshape(b, l, h, dv), s_fin


def gla_mixer(q, k, v, g1, r, state, w_g2, b_g, norm_g):
    b, l = q.shape[:2]
    q = q.astype(F32).reshape(b, l, GLA_HEADS, GLA_DK) * GLA_DK ** -0.5
    k = k.astype(F32).reshape(b, l, GLA_HEADS, GLA_DK)
    v = v.astype(F32).reshape(b, l, GLA_HEADS, GLA_DV)
    log_a = jax.nn.log_sigmoid(g1.astype(F32) @ w_g2.astype(F32) + b_g.astype(F32)).reshape(b, l, GLA_HEADS, GLA_DK) / GLA_TAU
    chunk = GLA_CHUNK if l % GLA_CHUNK == 0 else l
    o, new_state = gla_scan(q, k, v, log_a, state.astype(F32), chunk)
    o = rmsnorm(o, norm_g).reshape(b, l, GLA_HEADS * GLA_DV) * jax.nn.silu(r.astype(F32))
    return o.astype(r.dtype), new_state


def odd_split(h, b, l, pos, q_norm, w_qb, kv_norm):
    gq, gk, gv, g1, r, qa, kva, kr = split_cols(h.reshape(b, l, -1), OD_SPLITS)
    q = fused_matmul(qa.reshape(b * l, -1), w_qb, norm_g=q_norm).reshape(b, l, MLA_HEADS, MLA_NOPE + MLA_ROPE)
    q_nope, q_rope = q[..., :MLA_NOPE], rope(q[..., MLA_NOPE:], pos)
    return (gq, gk, gv, g1, r), (q_nope, q_rope, rmsnorm(kva, kv_norm), rope(kr, pos))


def mla_prompt(q_nope, q_rope, c, k_rope, w_kvb):
    b, l = c.shape[:2]
    kv = fused_matmul(c.reshape(b * l, -1), w_kvb).reshape(b, l, MLA_HEADS, MLA_NOPE + MLA_V)
    k_nope, v = kv[..., :MLA_NOPE], kv[..., MLA_NOPE:]
    nqb = l // MLA_QBLOCK
    kpos = jnp.arange(l)

    def block(args):
        i, qn, qr = args
        qpos = i * MLA_QBLOCK + jnp.arange(MLA_QBLOCK)
        s = (jnp.einsum('bqhd,bkhd->bhqk', qn, k_nope) + jnp.einsum('bqhr,bkr->bhqk', qr, k_rope)).astype(F32) * MLA_SCALE
        p = jax.nn.softmax(jnp.where(kpos[None, :] <= qpos[:, None], s, NEG), axis=-1)
        return jnp.einsum('bhqk,bkhd->bqhd', p, v)

    blk = lambda t: t.reshape(b, nqb, MLA_QBLOCK, *t.shape[2:]).swapaxes(0, 1)
    o = lax.map(block, (jnp.arange(nqb), blk(q_nope), blk(q_rope)))
    return o.swapaxes(0, 1).reshape(b, l, MLA_HEADS * MLA_V)


def mla_sample(q_nope, q_rope, c, k_rope, lat_pool, kr_pool, o_idx, page_table, w_kvb):
    db, s = c.shape[:2]
    w = w_kvb.reshape(MLA_KV_RANK, MLA_HEADS, MLA_NOPE + MLA_V)
    q_lat = jnp.einsum('bshd,chd->bshc', q_nope, w[..., :MLA_NOPE])
    c_past = lat_pool[o_idx, page_table].reshape(db, -1, MLA_KV_RANK)
    kr_past = kr_pool[o_idx, page_table].reshape(db, -1, MLA_ROPE)
    s_past = (jnp.einsum('bshc,btc->bsht', q_lat, c_past) + jnp.einsum('bshr,btr->bsht', q_rope, kr_past)).astype(F32) * MLA_SCALE
    s_new = (jnp.einsum('bshc,btc->bsht', q_lat, c) + jnp.einsum('bshr,btr->bsht', q_rope, k_rope)).astype(F32) * MLA_SCALE
    causal = jnp.arange(s)[None, :] <= jnp.arange(s)[:, None]
    s_new = jnp.where(causal[:, None, :], s_new, NEG)
    p = jax.nn.softmax(jnp.concatenate([s_past, s_new], axis=-1), axis=-1)
    n_past = c_past.shape[1]
    o_lat = jnp.einsum('bsht,btc->bshc', p[..., :n_past], c_past) + jnp.einsum('bsht,btc->bshc', p[..., n_past:], c)
    return jnp.einsum('bshc,chd->bshd', o_lat, w[..., MLA_NOPE:]).reshape(db, s, MLA_HEADS * MLA_V)


def kernel(x_prompt, x_sample, state_ssm, state_conv, cache_nsa_k_cmp, cache_nsa_v_cmp, cache_nsa_k_slc, cache_nsa_v_slc, cache_nsa_k_win, cache_nsa_v_win, state_gla, cache_mla_latent, cache_mla_krope, page_table, rel_bias, ev_norm, ev_w_in, ssd_conv_w, ssd_conv_b, ssd_dt_bias, ssd_a_log, ssd_d, ssd_norm, nsa_w_cmp_k, nsa_w_cmp_v, ev_w_out, od_norm, od_w_in, gla_w_gate2, gla_b_gate, gla_norm, mla_q_norm, mla_w_qb, mla_kv_norm, mla_w_kvb, od_w_out, ffn_norm, ffn_w_gate, ffn_w_up, ffn_w_down, final_norm):
    b, l = x_prompt.shape[:2]
    db, s = x_sample.shape[:2]
    depth = ffn_norm.shape[0]
    p0 = page_table.shape[1] * PAGE_SIZE
    pos_p = jnp.arange(l)
    pos_s = p0 + jnp.arange(s)
    names = ('ssm', 'conv', 'k_cmp', 'v_cmp', 'k_slc', 'v_slc', 'k_win', 'v_win', 'gla', 'lat', 'krope')
    newp = {n: [] for n in names}
    news = {n: [] for n in names}
    np_rows = b * l
    x = jnp.concatenate([x_prompt.reshape(np_rows, D_MODEL), x_sample.reshape(db * s, D_MODEL)], axis=0)
    for li in range(depth):
        if li % 2 == 0:
            e = li // 2
            ssd_w = (ssd_conv_w[e], ssd_conv_b[e], ssd_dt_bias[e], ssd_a_log[e], ssd_d[e], ssd_norm[e])
            parts = fused_matmul(x, even_weight(ev_w_in[e]), norm_g=ev_norm[e], groups=EV_GROUPS)
            z, xbc, dtr, q, kvs, g = even_split(parts, 0, b, l)
            y_ssd, conv_n, ssm_n = ssd_mixer(z, xbc, dtr, jnp.zeros((b, SSD_CONV - 1, SSD_CONV_CH), F32),
                                             jnp.zeros((b, SSD_HEADS, SSD_HEADDIM, SSD_STATE), F32), *ssd_w)
            y_nsa = nsa_prompt_attention(parts[2], parts[3], parts[4], nsa_w_cmp_k[e], nsa_w_cmp_v[e], rel_bias, b, l)
            mix_p = jnp.concatenate([y_ssd.reshape(np_rows, -1), y_nsa], axis=-1)
            nw = min(NSA_WINDOW, l)
            for n, t in zip(names[:8], (ssm_n, conv_n, kvs[0], kvs[1], kvs[2], kvs[3],
                                        kvs[4][:, -nw:], kvs[5][:, -nw:])):
                newp[n].append(t)
            z, xbc, dtr, q, kvs, g = even_split(parts, np_rows, db, s)
            y_ssd, conv_n, ssm_n = ssd_mixer(z, xbc, dtr, state_conv[e], state_ssm[e], *ssd_w)
            y_nsa, kw_n, vw_n = nsa_sample(q, kvs, g, e, cache_nsa_k_cmp, cache_nsa_v_cmp, cache_nsa_k_slc,
                                           cache_nsa_v_slc, cache_nsa_k_win[e], cache_nsa_v_win[e], page_table,
                                           nsa_w_cmp_k[e], nsa_w_cmp_v[e], rel_bias)
            mix_s = jnp.concatenate([y_ssd, y_nsa], axis=-1).reshape(db * s, -1)
            for n, t in zip(names[:8], (ssm_n, conv_n, kvs[0], kvs[1], kvs[2], kvs[3], kw_n, vw_n)):
                news[n].append(t)
            x = fused_matmul(jnp.concatenate([mix_p, mix_s], axis=0), ev_w_out[e], residual=x)
        else:
            o = li // 2
            gla_w = (gla_w_gate2[o], gla_b_gate[o], gla_norm[o])
            h = fused_matmul(x, od_w_in[o], norm_g=od_norm[o])
            gla_in, mla_in = odd_split(h[:np_rows], b, l, pos_p, mla_q_norm[o], mla_w_qb[o], mla_kv_norm[o])
            y_gla, gla_n = gla_mixer(*gla_in, jnp.zeros((b, GLA_HEADS, GLA_DK, GLA_DV), F32), *gla_w)
            y_mla = mla_prompt(*mla_in, mla_w_kvb[o])
            mix_p = jnp.concatenate([y_gla, y_mla], axis=-1).reshape(np_rows, -1)
            newp['gla'].append(gla_n)
            newp['lat'].append(mla_in[2])
            newp['krope'].append(mla_in[3])
            gla_in, mla_in = odd_split(h[np_rows:], db, s, pos_s, mla_q_norm[o], mla_w_qb[o], mla_kv_norm[o])
            y_gla, gla_n = gla_mixer(*gla_in, state_gla[o], *gla_w)
            y_mla = mla_sample(*mla_in, cache_mla_latent, cache_mla_krope, o, page_table, mla_w_kvb[o])
            mix_s = jnp.concatenate([y_gla, y_mla], axis=-1).reshape(db * s, -1)
            news['gla'].append(gla_n)
            news['lat'].append(mla_in[2])
            news['krope'].append(mla_in[3])
            x = fused_matmul(jnp.concatenate([mix_p, mix_s], axis=0), od_w_out[o], residual=x)
        x = swiglu_ffn(x, ffn_norm[li], ffn_w_gate[li], ffn_w_up[li], ffn_w_down[li],
                       final_g=final_norm if li == depth - 1 else None)
    y_prompt = x[:np_rows].reshape(b, l, D_MODEL)
    y_sample = x[np_rows:].reshape(db, s, D_MODEL)
    st = lambda d, n: jnp.stack(d[n])
    return (y_prompt, y_sample,
            st(newp, 'ssm'), st(news, 'ssm'), st(newp, 'conv'), st(news, 'conv'),
            st(newp, 'k_cmp'), st(news, 'k_cmp'), st(newp, 'v_cmp'), st(news, 'v_cmp'),
            st(newp, 'k_slc'), st(news, 'k_slc'), st(newp, 'v_slc'), st(news, 'v_slc'),
            st(newp, 'k_win'), st(news, 'k_win'), st(newp, 'v_win'), st(news, 'v_win'),
            st(newp, 'gla'), st(news, 'gla'), st(newp, 'lat'), st(news, 'lat'),
            st(newp, 'krope'), st(news, 'krope'))
```

```python
import functools
import math

import jax
import jax.numpy as jnp
import numpy as np
from jax import lax
from jax.experimental import pallas as pl
from jax.experimental.pallas import tpu as pltpu

F32 = jnp.float32
BF16 = jnp.bfloat16
EPS = 1e-6
NEG = -1e30

D_MODEL = 1024
PAGE_SIZE = 128

SSD_HEADS = 16
SSD_HEADDIM = 64
SSD_INNER = SSD_HEADS * SSD_HEADDIM
SSD_GROUPS = 2
SSD_STATE = 128
SSD_CONV = 4
SSD_CHUNK = 128
SSD_CONV_CH = SSD_INNER + 2 * SSD_GROUPS * SSD_STATE

NSA_HEADS = 16
NSA_KV_HEADS = 2
NSA_GROUP = NSA_HEADS // NSA_KV_HEADS
NSA_HEAD_DIM = 64
NSA_KV_WIDTH = NSA_KV_HEADS * NSA_HEAD_DIM
NSA_CMP_BLOCK = 32
NSA_CMP_STRIDE = 16
NSA_SEL_BLOCK = 64
NSA_TOPK = 16
NSA_WINDOW = 512
NSA_QBLOCK = 128
NSA_SCALE = NSA_HEAD_DIM ** -0.5
NSA_FORCE = 1e4

REL_BUCKETS = 32
REL_MAX_DIST = 128

GLA_HEADS = 4
GLA_DK = 128
GLA_DV = 256
GLA_GATE_RANK = 16
GLA_TAU = 16.0
GLA_CHUNK = 16

MLA_HEADS = 8
MLA_Q_RANK = 384
MLA_KV_RANK = 256
MLA_NOPE = 64
MLA_ROPE = 32
MLA_V = 64
MLA_QBLOCK = 128
MLA_SCALE = (MLA_NOPE + MLA_ROPE) ** -0.5
ROPE_THETA = 10000.0

EV_SPLITS = (SSD_INNER, SSD_CONV_CH, SSD_HEADS, NSA_HEADS * NSA_HEAD_DIM, 6 * NSA_KV_WIDTH, 3 * NSA_HEADS)
OD_SPLITS = (GLA_HEADS * GLA_DK, GLA_HEADS * GLA_DK, GLA_HEADS * GLA_DV, GLA_GATE_RANK, GLA_HEADS * GLA_DV,
             MLA_Q_RANK, MLA_KV_RANK, MLA_ROPE)

VMEM_LIMIT_BYTES = 56 * 1024 * 1024
LANE = 128


def _row_tile(m):
    for t in (512, 384, 256, 128):
        if m % t == 0:
            return t
    return m


def _col_chunks(n, width=512):
    out, c = [], 0
    while c < n:
        w = min(width, n - c)
        out.append((c, w))
        c += w
    return out


def _mm_body(*refs, norm, residual, groups):
    it = iter(refs)
    x_ref = next(it)
    g_ref = next(it) if norm else None
    w_ref = next(it)
    r_ref = next(it) if residual else None
    o_refs = list(it)
    x = x_ref[...]
    if norm:
        x = x * lax.rsqrt(jnp.mean(x * x, axis=-1, keepdims=True) + EPS) * g_ref[...]
    xb = x.astype(BF16)
    off = 0
    for o_ref, gw in zip(o_refs, groups):
        for c0, cw in _col_chunks(gw):
            acc = jnp.dot(xb, w_ref[:, off + c0:off + c0 + cw], preferred_element_type=F32)
            if residual:
                acc = acc + r_ref[:, off + c0:off + c0 + cw]
            o_ref[:, c0:c0 + cw] = acc
        off += gw


def fused_matmul(x, w, norm_g=None, residual=None, groups=None):
    m, k = x.shape
    n = w.shape[1]
    single = groups is None
    groups = (n,) if single else tuple(groups)
    assert sum(groups) == n
    tm = _row_tile(m)
    norm = norm_g is not None
    res = residual is not None
    args = [x]
    specs = [pl.BlockSpec((tm, k), lambda i: (i, 0))]
    if norm:
        args.append(norm_g.reshape(1, k).astype(F32))
        specs.append(pl.BlockSpec((1, k), lambda i: (0, 0)))
    args.append(w.astype(BF16))
    specs.append(pl.BlockSpec((k, n), lambda i: (0, 0)))
    if res:
        args.append(residual)
        specs.append(pl.BlockSpec((tm, n), lambda i: (i, 0)))
    outs = pl.pallas_call(
        functools.partial(_mm_body, norm=norm, residual=res, groups=groups),
        grid=(m // tm,),
        in_specs=specs,
        out_specs=[pl.BlockSpec((tm, gw), lambda i: (i, 0)) for gw in groups],
        out_shape=[jax.ShapeDtypeStruct((m, gw), F32) for gw in groups],
        compiler_params=pltpu.CompilerParams(dimension_semantics=("arbitrary",),
                                             vmem_limit_bytes=VMEM_LIMIT_BYTES),
        name="fused_matmul",
    )(*args)
    return outs[0] if single else outs


EV_GROUPS = (SSD_INNER, SSD_CONV_CH, NSA_HEADS * NSA_HEAD_DIM, 6 * NSA_KV_WIDTH, LANE)
SMALL_DT = 0
SMALL_GATE = SSD_HEADS
QB = NSA_QBLOCK
ROWS = NSA_GROUP * QB


def even_weight(w_in):
    z, xbc, dtr, q, kvs, g = split_cols(w_in, EV_SPLITS)
    pad = jnp.zeros((w_in.shape[0], LANE - SSD_HEADS - 3 * NSA_HEADS), w_in.dtype)
    return jnp.concatenate([z, xbc, q, kvs, dtr, g, pad], axis=1)


def _bucket_tile(rel):
    exact = REL_BUCKETS // 2
    n = jnp.maximum(rel, 0)
    nf = jnp.maximum(n, exact).astype(F32)
    large = exact + (jnp.log(nf / exact) / math.log(REL_MAX_DIST / exact) * (REL_BUCKETS - exact)).astype(jnp.int32)
    return jnp.where(n < exact, n, jnp.minimum(large, REL_BUCKETS - 1))


def _bias_tiles_body(table_ref, o_ref, *, base_step, base_off, kmul, qmul):
    t = pl.program_id(0)
    shape = o_ref.shape[2:]
    qi = lax.broadcasted_iota(jnp.int32, shape, 0)
    ki = lax.broadcasted_iota(jnp.int32, shape, 1)
    bucket = _bucket_tile(t * base_step + base_off + qmul * qi - kmul * ki)
    for hd in range(NSA_HEADS):
        acc = jnp.zeros(shape, F32)
        for bk in range(REL_BUCKETS):
            acc = jnp.where(bucket == bk, table_ref[bk, hd], acc)
        o_ref[0, hd] = acc


def bias_tiles(table, n_tiles, base_step, base_off, kmul, qmul=1, rows=QB):
    return pl.pallas_call(
        functools.partial(_bias_tiles_body, base_step=base_step, base_off=base_off, kmul=kmul, qmul=qmul),
        grid=(n_tiles,),
        in_specs=[pl.BlockSpec(memory_space=pltpu.SMEM)],
        out_specs=pl.BlockSpec((1, NSA_HEADS, rows, LANE), lambda t: (t, 0, 0, 0)),
        out_shape=jax.ShapeDtypeStruct((n_tiles, NSA_HEADS, rows, LANE), F32),
        compiler_params=pltpu.CompilerParams(dimension_semantics=("arbitrary",)),
        name="t5_bias_tiles",
    )(table.astype(F32))


def _compress_body(kv_ref, w_ref, o_ref):
    nseg = o_ref.shape[1]
    first = jnp.zeros(o_ref.shape[1:], F32)
    second = jnp.zeros(o_ref.shape[1:], F32)
    for r in range(NSA_CMP_STRIDE):
        rows = kv_ref[pl.ds(r, nseg, stride=NSA_CMP_STRIDE), :]
        first = first + rows * w_ref[r:r + 1, :]
        second = second + rows * w_ref[NSA_CMP_STRIDE + r:NSA_CMP_STRIDE + r + 1, :]
    out = first + pltpu.roll(second, nseg - 1, axis=0)
    row = lax.broadcasted_iota(jnp.int32, out.shape, 0)
    o_ref[0] = jnp.where(row < nseg - 1, out, 0.0)


def nsa_compress_prompt(kv, w_ck, w_cv, b, l):
    nseg = l // NSA_CMP_STRIDE
    w = jnp.concatenate([w_ck, w_ck, w_cv, w_cv], axis=1).astype(F32)
    return pl.pallas_call(
        _compress_body,
        grid=(b, 2),
        in_specs=[pl.BlockSpec((l, NSA_KV_WIDTH), lambda i, j: (i, j)),
                  pl.BlockSpec((NSA_CMP_BLOCK, NSA_KV_WIDTH), lambda i, j: (0, j))],
        out_specs=pl.BlockSpec((1, nseg, NSA_KV_WIDTH), lambda i, j: (i, 0, j)),
        out_shape=jax.ShapeDtypeStruct((b, nseg, 2 * NSA_KV_WIDTH), F32),
        compiler_params=pltpu.CompilerParams(dimension_semantics=("arbitrary", "arbitrary")),
        name="nsa_compress",
    )(kv, w)


def _nt_dot(a, b):
    return lax.dot_general(a, b, (((1,), (1,)), ((), ())), preferred_element_type=F32)


def _importance(p_group_sum, ovt):
    return jnp.dot(p_group_sum.astype(BF16), ovt, preferred_element_type=F32)


def _nsa_prompt_body(q_ref, kv_ref, sm_ref, cmp_ref, cmpb_ref, toeb_ref, ovt_ref, exp_ref, o_ref,
                     selm_ref, acc_ref, *, n_kt):
    i = pl.program_id(1)
    row = lax.broadcasted_iota(jnp.int32, (QB, LANE), 0)
    lane = lax.broadcasted_iota(jnp.int32, (QB, LANE), 1)
    low = lane < NSA_HEAD_DIM
    gates = jax.nn.sigmoid(sm_ref[...])
    qpos = i * QB + row

    def attend(qp, h, kcol, vcol, n_tiles, tile_of, use_sel, window):
        acc_ref[...] = jnp.zeros_like(acc_ref)

        def body(t, carry):
            m, l = carry
            kt = tile_of(t)
            d = i - kt
            k0 = pl.multiple_of(kt * QB, QB)
            k_t = kv_ref[pl.ds(k0, QB), kcol:kcol + LANE].astype(BF16)
            v_t = kv_ref[pl.ds(k0, QB), vcol:vcol + LANE].astype(BF16)
            s = _nt_dot(qp, k_t).reshape(NSA_GROUP, QB, LANE)
            s = s + toeb_ref[jnp.minimum(d, 2), pl.ds(h * NSA_GROUP, NSA_GROUP)]
            rel = d * QB + row - lane
            msk = rel >= 0
            if window:
                msk = msk & (rel < NSA_WINDOW)
            if use_sel:
                msk = msk & (selm_ref[kt] > 0.5)
            s = jnp.where(msk[None], s, NEG)
            m_new = jnp.maximum(m, jnp.max(s, axis=-1, keepdims=True))
            alpha = jnp.exp(m - m_new)
            p = jnp.exp(s - m_new)
            l_new = alpha * l + jnp.sum(p, axis=-1, keepdims=True)
            pv = jnp.dot(p.reshape(ROWS, LANE).astype(BF16), v_t, preferred_element_type=F32)
            acc_ref[...] = acc_ref[...] * alpha.reshape(ROWS, 1) + pv
            return m_new, l_new

        m0 = jnp.full((NSA_GROUP, QB, 1), NEG, F32)
        l0 = jnp.zeros((NSA_GROUP, QB, 1), F32)
        _, l = lax.fori_loop(0, n_tiles, body, (m0, l0))
        return acc_ref[...] / l.reshape(ROWS, 1)

    for h in range(NSA_KV_HEADS):
        in_half = low if h == 0 else jnp.logical_not(low)
        parts = []
        for g in range(NSA_GROUP):
            hd = h * NSA_GROUP + g
            t = q_ref[:, (hd // 2) * LANE:(hd // 2) * LANE + LANE]
            if hd % 2 != h:
                t = pltpu.roll(t, NSA_HEAD_DIM, axis=1)
            parts.append((jnp.where(in_half, t, 0.0) * NSA_SCALE).astype(BF16))
        qp = jnp.concatenate(parts, axis=0)

        kc = cmp_ref[0, :, 0:LANE].astype(BF16)
        vc = cmp_ref[0, :, LANE:2 * LANE].astype(BF16)
        s = _nt_dot(qp, kc).reshape(NSA_GROUP, QB, LANE) + cmpb_ref[0, pl.ds(h * NSA_GROUP, NSA_GROUP)]
        mc = (qpos - (NSA_CMP_STRIDE * lane + NSA_CMP_BLOCK - 1)) >= 0
        s = jnp.where(mc[None], s, NEG)
        e = jnp.exp(s - jnp.max(s, axis=-1, keepdims=True))
        p = e / jnp.sum(e, axis=-1, keepdims=True) * mc[None].astype(F32)
        pb = p.reshape(ROWS, LANE).astype(BF16)
        o_c = jnp.dot(pb, vc, preferred_element_type=F32)
        imp = _importance(jnp.sum(p, axis=0), ovt_ref[...])

        cur = qpos // NSA_SEL_BLOCK
        valid = lane <= cur
        forced = (lane == 0) | (lane == cur) | (lane == cur - 1)
        score = jnp.where(valid, jnp.where(forced, NSA_FORCE, imp), -1.0)
        cnt = jnp.zeros((QB, LANE), jnp.int32)
        for jp in range(n_kt * 2):
            col = score[:, jp:jp + 1]
            beats = (col > score) | ((col == score) & (lane > jp))
            cnt = cnt + beats.astype(jnp.int32)
        sel = ((cnt < NSA_TOPK) & valid & (lane < n_kt * 2)).astype(BF16)
        for kt in range(n_kt):
            selm_ref[kt] = jnp.dot(sel, exp_ref[:, kt * QB:(kt + 1) * QB], preferred_element_type=F32)

        o_s = attend(qp, h, 2 * LANE, 3 * LANE, i + 1, lambda t: t, True, False)
        o_w = attend(qp, h, 4 * LANE, 5 * LANE, jnp.minimum(i, NSA_WINDOW // QB) + 1, lambda t: i - t, False, True)

        outs = []
        for g in range(NSA_GROUP):
            c = SMALL_GATE + h * NSA_GROUP + g
            r0 = g * QB
            outs.append(gates[:, c:c + 1] * o_c[r0:r0 + QB]
                        + gates[:, c + NSA_HEADS:c + NSA_HEADS + 1] * o_s[r0:r0 + QB]
                        + gates[:, c + 2 * NSA_HEADS:c + 2 * NSA_HEADS + 1] * o_w[r0:r0 + QB])
        for pr in range(NSA_GROUP // 2):
            a, b2 = outs[2 * pr], outs[2 * pr + 1]
            if h == 0:
                b2 = pltpu.roll(b2, NSA_HEAD_DIM, axis=1)
            else:
                a = pltpu.roll(a, NSA_HEAD_DIM, axis=1)
            c0 = (h * NSA_GROUP + 2 * pr) * NSA_HEAD_DIM
            o_ref[:, c0:c0 + LANE] = jnp.where(low, a, b2)


def nsa_prompt_attention(q, kv, small, w_ck, w_cv, table, b, l):
    nqb = l // QB
    n_sel = l // NSA_SEL_BLOCK
    nseg = l // NSA_CMP_STRIDE
    assert nseg == LANE and n_sel <= LANE and l % QB == 0
    cmp = nsa_compress_prompt(kv, w_ck, w_cv, b, l)
    cmp_bias = bias_tiles(table, nqb, QB, -(NSA_CMP_BLOCK - 1), NSA_CMP_STRIDE)
    toe_bias = bias_tiles(table, 3, QB, 0, 1)
    jj = np.arange(LANE)[None, :]
    cc = np.arange(LANE)[:, None]
    lo = np.maximum(jj * NSA_SEL_BLOCK, cc * NSA_CMP_STRIDE)
    hi = np.minimum(jj * NSA_SEL_BLOCK + NSA_SEL_BLOCK, cc * NSA_CMP_STRIDE + NSA_CMP_BLOCK)
    ovt = np.where((jj < n_sel) & (cc < nseg - 1), np.maximum(hi - lo, 0) / NSA_CMP_STRIDE, 0.0)
    expand = (np.arange(l)[None, :] // NSA_SEL_BLOCK == np.arange(LANE)[:, None]).astype(np.float32)
    return pl.pallas_call(
        functools.partial(_nsa_prompt_body, n_kt=nqb),
        grid=(b, nqb),
        in_specs=[pl.BlockSpec((QB, NSA_HEADS * NSA_HEAD_DIM), lambda bi, i: (bi * nqb + i, 0)),
                  pl.BlockSpec((l, 6 * NSA_KV_WIDTH), lambda bi, i: (bi, 0)),
                  pl.BlockSpec((QB, LANE), lambda bi, i: (bi * nqb + i, 0)),
                  pl.BlockSpec((1, nseg, 2 * NSA_KV_WIDTH), lambda bi, i: (bi, 0, 0)),
                  pl.BlockSpec((1, NSA_HEADS, QB, LANE), lambda bi, i: (i, 0, 0, 0)),
                  pl.BlockSpec((3, NSA_HEADS, QB, LANE), lambda bi, i: (0, 0, 0, 0)),
                  pl.BlockSpec((LANE, LANE), lambda bi, i: (0, 0)),
                  pl.BlockSpec((LANE, l), lambda bi, i: (0, 0))],
        out_specs=pl.BlockSpec((QB, NSA_HEADS * NSA_HEAD_DIM), lambda bi, i: (bi * nqb + i, 0)),
        out_shape=jax.ShapeDtypeStruct((b * l, NSA_HEADS * NSA_HEAD_DIM), F32),
        scratch_shapes=[pltpu.VMEM((nqb, QB, QB), F32), pltpu.VMEM((ROWS, LANE), F32)],
        compiler_params=pltpu.CompilerParams(dimension_semantics=("arbitrary", "arbitrary"),
                                             vmem_limit_bytes=VMEM_LIMIT_BYTES),
        name="nsa_prompt",
    )(q, kv, small, cmp, cmp_bias, toe_bias, jnp.asarray(ovt, BF16), jnp.asarray(expand, BF16))


PAGES_PER_STEP = 16
NSEG_PAGE = PAGE_SIZE // NSA_CMP_STRIDE
SELF_RANK = 2


def _bf(x):
    return x.astype(BF16).astype(F32)


def _bias_rows(table, n_tiles, step, off, kmul):
    t = bias_tiles(table, n_tiles, step, off, kmul, qmul=0, rows=8)[:, :, 0, :]
    return t.transpose(1, 0, 2).reshape(NSA_KV_HEADS, NSA_GROUP, n_tiles * LANE)


def _nsa_sample_cmp_body(pt_ref, *refs, n_pages):
    npg = PAGES_PER_STEP
    kp, vp = refs[:npg], refs[npg:2 * npg]
    qp_ref, wk_ref, wv_ref, bias_ref, ovt_ref, oc_ref, idx_ref, fk, sk, fv, sv = refs[2 * npg:]
    g = pl.program_id(1)
    nseg = n_pages * NSEG_PAGE
    for k in range(npg):
        for src, w_ref, f_ref, s_ref in ((kp[k], wk_ref, fk, sk), (vp[k], wv_ref, fv, sv)):
            f = jnp.zeros((NSEG_PAGE, LANE), F32)
            s = jnp.zeros((NSEG_PAGE, LANE), F32)
            for r in range(NSA_CMP_STRIDE):
                rows = src[0, pl.ds(r, NSEG_PAGE, stride=NSA_CMP_STRIDE), :]
                f = f + rows * w_ref[r:r + 1, :]
                s = s + rows * w_ref[NSA_CMP_STRIDE + r:NSA_CMP_STRIDE + r + 1, :]
            row0 = pl.multiple_of((g * npg + k) * NSEG_PAGE, NSEG_PAGE)
            f_ref[pl.ds(row0, NSEG_PAGE), :] = f
            s_ref[pl.ds(row0, NSEG_PAGE), :] = s

    @pl.when(g == pl.num_programs(1) - 1)
    def _():
        rowi = lax.broadcasted_iota(jnp.int32, (nseg, LANE), 0)
        kc = jnp.where(rowi < nseg - 1, fk[...] + pltpu.roll(sk[...], nseg - 1, axis=0), 0.0).astype(BF16)
        vc = jnp.where(rowi < nseg - 1, fv[...] + pltpu.roll(sv[...], nseg - 1, axis=0), 0.0).astype(BF16)
        mc = lax.broadcasted_iota(jnp.int32, (NSA_GROUP, nseg), 1) < nseg - 1
        lane = lax.broadcasted_iota(jnp.int32, (8, LANE), 1)
        imps = []
        for h in range(NSA_KV_HEADS):
            qp = (qp_ref[0, h] * NSA_SCALE).astype(BF16)
            s = jnp.where(mc, _nt_dot(qp, kc) + bias_ref[h], NEG)
            e = jnp.exp(s - jnp.max(s, axis=-1, keepdims=True))
            p = e / jnp.sum(e, axis=-1, keepdims=True) * mc.astype(F32)
            pb = p.astype(BF16)
            oc_ref[0, h] = jnp.dot(pb, vc, preferred_element_type=F32)
            imps.append(_importance(jnp.sum(p, axis=0, keepdims=True), ovt_ref[...]))
        imp = jnp.concatenate(imps + [jnp.zeros((8 - NSA_KV_HEADS, LANE), F32)], axis=0)
        forced = (lane == 0) | (lane == LANE - 1)
        score = jnp.where(forced, NSA_FORCE, imp)
        cnt = (score < NSA_FORCE).astype(F32)
        for jp in range(LANE):
            col = score[:, jp:jp + 1]
            cnt = cnt + ((col > score) | ((col == score) & (lane > jp))).astype(F32)
        lanef = lane.astype(F32)
        idx = jnp.where(lane == SELF_RANK, float(LANE), 0.0)
        for k in range(NSA_TOPK):
            if k != SELF_RANK:
                v = jnp.sum(jnp.where(cnt == k, lanef, 0.0), axis=-1, keepdims=True)
                idx = jnp.where(lane == k, v, idx)
        idx_ref[0] = idx.astype(jnp.int32)


def nsa_sample_cmp(qp, k_pool, v_pool, e, page_table, w_ck, w_cv, table):
    db, n_pages = page_table.shape
    n_pool = k_pool.shape[1]
    p0 = n_pages * PAGE_SIZE
    nseg = n_pages * NSEG_PAGE
    npg = PAGES_PER_STEP
    assert p0 // NSA_SEL_BLOCK == LANE and n_pages % npg == 0
    kp = k_pool.reshape(-1, PAGE_SIZE, NSA_KV_WIDTH)
    vp = v_pool.reshape(-1, PAGE_SIZE, NSA_KV_WIDTH)
    bias = _bias_rows(table, nseg // LANE, -LANE * NSA_CMP_STRIDE, p0 - (NSA_CMP_BLOCK - 1), NSA_CMP_STRIDE)
    jj = np.arange(LANE)[None, :]
    cc = np.arange(nseg)[:, None]
    lo = np.maximum(jj * NSA_SEL_BLOCK, cc * NSA_CMP_STRIDE)
    hi = np.minimum(jj * NSA_SEL_BLOCK + NSA_SEL_BLOCK, cc * NSA_CMP_STRIDE + NSA_CMP_BLOCK)
    ovt = np.where(cc < nseg - 1, np.maximum(hi - lo, 0) / NSA_CMP_STRIDE, 0.0)

    def page_spec(k):
        return pl.BlockSpec((1, PAGE_SIZE, NSA_KV_WIDTH), lambda b, g, pt: (pt[b, g * npg + k] + e * n_pool, 0, 0))

    const = lambda shape: pl.BlockSpec(shape, lambda b, g, pt: (0,) * len(shape))
    grid_spec = pltpu.PrefetchScalarGridSpec(
        num_scalar_prefetch=1,
        grid=(db, n_pages // npg),
        in_specs=[page_spec(k) for k in range(npg)] * 2 + [
            pl.BlockSpec((1, NSA_KV_HEADS, NSA_GROUP, LANE), lambda b, g, pt: (b, 0, 0, 0)),
            const((NSA_CMP_BLOCK, NSA_KV_WIDTH)), const((NSA_CMP_BLOCK, NSA_KV_WIDTH)),
            const((NSA_KV_HEADS, NSA_GROUP, nseg)), const((nseg, LANE))],
        out_specs=[pl.BlockSpec((1, NSA_KV_HEADS, NSA_GROUP, LANE), lambda b, g, pt: (b, 0, 0, 0)),
                   pl.BlockSpec((1, 8, LANE), lambda b, g, pt: (b, 0, 0))],
        scratch_shapes=[pltpu.VMEM((nseg, LANE), F32)] * 4)
    tile2 = lambda w: jnp.concatenate([w, w], axis=1).astype(F32)
    return pl.pallas_call(
        functools.partial(_nsa_sample_cmp_body, n_pages=n_pages),
        grid_spec=grid_spec,
        out_shape=[jax.ShapeDtypeStruct((db, NSA_KV_HEADS, NSA_GROUP, LANE), F32),
                   jax.ShapeDtypeStruct((db, 8, LANE), jnp.int32)],
        compiler_params=pltpu.CompilerParams(dimension_semantics=("arbitrary", "arbitrary")),
        name="nsa_sample_cmp",
    )(page_table, *([kp] * npg), *([vp] * npg), qp, tile2(w_ck), tile2(w_cv), bias, jnp.asarray(ovt, BF16))


def _nsa_sample_attn_body(pt_ref, idx_ref, *refs, nbuf):
    nb = NSA_KV_HEADS * NSA_TOPK
    kb, vb = refs[:nb], refs[nb:2 * nb]
    (qp_ref, new_ref, kwin_ref, vwin_ref, oc_ref, gate_ref, bnear_ref, bwin_ref, bends_ref,
     o_ref, kwo_ref, vwo_ref) = refs[2 * nb:]
    b = pl.program_id(0)
    new = new_ref[0]
    ks_new, vs_new, kw_new, vw_new = (new[:, c * LANE:(c + 1) * LANE] for c in (2, 3, 4, 5))
    kwin = kwin_ref[0]
    vwin = vwin_ref[0]
    wlane = lax.broadcasted_iota(jnp.int32, (NSA_GROUP, nbuf), 1)
    half = NSA_SEL_BLOCK
    for h in range(NSA_KV_HEADS):
        qp = (qp_ref[0, h] * NSA_SCALE).astype(BF16)
        qpf = qp.astype(F32)
        b_self = bends_ref[h][:, 0:1]
        b_far = bends_ref[h][:, 1:2]
        near = bnear_ref[h]
        scores, slots = [], []
        for k in range(NSA_TOPK):
            if k == SELF_RANK:
                continue
            j = idx_ref[b, h * NSA_TOPK + k]
            bias = jnp.where(j == LANE - 1, near[:, half:], jnp.where(j == LANE - 2, near[:, :half], b_far))
            scores.append(_nt_dot(qp, kb[h * NSA_TOPK + k][0].astype(BF16)) + bias)
            slots.append(h * NSA_TOPK + k)
        s_self = jnp.sum(qpf * _bf(ks_new), axis=-1, keepdims=True) + b_self
        m = s_self
        for s in scores:
            m = jnp.maximum(m, jnp.max(s, axis=-1, keepdims=True))
        p_self = jnp.exp(s_self - m)
        l = p_self
        acc = _bf(p_self) * _bf(vs_new)
        for s, slot in zip(scores, slots):
            p = jnp.exp(s - m)
            l = l + jnp.sum(p, axis=-1, keepdims=True)
            acc = acc + jnp.dot(p.astype(BF16), vb[slot][0].astype(BF16), preferred_element_type=F32)
        o_s = acc / l
        s = jnp.where(wlane >= 1, _nt_dot(qp, kwin.astype(BF16)) + bwin_ref[h], NEG)
        s_self = jnp.sum(qpf * _bf(kw_new), axis=-1, keepdims=True) + b_self
        m = jnp.maximum(jnp.max(s, axis=-1, keepdims=True), s_self)
        p = jnp.exp(s - m)
        p_self = jnp.exp(s_self - m)
        l = jnp.sum(p, axis=-1, keepdims=True) + p_self
        o_w = (jnp.dot(p.astype(BF16), vwin.astype(BF16), preferred_element_type=F32)
               + _bf(p_self) * _bf(vw_new)) / l
        gates = jax.nn.sigmoid(gate_ref[0, h])
        o_h = gates[:, 0:1] * oc_ref[0, h] + gates[:, 1:2] * o_s + gates[:, 2:3] * o_w
        o_rot = pltpu.roll(o_h, NSA_HEAD_DIM, axis=1)
        low = lax.broadcasted_iota(jnp.int32, (1, LANE), 1) < NSA_HEAD_DIM
        for pr in range(NSA_GROUP // 2):
            a = (o_h if h == 0 else o_rot)[2 * pr:2 * pr + 1]
            b2 = (o_rot if h == 0 else o_h)[2 * pr + 1:2 * pr + 2]
            c0 = (h * NSA_GROUP + 2 * pr) * NSA_HEAD_DIM
            o_ref[0, :, c0:c0 + LANE] = jnp.where(low, a, b2)
    rowi = lax.broadcasted_iota(jnp.int32, (nbuf, LANE), 0)
    kwo_ref[0] = jnp.where(rowi == nbuf - 1, kw_new, pltpu.roll(kwin, nbuf - 1, axis=0))
    vwo_ref[0] = jnp.where(rowi == nbuf - 1, vw_new, pltpu.roll(vwin, nbuf - 1, axis=0))


def nsa_sample_attn(qp, kv_new, idx, o_c, gates_raw, k_pool, v_pool, e, k_win, v_win, page_table, table):
    db, n_pages = page_table.shape
    n_pool = k_pool.shape[1]
    p0 = n_pages * PAGE_SIZE
    nbuf = k_win.shape[1]
    nb = NSA_KV_HEADS * NSA_TOPK
    bpp = PAGE_SIZE // NSA_SEL_BLOCK
    assert nbuf == NSA_WINDOW and nbuf % LANE == 0 and p0 >= nbuf
    kp = k_pool.reshape(-1, NSA_SEL_BLOCK, NSA_KV_WIDTH)
    vp = v_pool.reshape(-1, NSA_SEL_BLOCK, NSA_KV_WIDTH)
    bnear = _bias_rows(table, 1, 0, 2 * NSA_SEL_BLOCK, 1)
    bwin = _bias_rows(table, nbuf // LANE, -LANE, nbuf, 1)
    tb = table.astype(F32).reshape(REL_BUCKETS, NSA_KV_HEADS, NSA_GROUP)
    bends = jnp.pad(jnp.stack([tb[0], tb[REL_BUCKETS - 1]], axis=-1), ((0, 0), (0, 0), (0, LANE - 2)))

    def blk_spec(slot):
        def imap(b, pt, ix):
            j = jnp.minimum(ix[b, slot], LANE - 1)
            return ((pt[b, j // bpp] + e * n_pool) * bpp + j % bpp, 0, 0)
        return pl.BlockSpec((1, NSA_SEL_BLOCK, NSA_KV_WIDTH), imap)

    per_b = lambda shape: pl.BlockSpec((1,) + shape, lambda b, pt, ix: (b,) + (0,) * len(shape))
    const = lambda shape: pl.BlockSpec(shape, lambda b, pt, ix: (0,) * len(shape))
    head_shape = (NSA_KV_HEADS, NSA_GROUP, LANE)
    grid_spec = pltpu.PrefetchScalarGridSpec(
        num_scalar_prefetch=2,
        grid=(db,),
        in_specs=[blk_spec(s) for s in range(nb)] * 2 + [
            per_b(head_shape), per_b((1, 6 * NSA_KV_WIDTH)), per_b((nbuf, NSA_KV_WIDTH)), per_b((nbuf, NSA_KV_WIDTH)),
            per_b(head_shape), per_b(head_shape), const(head_shape),
            const((NSA_KV_HEADS, NSA_GROUP, nbuf)), const(head_shape)],
        out_specs=[per_b((1, NSA_HEADS * NSA_HEAD_DIM)), per_b((nbuf, NSA_KV_WIDTH)), per_b((nbuf, NSA_KV_WIDTH))])
    return pl.pallas_call(
        functools.partial(_nsa_sample_attn_body, nbuf=nbuf),
        grid_spec=grid_spec,
        out_shape=[jax.ShapeDtypeStruct((db, 1, NSA_HEADS * NSA_HEAD_DIM), F32),
                   jax.ShapeDtypeStruct((db, nbuf, NSA_KV_WIDTH), F32),
                   jax.ShapeDtypeStruct((db, nbuf, NSA_KV_WIDTH), F32)],
        compiler_params=pltpu.CompilerParams(dimension_semantics=("arbitrary",)),
        name="nsa_sample_attn",
    )(page_table, idx, *([kp] * nb), *([vp] * nb), qp, kv_new.reshape(db, 1, -1),
      k_win.reshape(db, nbuf, NSA_KV_WIDTH), v_win.reshape(db, nbuf, NSA_KV_WIDTH), o_c, gates_raw,
      bnear, bwin, bends)


def nsa_sample_mixer(q, kv, small, e, k_cmp_pool, v_cmp_pool, k_slc_pool, v_slc_pool, k_win, v_win,
                     page_table, w_ck, w_cv, table):
    db = q.shape[0]
    qh = q.reshape(db, NSA_KV_HEADS, NSA_GROUP, NSA_HEAD_DIM)
    zeros = jnp.zeros_like(qh[:, 0])
    qp = jnp.stack([jnp.concatenate([qh[:, 0], zeros], axis=-1), jnp.concatenate([zeros, qh[:, 1]], axis=-1)], axis=1)
    o_c, idx = nsa_sample_cmp(qp, k_cmp_pool, v_cmp_pool, e, page_table, w_ck, w_cv, table)
    idx = idx[:, :NSA_KV_HEADS, :NSA_TOPK].reshape(db, NSA_KV_HEADS * NSA_TOPK)
    g = small[:, SMALL_GATE:SMALL_GATE + 3 * NSA_HEADS].reshape(db, 3, NSA_KV_HEADS, NSA_GROUP).transpose(0, 2, 3, 1)
    g = jnp.pad(g, ((0, 0), (0, 0), (0, 0), (0, LANE - 3)))
    y, kw_n, vw_n = nsa_sample_attn(qp, kv, idx, o_c, g, k_slc_pool, v_slc_pool, e, k_win, v_win, page_table, table)
    shape = (db, -1, NSA_KV_HEADS, NSA_HEAD_DIM)
    return y.reshape(db, NSA_HEADS * NSA_HEAD_DIM), kw_n.reshape(shape), vw_n.reshape(shape)


def _bmm_body(a_ref, b_ref, o_ref):
    o_ref[0] = jnp.dot(a_ref[0].astype(BF16), b_ref[0].astype(BF16), preferred_element_type=F32)


def batched_matmul(a, b):
    hh, m, k = a.shape
    n = b.shape[2]
    return pl.pallas_call(
        _bmm_body,
        grid=(hh,),
        in_specs=[pl.BlockSpec((1, m, k), lambda i: (i, 0, 0)), pl.BlockSpec((1, k, n), lambda i: (i, 0, 0))],
        out_specs=pl.BlockSpec((1, m, n), lambda i: (i, 0, 0)),
        out_shape=jax.ShapeDtypeStruct((hh, m, n), F32),
        compiler_params=pltpu.CompilerParams(dimension_semantics=("arbitrary",)),
        name="batched_matmul",
    )(a, b)


def _mla_sample_body(pt_ref, *refs):
    npg = PAGES_PER_STEP
    cp, rp = refs[:npg], refs[npg:2 * npg]
    qlat_ref, qrope_ref, cnew_ref, krnew_ref, o_ref, m_ref, l_ref, acc_ref = refs[2 * npg:]
    g = pl.program_id(1)

    @pl.when(g == 0)
    def _():
        m_ref[...] = jnp.full_like(m_ref, NEG)
        l_ref[...] = jnp.zeros_like(l_ref)
        acc_ref[...] = jnp.zeros_like(acc_ref)

    ql = qlat_ref[0].astype(BF16)
    qr = qrope_ref[0].astype(BF16)
    c_all = jnp.concatenate([r[0] for r in cp], axis=0).astype(BF16)
    r_all = jnp.concatenate([r[0] for r in rp], axis=0).astype(BF16)
    s = (_nt_dot(ql, c_all) + _nt_dot(qr, r_all)) * MLA_SCALE
    m_old = m_ref[...]
    m_new = jnp.maximum(m_old, jnp.max(s, axis=-1, keepdims=True))
    alpha = jnp.exp(m_old - m_new)
    p = jnp.exp(s - m_new)
    l_new = alpha * l_ref[...] + jnp.sum(p, axis=-1, keepdims=True)
    acc = alpha * acc_ref[...] + jnp.dot(p.astype(BF16), c_all, preferred_element_type=F32)
    m_ref[...] = m_new
    l_ref[...] = l_new
    acc_ref[...] = acc

    @pl.when(g == pl.num_programs(1) - 1)
    def _():
        cn = _bf(cnew_ref[0])
        kn = _bf(krnew_ref[0])
        s_self = (jnp.sum(ql.astype(F32) * cn, axis=-1, keepdims=True)
                  + jnp.sum(qr.astype(F32) * kn, axis=-1, keepdims=True)) * MLA_SCALE
        m2 = jnp.maximum(m_new, s_self)
        a = jnp.exp(m_new - m2)
        p_self = jnp.exp(s_self - m2)
        o_ref[0] = (a * acc + _bf(p_self) * cn) / (a * l_new + p_self)


def mla_sample_attention(q_lat, q_rope, c_new, kr_new, lat_pool, kr_pool, o, page_table):
    db, n_pages = page_table.shape
    n_pool = lat_pool.shape[1]
    npg = PAGES_PER_STEP
    assert n_pages % npg == 0
    cp = lat_pool.reshape(-1, PAGE_SIZE, MLA_KV_RANK)
    rp = kr_pool.reshape(-1, PAGE_SIZE, MLA_ROPE)

    def page_spec(k, width):
        return pl.BlockSpec((1, PAGE_SIZE, width), lambda b, g, pt: (pt[b, g * npg + k] + o * n_pool, 0, 0))

    per_b = lambda shape: pl.BlockSpec((1,) + shape, lambda b, g, pt: (b,) + (0,) * len(shape))
    grid_spec = pltpu.PrefetchScalarGridSpec(
        num_scalar_prefetch=1,
        grid=(db, n_pages // npg),
        in_specs=[page_spec(k, MLA_KV_RANK) for k in range(npg)] + [page_spec(k, MLA_ROPE) for k in range(npg)] + [
            per_b((MLA_HEADS, MLA_KV_RANK)), per_b((MLA_HEADS, MLA_ROPE)), per_b((1, MLA_KV_RANK)), per_b((1, MLA_ROPE))],
        out_specs=per_b((MLA_HEADS, MLA_KV_RANK)),
        scratch_shapes=[pltpu.VMEM((MLA_HEADS, 1), F32), pltpu.VMEM((MLA_HEADS, 1), F32),
                        pltpu.VMEM((MLA_HEADS, MLA_KV_RANK), F32)])
    return pl.pallas_call(
        _mla_sample_body,
        grid_spec=grid_spec,
        out_shape=jax.ShapeDtypeStruct((db, MLA_HEADS, MLA_KV_RANK), F32),
        compiler_params=pltpu.CompilerParams(dimension_semantics=("arbitrary", "arbitrary")),
        name="mla_sample",
    )(page_table, *([cp] * npg), *([rp] * npg), q_lat, q_rope, c_new.reshape(db, 1, -1), kr_new.reshape(db, 1, -1))


def mla_sample_mixer(q_nope, q_rope, c, k_rope, lat_pool, kr_pool, o_idx, page_table, w_kvb):
    db = c.shape[0]
    w = w_kvb.reshape(MLA_KV_RANK, MLA_HEADS, MLA_NOPE + MLA_V).transpose(1, 0, 2)
    q_lat = batched_matmul(q_nope.transpose(1, 0, 2), w[..., :MLA_NOPE].transpose(0, 2, 1)).transpose(1, 0, 2)
    o_lat = mla_sample_attention(q_lat, q_rope, c, k_rope, lat_pool, kr_pool, o_idx, page_table)
    out = batched_matmul(o_lat.transpose(1, 0, 2), w[..., MLA_NOPE:])
    return out.transpose(1, 0, 2).reshape(db, MLA_HEADS * MLA_V)


OD_GROUPS = (GLA_HEADS * GLA_DK, GLA_HEADS * GLA_DK, GLA_HEADS * GLA_DV, GLA_HEADS * GLA_DV, MLA_Q_RANK, MLA_KV_RANK, LANE)
MLA_PREP_ROWS = 128
MLA_TQ = 256


def odd_weight(w_in):
    gq, gk, gv, g1, r, qa, kva, kr = split_cols(w_in, OD_SPLITS)
    z = lambda n: jnp.zeros((w_in.shape[0], n), w_in.dtype)
    small = jnp.concatenate([g1, z(MLA_NOPE - GLA_GATE_RANK), kr, z(LANE - MLA_NOPE - MLA_ROPE)], axis=1)
    return jnp.concatenate([gq, gk, gv, r, qa, kva, small], axis=1)


def _head_blocks(w, widths, keep):
    k = w.shape[0]
    w = w.reshape(k, MLA_HEADS, sum(widths))[:, :, keep[0]:keep[1]]
    return jnp.pad(w, ((0, 0), (0, 0), (0, LANE - (keep[1] - keep[0])))).reshape(k, MLA_HEADS * LANE)


def rope_tables(pos):
    half = MLA_ROPE // 2
    freqs = ROPE_THETA ** (-jnp.arange(half, dtype=F32) / half)
    ang = pos.astype(F32)[:, None] * freqs
    cos, sin = jnp.cos(ang), jnp.sin(ang)
    n = pos.shape[0]
    z = lambda w: jnp.zeros((n, w), F32)
    tail = LANE - MLA_NOPE - MLA_ROPE
    c = jnp.concatenate([jnp.ones((n, MLA_NOPE), F32), cos, cos, z(tail)], axis=1)
    s1 = jnp.concatenate([z(MLA_NOPE), -sin, z(half), z(tail)], axis=1)
    s2 = jnp.concatenate([z(MLA_NOPE), z(half), sin, z(tail)], axis=1)
    return c, s1, s2


def _rope_block(x, c, s1, s2):
    half = MLA_ROPE // 2
    return x * c + pltpu.roll(x, LANE - half, axis=1) * s1 + pltpu.roll(x, half, axis=1) * s2


def _mla_prep_body(qa_ref, kva_ref, sm_ref, qg_ref, kg_ref, wq_ref, wk_ref, wv_ref, c_ref, s1_ref, s2_ref,
                   q_out, c_out, k_out, v_out, kr_out):
    norm = lambda x, g: x * lax.rsqrt(jnp.mean(x * x, axis=-1, keepdims=True) + EPS) * g
    c, s1, s2 = c_ref[...], s1_ref[...], s2_ref[...]
    lane = lax.broadcasted_iota(jnp.int32, c.shape, 1)
    q = jnp.dot(norm(qa_ref[...], qg_ref[...]).astype(BF16), wq_ref[...], preferred_element_type=F32)
    lat = norm(kva_ref[...], kg_ref[...])
    c_out[...] = lat
    latb = lat.astype(BF16)
    kn = jnp.dot(latb, wk_ref[...], preferred_element_type=F32)
    v_out[...] = jnp.dot(latb, wv_ref[...], preferred_element_type=F32).astype(BF16)
    kr = jnp.where(lane >= MLA_NOPE, _rope_block(sm_ref[...], c, s1, s2), 0.0)
    kr_out[...] = kr
    for h in range(MLA_HEADS):
        cols = slice(h * LANE, (h + 1) * LANE)
        q_out[:, cols] = _rope_block(q[:, cols], c, s1, s2)
        k_out[:, cols] = (kn[:, cols] + kr).astype(BF16)


def mla_prep(qa, kva, small, q_norm, w_qb, kv_norm, w_kvb, tables, pos_block):
    m = qa.shape[0]
    tm = MLA_PREP_ROWS
    wq = _head_blocks(w_qb, (MLA_NOPE, MLA_ROPE), (0, MLA_NOPE + MLA_ROPE)).astype(BF16)
    wk = _head_blocks(w_kvb, (MLA_NOPE, MLA_V), (0, MLA_NOPE)).astype(BF16)
    wv = _head_blocks(w_kvb, (MLA_NOPE, MLA_V), (MLA_NOPE, MLA_NOPE + MLA_V)).astype(BF16)
    wide = MLA_HEADS * LANE
    rows = lambda w: pl.BlockSpec((tm, w), lambda i: (i, 0))
    const = lambda a: pl.BlockSpec(a.shape, lambda i: (0, 0))
    tab = pl.BlockSpec((tm, LANE), lambda i: (pos_block(i), 0))
    qg = q_norm.reshape(1, -1).astype(F32)
    kg = kv_norm.reshape(1, -1).astype(F32)
    return pl.pallas_call(
        _mla_prep_body,
        grid=(m // tm,),
        in_specs=[rows(MLA_Q_RANK), rows(MLA_KV_RANK), rows(LANE), const(qg), const(kg), const(wq), const(wk), const(wv),
                  tab, tab, tab],
        out_specs=[rows(wide), rows(MLA_KV_RANK), rows(wide), rows(wide), rows(LANE)],
        out_shape=[jax.ShapeDtypeStruct((m, wide), F32), jax.ShapeDtypeStruct((m, MLA_KV_RANK), F32),
                   jax.ShapeDtypeStruct((m, wide), BF16), jax.ShapeDtypeStruct((m, wide), BF16),
                   jax.ShapeDtypeStruct((m, LANE), F32)],
        compiler_params=pltpu.CompilerParams(dimension_semantics=("arbitrary",), vmem_limit_bytes=VMEM_LIMIT_BYTES),
        name="mla_prep",
    )(qa, kva, small, qg, kg, wq, wk, wv, *tables)


def _mla_prompt_body(q_ref, k_ref, v_ref, o_ref):
    tq = q_ref.shape[0]
    qi = pl.program_id(1)
    row = lax.broadcasted_iota(jnp.int32, (tq, tq), 0)
    col = lax.broadcasted_iota(jnp.int32, (tq, tq), 1)
    low = lax.broadcasted_iota(jnp.int32, (tq, LANE), 1) < MLA_V
    outs = []
    for h in range(MLA_HEADS):
        cols = slice(h * LANE, (h + 1) * LANE)
        q = (q_ref[:, cols] * MLA_SCALE).astype(BF16)

        def body(kt, carry):
            m, l, acc = carry
            k0 = pl.multiple_of(kt * tq, tq)
            s = _nt_dot(q, k_ref[pl.ds(k0, tq), cols])
            s = jnp.where((kt - qi) * tq + col <= row, s, NEG)
            m_new = jnp.maximum(m, jnp.max(s, axis=-1, keepdims=True))
            alpha = jnp.exp(m - m_new)
            p = jnp.exp(s - m_new)
            l_new = alpha * l + jnp.sum(p, axis=-1, keepdims=True)
            pv = jnp.dot(p.astype(BF16), v_ref[pl.ds(k0, tq), cols], preferred_element_type=F32)
            return m_new, l_new, alpha * acc + pv

        init = (jnp.full((tq, 1), NEG, F32), jnp.zeros((tq, 1), F32), jnp.zeros((tq, LANE), F32))
        _, l, acc = lax.fori_loop(0, qi + 1, body, init)
        outs.append(acc / l)
    for pr in range(MLA_HEADS // 2):
        o_ref[:, pr * LANE:(pr + 1) * LANE] = jnp.where(low, outs[2 * pr], pltpu.roll(outs[2 * pr + 1], MLA_V, axis=1))


def mla_prompt_attention(q_rot, k_full, v_pad, b, l):
    tq = MLA_TQ
    nq = l // tq
    wide = MLA_HEADS * LANE
    return pl.pallas_call(
        _mla_prompt_body,
        grid=(b, nq),
        in_specs=[pl.BlockSpec((tq, wide), lambda bi, i: (bi * nq + i, 0)),
                  pl.BlockSpec((l, wide), lambda bi, i: (bi, 0)),
                  pl.BlockSpec((l, wide), lambda bi, i: (bi, 0))],
        out_specs=pl.BlockSpec((tq, MLA_HEADS * MLA_V), lambda bi, i: (bi * nq + i, 0)),
        out_shape=jax.ShapeDtypeStruct((b * l, MLA_HEADS * MLA_V), F32),
        compiler_params=pltpu.CompilerParams(dimension_semantics=("arbitrary", "arbitrary"),
                                             vmem_limit_bytes=VMEM_LIMIT_BYTES),
        name="mla_prompt",
    )(q_rot, k_full, v_pad)


def _ffn_body(x_ref, g_ref, wg_ref, wu_ref, wd_ref, fg_ref, o_ref, *, chunks, final):
    x = x_ref[...]
    h = (x * lax.rsqrt(jnp.mean(x * x, axis=-1, keepdims=True) + EPS) * g_ref[...]).astype(BF16)
    acc = x
    for c0, cw in chunks:
        a = jnp.dot(h, wg_ref[:, c0:c0 + cw], preferred_element_type=F32)
        u = jnp.dot(h, wu_ref[:, c0:c0 + cw], preferred_element_type=F32)
        act = (a * jax.nn.sigmoid(a) * u).astype(BF16)
        acc = acc + jnp.dot(act, wd_ref[c0:c0 + cw, :], preferred_element_type=F32)
    if final:
        acc = acc * lax.rsqrt(jnp.mean(acc * acc, axis=-1, keepdims=True) + EPS) * fg_ref[...]
    o_ref[...] = acc


def swiglu_ffn(x, g, w_gate, w_up, w_down, final_g=None):
    m, d = x.shape
    hdim = w_gate.shape[1]
    tm = _row_tile(m)
    final = final_g is not None
    fg = (final_g if final else g).reshape(1, d).astype(F32)
    wspec = lambda shape: pl.BlockSpec(shape, lambda i: (0, 0), pipeline_mode=pl.Buffered(1))
    return pl.pallas_call(
        functools.partial(_ffn_body, chunks=_col_chunks(hdim, 256), final=final),
        grid=(m // tm,),
        in_specs=[pl.BlockSpec((tm, d), lambda i: (i, 0)),
                  pl.BlockSpec((1, d), lambda i: (0, 0)),
                  wspec((d, hdim)), wspec((d, hdim)), wspec((hdim, d)),
                  pl.BlockSpec((1, d), lambda i: (0, 0))],
        out_specs=pl.BlockSpec((tm, d), lambda i: (i, 0)),
        out_shape=jax.ShapeDtypeStruct((m, d), F32),
        compiler_params=pltpu.CompilerParams(dimension_semantics=("arbitrary",),
                                             vmem_limit_bytes=VMEM_LIMIT_BYTES),
        name="swiglu_ffn",
    )(x, g.reshape(1, d).astype(F32), w_gate.astype(BF16), w_up.astype(BF16), w_down.astype(BF16), fg)


def split_cols(h, sizes):
    return jnp.split(h, [int(i) for i in np.cumsum(sizes)[:-1]], axis=-1)


def rmsnorm(x, g):
    xf = x.astype(F32)
    y = xf * lax.rsqrt(jnp.mean(xf * xf, axis=-1, keepdims=True) + EPS)
    return (y * g.astype(F32)).astype(x.dtype)


def rope(x, pos):
    half = x.shape[-1] // 2
    freqs = ROPE_THETA ** (-jnp.arange(half, dtype=F32) / half)
    ang = pos.astype(F32)[:, None] * freqs
    shape = (pos.shape[0],) + (1,) * (x.ndim - 3) + (half,)
    cos, sin = jnp.cos(ang).reshape(shape), jnp.sin(ang).reshape(shape)
    xf = x.astype(F32)
    x1, x2 = xf[..., :half], xf[..., half:]
    return jnp.concatenate([x1 * cos - x2 * sin, x2 * cos + x1 * sin], axis=-1).astype(x.dtype)


def t5_bucket(rel):
    exact = REL_BUCKETS // 2
    n = jnp.maximum(rel, 0)
    nf = jnp.maximum(n, exact).astype(F32)
    large = exact + (jnp.log(nf / exact) / math.log(REL_MAX_DIST / exact) * (REL_BUCKETS - exact)).astype(jnp.int32)
    return jnp.where(n < exact, n, jnp.minimum(large, REL_BUCKETS - 1))


def shared_bias(rel, table):
    lq, nk = rel.shape
    return table.astype(F32)[t5_bucket(rel)].reshape(lq, nk, NSA_KV_HEADS, NSA_GROUP).transpose(0, 2, 3, 1)


def masked_softmax(s, mask):
    return jax.nn.softmax(jnp.where(mask, s, NEG), axis=-1) * mask


def causal_conv(xbc, conv_state, w, b):
    full = jnp.concatenate([conv_state.astype(xbc.dtype), xbc], axis=1)
    ch = xbc.shape[-1]
    y = lax.conv_general_dilated(full, w[:, None, :].astype(xbc.dtype), (1,), 'VALID',
                                 dimension_numbers=('NWC', 'WIO', 'NWC'), feature_group_count=ch)
    return jax.nn.silu(y + b.astype(y.dtype)), full[:, -(SSD_CONV - 1):]


def segsum(x):
    t = x.shape[-1]
    cs = jnp.cumsum(x, axis=-1)
    diff = cs[..., :, None] - cs[..., None, :]
    return jnp.where(jnp.tril(jnp.ones((t, t), bool)), diff, -jnp.inf)


def ssd_scan(x, dt, a, bm, cm, s0, chunk):
    b, l, h, p = x.shape
    g, n = bm.shape[2], bm.shape[3]
    e = h // g
    nc = l // chunk
    xd = (x * dt[..., None]).reshape(b, nc, chunk, g, e, p)
    da = (dt * a).reshape(b, nc, chunk, g, e).transpose(0, 3, 4, 1, 2)
    bm = bm.reshape(b, nc, chunk, g, n)
    cm = cm.reshape(b, nc, chunk, g, n)
    a_cs = jnp.cumsum(da, axis=-1)
    lmat = jnp.exp(segsum(da))
    cb = jnp.einsum('bclgn,bcsgn->bcgls', cm, bm)
    y_diag = jnp.einsum('bcgls,bgecls,bcsgep->bclgep', cb, lmat, xd)
    decay_st = jnp.exp(a_cs[..., -1:] - a_cs)
    states = jnp.einsum('bclgn,bgecl,bclgep->bcgepn', bm, decay_st, xd)
    states = jnp.concatenate([s0.reshape(b, 1, g, e, p, n), states], axis=1)
    chunk_decay = jnp.exp(segsum(jnp.pad(a_cs[..., -1], ((0, 0), (0, 0), (0, 0), (1, 0)))))
    new_states = jnp.einsum('bgezc,bcgepn->bzgepn', chunk_decay, states)
    y_off = jnp.einsum('bclgn,bcgepn,bgecl->bclgep', cm, new_states[:, :-1], jnp.exp(a_cs))
    return (y_diag + y_off).reshape(b, l, h, p), new_states[:, -1].reshape(b, h, p, n)


def ssd_mixer(z, xbc, dt_raw, conv_state, ssm_state, conv_w, conv_b, dt_bias, a_log, d_skip, norm_g):
    b, l = z.shape[:2]
    xbc, new_conv = causal_conv(xbc, conv_state, conv_w, conv_b)
    xs, bm, cm = split_cols(xbc.astype(F32), (SSD_INNER, SSD_GROUPS * SSD_STATE, SSD_GROUPS * SSD_STATE))
    x = xs.reshape(b, l, SSD_HEADS, SSD_HEADDIM)
    bm = bm.reshape(b, l, SSD_GROUPS, SSD_STATE)
    cm = cm.reshape(b, l, SSD_GROUPS, SSD_STATE)
    dt = jax.nn.softplus(dt_raw.astype(F32) + dt_bias.astype(F32))
    a = -jnp.exp(a_log.astype(F32))
    chunk = SSD_CHUNK if l % SSD_CHUNK == 0 else l
    y, new_ssm = ssd_scan(x, dt, a, bm, cm, ssm_state.astype(F32), chunk)
    y = y + x * d_skip.astype(F32)[:, None]
    y = (y.reshape(b, l, SSD_INNER) * jax.nn.silu(z.astype(F32))).reshape(b, l, SSD_GROUPS, SSD_INNER // SSD_GROUPS)
    y = rmsnorm(y, norm_g.reshape(SSD_GROUPS, -1)).reshape(b, l, SSD_INNER)
    return y.astype(z.dtype), new_conv, new_ssm


def nsa_compress(k, w):
    b, t = k.shape[:2]
    nseg = t // NSA_CMP_STRIDE
    seg = k[:, :nseg * NSA_CMP_STRIDE].reshape(b, nseg, NSA_CMP_STRIDE, NSA_KV_HEADS, NSA_HEAD_DIM)
    first = jnp.einsum('bsrhd,rd->bshd', seg, w[:NSA_CMP_STRIDE])
    second = jnp.einsum('bsrhd,rd->bshd', seg, w[NSA_CMP_STRIDE:])
    return first[:, :-1] + second[:, 1:]


def sel_overlap(n_sel, n_cmp):
    j = jnp.arange(n_sel)[:, None]
    i = jnp.arange(n_cmp)[None, :]
    lo = jnp.maximum(j * NSA_SEL_BLOCK, i * NSA_CMP_STRIDE)
    hi = jnp.minimum(j * NSA_SEL_BLOCK + NSA_SEL_BLOCK, i * NSA_CMP_STRIDE + NSA_CMP_BLOCK)
    return (jnp.maximum(hi - lo, 0) / NSA_CMP_STRIDE).astype(F32)


def nsa_cmp_branch(q, qpos, kc, vc, table):
    nc = kc.shape[1]
    cend = jnp.arange(nc) * NSA_CMP_STRIDE + NSA_CMP_BLOCK - 1
    rel = qpos[:, None] - cend[None, :]
    s = jnp.einsum('bqhgd,bchd->bqhgc', q, kc).astype(F32) * NSA_SCALE + shared_bias(rel, table)
    p = masked_softmax(s, (rel >= 0)[:, None, None, :])
    return jnp.einsum('bqhgc,bchd->bqhgd', p, vc), p


def nsa_select(p_cmp, qpos, n_sel):
    imp = jnp.einsum('bqhgc,jc->bqhj', p_cmp, sel_overlap(n_sel, p_cmp.shape[-1]))
    j = jnp.arange(n_sel)[None, :]
    cur = (qpos // NSA_SEL_BLOCK)[:, None]
    valid = (j <= cur)[:, None, :]
    forced = ((j == 0) | (j == cur) | (j == cur - 1))[:, None, :]
    score = jnp.where(valid, jnp.where(forced, NSA_FORCE, imp), -1.0)
    if n_sel < NSA_TOPK:
        score = jnp.pad(score, ((0, 0), (0, 0), (0, 0), (0, NSA_TOPK - n_sel)), constant_values=-1.0)
    vals, idx = lax.top_k(score, NSA_TOPK)
    kpos = (idx[..., None] * NSA_SEL_BLOCK + jnp.arange(NSA_SEL_BLOCK)).reshape(*idx.shape[:-1], NSA_TOPK * NSA_SEL_BLOCK)
    kok = jnp.repeat(vals >= 0, NSA_SEL_BLOCK, axis=-1)
    return kpos, kok


def nsa_slc_branch(q, qpos, ks, vs, kpos, kok, table):
    rel = qpos[None, :, None, None] - kpos
    tb = table.astype(F32).reshape(REL_BUCKETS, NSA_KV_HEADS, NSA_GROUP)
    bias = jnp.moveaxis(tb[t5_bucket(rel), jnp.arange(NSA_KV_HEADS)[:, None]], -1, -2)
    s = jnp.einsum('bqhgd,bqhkd->bqhgk', q, ks).astype(F32) * NSA_SCALE + bias
    p = masked_softmax(s, (kok & (rel >= 0))[..., None, :])
    return jnp.einsum('bqhgk,bqhkd->bqhgd', p, vs)


def nsa_win_branch(q, qpos, kw, vw, wpos, table):
    rel = qpos[:, None] - wpos[None, :]
    mask = (rel >= 0) & (rel < NSA_WINDOW) & (wpos >= 0)[None, :]
    s = jnp.einsum('bqhgd,bkhd->bqhgk', q, kw).astype(F32) * NSA_SCALE + shared_bias(rel, table)
    p = masked_softmax(s, mask[:, None, None, :])
    return jnp.einsum('bqhgk,bkhd->bqhgd', p, vw)


def nsa_gate(g, o_c, o_s, o_w):
    return g[..., 0:1] * o_c + g[..., 1:2] * o_s + g[..., 2:3] * o_w


def gather_rows(pool, e, new, page_table, kpos):
    db, s = new.shape[:2]
    p0 = page_table.shape[1] * PAGE_SIZE
    bi = jnp.arange(db)[:, None, None, None]
    hi = jnp.arange(NSA_KV_HEADS)[None, None, :, None]
    pp = jnp.minimum(kpos, p0 - 1)
    past = pool[e, page_table[bi, pp // PAGE_SIZE], pp % PAGE_SIZE, hi]
    cur = new[bi, jnp.clip(kpos - p0, 0, s - 1), hi]
    return jnp.where((kpos < p0)[..., None], past, cur)


def nsa_sample(q, kvs, gates, e, k_cmp_pool, v_cmp_pool, k_slc_pool, v_slc_pool, k_win_buf, v_win_buf,
               page_table, w_ck, w_cv, table):
    kc_new, vc_new, ks_new, vs_new, kw_new, vw_new = kvs
    db, s = q.shape[:2]
    p0 = page_table.shape[1] * PAGE_SIZE
    qpos = p0 + jnp.arange(s)
    past = lambda pool: pool[e, page_table].reshape(db, p0, NSA_KV_HEADS, NSA_HEAD_DIM)
    kc = nsa_compress(jnp.concatenate([past(k_cmp_pool), kc_new], axis=1), w_ck)
    vc = nsa_compress(jnp.concatenate([past(v_cmp_pool), vc_new], axis=1), w_cv)
    o_c, p_c = nsa_cmp_branch(q, qpos, kc, vc, table)
    kpos, kok = nsa_select(p_c, qpos, -(-(p0 + s) // NSA_SEL_BLOCK))
    ksel = gather_rows(k_slc_pool, e, ks_new, page_table, kpos)
    vsel = gather_rows(v_slc_pool, e, vs_new, page_table, kpos)
    o_s = nsa_slc_branch(q, qpos, ksel, vsel, kpos, kok, table)
    nbuf = k_win_buf.shape[1]
    kw = jnp.concatenate([k_win_buf.astype(kw_new.dtype), kw_new], axis=1)
    vw = jnp.concatenate([v_win_buf.astype(vw_new.dtype), vw_new], axis=1)
    wpos = p0 - nbuf + jnp.arange(nbuf + s)
    o_w = nsa_win_branch(q, qpos, kw, vw, wpos, table)
    out = nsa_gate(gates, o_c, o_s, o_w).reshape(db, s, NSA_HEADS * NSA_HEAD_DIM)
    return out, kw[:, -nbuf:], vw[:, -nbuf:]


def even_split(parts, r0, b, l):
    z, xbc, q, kv, small = (t[r0:r0 + b * l].reshape(b, l, -1) for t in parts)
    dtr = small[..., SMALL_DT:SMALL_DT + SSD_HEADS]
    g = small[..., SMALL_GATE:SMALL_GATE + 3 * NSA_HEADS]
    q = q.reshape(b, l, NSA_KV_HEADS, NSA_GROUP, NSA_HEAD_DIM)
    kvs = tuple(t.reshape(b, l, NSA_KV_HEADS, NSA_HEAD_DIM) for t in split_cols(kv, (NSA_KV_WIDTH,) * 6))
    g = jax.nn.sigmoid(g.astype(F32)).reshape(b, l, 3, NSA_KV_HEADS, NSA_GROUP).transpose(0, 1, 3, 4, 2)
    return z, xbc, dtr, q, kvs, g


def gla_scan(q, k, v, log_a, s0, chunk):
    b, l, h, dk = q.shape
    dv = v.shape[-1]
    nc = l // chunk
    q, k, v, log_a = [t.reshape(b, nc, chunk, h, t.shape[-1]) for t in (q, k, v, log_a)]
    bc = jnp.cumsum(log_a, axis=2)
    blast = bc[:, :, -1:]
    qt = q * jnp.exp(bc)
    kt = k * jnp.exp(-bc)
    kd = k * jnp.exp(blast - bc)
    causal = jnp.tril(jnp.ones((chunk, chunk), bool))
    att = jnp.where(causal, jnp.einsum('bclhk,bcshk->bchls', qt, kt), 0.0)
    o_intra = jnp.einsum('bchls,bcshv->bclhv', att, v)

    def step(st, inp):
        qt_c, kd_c, v_c, dec_c = inp
        o_c = jnp.einsum('blhk,bhkv->blhv', qt_c, st)
        st = jnp.exp(dec_c)[..., None] * st + jnp.einsum('blhk,blhv->bhkv', kd_c, v_c)
        return st, o_c

    xs = tuple(jnp.swapaxes(t, 0, 1) for t in (qt, kd, v, blast[:, :, 0]))
    s_fin, o_inter = lax.scan(step, s0, xs)
    return (o_intra + jnp.swapaxes(o_inter, 0, 1)).reshape(b, l, h, dv), s_fin


def gla_mixer(q, k, v, g1, r, state, w_g2, b_g, norm_g):
    b, l = q.shape[:2]
    q = q.astype(F32).reshape(b, l, GLA_HEADS, GLA_DK) * GLA_DK ** -0.5
    k = k.astype(F32).reshape(b, l, GLA_HEADS, GLA_DK)
    v = v.astype(F32).reshape(b, l, GLA_HEADS, GLA_DV)
    log_a = jax.nn.log_sigmoid(g1.astype(F32) @ w_g2.astype(F32) + b_g.astype(F32)).reshape(b, l, GLA_HEADS, GLA_DK) / GLA_TAU
    chunk = GLA_CHUNK if l % GLA_CHUNK == 0 else l
    o, new_state = gla_scan(q, k, v, log_a, state.astype(F32), chunk)
    o = rmsnorm(o, norm_g).reshape(b, l, GLA_HEADS * GLA_DV) * jax.nn.silu(r.astype(F32))
    return o.astype(r.dtype), new_state


def odd_split(h, b, l, pos, q_norm, w_qb, kv_norm):
    gq, gk, gv, g1, r, qa, kva, kr = split_cols(h.reshape(b, l, -1), OD_SPLITS)
    q = fused_matmul(qa.reshape(b * l, -1), w_qb, norm_g=q_norm).reshape(b, l, MLA_HEADS, MLA_NOPE + MLA_ROPE)
    q_nope, q_rope = q[..., :MLA_NOPE], rope(q[..., MLA_NOPE:], pos)
    return (gq, gk, gv, g1, r), (q_nope, q_rope, rmsnorm(kva, kv_norm), rope(kr, pos))


def mla_prompt(q_nope, q_rope, c, k_rope, w_kvb):
    b, l = c.shape[:2]
    kv = fused_matmul(c.reshape(b * l, -1), w_kvb).reshape(b, l, MLA_HEADS, MLA_NOPE + MLA_V)
    k_nope, v = kv[..., :MLA_NOPE], kv[..., MLA_NOPE:]
    nqb = l // MLA_QBLOCK
    kpos = jnp.arange(l)

    def block(args):
        i, qn, qr = args
        qpos = i * MLA_QBLOCK + jnp.arange(MLA_QBLOCK)
        s = (jnp.einsum('bqhd,bkhd->bhqk', qn, k_nope) + jnp.einsum('bqhr,bkr->bhqk', qr, k_rope)).astype(F32) * MLA_SCALE
        p = jax.nn.softmax(jnp.where(kpos[None, :] <= qpos[:, None], s, NEG), axis=-1)
        return jnp.einsum('bhqk,bkhd->bqhd', p, v)

    blk = lambda t: t.reshape(b, nqb, MLA_QBLOCK, *t.shape[2:]).swapaxes(0, 1)
    o = lax.map(block, (jnp.arange(nqb), blk(q_nope), blk(q_rope)))
    return o.swapaxes(0, 1).reshape(b, l, MLA_HEADS * MLA_V)


def mla_sample(q_nope, q_rope, c, k_rope, lat_pool, kr_pool, o_idx, page_table, w_kvb):
    db, s = c.shape[:2]
    w = w_kvb.reshape(MLA_KV_RANK, MLA_HEADS, MLA_NOPE + MLA_V)
    q_lat = jnp.einsum('bshd,chd->bshc', q_nope, w[..., :MLA_NOPE])
    c_past = lat_pool[o_idx, page_table].reshape(db, -1, MLA_KV_RANK)
    kr_past = kr_pool[o_idx, page_table].reshape(db, -1, MLA_ROPE)
    s_past = (jnp.einsum('bshc,btc->bsht', q_lat, c_past) + jnp.einsum('bshr,btr->bsht', q_rope, kr_past)).astype(F32) * MLA_SCALE
    s_new = (jnp.einsum('bshc,btc->bsht', q_lat, c) + jnp.einsum('bshr,btr->bsht', q_rope, k_rope)).astype(F32) * MLA_SCALE
    causal = jnp.arange(s)[None, :] <= jnp.arange(s)[:, None]
    s_new = jnp.where(causal[:, None, :], s_new, NEG)
    p = jax.nn.softmax(jnp.concatenate([s_past, s_new], axis=-1), axis=-1)
    n_past = c_past.shape[1]
    o_lat = jnp.einsum('bsht,btc->bshc', p[..., :n_past], c_past) + jnp.einsum('bsht,btc->bshc', p[..., n_past:], c)
    return jnp.einsum('bshc,chd->bshd', o_lat, w[..., MLA_NOPE:]).reshape(db, s, MLA_HEADS * MLA_V)


def kernel(x_prompt, x_sample, state_ssm, state_conv, cache_nsa_k_cmp, cache_nsa_v_cmp, cache_nsa_k_slc, cache_nsa_v_slc, cache_nsa_k_win, cache_nsa_v_win, state_gla, cache_mla_latent, cache_mla_krope, page_table, rel_bias, ev_norm, ev_w_in, ssd_conv_w, ssd_conv_b, ssd_dt_bias, ssd_a_log, ssd_d, ssd_norm, nsa_w_cmp_k, nsa_w_cmp_v, ev_w_out, od_norm, od_w_in, gla_w_gate2, gla_b_gate, gla_norm, mla_q_norm, mla_w_qb, mla_kv_norm, mla_w_kvb, od_w_out, ffn_norm, ffn_w_gate, ffn_w_up, ffn_w_down, final_norm):
    b, l = x_prompt.shape[:2]
    db, s = x_sample.shape[:2]
    depth = ffn_norm.shape[0]
    p0 = page_table.shape[1] * PAGE_SIZE
    pos_p = jnp.arange(l)
    pos_s = p0 + jnp.arange(s)
    names = ('ssm', 'conv', 'k_cmp', 'v_cmp', 'k_slc', 'v_slc', 'k_win', 'v_win', 'gla', 'lat', 'krope')
    newp = {n: [] for n in names}
    news = {n: [] for n in names}
    np_rows = b * l
    x = jnp.concatenate([x_prompt.reshape(np_rows, D_MODEL), x_sample.reshape(db * s, D_MODEL)], axis=0)
    for li in range(depth):
        if li % 2 == 0:
            e = li // 2
            ssd_w = (ssd_conv_w[e], ssd_conv_b[e], ssd_dt_bias[e], ssd_a_log[e], ssd_d[e], ssd_norm[e])
            parts = fused_matmul(x, even_weight(ev_w_in[e]), norm_g=ev_norm[e], groups=EV_GROUPS)
            z, xbc, dtr, q, kvs, g = even_split(parts, 0, b, l)
            y_ssd, conv_n, ssm_n = ssd_mixer(z, xbc, dtr, jnp.zeros((b, SSD_CONV - 1, SSD_CONV_CH), F32),
                                             jnp.zeros((b, SSD_HEADS, SSD_HEADDIM, SSD_STATE), F32), *ssd_w)
            y_nsa = nsa_prompt_attention(parts[2], parts[3], parts[4], nsa_w_cmp_k[e], nsa_w_cmp_v[e], rel_bias, b, l)
            mix_p = jnp.concatenate([y_ssd.reshape(np_rows, -1), y_nsa], axis=-1)
            nw = min(NSA_WINDOW, l)
            for n, t in zip(names[:8], (ssm_n, conv_n, kvs[0], kvs[1], kvs[2], kvs[3],
                                        kvs[4][:, -nw:], kvs[5][:, -nw:])):
                newp[n].append(t)
            z, xbc, dtr, q, kvs, g = even_split(parts, np_rows, db, s)
            y_ssd, conv_n, ssm_n = ssd_mixer(z, xbc, dtr, state_conv[e], state_ssm[e], *ssd_w)
            y_nsa, kw_n, vw_n = nsa_sample_mixer(parts[2][np_rows:], parts[3][np_rows:], parts[4][np_rows:], e,
                                                 cache_nsa_k_cmp, cache_nsa_v_cmp, cache_nsa_k_slc, cache_nsa_v_slc,
                                                 cache_nsa_k_win[e], cache_nsa_v_win[e], page_table,
                                                 nsa_w_cmp_k[e], nsa_w_cmp_v[e], rel_bias)
            mix_s = jnp.concatenate([y_ssd.reshape(db * s, -1), y_nsa], axis=-1)
            for n, t in zip(names[:8], (ssm_n, conv_n, kvs[0], kvs[1], kvs[2], kvs[3], kw_n, vw_n)):
                news[n].append(t)
            x = fused_matmul(jnp.concatenate([mix_p, mix_s], axis=0), ev_w_out[e], residual=x)
        else:
            o = li // 2
            gla_w = (gla_w_gate2[o], gla_b_gate[o], gla_norm[o])
            gq, gk, gv, r, qa, kva, small = fused_matmul(x, odd_weight(od_w_in[o]), norm_g=od_norm[o],
                                                         groups=OD_GROUPS)
            assert db * s == MLA_PREP_ROWS and l % MLA_PREP_ROWS == 0
            nblk = l // MLA_PREP_ROWS
            tables = tuple(jnp.concatenate(t, axis=0)
                           for t in zip(rope_tables(pos_p), rope_tables(jnp.tile(pos_s, db))))
            q_rot, lat, k_full, v_pad, kr_rot = mla_prep(
                qa, kva, small, mla_q_norm[o], mla_w_qb[o], mla_kv_norm[o], mla_w_kvb[o], tables,
                lambda i: jnp.where(i < b * nblk, i % nblk, nblk))
            kr_rot = kr_rot[:, MLA_NOPE:MLA_NOPE + MLA_ROPE]
            g1 = small[:, :GLA_GATE_RANK]
            gla_rows = lambda r0, nb, nl: tuple(t[r0:r0 + nb * nl].reshape(nb, nl, -1) for t in (gq, gk, gv, g1, r))
            y_gla, gla_n = gla_mixer(*gla_rows(0, b, l), jnp.zeros((b, GLA_HEADS, GLA_DK, GLA_DV), F32), *gla_w)
            y_mla = mla_prompt_attention(q_rot, k_full, v_pad, b, l)
            mix_p = jnp.concatenate([y_gla.reshape(np_rows, -1), y_mla], axis=-1)
            newp['gla'].append(gla_n)
            newp['lat'].append(lat[:np_rows].reshape(b, l, -1))
            newp['krope'].append(kr_rot[:np_rows].reshape(b, l, -1))
            y_gla, gla_n = gla_mixer(*gla_rows(np_rows, db, s), state_gla[o], *gla_w)
            q_s = q_rot[np_rows:].reshape(db * s, MLA_HEADS, LANE)
            y_mla = mla_sample_mixer(q_s[..., :MLA_NOPE], q_s[..., MLA_NOPE:MLA_NOPE + MLA_ROPE], lat[np_rows:],
                                     kr_rot[np_rows:], cache_mla_latent, cache_mla_krope, o, page_table, mla_w_kvb[o])
            mix_s = jnp.concatenate([y_gla.reshape(db * s, -1), y_mla], axis=-1)
            news['gla'].append(gla_n)
            news['lat'].append(lat[np_rows:].reshape(db, s, -1))
            news['krope'].append(kr_rot[np_rows:].reshape(db, s, -1))
            x = fused_matmul(jnp.concatenate([mix_p, mix_s], axis=0), od_w_out[o], residual=x)
        x = swiglu_ffn(x, ffn_norm[li], ffn_w_gate[li], ffn_w_up[li], ffn_w_down[li],
                       final_g=final_norm if li == depth - 1 else None)
    y_prompt = x[:np_rows].reshape(b, l, D_MODEL)
    y_sample = x[np_rows:].reshape(db, s, D_MODEL)
    st = lambda d, n: jnp.stack(d[n])
    return (y_prompt, y_sample,
            st(newp, 'ssm'), st(news, 'ssm'), st(newp, 'conv'), st(news, 'conv'),
            st(newp, 'k_cmp'), st(news, 'k_cmp'), st(newp, 'v_cmp'), st(news, 'v_cmp'),
            st(newp, 'k_slc'), st(news, 'k_slc'), st(newp, 'v_slc'), st(news, 'v_slc'),
            st(newp, 'k_win'), st(news, 'k_win'), st(newp, 'v_win'), st(news, 'v_win'),
            st(newp, 'gla'), st(news, 'gla'), st(newp, 'lat'), st(news, 'lat'),
            st(newp, 'krope'), st(news, 'krope'))
```

```python
import functools
import math

import jax
import jax.numpy as jnp
import numpy as np
from jax import lax
from jax.experimental import pallas as pl
from jax.experimental.pallas import tpu as pltpu

F32 = jnp.float32
BF16 = jnp.bfloat16
EPS = 1e-6
NEG = -1e30

D_MODEL = 1024
PAGE_SIZE = 128

SSD_HEADS = 16
SSD_HEADDIM = 64
SSD_INNER = SSD_HEADS * SSD_HEADDIM
SSD_GROUPS = 2
SSD_STATE = 128
SSD_CONV = 4
SSD_CHUNK = 128
SSD_CONV_CH = SSD_INNER + 2 * SSD_GROUPS * SSD_STATE

NSA_HEADS = 16
NSA_KV_HEADS = 2
NSA_GROUP = NSA_HEADS // NSA_KV_HEADS
NSA_HEAD_DIM = 64
NSA_KV_WIDTH = NSA_KV_HEADS * NSA_HEAD_DIM
NSA_CMP_BLOCK = 32
NSA_CMP_STRIDE = 16
NSA_SEL_BLOCK = 64
NSA_TOPK = 16
NSA_WINDOW = 512
NSA_QBLOCK = 128
NSA_SCALE = NSA_HEAD_DIM ** -0.5
NSA_FORCE = 1e4

REL_BUCKETS = 32
REL_MAX_DIST = 128

GLA_HEADS = 4
GLA_DK = 128
GLA_DV = 256
GLA_GATE_RANK = 16
GLA_TAU = 16.0
GLA_CHUNK = 16

MLA_HEADS = 8
MLA_Q_RANK = 384
MLA_KV_RANK = 256
MLA_NOPE = 64
MLA_ROPE = 32
MLA_V = 64
MLA_QBLOCK = 128
MLA_SCALE = (MLA_NOPE + MLA_ROPE) ** -0.5
ROPE_THETA = 10000.0

EV_SPLITS = (SSD_INNER, SSD_CONV_CH, SSD_HEADS, NSA_HEADS * NSA_HEAD_DIM, 6 * NSA_KV_WIDTH, 3 * NSA_HEADS)
OD_SPLITS = (GLA_HEADS * GLA_DK, GLA_HEADS * GLA_DK, GLA_HEADS * GLA_DV, GLA_GATE_RANK, GLA_HEADS * GLA_DV,
             MLA_Q_RANK, MLA_KV_RANK, MLA_ROPE)

VMEM_LIMIT_BYTES = 56 * 1024 * 1024
LANE = 128


def _row_tile(m):
    for t in (512, 384, 256, 128):
        if m % t == 0:
            return t
    return m


def _col_chunks(n, width=512):
    out, c = [], 0
    while c < n:
        w = min(width, n - c)
        out.append((c, w))
        c += w
    return out


def _mm_body(*refs, norm, residual, groups):
    it = iter(refs)
    x_ref = next(it)
    g_ref = next(it) if norm else None
    w_ref = next(it)
    r_ref = next(it) if residual else None
    o_refs = list(it)
    x = x_ref[...]
    if norm:
        x = x * lax.rsqrt(jnp.mean(x * x, axis=-1, keepdims=True) + EPS) * g_ref[...]
    xb = x.astype(BF16)
    off = 0
    for o_ref, gw in zip(o_refs, groups):
        for c0, cw in _col_chunks(gw):
            acc = jnp.dot(xb, w_ref[:, off + c0:off + c0 + cw], preferred_element_type=F32)
            if residual:
                acc = acc + r_ref[:, off + c0:off + c0 + cw]
            o_ref[:, c0:c0 + cw] = acc
        off += gw


def fused_matmul(x, w, norm_g=None, residual=None, groups=None):
    m, k = x.shape
    n = w.shape[1]
    single = groups is None
    groups = (n,) if single else tuple(groups)
    assert sum(groups) == n
    tm = _row_tile(m)
    norm = norm_g is not None
    res = residual is not None
    args = [x]
    specs = [pl.BlockSpec((tm, k), lambda i: (i, 0))]
    if norm:
        args.append(norm_g.reshape(1, k).astype(F32))
        specs.append(pl.BlockSpec((1, k), lambda i: (0, 0)))
    args.append(w.astype(BF16))
    specs.append(pl.BlockSpec((k, n), lambda i: (0, 0)))
    if res:
        args.append(residual)
        specs.append(pl.BlockSpec((tm, n), lambda i: (i, 0)))
    outs = pl.pallas_call(
        functools.partial(_mm_body, norm=norm, residual=res, groups=groups),
        grid=(m // tm,),
        in_specs=specs,
        out_specs=[pl.BlockSpec((tm, gw), lambda i: (i, 0)) for gw in groups],
        out_shape=[jax.ShapeDtypeStruct((m, gw), F32) for gw in groups],
        compiler_params=pltpu.CompilerParams(dimension_semantics=("arbitrary",),
                                             vmem_limit_bytes=VMEM_LIMIT_BYTES),
        name="fused_matmul",
    )(*args)
    return outs[0] if single else outs


EV_GROUPS = (SSD_INNER, SSD_CONV_CH, NSA_HEADS * NSA_HEAD_DIM, 6 * NSA_KV_WIDTH, LANE)
SMALL_DT = 0
SMALL_GATE = SSD_HEADS
QB = NSA_QBLOCK
ROWS = NSA_GROUP * QB


def even_weight(w_in):
    z, xbc, dtr, q, kvs, g = split_cols(w_in, EV_SPLITS)
    pad = jnp.zeros((w_in.shape[0], LANE - SSD_HEADS - 3 * NSA_HEADS), w_in.dtype)
    return jnp.concatenate([z, xbc, q, kvs, dtr, g, pad], axis=1)


def _bucket_tile(rel):
    exact = REL_BUCKETS // 2
    n = jnp.maximum(rel, 0)
    nf = jnp.maximum(n, exact).astype(F32)
    large = exact + (jnp.log(nf / exact) / math.log(REL_MAX_DIST / exact) * (REL_BUCKETS - exact)).astype(jnp.int32)
    return jnp.where(n < exact, n, jnp.minimum(large, REL_BUCKETS - 1))


def _bias_tiles_body(table_ref, o_ref, *, base_step, base_off, kmul, qmul):
    t = pl.program_id(0)
    shape = o_ref.shape[2:]
    qi = lax.broadcasted_iota(jnp.int32, shape, 0)
    ki = lax.broadcasted_iota(jnp.int32, shape, 1)
    bucket = _bucket_tile(t * base_step + base_off + qmul * qi - kmul * ki)
    for hd in range(NSA_HEADS):
        acc = jnp.zeros(shape, F32)
        for bk in range(REL_BUCKETS):
            acc = jnp.where(bucket == bk, table_ref[bk, hd], acc)
        o_ref[0, hd] = acc


def bias_tiles(table, n_tiles, base_step, base_off, kmul, qmul=1, rows=QB):
    return pl.pallas_call(
        functools.partial(_bias_tiles_body, base_step=base_step, base_off=base_off, kmul=kmul, qmul=qmul),
        grid=(n_tiles,),
        in_specs=[pl.BlockSpec(memory_space=pltpu.SMEM)],
        out_specs=pl.BlockSpec((1, NSA_HEADS, rows, LANE), lambda t: (t, 0, 0, 0)),
        out_shape=jax.ShapeDtypeStruct((n_tiles, NSA_HEADS, rows, LANE), F32),
        compiler_params=pltpu.CompilerParams(dimension_semantics=("arbitrary",)),
        name="t5_bias_tiles",
    )(table.astype(F32))


def _compress_body(kv_ref, w_ref, o_ref):
    nseg = o_ref.shape[1]
    first = jnp.zeros(o_ref.shape[1:], F32)
    second = jnp.zeros(o_ref.shape[1:], F32)
    for r in range(NSA_CMP_STRIDE):
        rows = kv_ref[pl.ds(r, nseg, stride=NSA_CMP_STRIDE), :]
        first = first + rows * w_ref[r:r + 1, :]
        second = second + rows * w_ref[NSA_CMP_STRIDE + r:NSA_CMP_STRIDE + r + 1, :]
    out = first + pltpu.roll(second, nseg - 1, axis=0)
    row = lax.broadcasted_iota(jnp.int32, out.shape, 0)
    o_ref[0] = jnp.where(row < nseg - 1, out, 0.0)


def nsa_compress_prompt(kv, w_ck, w_cv, b, l):
    nseg = l // NSA_CMP_STRIDE
    w = jnp.concatenate([w_ck, w_ck, w_cv, w_cv], axis=1).astype(F32)
    return pl.pallas_call(
        _compress_body,
        grid=(b, 2),
        in_specs=[pl.BlockSpec((l, NSA_KV_WIDTH), lambda i, j: (i, j)),
                  pl.BlockSpec((NSA_CMP_BLOCK, NSA_KV_WIDTH), lambda i, j: (0, j))],
        out_specs=pl.BlockSpec((1, nseg, NSA_KV_WIDTH), lambda i, j: (i, 0, j)),
        out_shape=jax.ShapeDtypeStruct((b, nseg, 2 * NSA_KV_WIDTH), F32),
        compiler_params=pltpu.CompilerParams(dimension_semantics=("arbitrary", "arbitrary")),
        name="nsa_compress",
    )(kv, w)


def _nt_dot(a, b):
    return lax.dot_general(a, b, (((1,), (1,)), ((), ())), preferred_element_type=F32)


def _importance(p_group_sum, ovt):
    return jnp.dot(p_group_sum.astype(BF16), ovt, preferred_element_type=F32)


def _nsa_prompt_body(q_ref, kv_ref, sm_ref, cmp_ref, cmpb_ref, toeb_ref, ovt_ref, exp_ref, o_ref,
                     selm_ref, acc_ref, *, n_kt):
    i = pl.program_id(1)
    row = lax.broadcasted_iota(jnp.int32, (QB, LANE), 0)
    lane = lax.broadcasted_iota(jnp.int32, (QB, LANE), 1)
    low = lane < NSA_HEAD_DIM
    gates = jax.nn.sigmoid(sm_ref[...])
    qpos = i * QB + row

    def attend(qp, h, kcol, vcol, n_tiles, tile_of, use_sel, window):
        acc_ref[...] = jnp.zeros_like(acc_ref)

        def body(t, carry):
            m, l = carry
            kt = tile_of(t)
            d = i - kt
            k0 = pl.multiple_of(kt * QB, QB)
            k_t = kv_ref[pl.ds(k0, QB), kcol:kcol + LANE].astype(BF16)
            v_t = kv_ref[pl.ds(k0, QB), vcol:vcol + LANE].astype(BF16)
            s = _nt_dot(qp, k_t).reshape(NSA_GROUP, QB, LANE)
            s = s + toeb_ref[jnp.minimum(d, 2), pl.ds(h * NSA_GROUP, NSA_GROUP)]
            rel = d * QB + row - lane
            msk = rel >= 0
            if window:
                msk = msk & (rel < NSA_WINDOW)
            if use_sel:
                msk = msk & (selm_ref[kt] > 0.5)
            s = jnp.where(msk[None], s, NEG)
            m_new = jnp.maximum(m, jnp.max(s, axis=-1, keepdims=True))
            alpha = jnp.exp(m - m_new)
            p = jnp.exp(s - m_new)
            l_new = alpha * l + jnp.sum(p, axis=-1, keepdims=True)
            pv = jnp.dot(p.reshape(ROWS, LANE).astype(BF16), v_t, preferred_element_type=F32)
            acc_ref[...] = acc_ref[...] * alpha.reshape(ROWS, 1) + pv
            return m_new, l_new

        m0 = jnp.full((NSA_GROUP, QB, 1), NEG, F32)
        l0 = jnp.zeros((NSA_GROUP, QB, 1), F32)
        _, l = lax.fori_loop(0, n_tiles, body, (m0, l0))
        return acc_ref[...] / l.reshape(ROWS, 1)

    for h in range(NSA_KV_HEADS):
        in_half = low if h == 0 else jnp.logical_not(low)
        parts = []
        for g in range(NSA_GROUP):
            hd = h * NSA_GROUP + g
            t = q_ref[:, (hd // 2) * LANE:(hd // 2) * LANE + LANE]
            if hd % 2 != h:
                t = pltpu.roll(t, NSA_HEAD_DIM, axis=1)
            parts.append((jnp.where(in_half, t, 0.0) * NSA_SCALE).astype(BF16))
        qp = jnp.concatenate(parts, axis=0)

        kc = cmp_ref[0, :, 0:LANE].astype(BF16)
        vc = cmp_ref[0, :, LANE:2 * LANE].astype(BF16)
        s = _nt_dot(qp, kc).reshape(NSA_GROUP, QB, LANE) + cmpb_ref[0, pl.ds(h * NSA_GROUP, NSA_GROUP)]
        mc = (qpos - (NSA_CMP_STRIDE * lane + NSA_CMP_BLOCK - 1)) >= 0
        s = jnp.where(mc[None], s, NEG)
        e = jnp.exp(s - jnp.max(s, axis=-1, keepdims=True))
        p = e / jnp.sum(e, axis=-1, keepdims=True) * mc[None].astype(F32)
        pb = p.reshape(ROWS, LANE).astype(BF16)
        o_c = jnp.dot(pb, vc, preferred_element_type=F32)
        imp = _importance(jnp.sum(p, axis=0), ovt_ref[...])

        cur = qpos // NSA_SEL_BLOCK
        valid = lane <= cur
        forced = (lane == 0) | (lane == cur) | (lane == cur - 1)
        score = jnp.where(valid, jnp.where(forced, NSA_FORCE, imp), -1.0)
        cnt = jnp.zeros((QB, LANE), jnp.int32)
        for jp in range(n_kt * 2):
            col = score[:, jp:jp + 1]
            beats = (col > score) | ((col == score) & (lane > jp))
            cnt = cnt + beats.astype(jnp.int32)
        sel = ((cnt < NSA_TOPK) & valid & (lane < n_kt * 2)).astype(BF16)
        for kt in range(n_kt):
            selm_ref[kt] = jnp.dot(sel, exp_ref[:, kt * QB:(kt + 1) * QB], preferred_element_type=F32)

        o_s = attend(qp, h, 2 * LANE, 3 * LANE, i + 1, lambda t: t, True, False)
        o_w = attend(qp, h, 4 * LANE, 5 * LANE, jnp.minimum(i, NSA_WINDOW // QB) + 1, lambda t: i - t, False, True)

        outs = []
        for g in range(NSA_GROUP):
            c = SMALL_GATE + h * NSA_GROUP + g
            r0 = g * QB
            outs.append(gates[:, c:c + 1] * o_c[r0:r0 + QB]
                        + gates[:, c + NSA_HEADS:c + NSA_HEADS + 1] * o_s[r0:r0 + QB]
                        + gates[:, c + 2 * NSA_HEADS:c + 2 * NSA_HEADS + 1] * o_w[r0:r0 + QB])
        for pr in range(NSA_GROUP // 2):
            a, b2 = outs[2 * pr], outs[2 * pr + 1]
            if h == 0:
                b2 = pltpu.roll(b2, NSA_HEAD_DIM, axis=1)
            else:
                a = pltpu.roll(a, NSA_HEAD_DIM, axis=1)
            c0 = (h * NSA_GROUP + 2 * pr) * NSA_HEAD_DIM
            o_ref[:, c0:c0 + LANE] = jnp.where(low, a, b2)


def nsa_prompt_attention(q, kv, small, w_ck, w_cv, table, b, l):
    nqb = l // QB
    n_sel = l // NSA_SEL_BLOCK
    nseg = l // NSA_CMP_STRIDE
    assert nseg == LANE and n_sel <= LANE and l % QB == 0
    cmp = nsa_compress_prompt(kv, w_ck, w_cv, b, l)
    cmp_bias = bias_tiles(table, nqb, QB, -(NSA_CMP_BLOCK - 1), NSA_CMP_STRIDE)
    toe_bias = bias_tiles(table, 3, QB, 0, 1)
    jj = np.arange(LANE)[None, :]
    cc = np.arange(LANE)[:, None]
    lo = np.maximum(jj * NSA_SEL_BLOCK, cc * NSA_CMP_STRIDE)
    hi = np.minimum(jj * NSA_SEL_BLOCK + NSA_SEL_BLOCK, cc * NSA_CMP_STRIDE + NSA_CMP_BLOCK)
    ovt = np.where((jj < n_sel) & (cc < nseg - 1), np.maximum(hi - lo, 0) / NSA_CMP_STRIDE, 0.0)
    expand = (np.arange(l)[None, :] // NSA_SEL_BLOCK == np.arange(LANE)[:, None]).astype(np.float32)
    return pl.pallas_call(
        functools.partial(_nsa_prompt_body, n_kt=nqb),
        grid=(b, nqb),
        in_specs=[pl.BlockSpec((QB, NSA_HEADS * NSA_HEAD_DIM), lambda bi, i: (bi * nqb + i, 0)),
                  pl.BlockSpec((l, 6 * NSA_KV_WIDTH), lambda bi, i: (bi, 0)),
                  pl.BlockSpec((QB, LANE), lambda bi, i: (bi * nqb + i, 0)),
                  pl.BlockSpec((1, nseg, 2 * NSA_KV_WIDTH), lambda bi, i: (bi, 0, 0)),
                  pl.BlockSpec((1, NSA_HEADS, QB, LANE), lambda bi, i: (i, 0, 0, 0)),
                  pl.BlockSpec((3, NSA_HEADS, QB, LANE), lambda bi, i: (0, 0, 0, 0)),
                  pl.BlockSpec((LANE, LANE), lambda bi, i: (0, 0)),
                  pl.BlockSpec((LANE, l), lambda bi, i: (0, 0))],
        out_specs=pl.BlockSpec((QB, NSA_HEADS * NSA_HEAD_DIM), lambda bi, i: (bi * nqb + i, 0)),
        out_shape=jax.ShapeDtypeStruct((b * l, NSA_HEADS * NSA_HEAD_DIM), F32),
        scratch_shapes=[pltpu.VMEM((nqb, QB, QB), F32), pltpu.VMEM((ROWS, LANE), F32)],
        compiler_params=pltpu.CompilerParams(dimension_semantics=("arbitrary", "arbitrary"),
                                             vmem_limit_bytes=VMEM_LIMIT_BYTES),
        name="nsa_prompt",
    )(q, kv, small, cmp, cmp_bias, toe_bias, jnp.asarray(ovt, BF16), jnp.asarray(expand, BF16))


PAGES_PER_STEP = 16
NSEG_PAGE = PAGE_SIZE // NSA_CMP_STRIDE
SELF_RANK = 2


def _bf(x):
    return x.astype(BF16).astype(F32)


def _bias_rows(table, n_tiles, step, off, kmul):
    t = bias_tiles(table, n_tiles, step, off, kmul, qmul=0, rows=8)[:, :, 0, :]
    return t.transpose(1, 0, 2).reshape(NSA_KV_HEADS, NSA_GROUP, n_tiles * LANE)


def _nsa_sample_cmp_body(pt_ref, *refs, n_pages):
    npg = PAGES_PER_STEP
    kp, vp = refs[:npg], refs[npg:2 * npg]
    qp_ref, wk_ref, wv_ref, bias_ref, ovt_ref, oc_ref, idx_ref, fk, sk, fv, sv = refs[2 * npg:]
    g = pl.program_id(1)
    nseg = n_pages * NSEG_PAGE
    for k in range(npg):
        for src, w_ref, f_ref, s_ref in ((kp[k], wk_ref, fk, sk), (vp[k], wv_ref, fv, sv)):
            f = jnp.zeros((NSEG_PAGE, LANE), F32)
            s = jnp.zeros((NSEG_PAGE, LANE), F32)
            for r in range(NSA_CMP_STRIDE):
                rows = src[0, pl.ds(r, NSEG_PAGE, stride=NSA_CMP_STRIDE), :]
                f = f + rows * w_ref[r:r + 1, :]
                s = s + rows * w_ref[NSA_CMP_STRIDE + r:NSA_CMP_STRIDE + r + 1, :]
            row0 = pl.multiple_of((g * npg + k) * NSEG_PAGE, NSEG_PAGE)
            f_ref[pl.ds(row0, NSEG_PAGE), :] = f
            s_ref[pl.ds(row0, NSEG_PAGE), :] = s

    @pl.when(g == pl.num_programs(1) - 1)
    def _():
        rowi = lax.broadcasted_iota(jnp.int32, (nseg, LANE), 0)
        kc = jnp.where(rowi < nseg - 1, fk[...] + pltpu.roll(sk[...], nseg - 1, axis=0), 0.0).astype(BF16)
        vc = jnp.where(rowi < nseg - 1, fv[...] + pltpu.roll(sv[...], nseg - 1, axis=0), 0.0).astype(BF16)
        mc = lax.broadcasted_iota(jnp.int32, (NSA_GROUP, nseg), 1) < nseg - 1
        lane = lax.broadcasted_iota(jnp.int32, (8, LANE), 1)
        imps = []
        for h in range(NSA_KV_HEADS):
            qp = (qp_ref[0, h] * NSA_SCALE).astype(BF16)
            s = jnp.where(mc, _nt_dot(qp, kc) + bias_ref[h], NEG)
            e = jnp.exp(s - jnp.max(s, axis=-1, keepdims=True))
            p = e / jnp.sum(e, axis=-1, keepdims=True) * mc.astype(F32)
            pb = p.astype(BF16)
            oc_ref[0, h] = jnp.dot(pb, vc, preferred_element_type=F32)
            imps.append(_importance(jnp.sum(p, axis=0, keepdims=True), ovt_ref[...]))
        imp = jnp.concatenate(imps + [jnp.zeros((8 - NSA_KV_HEADS, LANE), F32)], axis=0)
        forced = (lane == 0) | (lane == LANE - 1)
        score = jnp.where(forced, NSA_FORCE, imp)
        cnt = (score < NSA_FORCE).astype(F32)
        for jp in range(LANE):
            col = score[:, jp:jp + 1]
            cnt = cnt + ((col > score) | ((col == score) & (lane > jp))).astype(F32)
        lanef = lane.astype(F32)
        idx = jnp.where(lane == SELF_RANK, float(LANE), 0.0)
        for k in range(NSA_TOPK):
            if k != SELF_RANK:
                v = jnp.sum(jnp.where(cnt == k, lanef, 0.0), axis=-1, keepdims=True)
                idx = jnp.where(lane == k, v, idx)
        idx_ref[0] = idx.astype(jnp.int32)


def nsa_sample_cmp(qp, k_pool, v_pool, e, page_table, w_ck, w_cv, table):
    db, n_pages = page_table.shape
    n_pool = k_pool.shape[1]
    p0 = n_pages * PAGE_SIZE
    nseg = n_pages * NSEG_PAGE
    npg = PAGES_PER_STEP
    assert p0 // NSA_SEL_BLOCK == LANE and n_pages % npg == 0
    kp = k_pool.reshape(-1, PAGE_SIZE, NSA_KV_WIDTH)
    vp = v_pool.reshape(-1, PAGE_SIZE, NSA_KV_WIDTH)
    bias = _bias_rows(table, nseg // LANE, -LANE * NSA_CMP_STRIDE, p0 - (NSA_CMP_BLOCK - 1), NSA_CMP_STRIDE)
    jj = np.arange(LANE)[None, :]
    cc = np.arange(nseg)[:, None]
    lo = np.maximum(jj * NSA_SEL_BLOCK, cc * NSA_CMP_STRIDE)
    hi = np.minimum(jj * NSA_SEL_BLOCK + NSA_SEL_BLOCK, cc * NSA_CMP_STRIDE + NSA_CMP_BLOCK)
    ovt = np.where(cc < nseg - 1, np.maximum(hi - lo, 0) / NSA_CMP_STRIDE, 0.0)

    def page_spec(k):
        return pl.BlockSpec((1, PAGE_SIZE, NSA_KV_WIDTH), lambda b, g, pt: (pt[b, g * npg + k] + e * n_pool, 0, 0))

    const = lambda shape: pl.BlockSpec(shape, lambda b, g, pt: (0,) * len(shape))
    grid_spec = pltpu.PrefetchScalarGridSpec(
        num_scalar_prefetch=1,
        grid=(db, n_pages // npg),
        in_specs=[page_spec(k) for k in range(npg)] * 2 + [
            pl.BlockSpec((1, NSA_KV_HEADS, NSA_GROUP, LANE), lambda b, g, pt: (b, 0, 0, 0)),
            const((NSA_CMP_BLOCK, NSA_KV_WIDTH)), const((NSA_CMP_BLOCK, NSA_KV_WIDTH)),
            const((NSA_KV_HEADS, NSA_GROUP, nseg)), const((nseg, LANE))],
        out_specs=[pl.BlockSpec((1, NSA_KV_HEADS, NSA_GROUP, LANE), lambda b, g, pt: (b, 0, 0, 0)),
                   pl.BlockSpec((1, 8, LANE), lambda b, g, pt: (b, 0, 0))],
        scratch_shapes=[pltpu.VMEM((nseg, LANE), F32)] * 4)
    tile2 = lambda w: jnp.concatenate([w, w], axis=1).astype(F32)
    return pl.pallas_call(
        functools.partial(_nsa_sample_cmp_body, n_pages=n_pages),
        grid_spec=grid_spec,
        out_shape=[jax.ShapeDtypeStruct((db, NSA_KV_HEADS, NSA_GROUP, LANE), F32),
                   jax.ShapeDtypeStruct((db, 8, LANE), jnp.int32)],
        compiler_params=pltpu.CompilerParams(dimension_semantics=("arbitrary", "arbitrary")),
        name="nsa_sample_cmp",
    )(page_table, *([kp] * npg), *([vp] * npg), qp, tile2(w_ck), tile2(w_cv), bias, jnp.asarray(ovt, BF16))


def _nsa_sample_attn_body(pt_ref, idx_ref, *refs, nbuf):
    nb = NSA_KV_HEADS * NSA_TOPK
    kb, vb = refs[:nb], refs[nb:2 * nb]
    (qp_ref, new_ref, kwin_ref, vwin_ref, oc_ref, gate_ref, bnear_ref, bwin_ref, bends_ref,
     o_ref, kwo_ref, vwo_ref) = refs[2 * nb:]
    b = pl.program_id(0)
    new = new_ref[0]
    ks_new, vs_new, kw_new, vw_new = (new[:, c * LANE:(c + 1) * LANE] for c in (2, 3, 4, 5))
    kwin = kwin_ref[0]
    vwin = vwin_ref[0]
    wlane = lax.broadcasted_iota(jnp.int32, (NSA_GROUP, nbuf), 1)
    half = NSA_SEL_BLOCK
    bpp = PAGE_SIZE // NSA_SEL_BLOCK
    for h in range(NSA_KV_HEADS):
        qp = (qp_ref[0, h] * NSA_SCALE).astype(BF16)
        qpf = qp.astype(F32)
        b_self = bends_ref[h][:, 0:1]
        b_far = bends_ref[h][:, 1:2]
        near = bnear_ref[h]
        scores, slots = [], []
        for k in range(NSA_TOPK):
            if k == SELF_RANK:
                continue
            j = idx_ref[b, h * NSA_TOPK + k]
            bias = jnp.where(j == LANE - 1, near[:, half:], jnp.where(j == LANE - 2, near[:, :half], b_far))
            rows = pl.ds(pl.multiple_of((jnp.minimum(j, LANE - 1) % bpp) * half, half), half)
            scores.append(_nt_dot(qp, kb[h * NSA_TOPK + k][0, rows, :].astype(BF16)) + bias)
            slots.append((h * NSA_TOPK + k, rows))
        s_self = jnp.sum(qpf * _bf(ks_new), axis=-1, keepdims=True) + b_self
        m = s_self
        for s in scores:
            m = jnp.maximum(m, jnp.max(s, axis=-1, keepdims=True))
        p_self = jnp.exp(s_self - m)
        l = p_self
        acc = _bf(p_self) * _bf(vs_new)
        for s, (slot, rows) in zip(scores, slots):
            p = jnp.exp(s - m)
            l = l + jnp.sum(p, axis=-1, keepdims=True)
            acc = acc + jnp.dot(p.astype(BF16), vb[slot][0, rows, :].astype(BF16), preferred_element_type=F32)
        o_s = acc / l
        s = jnp.where(wlane >= 1, _nt_dot(qp, kwin.astype(BF16)) + bwin_ref[h], NEG)
        s_self = jnp.sum(qpf * _bf(kw_new), axis=-1, keepdims=True) + b_self
        m = jnp.maximum(jnp.max(s, axis=-1, keepdims=True), s_self)
        p = jnp.exp(s - m)
        p_self = jnp.exp(s_self - m)
        l = jnp.sum(p, axis=-1, keepdims=True) + p_self
        o_w = (jnp.dot(p.astype(BF16), vwin.astype(BF16), preferred_element_type=F32)
               + _bf(p_self) * _bf(vw_new)) / l
        gates = jax.nn.sigmoid(gate_ref[0, h])
        o_h = gates[:, 0:1] * oc_ref[0, h] + gates[:, 1:2] * o_s + gates[:, 2:3] * o_w
        o_rot = pltpu.roll(o_h, NSA_HEAD_DIM, axis=1)
        low = lax.broadcasted_iota(jnp.int32, (1, LANE), 1) < NSA_HEAD_DIM
        for pr in range(NSA_GROUP // 2):
            a = (o_h if h == 0 else o_rot)[2 * pr:2 * pr + 1]
            b2 = (o_rot if h == 0 else o_h)[2 * pr + 1:2 * pr + 2]
            c0 = (h * NSA_GROUP + 2 * pr) * NSA_HEAD_DIM
            o_ref[0, :, c0:c0 + LANE] = jnp.where(low, a, b2)
    rowi = lax.broadcasted_iota(jnp.int32, (nbuf, LANE), 0)
    kwo_ref[0] = jnp.where(rowi == nbuf - 1, kw_new, pltpu.roll(kwin, nbuf - 1, axis=0))
    vwo_ref[0] = jnp.where(rowi == nbuf - 1, vw_new, pltpu.roll(vwin, nbuf - 1, axis=0))


def nsa_sample_attn(qp, kv_new, idx, o_c, gates_raw, k_pool, v_pool, e, k_win, v_win, page_table, table):
    db, n_pages = page_table.shape
    n_pool = k_pool.shape[1]
    p0 = n_pages * PAGE_SIZE
    nbuf = k_win.shape[1]
    nb = NSA_KV_HEADS * NSA_TOPK
    bpp = PAGE_SIZE // NSA_SEL_BLOCK
    assert nbuf == NSA_WINDOW and nbuf % LANE == 0 and p0 >= nbuf
    kp = k_pool.reshape(-1, PAGE_SIZE, NSA_KV_WIDTH)
    vp = v_pool.reshape(-1, PAGE_SIZE, NSA_KV_WIDTH)
    bnear = _bias_rows(table, 1, 0, 2 * NSA_SEL_BLOCK, 1)
    bwin = _bias_rows(table, nbuf // LANE, -LANE, nbuf, 1)
    tb = table.astype(F32).reshape(REL_BUCKETS, NSA_KV_HEADS, NSA_GROUP)
    bends = jnp.pad(jnp.stack([tb[0], tb[REL_BUCKETS - 1]], axis=-1), ((0, 0), (0, 0), (0, LANE - 2)))

    def blk_spec(slot):
        def imap(b, pt, ix):
            j = jnp.minimum(ix[b, slot], LANE - 1)
            return (pt[b, j // bpp] + e * n_pool, 0, 0)
        return pl.BlockSpec((1, PAGE_SIZE, NSA_KV_WIDTH), imap)

    per_b = lambda shape: pl.BlockSpec((1,) + shape, lambda b, pt, ix: (b,) + (0,) * len(shape))
    const = lambda shape: pl.BlockSpec(shape, lambda b, pt, ix: (0,) * len(shape))
    head_shape = (NSA_KV_HEADS, NSA_GROUP, LANE)
    grid_spec = pltpu.PrefetchScalarGridSpec(
        num_scalar_prefetch=2,
        grid=(db,),
        in_specs=[blk_spec(s) for s in range(nb)] * 2 + [
            per_b(head_shape), per_b((1, 6 * NSA_KV_WIDTH)), per_b((nbuf, NSA_KV_WIDTH)), per_b((nbuf, NSA_KV_WIDTH)),
            per_b(head_shape), per_b(head_shape), const(head_shape),
            const((NSA_KV_HEADS, NSA_GROUP, nbuf)), const(head_shape)],
        out_specs=[per_b((1, NSA_HEADS * NSA_HEAD_DIM)), per_b((nbuf, NSA_KV_WIDTH)), per_b((nbuf, NSA_KV_WIDTH))])
    return pl.pallas_call(
        functools.partial(_nsa_sample_attn_body, nbuf=nbuf),
        grid_spec=grid_spec,
        out_shape=[jax.ShapeDtypeStruct((db, 1, NSA_HEADS * NSA_HEAD_DIM), F32),
                   jax.ShapeDtypeStruct((db, nbuf, NSA_KV_WIDTH), F32),
                   jax.ShapeDtypeStruct((db, nbuf, NSA_KV_WIDTH), F32)],
        compiler_params=pltpu.CompilerParams(dimension_semantics=("arbitrary",)),
        name="nsa_sample_attn",
    )(page_table, idx, *([kp] * nb), *([vp] * nb), qp, kv_new.reshape(db, 1, -1),
      k_win.reshape(db, nbuf, NSA_KV_WIDTH), v_win.reshape(db, nbuf, NSA_KV_WIDTH), o_c, gates_raw,
      bnear, bwin, bends)


def nsa_sample_mixer(q, kv, small, e, k_cmp_pool, v_cmp_pool, k_slc_pool, v_slc_pool, k_win, v_win,
                     page_table, w_ck, w_cv, table):
    db = q.shape[0]
    qh = q.reshape(db, NSA_KV_HEADS, NSA_GROUP, NSA_HEAD_DIM)
    zeros = jnp.zeros_like(qh[:, 0])
    qp = jnp.stack([jnp.concatenate([qh[:, 0], zeros], axis=-1), jnp.concatenate([zeros, qh[:, 1]], axis=-1)], axis=1)
    o_c, idx = nsa_sample_cmp(qp, k_cmp_pool, v_cmp_pool, e, page_table, w_ck, w_cv, table)
    idx = idx[:, :NSA_KV_HEADS, :NSA_TOPK].reshape(db, NSA_KV_HEADS * NSA_TOPK)
    g = small[:, SMALL_GATE:SMALL_GATE + 3 * NSA_HEADS].reshape(db, 3, NSA_KV_HEADS, NSA_GROUP).transpose(0, 2, 3, 1)
    g = jnp.pad(g, ((0, 0), (0, 0), (0, 0), (0, LANE - 3)))
    y, kw_n, vw_n = nsa_sample_attn(qp, kv, idx, o_c, g, k_slc_pool, v_slc_pool, e, k_win, v_win, page_table, table)
    shape = (db, -1, NSA_KV_HEADS, NSA_HEAD_DIM)
    return y.reshape(db, NSA_HEADS * NSA_HEAD_DIM), kw_n.reshape(shape), vw_n.reshape(shape)


def _bmm_body(a_ref, b_ref, o_ref):
    o_ref[0] = jnp.dot(a_ref[0].astype(BF16), b_ref[0].astype(BF16), preferred_element_type=F32)


def batched_matmul(a, b):
    hh, m, k = a.shape
    n = b.shape[2]
    return pl.pallas_call(
        _bmm_body,
        grid=(hh,),
        in_specs=[pl.BlockSpec((1, m, k), lambda i: (i, 0, 0)), pl.BlockSpec((1, k, n), lambda i: (i, 0, 0))],
        out_specs=pl.BlockSpec((1, m, n), lambda i: (i, 0, 0)),
        out_shape=jax.ShapeDtypeStruct((hh, m, n), F32),
        compiler_params=pltpu.CompilerParams(dimension_semantics=("arbitrary",)),
        name="batched_matmul",
    )(a, b)


def _mla_sample_body(pt_ref, *refs):
    npg = PAGES_PER_STEP
    cp, rp = refs[:npg], refs[npg:2 * npg]
    qlat_ref, qrope_ref, cnew_ref, krnew_ref, o_ref, m_ref, l_ref, acc_ref = refs[2 * npg:]
    g = pl.program_id(1)

    @pl.when(g == 0)
    def _():
        m_ref[...] = jnp.full_like(m_ref, NEG)
        l_ref[...] = jnp.zeros_like(l_ref)
        acc_ref[...] = jnp.zeros_like(acc_ref)

    ql = qlat_ref[0].astype(BF16)
    qr = qrope_ref[0].astype(BF16)
    c_all = jnp.concatenate([r[0] for r in cp], axis=0).astype(BF16)
    r_all = jnp.concatenate([r[0] for r in rp], axis=0).astype(BF16)
    s = (_nt_dot(ql, c_all) + _nt_dot(qr, r_all)) * MLA_SCALE
    m_old = m_ref[...]
    m_new = jnp.maximum(m_old, jnp.max(s, axis=-1, keepdims=True))
    alpha = jnp.exp(m_old - m_new)
    p = jnp.exp(s - m_new)
    l_new = alpha * l_ref[...] + jnp.sum(p, axis=-1, keepdims=True)
    acc = alpha * acc_ref[...] + jnp.dot(p.astype(BF16), c_all, preferred_element_type=F32)
    m_ref[...] = m_new
    l_ref[...] = l_new
    acc_ref[...] = acc

    @pl.when(g == pl.num_programs(1) - 1)
    def _():
        cn = _bf(cnew_ref[0])
        kn = _bf(krnew_ref[0])
        s_self = (jnp.sum(ql.astype(F32) * cn, axis=-1, keepdims=True)
                  + jnp.sum(qr.astype(F32) * kn, axis=-1, keepdims=True)) * MLA_SCALE
        m2 = jnp.maximum(m_new, s_self)
        a = jnp.exp(m_new - m2)
        p_self = jnp.exp(s_self - m2)
        o_ref[0] = (a * acc + _bf(p_self) * cn) / (a * l_new + p_self)


def mla_sample_attention(q_lat, q_rope, c_new, kr_new, lat_pool, kr_pool, o, page_table):
    db, n_pages = page_table.shape
    n_pool = lat_pool.shape[1]
    npg = PAGES_PER_STEP
    assert n_pages % npg == 0
    cp = lat_pool.reshape(-1, PAGE_SIZE, MLA_KV_RANK)
    rp = kr_pool.reshape(-1, PAGE_SIZE, MLA_ROPE)

    def page_spec(k, width):
        return pl.BlockSpec((1, PAGE_SIZE, width), lambda b, g, pt: (pt[b, g * npg + k] + o * n_pool, 0, 0))

    per_b = lambda shape: pl.BlockSpec((1,) + shape, lambda b, g, pt: (b,) + (0,) * len(shape))
    grid_spec = pltpu.PrefetchScalarGridSpec(
        num_scalar_prefetch=1,
        grid=(db, n_pages // npg),
        in_specs=[page_spec(k, MLA_KV_RANK) for k in range(npg)] + [page_spec(k, MLA_ROPE) for k in range(npg)] + [
            per_b((MLA_HEADS, MLA_KV_RANK)), per_b((MLA_HEADS, MLA_ROPE)), per_b((1, MLA_KV_RANK)), per_b((1, MLA_ROPE))],
        out_specs=per_b((MLA_HEADS, MLA_KV_RANK)),
        scratch_shapes=[pltpu.VMEM((MLA_HEADS, 1), F32), pltpu.VMEM((MLA_HEADS, 1), F32),
                        pltpu.VMEM((MLA_HEADS, MLA_KV_RANK), F32)])
    return pl.pallas_call(
        _mla_sample_body,
        grid_spec=grid_spec,
        out_shape=jax.ShapeDtypeStruct((db, MLA_HEADS, MLA_KV_RANK), F32),
        compiler_params=pltpu.CompilerParams(dimension_semantics=("arbitrary", "arbitrary")),
        name="mla_sample",
    )(page_table, *([cp] * npg), *([rp] * npg), q_lat, q_rope, c_new.reshape(db, 1, -1), kr_new.reshape(db, 1, -1))


def mla_sample_mixer(q_nope, q_rope, c, k_rope, lat_pool, kr_pool, o_idx, page_table, w_kvb):
    db = c.shape[0]
    w = w_kvb.reshape(MLA_KV_RANK, MLA_HEADS, MLA_NOPE + MLA_V).transpose(1, 0, 2)
    q_lat = batched_matmul(q_nope.transpose(1, 0, 2), w[..., :MLA_NOPE].transpose(0, 2, 1)).transpose(1, 0, 2)
    o_lat = mla_sample_attention(q_lat, q_rope, c, k_rope, lat_pool, kr_pool, o_idx, page_table)
    out = batched_matmul(o_lat.transpose(1, 0, 2), w[..., MLA_NOPE:])
    return out.transpose(1, 0, 2).reshape(db, MLA_HEADS * MLA_V)


OD_GROUPS = (GLA_HEADS * GLA_DK, GLA_HEADS * GLA_DK, GLA_HEADS * GLA_DV, GLA_HEADS * GLA_DV, MLA_Q_RANK, MLA_KV_RANK, LANE)
MLA_PREP_ROWS = 128
MLA_TQ = 256


def odd_weight(w_in):
    gq, gk, gv, g1, r, qa, kva, kr = split_cols(w_in, OD_SPLITS)
    z = lambda n: jnp.zeros((w_in.shape[0], n), w_in.dtype)
    small = jnp.concatenate([g1, z(MLA_NOPE - GLA_GATE_RANK), kr, z(LANE - MLA_NOPE - MLA_ROPE)], axis=1)
    return jnp.concatenate([gq, gk, gv, r, qa, kva, small], axis=1)


def _head_blocks(w, widths, keep):
    k = w.shape[0]
    w = w.reshape(k, MLA_HEADS, sum(widths))[:, :, keep[0]:keep[1]]
    return jnp.pad(w, ((0, 0), (0, 0), (0, LANE - (keep[1] - keep[0])))).reshape(k, MLA_HEADS * LANE)


def rope_tables(pos):
    half = MLA_ROPE // 2
    freqs = ROPE_THETA ** (-jnp.arange(half, dtype=F32) / half)
    ang = pos.astype(F32)[:, None] * freqs
    cos, sin = jnp.cos(ang), jnp.sin(ang)
    n = pos.shape[0]
    z = lambda w: jnp.zeros((n, w), F32)
    tail = LANE - MLA_NOPE - MLA_ROPE
    c = jnp.concatenate([jnp.ones((n, MLA_NOPE), F32), cos, cos, z(tail)], axis=1)
    s1 = jnp.concatenate([z(MLA_NOPE), -sin, z(half), z(tail)], axis=1)
    s2 = jnp.concatenate([z(MLA_NOPE), z(half), sin, z(tail)], axis=1)
    return c, s1, s2


def _rope_block(x, c, s1, s2):
    half = MLA_ROPE // 2
    return x * c + pltpu.roll(x, LANE - half, axis=1) * s1 + pltpu.roll(x, half, axis=1) * s2


def _mla_prep_body(qa_ref, kva_ref, sm_ref, qg_ref, kg_ref, wq_ref, wk_ref, wv_ref, c_ref, s1_ref, s2_ref,
                   q_out, c_out, k_out, v_out, kr_out):
    norm = lambda x, g: x * lax.rsqrt(jnp.mean(x * x, axis=-1, keepdims=True) + EPS) * g
    c, s1, s2 = c_ref[...], s1_ref[...], s2_ref[...]
    lane = lax.broadcasted_iota(jnp.int32, c.shape, 1)
    q = jnp.dot(norm(qa_ref[...], qg_ref[...]).astype(BF16), wq_ref[...], preferred_element_type=F32)
    lat = norm(kva_ref[...], kg_ref[...])
    c_out[...] = lat
    latb = lat.astype(BF16)
    kn = jnp.dot(latb, wk_ref[...], preferred_element_type=F32)
    v_out[...] = jnp.dot(latb, wv_ref[...], preferred_element_type=F32).astype(BF16)
    kr = jnp.where(lane >= MLA_NOPE, _rope_block(sm_ref[...], c, s1, s2), 0.0)
    kr_out[...] = kr
    for h in range(MLA_HEADS):
        cols = slice(h * LANE, (h + 1) * LANE)
        q_out[:, cols] = _rope_block(q[:, cols], c, s1, s2)
        k_out[:, cols] = (kn[:, cols] + kr).astype(BF16)


def mla_prep(qa, kva, small, q_norm, w_qb, kv_norm, w_kvb, tables, pos_block):
    m = qa.shape[0]
    tm = MLA_PREP_ROWS
    wq = _head_blocks(w_qb, (MLA_NOPE, MLA_ROPE), (0, MLA_NOPE + MLA_ROPE)).astype(BF16)
    wk = _head_blocks(w_kvb, (MLA_NOPE, MLA_V), (0, MLA_NOPE)).astype(BF16)
    wv = _head_blocks(w_kvb, (MLA_NOPE, MLA_V), (MLA_NOPE, MLA_NOPE + MLA_V)).astype(BF16)
    wide = MLA_HEADS * LANE
    rows = lambda w: pl.BlockSpec((tm, w), lambda i: (i, 0))
    const = lambda a: pl.BlockSpec(a.shape, lambda i: (0, 0))
    tab = pl.BlockSpec((tm, LANE), lambda i: (pos_block(i), 0))
    qg = q_norm.reshape(1, -1).astype(F32)
    kg = kv_norm.reshape(1, -1).astype(F32)
    return pl.pallas_call(
        _mla_prep_body,
        grid=(m // tm,),
        in_specs=[rows(MLA_Q_RANK), rows(MLA_KV_RANK), rows(LANE), const(qg), const(kg), const(wq), const(wk), const(wv),
                  tab, tab, tab],
        out_specs=[rows(wide), rows(MLA_KV_RANK), rows(wide), rows(wide), rows(LANE)],
        out_shape=[jax.ShapeDtypeStruct((m, wide), F32), jax.ShapeDtypeStruct((m, MLA_KV_RANK), F32),
                   jax.ShapeDtypeStruct((m, wide), BF16), jax.ShapeDtypeStruct((m, wide), BF16),
                   jax.ShapeDtypeStruct((m, LANE), F32)],
        compiler_params=pltpu.CompilerParams(dimension_semantics=("arbitrary",), vmem_limit_bytes=VMEM_LIMIT_BYTES),
        name="mla_prep",
    )(qa, kva, small, qg, kg, wq, wk, wv, *tables)


def _mla_prompt_body(q_ref, k_ref, v_ref, o_ref):
    tq = q_ref.shape[0]
    qi = pl.program_id(1)
    row = lax.broadcasted_iota(jnp.int32, (tq, tq), 0)
    col = lax.broadcasted_iota(jnp.int32, (tq, tq), 1)
    low = lax.broadcasted_iota(jnp.int32, (tq, LANE), 1) < MLA_V
    outs = []
    for h in range(MLA_HEADS):
        cols = slice(h * LANE, (h + 1) * LANE)
        q = (q_ref[:, cols] * MLA_SCALE).astype(BF16)

        def body(kt, carry):
            m, l, acc = carry
            k0 = pl.multiple_of(kt * tq, tq)
            s = _nt_dot(q, k_ref[pl.ds(k0, tq), cols])
            s = jnp.where((kt - qi) * tq + col <= row, s, NEG)
            m_new = jnp.maximum(m, jnp.max(s, axis=-1, keepdims=True))
            alpha = jnp.exp(m - m_new)
            p = jnp.exp(s - m_new)
            l_new = alpha * l + jnp.sum(p, axis=-1, keepdims=True)
            pv = jnp.dot(p.astype(BF16), v_ref[pl.ds(k0, tq), cols], preferred_element_type=F32)
            return m_new, l_new, alpha * acc + pv

        init = (jnp.full((tq, 1), NEG, F32), jnp.zeros((tq, 1), F32), jnp.zeros((tq, LANE), F32))
        _, l, acc = lax.fori_loop(0, qi + 1, body, init)
        outs.append(acc / l)
    for pr in range(MLA_HEADS // 2):
        o_ref[:, pr * LANE:(pr + 1) * LANE] = jnp.where(low, outs[2 * pr], pltpu.roll(outs[2 * pr + 1], MLA_V, axis=1))


def mla_prompt_attention(q_rot, k_full, v_pad, b, l):
    tq = MLA_TQ
    nq = l // tq
    wide = MLA_HEADS * LANE
    return pl.pallas_call(
        _mla_prompt_body,
        grid=(b, nq),
        in_specs=[pl.BlockSpec((tq, wide), lambda bi, i: (bi * nq + i, 0)),
                  pl.BlockSpec((l, wide), lambda bi, i: (bi, 0)),
                  pl.BlockSpec((l, wide), lambda bi, i: (bi, 0))],
        out_specs=pl.BlockSpec((tq, MLA_HEADS * MLA_V), lambda bi, i: (bi * nq + i, 0)),
        out_shape=jax.ShapeDtypeStruct((b * l, MLA_HEADS * MLA_V), F32),
        compiler_params=pltpu.CompilerParams(dimension_semantics=("arbitrary", "arbitrary"),
                                             vmem_limit_bytes=VMEM_LIMIT_BYTES),
        name="mla_prompt",
    )(q_rot, k_full, v_pad)


SSD_PAIRS = SSD_HEADS // 2
PAIR_PER_GROUP = SSD_PAIRS // SSD_GROUPS
CONV_PAD = 8


def _exact_dot(a, b):
    return jnp.dot(a, b, preferred_element_type=F32, precision=lax.Precision.HIGHEST)


def _softplus(x):
    u = jnp.exp(-jnp.abs(x))
    w = 1.0 + u
    return jnp.maximum(x, 0.0) + jnp.where(w == 1.0, u, jnp.log(w) * (u / (w - 1.0)))


def _silu(x):
    return x * jax.nn.sigmoid(x)


def _group_rmsnorm(y, g):
    gw = SSD_INNER // SSD_GROUPS
    outs = []
    for gi in range(SSD_GROUPS):
        yg = y[:, gi * gw:(gi + 1) * gw]
        outs.append(yg * lax.rsqrt(jnp.mean(yg * yg, axis=-1, keepdims=True) + EPS))
    return jnp.concatenate(outs, axis=1) * g


def _ssd_prompt_body(z_ref, xbc_ref, sm_ref, cw_ref, cb_ref, dtb_ref, alog_ref, d_ref, g_ref, ex_ref,
                     y_ref, st_ref, cv_ref, buf, state):
    c = pl.program_id(1)
    nc = pl.num_programs(1)
    t = SSD_CHUNK

    @pl.when(c == 0)
    def _():
        buf[0:CONV_PAD, :] = jnp.zeros((CONV_PAD, SSD_CONV_CH), F32)
        state[...] = jnp.zeros_like(state)

    buf[CONV_PAD:CONV_PAD + t, :] = xbc_ref[...]
    conv = cb_ref[...] + cw_ref[SSD_CONV - 1:SSD_CONV, :] * xbc_ref[...]
    for k in range(SSD_CONV - 1):
        conv = conv + cw_ref[k:k + 1, :] * buf[pl.ds(CONV_PAD - (SSD_CONV - 1) + k, t), :]
    buf[0:CONV_PAD, :] = buf[t:t + CONV_PAD, :]
    xc = _silu(conv)
    xs = xc[:, :SSD_INNER]
    bm = xc[:, SSD_INNER:SSD_INNER + SSD_GROUPS * SSD_STATE].astype(BF16)
    cm = xc[:, SSD_INNER + SSD_GROUPS * SSD_STATE:].astype(BF16)

    dt = _softplus(sm_ref[...] + dtb_ref[...])
    da = dt * (-jnp.exp(alog_ref[...]))
    row = lax.broadcasted_iota(jnp.int32, (t, t), 0)
    col = lax.broadcasted_iota(jnp.int32, (t, t), 1)
    tril = row >= col
    a_cs = _exact_dot(tril.astype(F32), da)
    a_cs_t = a_cs.T
    ex = ex_ref[...]
    dt_x = _exact_dot(dt, ex)
    acs_x = _exact_dot(a_cs, ex)
    last_x = acs_x[t - 1:t, :]
    xd = xs * dt_x
    xdw = xd * jnp.exp(last_x - acs_x)
    xdb = xd.astype(BF16)
    e_in = jnp.exp(acs_x)
    low = lax.broadcasted_iota(jnp.int32, (t, LANE), 1) < SSD_HEADDIM
    zero = jnp.zeros((t, LANE), BF16)

    y_parts = []
    for j in range(SSD_PAIRS):
        g = j // PAIR_PER_GROUP
        cg = cm[:, g * SSD_STATE:(g + 1) * SSD_STATE]
        bg = bm[:, g * SSD_STATE:(g + 1) * SSD_STATE]
        cb = _nt_dot(cg, bg)
        cols = slice(j * LANE, (j + 1) * LANE)
        xp = xdb[:, cols]
        y = jnp.zeros((t, LANE), F32)
        for hh in range(2):
            h = 2 * j + hh
            decay = jnp.where(tril, jnp.exp(a_cs[:, h:h + 1] - a_cs_t[h:h + 1, :]), 0.0)
            xh = jnp.where(low if hh == 0 else jnp.logical_not(low), xp, zero)
            y = y + jnp.dot((cb * decay).astype(BF16), xh, preferred_element_type=F32)
        s_old = state[j]
        y = y + _nt_dot(cg, s_old.astype(BF16)) * e_in[:, cols]
        dec_rows = jnp.broadcast_to(jnp.exp(last_x[:, cols]), (LANE, LANE)).T
        state[j] = dec_rows * s_old + jnp.dot(xdw[:, cols].T.astype(BF16), bg, preferred_element_type=F32)
        y_parts.append(y)
    y = jnp.concatenate(y_parts, axis=1) + xs * d_ref[...]
    y_ref[...] = _group_rmsnorm(y * _silu(z_ref[...]), g_ref[...])

    @pl.when(c == nc - 1)
    def _():
        st_ref[0] = state[...]
        cv_ref[0] = xbc_ref[t - (SSD_CONV - 1):t, :]


def _head_expand():
    return jnp.asarray(np.arange(LANE)[:, None] == (np.arange(SSD_INNER)[None, :] // SSD_HEADDIM), F32)


def _ssd_params(conv_w, conv_b, dt_bias, a_log, d_skip, norm_g):
    pad = lambda v: jnp.pad(v.astype(F32), (0, LANE - SSD_HEADS)).reshape(1, LANE)
    return (conv_w.astype(F32), conv_b.astype(F32).reshape(1, -1), pad(dt_bias), pad(a_log),
            jnp.repeat(d_skip.astype(F32), SSD_HEADDIM).reshape(1, -1), norm_g.astype(F32).reshape(1, -1), _head_expand())


def ssd_prompt(z, xbc, small, b, l, conv_w, conv_b, dt_bias, a_log, d_skip, norm_g):
    t = SSD_CHUNK
    nc = l // t
    params = _ssd_params(conv_w, conv_b, dt_bias, a_log, d_skip, norm_g)
    rows = lambda w: pl.BlockSpec((t, w), lambda bi, c: (bi * nc + c, 0))
    const = lambda a: pl.BlockSpec(a.shape, lambda bi, c: (0,) * a.ndim)
    y, st, cv = pl.pallas_call(
        _ssd_prompt_body,
        grid=(b, nc),
        in_specs=[rows(SSD_INNER), rows(SSD_CONV_CH), rows(LANE)] + [const(p) for p in params],
        out_specs=[rows(SSD_INNER),
                   pl.BlockSpec((1, SSD_PAIRS, LANE, SSD_STATE), lambda bi, c: (bi, 0, 0, 0)),
                   pl.BlockSpec((1, SSD_CONV - 1, SSD_CONV_CH), lambda bi, c: (bi, 0, 0))],
        out_shape=[jax.ShapeDtypeStruct((b * l, SSD_INNER), F32),
                   jax.ShapeDtypeStruct((b, SSD_PAIRS, LANE, SSD_STATE), F32),
                   jax.ShapeDtypeStruct((b, SSD_CONV - 1, SSD_CONV_CH), F32)],
        scratch_shapes=[pltpu.VMEM((CONV_PAD + t, SSD_CONV_CH), F32), pltpu.VMEM((SSD_PAIRS, LANE, SSD_STATE), F32)],
        compiler_params=pltpu.CompilerParams(dimension_semantics=("arbitrary", "arbitrary"),
                                             vmem_limit_bytes=VMEM_LIMIT_BYTES),
        name="ssd_prompt",
    )(z, xbc, small, *params)
    return y, st.reshape(b, SSD_HEADS, SSD_HEADDIM, SSD_STATE), cv


def _ssd_sample_body(z_ref, xbc_ref, sm_ref, cs_ref, st_ref, cw_ref, cb_ref, dtb_ref, alog_ref, d_ref, g_ref, ex_ref,
                     y_ref, sto_ref, cvo_ref):
    x_new = xbc_ref[0]
    cs = cs_ref[0]
    conv = cb_ref[...] + cw_ref[SSD_CONV - 1:SSD_CONV, :] * x_new
    for k in range(SSD_CONV - 1):
        conv = conv + cw_ref[k:k + 1, :] * cs[k:k + 1, :]
    cvo_ref[0] = jnp.concatenate([cs[1:], x_new], axis=0)
    xc = _silu(conv)
    xs = xc[:, :SSD_INNER]
    bm = xc[:, SSD_INNER:SSD_INNER + SSD_GROUPS * SSD_STATE]
    cm = xc[:, SSD_INNER + SSD_GROUPS * SSD_STATE:]
    dt = jnp.broadcast_to(_softplus(sm_ref[0] + dtb_ref[...]), (8, LANE))
    da = dt * (-jnp.exp(alog_ref[...]))
    ex = ex_ref[...]
    xd = xs * _exact_dot(dt, ex)[0:1]
    dec_x = jnp.exp(_exact_dot(da, ex)[0:1])
    y_parts = []
    for j in range(SSD_PAIRS):
        g = j // PAIR_PER_GROUP
        cols = slice(j * LANE, (j + 1) * LANE)
        bg = bm[:, g * SSD_STATE:(g + 1) * SSD_STATE]
        cg = jnp.broadcast_to(cm[:, g * SSD_STATE:(g + 1) * SSD_STATE], (8, SSD_STATE)).astype(BF16)
        dec_rows = jnp.broadcast_to(dec_x[:, cols], (LANE, LANE)).T
        xd_rows = jnp.broadcast_to(xd[:, cols], (LANE, LANE)).T
        s_new = dec_rows * st_ref[0, j] + xd_rows * bg
        sto_ref[0, j] = s_new
        y_parts.append(_nt_dot(cg, s_new.astype(BF16))[0:1])
    y = jnp.concatenate(y_parts, axis=1) + xs * d_ref[...]
    y_ref[0] = _group_rmsnorm(y * _silu(z_ref[0]), g_ref[...])


def ssd_sample(z, xbc, small, conv_state, ssm_state, conv_w, conv_b, dt_bias, a_log, d_skip, norm_g):
    db = z.shape[0]
    params = _ssd_params(conv_w, conv_b, dt_bias, a_log, d_skip, norm_g)
    per_b = lambda shape: pl.BlockSpec((1,) + shape, lambda i: (i,) + (0,) * len(shape))
    const = lambda a: pl.BlockSpec(a.shape, lambda i: (0,) * a.ndim)
    st_shape = (SSD_PAIRS, LANE, SSD_STATE)
    y, st, cv = pl.pallas_call(
        _ssd_sample_body,
        grid=(db,),
        in_specs=[per_b((1, SSD_INNER)), per_b((1, SSD_CONV_CH)), per_b((1, LANE)), per_b((SSD_CONV - 1, SSD_CONV_CH)),
                  per_b(st_shape)] + [const(p) for p in params],
        out_specs=[per_b((1, SSD_INNER)), per_b(st_shape), per_b((SSD_CONV - 1, SSD_CONV_CH))],
        out_shape=[jax.ShapeDtypeStruct((db, 1, SSD_INNER), F32), jax.ShapeDtypeStruct((db,) + st_shape, F32),
                   jax.ShapeDtypeStruct((db, SSD_CONV - 1, SSD_CONV_CH), F32)],
        compiler_params=pltpu.CompilerParams(dimension_semantics=("arbitrary",)),
        name="ssd_sample",
    )(z.reshape(db, 1, -1), xbc.reshape(db, 1, -1), small.reshape(db, 1, -1), conv_state,
      ssm_state.reshape((db,) + st_shape), *params)
    return y.reshape(db, SSD_INNER), st.reshape(db, SSD_HEADS, SSD_HEADDIM, SSD_STATE), cv


GLA_BLOCK = 128
GLA_INNER = GLA_HEADS * GLA_DV


def _log_gate(small, w2_ref, bg_ref):
    logits = jnp.dot(small.astype(BF16), w2_ref[...], preferred_element_type=F32) + bg_ref[...]
    return -_softplus(-logits) / GLA_TAU


def _head_rmsnorm_gate(o_heads, g_ref, r):
    outs = [o * lax.rsqrt(jnp.mean(o * o, axis=-1, keepdims=True) + EPS) for o in o_heads]
    return jnp.concatenate(outs, axis=1) * g_ref[...] * _silu(r)


def _gla_prompt_body(q_ref, k_ref, v_ref, r_ref, sm_ref, w2_ref, bg_ref, g_ref, y_ref, st_ref, state):
    blk = pl.program_id(1)
    t = GLA_BLOCK

    @pl.when(blk == 0)
    def _():
        state[...] = jnp.zeros_like(state)

    la = _log_gate(sm_ref[...], w2_ref, bg_ref)
    row = lax.broadcasted_iota(jnp.int32, (t, t), 0)
    col = lax.broadcasted_iota(jnp.int32, (t, t), 1)
    same = (row // GLA_CHUNK) == (col // GLA_CHUNK)
    causal = same & (row >= col)
    bc = _exact_dot(causal.astype(F32), la)
    is_last = col == (row // GLA_CHUNK) * GLA_CHUNK + GLA_CHUNK - 1
    bl = _exact_dot(is_last.astype(F32), bc)
    qt = q_ref[...] * (GLA_DK ** -0.5) * jnp.exp(bc)
    kt = k_ref[...] * jnp.exp(-bc)
    kd = k_ref[...] * jnp.exp(bl - bc)
    o_heads = []
    for h in range(GLA_HEADS):
        kc = slice(h * GLA_DK, (h + 1) * GLA_DK)
        vb = v_ref[:, h * GLA_DV:(h + 1) * GLA_DV].astype(BF16)
        qb = qt[:, kc].astype(BF16)
        att = jnp.where(causal, _nt_dot(qb, kt[:, kc].astype(BF16)), 0.0)
        o_intra = jnp.dot(att.astype(BF16), vb, preferred_element_type=F32)
        kd_t = kd[:, kc].T
        dec_t = jnp.exp(bl[:, kc]).T
        s = state[h]
        inter = []
        for c in range(t // GLA_CHUNK):
            r0 = c * GLA_CHUNK
            inter.append(jnp.dot(qb[r0:r0 + GLA_CHUNK], s.astype(BF16), preferred_element_type=F32))
            kd_c = jnp.where((col // GLA_CHUNK) == c, kd_t, 0.0).astype(BF16)
            s = dec_t[:, r0:r0 + 1] * s + jnp.dot(kd_c, vb, preferred_element_type=F32)
        state[h] = s
        o_heads.append(o_intra + jnp.concatenate(inter, axis=0))
    y_ref[...] = _head_rmsnorm_gate(o_heads, g_ref, r_ref[...])

    @pl.when(blk == pl.num_programs(1) - 1)
    def _():
        st_ref[0] = state[...]


def _gla_params(w_g2, b_g, norm_g):
    w2 = jnp.pad(w_g2, ((0, LANE - GLA_GATE_RANK), (0, 0))).astype(BF16)
    return w2, b_g.astype(F32).reshape(1, -1), jnp.tile(norm_g.astype(F32), GLA_HEADS).reshape(1, -1)


def gla_prompt(gq, gk, gv, r, small, b, l, w_g2, b_g, norm_g):
    t = GLA_BLOCK
    nb = l // t
    params = _gla_params(w_g2, b_g, norm_g)
    rows = lambda w: pl.BlockSpec((t, w), lambda bi, i: (bi * nb + i, 0))
    const = lambda a: pl.BlockSpec(a.shape, lambda bi, i: (0,) * a.ndim)
    st_shape = (GLA_HEADS, GLA_DK, GLA_DV)
    return pl.pallas_call(
        _gla_prompt_body,
        grid=(b, nb),
        in_specs=[rows(GLA_HEADS * GLA_DK), rows(GLA_HEADS * GLA_DK), rows(GLA_INNER), rows(GLA_INNER), rows(LANE)]
        + [const(p) for p in params],
        out_specs=[rows(GLA_INNER), pl.BlockSpec((1,) + st_shape, lambda bi, i: (bi, 0, 0, 0))],
        out_shape=[jax.ShapeDtypeStruct((b * l, GLA_INNER), F32), jax.ShapeDtypeStruct((b,) + st_shape, F32)],
        scratch_shapes=[pltpu.VMEM(st_shape, F32)],
        compiler_params=pltpu.CompilerParams(dimension_semantics=("arbitrary", "arbitrary"),
                                             vmem_limit_bytes=VMEM_LIMIT_BYTES),
        name="gla_prompt",
    )(gq, gk, gv, r, small, *params)


def _gla_sample_body(q_ref, k_ref, v_ref, r_ref, sm_ref, st_ref, w2_ref, bg_ref, g_ref, y_ref, sto_ref):
    la = _log_gate(jnp.broadcast_to(sm_ref[0], (8, LANE)), w2_ref, bg_ref)[0:1]
    a = jnp.exp(la)
    qt = q_ref[0] * (GLA_DK ** -0.5) * a
    kt = k_ref[0] * jnp.exp(-la)
    o_heads = []
    for h in range(GLA_HEADS):
        kc = slice(h * GLA_DK, (h + 1) * GLA_DK)
        vb = _bf(v_ref[0, :, h * GLA_DV:(h + 1) * GLA_DV])
        qb = _bf(qt[:, kc])
        att = jnp.sum(qb * _bf(kt[:, kc]), axis=-1, keepdims=True)
        s_old = st_ref[0, h]
        o_inter = jnp.dot(jnp.broadcast_to(qb, (8, GLA_DK)).astype(BF16), s_old.astype(BF16),
                          preferred_element_type=F32)[0:1]
        a_rows = jnp.broadcast_to(a[:, kc], (GLA_DK, GLA_DK)).T[:, 0:1]
        k_rows = jnp.broadcast_to(k_ref[0, :, kc], (GLA_DK, GLA_DK)).T[:, 0:1]
        sto_ref[0, h] = a_rows * s_old + k_rows * v_ref[0, :, h * GLA_DV:(h + 1) * GLA_DV]
        o_heads.append(_bf(att) * vb + o_inter)
    y_ref[0] = _head_rmsnorm_gate(o_heads, g_ref, r_ref[0])


def gla_sample(gq, gk, gv, r, small, state, w_g2, b_g, norm_g):
    db = gq.shape[0]
    params = _gla_params(w_g2, b_g, norm_g)
    per_b = lambda shape: pl.BlockSpec((1,) + shape, lambda i: (i,) + (0,) * len(shape))
    const = lambda a: pl.BlockSpec(a.shape, lambda i: (0,) * a.ndim)
    st_shape = (GLA_HEADS, GLA_DK, GLA_DV)
    row3 = lambda x: x.reshape(db, 1, -1)
    y, st = pl.pallas_call(
        _gla_sample_body,
        grid=(db,),
        in_specs=[per_b((1, GLA_HEADS * GLA_DK)), per_b((1, GLA_HEADS * GLA_DK)), per_b((1, GLA_INNER)),
                  per_b((1, GLA_INNER)), per_b((1, LANE)), per_b(st_shape)] + [const(p) for p in params],
        out_specs=[per_b((1, GLA_INNER)), per_b(st_shape)],
        out_shape=[jax.ShapeDtypeStruct((db, 1, GLA_INNER), F32), jax.ShapeDtypeStruct((db,) + st_shape, F32)],
        compiler_params=pltpu.CompilerParams(dimension_semantics=("arbitrary",)),
        name="gla_sample",
    )(row3(gq), row3(gk), row3(gv), row3(r), row3(small), state, *params)
    return y.reshape(db, GLA_INNER), st


def _ffn_body(x_ref, g_ref, wg_ref, wu_ref, wd_ref, fg_ref, o_ref, *, chunks, final):
    x = x_ref[...]
    h = (x * lax.rsqrt(jnp.mean(x * x, axis=-1, keepdims=True) + EPS) * g_ref[...]).astype(BF16)
    acc = x
    for c0, cw in chunks:
        a = jnp.dot(h, wg_ref[:, c0:c0 + cw], preferred_element_type=F32)
        u = jnp.dot(h, wu_ref[:, c0:c0 + cw], preferred_element_type=F32)
        act = (a * jax.nn.sigmoid(a) * u).astype(BF16)
        acc = acc + jnp.dot(act, wd_ref[c0:c0 + cw, :], preferred_element_type=F32)
    if final:
        acc = acc * lax.rsqrt(jnp.mean(acc * acc, axis=-1, keepdims=True) + EPS) * fg_ref[...]
    o_ref[...] = acc


def swiglu_ffn(x, g, w_gate, w_up, w_down, final_g=None):
    m, d = x.shape
    hdim = w_gate.shape[1]
    tm = _row_tile(m)
    final = final_g is not None
    fg = (final_g if final else g).reshape(1, d).astype(F32)
    wspec = lambda shape: pl.BlockSpec(shape, lambda i: (0, 0), pipeline_mode=pl.Buffered(1))
    return pl.pallas_call(
        functools.partial(_ffn_body, chunks=_col_chunks(hdim, 256), final=final),
        grid=(m // tm,),
        in_specs=[pl.BlockSpec((tm, d), lambda i: (i, 0)),
                  pl.BlockSpec((1, d), lambda i: (0, 0)),
                  wspec((d, hdim)), wspec((d, hdim)), wspec((hdim, d)),
                  pl.BlockSpec((1, d), lambda i: (0, 0))],
        out_specs=pl.BlockSpec((tm, d), lambda i: (i, 0)),
        out_shape=jax.ShapeDtypeStruct((m, d), F32),
        compiler_params=pltpu.CompilerParams(dimension_semantics=("arbitrary",),
                                             vmem_limit_bytes=VMEM_LIMIT_BYTES),
        name="swiglu_ffn",
    )(x, g.reshape(1, d).astype(F32), w_gate.astype(BF16), w_up.astype(BF16), w_down.astype(BF16), fg)


def split_cols(h, sizes):
    return jnp.split(h, [int(i) for i in np.cumsum(sizes)[:-1]], axis=-1)


def rmsnorm(x, g):
    xf = x.astype(F32)
    y = xf * lax.rsqrt(jnp.mean(xf * xf, axis=-1, keepdims=True) + EPS)
    return (y * g.astype(F32)).astype(x.dtype)


def rope(x, pos):
    half = x.shape[-1] // 2
    freqs = ROPE_THETA ** (-jnp.arange(half, dtype=F32) / half)
    ang = pos.astype(F32)[:, None] * freqs
    shape = (pos.shape[0],) + (1,) * (x.ndim - 3) + (half,)
    cos, sin = jnp.cos(ang).reshape(shape), jnp.sin(ang).reshape(shape)
    xf = x.astype(F32)
    x1, x2 = xf[..., :half], xf[..., half:]
    return jnp.concatenate([x1 * cos - x2 * sin, x2 * cos + x1 * sin], axis=-1).astype(x.dtype)


def t5_bucket(rel):
    exact = REL_BUCKETS // 2
    n = jnp.maximum(rel, 0)
    nf = jnp.maximum(n, exact).astype(F32)
    large = exact + (jnp.log(nf / exact) / math.log(REL_MAX_DIST / exact) * (REL_BUCKETS - exact)).astype(jnp.int32)
    return jnp.where(n < exact, n, jnp.minimum(large, REL_BUCKETS - 1))


def shared_bias(rel, table):
    lq, nk = rel.shape
    return table.astype(F32)[t5_bucket(rel)].reshape(lq, nk, NSA_KV_HEADS, NSA_GROUP).transpose(0, 2, 3, 1)


def masked_softmax(s, mask):
    return jax.nn.softmax(jnp.where(mask, s, NEG), axis=-1) * mask


def causal_conv(xbc, conv_state, w, b):
    full = jnp.concatenate([conv_state.astype(xbc.dtype), xbc], axis=1)
    ch = xbc.shape[-1]
    y = lax.conv_general_dilated(full, w[:, None, :].astype(xbc.dtype), (1,), 'VALID',
                                 dimension_numbers=('NWC', 'WIO', 'NWC'), feature_group_count=ch)
    return jax.nn.silu(y + b.astype(y.dtype)), full[:, -(SSD_CONV - 1):]


def segsum(x):
    t = x.shape[-1]
    cs = jnp.cumsum(x, axis=-1)
    diff = cs[..., :, None] - cs[..., None, :]
    return jnp.where(jnp.tril(jnp.ones((t, t), bool)), diff, -jnp.inf)


def ssd_scan(x, dt, a, bm, cm, s0, chunk):
    b, l, h, p = x.shape
    g, n = bm.shape[2], bm.shape[3]
    e = h // g
    nc = l // chunk
    xd = (x * dt[..., None]).reshape(b, nc, chunk, g, e, p)
    da = (dt * a).reshape(b, nc, chunk, g, e).transpose(0, 3, 4, 1, 2)
    bm = bm.reshape(b, nc, chunk, g, n)
    cm = cm.reshape(b, nc, chunk, g, n)
    a_cs = jnp.cumsum(da, axis=-1)
    lmat = jnp.exp(segsum(da))
    cb = jnp.einsum('bclgn,bcsgn->bcgls', cm, bm)
    y_diag = jnp.einsum('bcgls,bgecls,bcsgep->bclgep', cb, lmat, xd)
    decay_st = jnp.exp(a_cs[..., -1:] - a_cs)
    states = jnp.einsum('bclgn,bgecl,bclgep->bcgepn', bm, decay_st, xd)
    states = jnp.concatenate([s0.reshape(b, 1, g, e, p, n), states], axis=1)
    chunk_decay = jnp.exp(segsum(jnp.pad(a_cs[..., -1], ((0, 0), (0, 0), (0, 0), (1, 0)))))
    new_states = jnp.einsum('bgezc,bcgepn->bzgepn', chunk_decay, states)
    y_off = jnp.einsum('bclgn,bcgepn,bgecl->bclgep', cm, new_states[:, :-1], jnp.exp(a_cs))
    return (y_diag + y_off).reshape(b, l, h, p), new_states[:, -1].reshape(b, h, p, n)


def ssd_mixer(z, xbc, dt_raw, conv_state, ssm_state, conv_w, conv_b, dt_bias, a_log, d_skip, norm_g):
    b, l = z.shape[:2]
    xbc, new_conv = causal_conv(xbc, conv_state, conv_w, conv_b)
    xs, bm, cm = split_cols(xbc.astype(F32), (SSD_INNER, SSD_GROUPS * SSD_STATE, SSD_GROUPS * SSD_STATE))
    x = xs.reshape(b, l, SSD_HEADS, SSD_HEADDIM)
    bm = bm.reshape(b, l, SSD_GROUPS, SSD_STATE)
    cm = cm.reshape(b, l, SSD_GROUPS, SSD_STATE)
    dt = jax.nn.softplus(dt_raw.astype(F32) + dt_bias.astype(F32))
    a = -jnp.exp(a_log.astype(F32))
    chunk = SSD_CHUNK if l % SSD_CHUNK == 0 else l
    y, new_ssm = ssd_scan(x, dt, a, bm, cm, ssm_state.astype(F32), chunk)
    y = y + x * d_skip.astype(F32)[:, None]
    y = (y.reshape(b, l, SSD_INNER) * jax.nn.silu(z.astype(F32))).reshape(b, l, SSD_GROUPS, SSD_INNER // SSD_GROUPS)
    y = rmsnorm(y, norm_g.reshape(SSD_GROUPS, -1)).reshape(b, l, SSD_INNER)
    return y.astype(z.dtype), new_conv, new_ssm


def nsa_compress(k, w):
    b, t = k.shape[:2]
    nseg = t // NSA_CMP_STRIDE
    seg = k[:, :nseg * NSA_CMP_STRIDE].reshape(b, nseg, NSA_CMP_STRIDE, NSA_KV_HEADS, NSA_HEAD_DIM)
    first = jnp.einsum('bsrhd,rd->bshd', seg, w[:NSA_CMP_STRIDE])
    second = jnp.einsum('bsrhd,rd->bshd', seg, w[NSA_CMP_STRIDE:])
    return first[:, :-1] + second[:, 1:]


def sel_overlap(n_sel, n_cmp):
    j = jnp.arange(n_sel)[:, None]
    i = jnp.arange(n_cmp)[None, :]
    lo = jnp.maximum(j * NSA_SEL_BLOCK, i * NSA_CMP_STRIDE)
    hi = jnp.minimum(j * NSA_SEL_BLOCK + NSA_SEL_BLOCK, i * NSA_CMP_STRIDE + NSA_CMP_BLOCK)
    return (jnp.maximum(hi - lo, 0) / NSA_CMP_STRIDE).astype(F32)


def nsa_cmp_branch(q, qpos, kc, vc, table):
    nc = kc.shape[1]
    cend = jnp.arange(nc) * NSA_CMP_STRIDE + NSA_CMP_BLOCK - 1
    rel = qpos[:, None] - cend[None, :]
    s = jnp.einsum('bqhgd,bchd->bqhgc', q, kc).astype(F32) * NSA_SCALE + shared_bias(rel, table)
    p = masked_softmax(s, (rel >= 0)[:, None, None, :])
    return jnp.einsum('bqhgc,bchd->bqhgd', p, vc), p


def nsa_select(p_cmp, qpos, n_sel):
    imp = jnp.einsum('bqhgc,jc->bqhj', p_cmp, sel_overlap(n_sel, p_cmp.shape[-1]))
    j = jnp.arange(n_sel)[None, :]
    cur = (qpos // NSA_SEL_BLOCK)[:, None]
    valid = (j <= cur)[:, None, :]
    forced = ((j == 0) | (j == cur) | (j == cur - 1))[:, None, :]
    score = jnp.where(valid, jnp.where(forced, NSA_FORCE, imp), -1.0)
    if n_sel < NSA_TOPK:
        score = jnp.pad(score, ((0, 0), (0, 0), (0, 0), (0, NSA_TOPK - n_sel)), constant_values=-1.0)
    vals, idx = lax.top_k(score, NSA_TOPK)
    kpos = (idx[..., None] * NSA_SEL_BLOCK + jnp.arange(NSA_SEL_BLOCK)).reshape(*idx.shape[:-1], NSA_TOPK * NSA_SEL_BLOCK)
    kok = jnp.repeat(vals >= 0, NSA_SEL_BLOCK, axis=-1)
    return kpos, kok


def nsa_slc_branch(q, qpos, ks, vs, kpos, kok, table):
    rel = qpos[None, :, None, None] - kpos
    tb = table.astype(F32).reshape(REL_BUCKETS, NSA_KV_HEADS, NSA_GROUP)
    bias = jnp.moveaxis(tb[t5_bucket(rel), jnp.arange(NSA_KV_HEADS)[:, None]], -1, -2)
    s = jnp.einsum('bqhgd,bqhkd->bqhgk', q, ks).astype(F32) * NSA_SCALE + bias
    p = masked_softmax(s, (kok & (rel >= 0))[..., None, :])
    return jnp.einsum('bqhgk,bqhkd->bqhgd', p, vs)


def nsa_win_branch(q, qpos, kw, vw, wpos, table):
    rel = qpos[:, None] - wpos[None, :]
    mask = (rel >= 0) & (rel < NSA_WINDOW) & (wpos >= 0)[None, :]
    s = jnp.einsum('bqhgd,bkhd->bqhgk', q, kw).astype(F32) * NSA_SCALE + shared_bias(rel, table)
    p = masked_softmax(s, mask[:, None, None, :])
    return jnp.einsum('bqhgk,bkhd->bqhgd', p, vw)


def nsa_gate(g, o_c, o_s, o_w):
    return g[..., 0:1] * o_c + g[..., 1:2] * o_s + g[..., 2:3] * o_w


def gather_rows(pool, e, new, page_table, kpos):
    db, s = new.shape[:2]
    p0 = page_table.shape[1] * PAGE_SIZE
    bi = jnp.arange(db)[:, None, None, None]
    hi = jnp.arange(NSA_KV_HEADS)[None, None, :, None]
    pp = jnp.minimum(kpos, p0 - 1)
    past = pool[e, page_table[bi, pp // PAGE_SIZE], pp % PAGE_SIZE, hi]
    cur = new[bi, jnp.clip(kpos - p0, 0, s - 1), hi]
    return jnp.where((kpos < p0)[..., None], past, cur)


def nsa_sample(q, kvs, gates, e, k_cmp_pool, v_cmp_pool, k_slc_pool, v_slc_pool, k_win_buf, v_win_buf,
               page_table, w_ck, w_cv, table):
    kc_new, vc_new, ks_new, vs_new, kw_new, vw_new = kvs
    db, s = q.shape[:2]
    p0 = page_table.shape[1] * PAGE_SIZE
    qpos = p0 + jnp.arange(s)
    past = lambda pool: pool[e, page_table].reshape(db, p0, NSA_KV_HEADS, NSA_HEAD_DIM)
    kc = nsa_compress(jnp.concatenate([past(k_cmp_pool), kc_new], axis=1), w_ck)
    vc = nsa_compress(jnp.concatenate([past(v_cmp_pool), vc_new], axis=1), w_cv)
    o_c, p_c = nsa_cmp_branch(q, qpos, kc, vc, table)
    kpos, kok = nsa_select(p_c, qpos, -(-(p0 + s) // NSA_SEL_BLOCK))
    ksel = gather_rows(k_slc_pool, e, ks_new, page_table, kpos)
    vsel = gather_rows(v_slc_pool, e, vs_new, page_table, kpos)
    o_s = nsa_slc_branch(q, qpos, ksel, vsel, kpos, kok, table)
    nbuf = k_win_buf.shape[1]
    kw = jnp.concatenate([k_win_buf.astype(kw_new.dtype), kw_new], axis=1)
    vw = jnp.concatenate([v_win_buf.astype(vw_new.dtype), vw_new], axis=1)
    wpos = p0 - nbuf + jnp.arange(nbuf + s)
    o_w = nsa_win_branch(q, qpos, kw, vw, wpos, table)
    out = nsa_gate(gates, o_c, o_s, o_w).reshape(db, s, NSA_HEADS * NSA_HEAD_DIM)
    return out, kw[:, -nbuf:], vw[:, -nbuf:]


def even_split(parts, r0, b, l):
    z, xbc, q, kv, small = (t[r0:r0 + b * l].reshape(b, l, -1) for t in parts)
    dtr = small[..., SMALL_DT:SMALL_DT + SSD_HEADS]
    g = small[..., SMALL_GATE:SMALL_GATE + 3 * NSA_HEADS]
    q = q.reshape(b, l, NSA_KV_HEADS, NSA_GROUP, NSA_HEAD_DIM)
    kvs = tuple(t.reshape(b, l, NSA_KV_HEADS, NSA_HEAD_DIM) for t in split_cols(kv, (NSA_KV_WIDTH,) * 6))
    g = jax.nn.sigmoid(g.astype(F32)).reshape(b, l, 3, NSA_KV_HEADS, NSA_GROUP).transpose(0, 1, 3, 4, 2)
    return z, xbc, dtr, q, kvs, g


def gla_scan(q, k, v, log_a, s0, chunk):
    b, l, h, dk = q.shape
    dv = v.shape[-1]
    nc = l // chunk
    q, k, v, log_a = [t.reshape(b, nc, chunk, h, t.shape[-1]) for t in (q, k, v, log_a)]
    bc = jnp.cumsum(log_a, axis=2)
    blast = bc[:, :, -1:]
    qt = q * jnp.exp(bc)
    kt = k * jnp.exp(-bc)
    kd = k * jnp.exp(blast - bc)
    causal = jnp.tril(jnp.ones((chunk, chunk), bool))
    att = jnp.where(causal, jnp.einsum('bclhk,bcshk->bchls', qt, kt), 0.0)
    o_intra = jnp.einsum('bchls,bcshv->bclhv', att, v)

    def step(st, inp):
        qt_c, kd_c, v_c, dec_c = inp
        o_c = jnp.einsum('blhk,bhkv->blhv', qt_c, st)
        st = jnp.exp(dec_c)[..., None] * st + jnp.einsum('blhk,blhv->bhkv', kd_c, v_c)
        return st, o_c

    xs = tuple(jnp.swapaxes(t, 0, 1) for t in (qt, kd, v, blast[:, :, 0]))
    s_fin, o_inter = lax.scan(step, s0, xs)
    return (o_intra + jnp.swapaxes(o_inter, 0, 1)).reshape(b, l, h, dv), s_fin


def gla_mixer(q, k, v, g1, r, state, w_g2, b_g, norm_g):
    b, l = q.shape[:2]
    q = q.astype(F32).reshape(b, l, GLA_HEADS, GLA_DK) * GLA_DK ** -0.5
    k = k.astype(F32).reshape(b, l, GLA_HEADS, GLA_DK)
    v = v.astype(F32).reshape(b, l, GLA_HEADS, GLA_DV)
    log_a = jax.nn.log_sigmoid(g1.astype(F32) @ w_g2.astype(F32) + b_g.astype(F32)).reshape(b, l, GLA_HEADS, GLA_DK) / GLA_TAU
    chunk = GLA_CHUNK if l % GLA_CHUNK == 0 else l
    o, new_state = gla_scan(q, k, v, log_a, state.astype(F32), chunk)
    o = rmsnorm(o, norm_g).reshape(b, l, GLA_HEADS * GLA_DV) * jax.nn.silu(r.astype(F32))
    return o.astype(r.dtype), new_state


def odd_split(h, b, l, pos, q_norm, w_qb, kv_norm):
    gq, gk, gv, g1, r, qa, kva, kr = split_cols(h.reshape(b, l, -1), OD_SPLITS)
    q = fused_matmul(qa.reshape(b * l, -1), w_qb, norm_g=q_norm).reshape(b, l, MLA_HEADS, MLA_NOPE + MLA_ROPE)
    q_nope, q_rope = q[..., :MLA_NOPE], rope(q[..., MLA_NOPE:], pos)
    return (gq, gk, gv, g1, r), (q_nope, q_rope, rmsnorm(kva, kv_norm), rope(kr, pos))


def mla_prompt(q_nope, q_rope, c, k_rope, w_kvb):
    b, l = c.shape[:2]
    kv = fused_matmul(c.reshape(b * l, -1), w_kvb).reshape(b, l, MLA_HEADS, MLA_NOPE + MLA_V)
    k_nope, v = kv[..., :MLA_NOPE], kv[..., MLA_NOPE:]
    nqb = l // MLA_QBLOCK
    kpos = jnp.arange(l)

    def block(args):
        i, qn, qr = args
        qpos = i * MLA_QBLOCK + jnp.arange(MLA_QBLOCK)
        s = (jnp.einsum('bqhd,bkhd->bhqk', qn, k_nope) + jnp.einsum('bqhr,bkr->bhqk', qr, k_rope)).astype(F32) * MLA_SCALE
        p = jax.nn.softmax(jnp.where(kpos[None, :] <= qpos[:, None], s, NEG), axis=-1)
        return jnp.einsum('bhqk,bkhd->bqhd', p, v)

    blk = lambda t: t.reshape(b, nqb, MLA_QBLOCK, *t.shape[2:]).swapaxes(0, 1)
    o = lax.map(block, (jnp.arange(nqb), blk(q_nope), blk(q_rope)))
    return o.swapaxes(0, 1).reshape(b, l, MLA_HEADS * MLA_V)


def mla_sample(q_nope, q_rope, c, k_rope, lat_pool, kr_pool, o_idx, page_table, w_kvb):
    db, s = c.shape[:2]
    w = w_kvb.reshape(MLA_KV_RANK, MLA_HEADS, MLA_NOPE + MLA_V)
    q_lat = jnp.einsum('bshd,chd->bshc', q_nope, w[..., :MLA_NOPE])
    c_past = lat_pool[o_idx, page_table].reshape(db, -1, MLA_KV_RANK)
    kr_past = kr_pool[o_idx, page_table].reshape(db, -1, MLA_ROPE)
    s_past = (jnp.einsum('bshc,btc->bsht', q_lat, c_past) + jnp.einsum('bshr,btr->bsht', q_rope, kr_past)).astype(F32) * MLA_SCALE
    s_new = (jnp.einsum('bshc,btc->bsht', q_lat, c) + jnp.einsum('bshr,btr->bsht', q_rope, k_rope)).astype(F32) * MLA_SCALE
    causal = jnp.arange(s)[None, :] <= jnp.arange(s)[:, None]
    s_new = jnp.where(causal[:, None, :], s_new, NEG)
    p = jax.nn.softmax(jnp.concatenate([s_past, s_new], axis=-1), axis=-1)
    n_past = c_past.shape[1]
    o_lat = jnp.einsum('bsht,btc->bshc', p[..., :n_past], c_past) + jnp.einsum('bsht,btc->bshc', p[..., n_past:], c)
    return jnp.einsum('bshc,chd->bshd', o_lat, w[..., MLA_NOPE:]).reshape(db, s, MLA_HEADS * MLA_V)


def kernel(x_prompt, x_sample, state_ssm, state_conv, cache_nsa_k_cmp, cache_nsa_v_cmp, cache_nsa_k_slc, cache_nsa_v_slc, cache_nsa_k_win, cache_nsa_v_win, state_gla, cache_mla_latent, cache_mla_krope, page_table, rel_bias, ev_norm, ev_w_in, ssd_conv_w, ssd_conv_b, ssd_dt_bias, ssd_a_log, ssd_d, ssd_norm, nsa_w_cmp_k, nsa_w_cmp_v, ev_w_out, od_norm, od_w_in, gla_w_gate2, gla_b_gate, gla_norm, mla_q_norm, mla_w_qb, mla_kv_norm, mla_w_kvb, od_w_out, ffn_norm, ffn_w_gate, ffn_w_up, ffn_w_down, final_norm):
    b, l = x_prompt.shape[:2]
    db, s = x_sample.shape[:2]
    depth = ffn_norm.shape[0]
    p0 = page_table.shape[1] * PAGE_SIZE
    pos_p = jnp.arange(l)
    pos_s = p0 + jnp.arange(s)
    names = ('ssm', 'conv', 'k_cmp', 'v_cmp', 'k_slc', 'v_slc', 'k_win', 'v_win', 'gla', 'lat', 'krope')
    newp = {n: [] for n in names}
    news = {n: [] for n in names}
    np_rows = b * l
    x = jnp.concatenate([x_prompt.reshape(np_rows, D_MODEL), x_sample.reshape(db * s, D_MODEL)], axis=0)
    for li in range(depth):
        if li % 2 == 0:
            e = li // 2
            ssd_w = (ssd_conv_w[e], ssd_conv_b[e], ssd_dt_bias[e], ssd_a_log[e], ssd_d[e], ssd_norm[e])
            parts = fused_matmul(x, even_weight(ev_w_in[e]), norm_g=ev_norm[e], groups=EV_GROUPS)
            z_all, xbc_all, _, kv_all, small_all = parts
            kv_heads = lambda r0, nb, nl: tuple(
                t.reshape(nb, nl, NSA_KV_HEADS, NSA_HEAD_DIM)
                for t in split_cols(kv_all[r0:r0 + nb * nl], (NSA_KV_WIDTH,) * 6))
            y_ssd, ssm_n, conv_n = ssd_prompt(z_all, xbc_all, small_all, b, l, *ssd_w)
            y_nsa = nsa_prompt_attention(parts[2], parts[3], parts[4], nsa_w_cmp_k[e], nsa_w_cmp_v[e], rel_bias, b, l)
            mix_p = jnp.concatenate([y_ssd, y_nsa], axis=-1)
            nw = min(NSA_WINDOW, l)
            kvs = kv_heads(0, b, l)
            for n, t in zip(names[:8], (ssm_n, conv_n, kvs[0], kvs[1], kvs[2], kvs[3],
                                        kvs[4][:, -nw:], kvs[5][:, -nw:])):
                newp[n].append(t)
            kvs = kv_heads(np_rows, db, s)
            y_ssd, ssm_n, conv_n = ssd_sample(z_all[np_rows:], xbc_all[np_rows:], small_all[np_rows:],
                                              state_conv[e], state_ssm[e], *ssd_w)
            y_nsa, kw_n, vw_n = nsa_sample_mixer(parts[2][np_rows:], parts[3][np_rows:], parts[4][np_rows:], e,
                                                 cache_nsa_k_cmp, cache_nsa_v_cmp, cache_nsa_k_slc, cache_nsa_v_slc,
                                                 cache_nsa_k_win[e], cache_nsa_v_win[e], page_table,
                                                 nsa_w_cmp_k[e], nsa_w_cmp_v[e], rel_bias)
            mix_s = jnp.concatenate([y_ssd, y_nsa], axis=-1)
            for n, t in zip(names[:8], (ssm_n, conv_n, kvs[0], kvs[1], kvs[2], kvs[3], kw_n, vw_n)):
                news[n].append(t)
            x = fused_matmul(jnp.concatenate([mix_p, mix_s], axis=0), ev_w_out[e], residual=x)
        else:
            o = li // 2
            gla_w = (gla_w_gate2[o], gla_b_gate[o], gla_norm[o])
            gq, gk, gv, r, qa, kva, small = fused_matmul(x, odd_weight(od_w_in[o]), norm_g=od_norm[o],
                                                         groups=OD_GROUPS)
            assert db * s == MLA_PREP_ROWS and l % MLA_PREP_ROWS == 0
            nblk = l // MLA_PREP_ROWS
            tables = tuple(jnp.concatenate(t, axis=0)
                           for t in zip(rope_tables(pos_p), rope_tables(jnp.tile(pos_s, db))))
            q_rot, lat, k_full, v_pad, kr_rot = mla_prep(
                qa, kva, small, mla_q_norm[o], mla_w_qb[o], mla_kv_norm[o], mla_w_kvb[o], tables,
                lambda i: jnp.where(i < b * nblk, i % nblk, nblk))
            kr_rot = kr_rot[:, MLA_NOPE:MLA_NOPE + MLA_ROPE]
            assert s == 1
            y_gla, gla_n = gla_prompt(gq, gk, gv, r, small, b, l, *gla_w)
            y_mla = mla_prompt_attention(q_rot, k_full, v_pad, b, l)
            mix_p = jnp.concatenate([y_gla, y_mla], axis=-1)
            newp['gla'].append(gla_n)
            newp['lat'].append(lat[:np_rows].reshape(b, l, -1))
            newp['krope'].append(kr_rot[:np_rows].reshape(b, l, -1))
            y_gla, gla_n = gla_sample(gq[np_rows:], gk[np_rows:], gv[np_rows:], r[np_rows:], small[np_rows:],
                                      state_gla[o], *gla_w)
            q_s = q_rot[np_rows:].reshape(db * s, MLA_HEADS, LANE)
            y_mla = mla_sample_mixer(q_s[..., :MLA_NOPE], q_s[..., MLA_NOPE:MLA_NOPE + MLA_ROPE], lat[np_rows:],
                                     kr_rot[np_rows:], cache_mla_latent, cache_mla_krope, o, page_table, mla_w_kvb[o])
            mix_s = jnp.concatenate([y_gla, y_mla], axis=-1)
            news['gla'].append(gla_n)
            news['lat'].append(lat[np_rows:].reshape(db, s, -1))
            news['krope'].append(kr_rot[np_rows:].reshape(db, s, -1))
            x = fused_matmul(jnp.concatenate([mix_p, mix_s], axis=0), od_w_out[o], residual=x)
        x = swiglu_ffn(x, ffn_norm[li], ffn_w_gate[li], ffn_w_up[li], ffn_w_down[li],
                       final_g=final_norm if li == depth - 1 else None)
    y_prompt = x[:np_rows].reshape(b, l, D_MODEL)
    y_sample = x[np_rows:].reshape(db, s, D_MODEL)
    st = lambda d, n: jnp.stack(d[n])
    return (y_prompt, y_sample,
            st(newp, 'ssm'), st(news, 'ssm'), st(newp, 'conv'), st(news, 'conv'),
            st(newp, 'k_cmp'), st(news, 'k_cmp'), st(newp, 'v_cmp'), st(news, 'v_cmp'),
            st(newp, 'k_slc'), st(news, 'k_slc'), st(newp, 'v_slc'), st(news, 'v_slc'),
            st(newp, 'k_win'), st(news, 'k_win'), st(newp, 'v_win'), st(news, 'v_win'),
            st(newp, 'gla'), st(news, 'gla'), st(newp, 'lat'), st(news, 'lat'),
            st(newp, 'krope'), st(news, 'krope'))
```

```python
import functools
import math

import jax
import jax.numpy as jnp
import numpy as np
from jax import lax
from jax.experimental import pallas as pl
from jax.experimental.pallas import tpu as pltpu

F32 = jnp.float32
BF16 = jnp.bfloat16
EPS = 1e-6
NEG = -1e30

D_MODEL = 1024
PAGE_SIZE = 128

SSD_HEADS = 16
SSD_HEADDIM = 64
SSD_INNER = SSD_HEADS * SSD_HEADDIM
SSD_GROUPS = 2
SSD_STATE = 128
SSD_CONV = 4
SSD_CHUNK = 128
SSD_CONV_CH = SSD_INNER + 2 * SSD_GROUPS * SSD_STATE

NSA_HEADS = 16
NSA_KV_HEADS = 2
NSA_GROUP = NSA_HEADS // NSA_KV_HEADS
NSA_HEAD_DIM = 64
NSA_KV_WIDTH = NSA_KV_HEADS * NSA_HEAD_DIM
NSA_CMP_BLOCK = 32
NSA_CMP_STRIDE = 16
NSA_SEL_BLOCK = 64
NSA_TOPK = 16
NSA_WINDOW = 512
NSA_QBLOCK = 128
NSA_SCALE = NSA_HEAD_DIM ** -0.5
NSA_FORCE = 1e4

REL_BUCKETS = 32
REL_MAX_DIST = 128

GLA_HEADS = 4
GLA_DK = 128
GLA_DV = 256
GLA_GATE_RANK = 16
GLA_TAU = 16.0
GLA_CHUNK = 16

MLA_HEADS = 8
MLA_Q_RANK = 384
MLA_KV_RANK = 256
MLA_NOPE = 64
MLA_ROPE = 32
MLA_V = 64
MLA_QBLOCK = 128
MLA_SCALE = (MLA_NOPE + MLA_ROPE) ** -0.5
ROPE_THETA = 10000.0

EV_SPLITS = (SSD_INNER, SSD_CONV_CH, SSD_HEADS, NSA_HEADS * NSA_HEAD_DIM, 6 * NSA_KV_WIDTH, 3 * NSA_HEADS)
OD_SPLITS = (GLA_HEADS * GLA_DK, GLA_HEADS * GLA_DK, GLA_HEADS * GLA_DV, GLA_GATE_RANK, GLA_HEADS * GLA_DV,
             MLA_Q_RANK, MLA_KV_RANK, MLA_ROPE)

VMEM_LIMIT_BYTES = 56 * 1024 * 1024
LANE = 128


def _row_tile(m):
    for t in (512, 384, 256, 128):
        if m % t == 0:
            return t
    return m


def _col_chunks(n, width=512):
    out, c = [], 0
    while c < n:
        w = min(width, n - c)
        out.append((c, w))
        c += w
    return out


def _mm_body(*refs, norm, residual, groups):
    it = iter(refs)
    x_ref = next(it)
    g_ref = next(it) if norm else None
    w_ref = next(it)
    r_ref = next(it) if residual else None
    o_refs = list(it)
    x = x_ref[...]
    if norm:
        x = x * lax.rsqrt(jnp.mean(x * x, axis=-1, keepdims=True) + EPS) * g_ref[...]
    xb = x.astype(BF16)
    off = 0
    for o_ref, gw in zip(o_refs, groups):
        for c0, cw in _col_chunks(gw):
            acc = jnp.dot(xb, w_ref[:, off + c0:off + c0 + cw], preferred_element_type=F32)
            if residual:
                acc = acc + r_ref[:, off + c0:off + c0 + cw]
            o_ref[:, c0:c0 + cw] = acc
        off += gw


def fused_matmul(x, w, norm_g=None, residual=None, groups=None):
    m, k = x.shape
    n = w.shape[1]
    single = groups is None
    groups = (n,) if single else tuple(groups)
    assert sum(groups) == n
    tm = _row_tile(m)
    norm = norm_g is not None
    res = residual is not None
    args = [x]
    specs = [pl.BlockSpec((tm, k), lambda i: (i, 0))]
    if norm:
        args.append(norm_g.reshape(1, k).astype(F32))
        specs.append(pl.BlockSpec((1, k), lambda i: (0, 0)))
    args.append(w.astype(BF16))
    specs.append(pl.BlockSpec((k, n), lambda i: (0, 0)))
    if res:
        args.append(residual)
        specs.append(pl.BlockSpec((tm, n), lambda i: (i, 0)))
    outs = pl.pallas_call(
        functools.partial(_mm_body, norm=norm, residual=res, groups=groups),
        grid=(m // tm,),
        in_specs=specs,
        out_specs=[pl.BlockSpec((tm, gw), lambda i: (i, 0)) for gw in groups],
        out_shape=[jax.ShapeDtypeStruct((m, gw), F32) for gw in groups],
        compiler_params=pltpu.CompilerParams(dimension_semantics=("arbitrary",),
                                             vmem_limit_bytes=VMEM_LIMIT_BYTES),
        name="fused_matmul",
    )(*args)
    return outs[0] if single else outs


EV_GROUPS = (SSD_INNER, SSD_CONV_CH, NSA_HEADS * NSA_HEAD_DIM, 6 * NSA_KV_WIDTH, LANE)
SMALL_DT = 0
SMALL_GATE = SSD_HEADS
QB = NSA_QBLOCK
ROWS = NSA_GROUP * QB


def even_weight(w_in):
    z, xbc, dtr, q, kvs, g = split_cols(w_in, EV_SPLITS)
    pad = jnp.zeros((w_in.shape[0], LANE - SSD_HEADS - 3 * NSA_HEADS), w_in.dtype)
    return jnp.concatenate([z, xbc, q, kvs, dtr, g, pad], axis=1)


def _bucket_tile(rel):
    exact = REL_BUCKETS // 2
    n = jnp.maximum(rel, 0)
    nf = jnp.maximum(n, exact).astype(F32)
    large = exact + (jnp.log(nf / exact) / math.log(REL_MAX_DIST / exact) * (REL_BUCKETS - exact)).astype(jnp.int32)
    return jnp.where(n < exact, n, jnp.minimum(large, REL_BUCKETS - 1))


def _bias_tiles_body(table_ref, o_ref, *, base_step, base_off, kmul, qmul):
    t = pl.program_id(0)
    shape = o_ref.shape[2:]
    qi = lax.broadcasted_iota(jnp.int32, shape, 0)
    ki = lax.broadcasted_iota(jnp.int32, shape, 1)
    bucket = _bucket_tile(t * base_step + base_off + qmul * qi - kmul * ki)
    for hd in range(NSA_HEADS):
        acc = jnp.zeros(shape, F32)
        for bk in range(REL_BUCKETS):
            acc = jnp.where(bucket == bk, table_ref[bk, hd], acc)
        o_ref[0, hd] = acc


def bias_tiles(table, n_tiles, base_step, base_off, kmul, qmul=1, rows=QB):
    return pl.pallas_call(
        functools.partial(_bias_tiles_body, base_step=base_step, base_off=base_off, kmul=kmul, qmul=qmul),
        grid=(n_tiles,),
        in_specs=[pl.BlockSpec(memory_space=pltpu.SMEM)],
        out_specs=pl.BlockSpec((1, NSA_HEADS, rows, LANE), lambda t: (t, 0, 0, 0)),
        out_shape=jax.ShapeDtypeStruct((n_tiles, NSA_HEADS, rows, LANE), F32),
        compiler_params=pltpu.CompilerParams(dimension_semantics=("arbitrary",)),
        name="t5_bias_tiles",
    )(table.astype(F32))


def _compress_body(kv_ref, w_ref, o_ref):
    nseg = o_ref.shape[1]
    first = jnp.zeros(o_ref.shape[1:], F32)
    second = jnp.zeros(o_ref.shape[1:], F32)
    for r in range(NSA_CMP_STRIDE):
        rows = kv_ref[pl.ds(r, nseg, stride=NSA_CMP_STRIDE), :]
        first = first + rows * w_ref[r:r + 1, :]
        second = second + rows * w_ref[NSA_CMP_STRIDE + r:NSA_CMP_STRIDE + r + 1, :]
    out = first + pltpu.roll(second, nseg - 1, axis=0)
    row = lax.broadcasted_iota(jnp.int32, out.shape, 0)
    o_ref[0] = jnp.where(row < nseg - 1, out, 0.0)


def nsa_compress_prompt(kv, w_ck, w_cv, b, l):
    nseg = l // NSA_CMP_STRIDE
    w = jnp.concatenate([w_ck, w_ck, w_cv, w_cv], axis=1).astype(F32)
    return pl.pallas_call(
        _compress_body,
        grid=(b, 2),
        in_specs=[pl.BlockSpec((l, NSA_KV_WIDTH), lambda i, j: (i, j)),
                  pl.BlockSpec((NSA_CMP_BLOCK, NSA_KV_WIDTH), lambda i, j: (0, j))],
        out_specs=pl.BlockSpec((1, nseg, NSA_KV_WIDTH), lambda i, j: (i, 0, j)),
        out_shape=jax.ShapeDtypeStruct((b, nseg, 2 * NSA_KV_WIDTH), F32),
        compiler_params=pltpu.CompilerParams(dimension_semantics=("arbitrary", "arbitrary")),
        name="nsa_compress",
    )(kv, w)


def _nt_dot(a, b):
    return lax.dot_general(a, b, (((1,), (1,)), ((), ())), preferred_element_type=F32)


def _importance(p_group_sum, ovt):
    return jnp.dot(p_group_sum.astype(BF16), ovt, preferred_element_type=F32)


def _nsa_prompt_body(q_ref, kv_ref, sm_ref, cmp_ref, cmpb_ref, toeb_ref, ov_ref, exp_ref, o_ref,
                     selm_ref, acc_ref, *, n_kt):
    i = pl.program_id(1)
    n_sel = 2 * n_kt
    row = lax.broadcasted_iota(jnp.int32, (QB, LANE), 0)
    lane = lax.broadcasted_iota(jnp.int32, (QB, LANE), 1)
    low = lane < NSA_HEAD_DIM
    gates = jax.nn.sigmoid(sm_ref[...])
    qpos = i * QB + row

    def attend(qp, h, kcol, vcol, n_tiles, tile_of, use_sel, window):
        acc_ref[...] = jnp.zeros_like(acc_ref)

        def body(t, carry):
            m, l = carry
            kt = tile_of(t)
            d = i - kt
            k0 = pl.multiple_of(kt * QB, QB)
            k_t = kv_ref[pl.ds(k0, QB), kcol:kcol + LANE].astype(BF16)
            v_t = kv_ref[pl.ds(k0, QB), vcol:vcol + LANE].astype(BF16)
            s = _nt_dot(qp, k_t).reshape(NSA_GROUP, QB, LANE)
            s = s + toeb_ref[jnp.minimum(d, 2), pl.ds(h * NSA_GROUP, NSA_GROUP)]
            rel = d * QB + row - lane
            msk = rel >= 0
            if window:
                msk = msk & (rel < NSA_WINDOW)
            if use_sel:
                msk = msk & (selm_ref[kt] > 0.5)
            s = jnp.where(msk[None], s, NEG)
            m_new = jnp.maximum(m, jnp.max(s, axis=-1, keepdims=True))
            alpha = jnp.exp(m - m_new)
            p = jnp.exp(s - m_new)
            l_new = alpha * l + jnp.sum(p, axis=-1, keepdims=True)
            pv = jnp.dot(p.reshape(ROWS, LANE).astype(BF16), v_t, preferred_element_type=F32)
            acc_ref[...] = acc_ref[...] * alpha.reshape(ROWS, 1) + pv
            return m_new, l_new

        m0 = jnp.full((NSA_GROUP, QB, 1), NEG, F32)
        l0 = jnp.zeros((NSA_GROUP, QB, 1), F32)
        _, l = lax.fori_loop(0, n_tiles, body, (m0, l0))
        return acc_ref[...] / l.reshape(ROWS, 1)

    for h in range(NSA_KV_HEADS):
        in_half = low if h == 0 else jnp.logical_not(low)
        parts = []
        for g in range(NSA_GROUP):
            hd = h * NSA_GROUP + g
            t = q_ref[:, (hd // 2) * LANE:(hd // 2) * LANE + LANE]
            if hd % 2 != h:
                t = pltpu.roll(t, NSA_HEAD_DIM, axis=1)
            parts.append((jnp.where(in_half, t, 0.0) * NSA_SCALE).astype(BF16))
        qp = jnp.concatenate(parts, axis=0)

        kc = cmp_ref[0, :, 0:LANE].astype(BF16)
        vc = cmp_ref[0, :, LANE:2 * LANE].astype(BF16)
        s = _nt_dot(qp, kc).reshape(NSA_GROUP, QB, LANE) + cmpb_ref[0, pl.ds(h * NSA_GROUP, NSA_GROUP)]
        mc = (qpos - (NSA_CMP_STRIDE * lane + NSA_CMP_BLOCK - 1)) >= 0
        s = jnp.where(mc[None], s, NEG)
        e = jnp.exp(s - jnp.max(s, axis=-1, keepdims=True))
        p = e / jnp.sum(e, axis=-1, keepdims=True) * mc[None].astype(F32)
        o_c = jnp.dot(p.reshape(ROWS, LANE).astype(BF16), vc, preferred_element_type=F32)
        imp_t = _nt_dot(ov_ref[...], jnp.sum(p, axis=0).astype(BF16))[0:n_sel]

        jrow = lax.broadcasted_iota(jnp.int32, (n_sel, LANE), 0)
        cur = (i * QB + lax.broadcasted_iota(jnp.int32, (n_sel, LANE), 1)) // NSA_SEL_BLOCK
        valid = jrow <= cur
        forced = (jrow == 0) | (jrow == cur) | (jrow == cur - 1)
        score = jnp.where(valid, jnp.where(forced, NSA_FORCE, imp_t), -1.0)
        cnt = jnp.zeros((n_sel, LANE), jnp.int32)
        for jp in range(n_sel):
            other = score[jp:jp + 1, :]
            beats = (other > score) | ((other == score) & (jrow > jp))
            cnt = cnt + beats.astype(jnp.int32)
        sel_t = ((cnt < NSA_TOPK) & valid).astype(F32)
        sel = jnp.concatenate([sel_t, jnp.zeros((LANE - n_sel, LANE), F32)], axis=0).T.astype(BF16)
        for kt in range(n_kt):
            selm_ref[kt] = jnp.dot(sel, exp_ref[:, kt * QB:(kt + 1) * QB], preferred_element_type=F32)

        o_s = attend(qp, h, 2 * LANE, 3 * LANE, i + 1, lambda t: t, True, False)
        o_w = attend(qp, h, 4 * LANE, 5 * LANE, jnp.minimum(i, NSA_WINDOW // QB) + 1, lambda t: i - t, False, True)

        outs = []
        for g in range(NSA_GROUP):
            c = SMALL_GATE + h * NSA_GROUP + g
            r0 = g * QB
            outs.append(gates[:, c:c + 1] * o_c[r0:r0 + QB]
                        + gates[:, c + NSA_HEADS:c + NSA_HEADS + 1] * o_s[r0:r0 + QB]
                        + gates[:, c + 2 * NSA_HEADS:c + 2 * NSA_HEADS + 1] * o_w[r0:r0 + QB])
        for pr in range(NSA_GROUP // 2):
            a, b2 = outs[2 * pr], outs[2 * pr + 1]
            if h == 0:
                b2 = pltpu.roll(b2, NSA_HEAD_DIM, axis=1)
            else:
                a = pltpu.roll(a, NSA_HEAD_DIM, axis=1)
            c0 = (h * NSA_GROUP + 2 * pr) * NSA_HEAD_DIM
            o_ref[:, c0:c0 + LANE] = jnp.where(low, a, b2)


def nsa_prompt_attention(q, kv, small, w_ck, w_cv, table, b, l):
    nqb = l // QB
    n_sel = l // NSA_SEL_BLOCK
    nseg = l // NSA_CMP_STRIDE
    assert nseg == LANE and n_sel <= LANE and l % QB == 0
    cmp = nsa_compress_prompt(kv, w_ck, w_cv, b, l)
    cmp_bias = bias_tiles(table, nqb, QB, -(NSA_CMP_BLOCK - 1), NSA_CMP_STRIDE)
    toe_bias = bias_tiles(table, 3, QB, 0, 1)
    jj = np.arange(LANE)[None, :]
    cc = np.arange(LANE)[:, None]
    lo = np.maximum(jj * NSA_SEL_BLOCK, cc * NSA_CMP_STRIDE)
    hi = np.minimum(jj * NSA_SEL_BLOCK + NSA_SEL_BLOCK, cc * NSA_CMP_STRIDE + NSA_CMP_BLOCK)
    ovt = np.where((jj < n_sel) & (cc < nseg - 1), np.maximum(hi - lo, 0) / NSA_CMP_STRIDE, 0.0)
    expand = (np.arange(l)[None, :] // NSA_SEL_BLOCK == np.arange(LANE)[:, None]).astype(np.float32)
    return pl.pallas_call(
        functools.partial(_nsa_prompt_body, n_kt=nqb),
        grid=(b, nqb),
        in_specs=[pl.BlockSpec((QB, NSA_HEADS * NSA_HEAD_DIM), lambda bi, i: (bi * nqb + i, 0)),
                  pl.BlockSpec((l, 6 * NSA_KV_WIDTH), lambda bi, i: (bi, 0)),
                  pl.BlockSpec((QB, LANE), lambda bi, i: (bi * nqb + i, 0)),
                  pl.BlockSpec((1, nseg, 2 * NSA_KV_WIDTH), lambda bi, i: (bi, 0, 0)),
                  pl.BlockSpec((1, NSA_HEADS, QB, LANE), lambda bi, i: (i, 0, 0, 0)),
                  pl.BlockSpec((3, NSA_HEADS, QB, LANE), lambda bi, i: (0, 0, 0, 0)),
                  pl.BlockSpec((LANE, LANE), lambda bi, i: (0, 0)),
                  pl.BlockSpec((LANE, l), lambda bi, i: (0, 0))],
        out_specs=pl.BlockSpec((QB, NSA_HEADS * NSA_HEAD_DIM), lambda bi, i: (bi * nqb + i, 0)),
        out_shape=jax.ShapeDtypeStruct((b * l, NSA_HEADS * NSA_HEAD_DIM), F32),
        scratch_shapes=[pltpu.VMEM((nqb, QB, QB), F32), pltpu.VMEM((ROWS, LANE), F32)],
        compiler_params=pltpu.CompilerParams(dimension_semantics=("arbitrary", "arbitrary"),
                                             vmem_limit_bytes=VMEM_LIMIT_BYTES),
        name="nsa_prompt",
    )(q, kv, small, cmp, cmp_bias, toe_bias, jnp.asarray(ovt.T, BF16), jnp.asarray(expand, BF16))


PAGES_PER_STEP = 16
NSEG_PAGE = PAGE_SIZE // NSA_CMP_STRIDE
SELF_RANK = 2


def _bf(x):
    return x.astype(BF16).astype(F32)


def _bias_rows(table, n_tiles, step, off, kmul):
    t = bias_tiles(table, n_tiles, step, off, kmul, qmul=0, rows=8)[:, :, 0, :]
    return t.transpose(1, 0, 2).reshape(NSA_KV_HEADS, NSA_GROUP, n_tiles * LANE)


def _nsa_sample_cmp_body(pt_ref, *refs, n_pages):
    npg = PAGES_PER_STEP
    kp, vp = refs[:npg], refs[npg:2 * npg]
    qp_ref, wk_ref, wv_ref, bias_ref, ovt_ref, oc_ref, idx_ref, fk, sk, fv, sv = refs[2 * npg:]
    g = pl.program_id(1)
    nseg = n_pages * NSEG_PAGE
    for k in range(npg):
        for src, w_ref, f_ref, s_ref in ((kp[k], wk_ref, fk, sk), (vp[k], wv_ref, fv, sv)):
            f = jnp.zeros((NSEG_PAGE, LANE), F32)
            s = jnp.zeros((NSEG_PAGE, LANE), F32)
            for r in range(NSA_CMP_STRIDE):
                rows = src[0, pl.ds(r, NSEG_PAGE, stride=NSA_CMP_STRIDE), :]
                f = f + rows * w_ref[r:r + 1, :]
                s = s + rows * w_ref[NSA_CMP_STRIDE + r:NSA_CMP_STRIDE + r + 1, :]
            row0 = pl.multiple_of((g * npg + k) * NSEG_PAGE, NSEG_PAGE)
            f_ref[pl.ds(row0, NSEG_PAGE), :] = f
            s_ref[pl.ds(row0, NSEG_PAGE), :] = s

    @pl.when(g == pl.num_programs(1) - 1)
    def _():
        rowi = lax.broadcasted_iota(jnp.int32, (nseg, LANE), 0)
        kc = jnp.where(rowi < nseg - 1, fk[...] + pltpu.roll(sk[...], nseg - 1, axis=0), 0.0).astype(BF16)
        vc = jnp.where(rowi < nseg - 1, fv[...] + pltpu.roll(sv[...], nseg - 1, axis=0), 0.0).astype(BF16)
        mc = lax.broadcasted_iota(jnp.int32, (NSA_GROUP, nseg), 1) < nseg - 1
        lane = lax.broadcasted_iota(jnp.int32, (8, LANE), 1)
        imps = []
        for h in range(NSA_KV_HEADS):
            qp = (qp_ref[0, h] * NSA_SCALE).astype(BF16)
            s = jnp.where(mc, _nt_dot(qp, kc) + bias_ref[h], NEG)
            e = jnp.exp(s - jnp.max(s, axis=-1, keepdims=True))
            p = e / jnp.sum(e, axis=-1, keepdims=True) * mc.astype(F32)
            pb = p.astype(BF16)
            oc_ref[0, h] = jnp.dot(pb, vc, preferred_element_type=F32)
            imps.append(_importance(jnp.sum(p, axis=0, keepdims=True), ovt_ref[...]))
        imp = jnp.concatenate(imps + [jnp.zeros((8 - NSA_KV_HEADS, LANE), F32)], axis=0)
        forced = (lane == 0) | (lane == LANE - 1)
        score = jnp.where(forced, NSA_FORCE, imp)
        cnt = (score < NSA_FORCE).astype(F32)
        for jp in range(LANE):
            col = score[:, jp:jp + 1]
            cnt = cnt + ((col > score) | ((col == score) & (lane > jp))).astype(F32)
        lanef = lane.astype(F32)
        idx = jnp.where(lane == SELF_RANK, float(LANE), 0.0)
        for k in range(NSA_TOPK):
            if k != SELF_RANK:
                v = jnp.sum(jnp.where(cnt == k, lanef, 0.0), axis=-1, keepdims=True)
                idx = jnp.where(lane == k, v, idx)
        idx_ref[0] = idx.astype(jnp.int32)


def nsa_sample_cmp(qp, k_pool, v_pool, e, page_table, w_ck, w_cv, table):
    db, n_pages = page_table.shape
    n_pool = k_pool.shape[1]
    p0 = n_pages * PAGE_SIZE
    nseg = n_pages * NSEG_PAGE
    npg = PAGES_PER_STEP
    assert p0 // NSA_SEL_BLOCK == LANE and n_pages % npg == 0
    kp = k_pool.reshape(-1, PAGE_SIZE, NSA_KV_WIDTH)
    vp = v_pool.reshape(-1, PAGE_SIZE, NSA_KV_WIDTH)
    bias = _bias_rows(table, nseg // LANE, -LANE * NSA_CMP_STRIDE, p0 - (NSA_CMP_BLOCK - 1), NSA_CMP_STRIDE)
    jj = np.arange(LANE)[None, :]
    cc = np.arange(nseg)[:, None]
    lo = np.maximum(jj * NSA_SEL_BLOCK, cc * NSA_CMP_STRIDE)
    hi = np.minimum(jj * NSA_SEL_BLOCK + NSA_SEL_BLOCK, cc * NSA_CMP_STRIDE + NSA_CMP_BLOCK)
    ovt = np.where(cc < nseg - 1, np.maximum(hi - lo, 0) / NSA_CMP_STRIDE, 0.0)

    def page_spec(k):
        return pl.BlockSpec((1, PAGE_SIZE, NSA_KV_WIDTH), lambda b, g, pt: (pt[b, g * npg + k] + e * n_pool, 0, 0))

    const = lambda shape: pl.BlockSpec(shape, lambda b, g, pt: (0,) * len(shape))
    grid_spec = pltpu.PrefetchScalarGridSpec(
        num_scalar_prefetch=1,
        grid=(db, n_pages // npg),
        in_specs=[page_spec(k) for k in range(npg)] * 2 + [
            pl.BlockSpec((1, NSA_KV_HEADS, NSA_GROUP, LANE), lambda b, g, pt: (b, 0, 0, 0)),
            const((NSA_CMP_BLOCK, NSA_KV_WIDTH)), const((NSA_CMP_BLOCK, NSA_KV_WIDTH)),
            const((NSA_KV_HEADS, NSA_GROUP, nseg)), const((nseg, LANE))],
        out_specs=[pl.BlockSpec((1, NSA_KV_HEADS, NSA_GROUP, LANE), lambda b, g, pt: (b, 0, 0, 0)),
                   pl.BlockSpec((1, 8, LANE), lambda b, g, pt: (b, 0, 0))],
        scratch_shapes=[pltpu.VMEM((nseg, LANE), F32)] * 4)
    tile2 = lambda w: jnp.concatenate([w, w], axis=1).astype(F32)
    return pl.pallas_call(
        functools.partial(_nsa_sample_cmp_body, n_pages=n_pages),
        grid_spec=grid_spec,
        out_shape=[jax.ShapeDtypeStruct((db, NSA_KV_HEADS, NSA_GROUP, LANE), F32),
                   jax.ShapeDtypeStruct((db, 8, LANE), jnp.int32)],
        compiler_params=pltpu.CompilerParams(dimension_semantics=("arbitrary", "arbitrary")),
        name="nsa_sample_cmp",
    )(page_table, *([kp] * npg), *([vp] * npg), qp, tile2(w_ck), tile2(w_cv), bias, jnp.asarray(ovt, BF16))


def _nsa_sample_attn_body(pt_ref, idx_ref, *refs, nbuf):
    nb = NSA_KV_HEADS * NSA_TOPK
    kb, vb = refs[:nb], refs[nb:2 * nb]
    (qp_ref, new_ref, kwin_ref, vwin_ref, oc_ref, gate_ref, bnear_ref, bwin_ref, bends_ref,
     o_ref, kwo_ref, vwo_ref) = refs[2 * nb:]
    b = pl.program_id(0)
    new = new_ref[0]
    ks_new, vs_new, kw_new, vw_new = (new[:, c * LANE:(c + 1) * LANE] for c in (2, 3, 4, 5))
    kwin = kwin_ref[0]
    vwin = vwin_ref[0]
    wlane = lax.broadcasted_iota(jnp.int32, (NSA_GROUP, nbuf), 1)
    half = NSA_SEL_BLOCK
    bpp = PAGE_SIZE // NSA_SEL_BLOCK
    for h in range(NSA_KV_HEADS):
        qp = (qp_ref[0, h] * NSA_SCALE).astype(BF16)
        qpf = qp.astype(F32)
        b_self = bends_ref[h][:, 0:1]
        b_far = bends_ref[h][:, 1:2]
        near = bnear_ref[h]
        scores, slots = [], []
        for k in range(NSA_TOPK):
            if k == SELF_RANK:
                continue
            j = idx_ref[b, h * NSA_TOPK + k]
            bias = jnp.where(j == LANE - 1, near[:, half:], jnp.where(j == LANE - 2, near[:, :half], b_far))
            rows = pl.ds(pl.multiple_of((jnp.minimum(j, LANE - 1) % bpp) * half, half), half)
            scores.append(_nt_dot(qp, kb[h * NSA_TOPK + k][0, rows, :].astype(BF16)) + bias)
            slots.append((h * NSA_TOPK + k, rows))
        s_self = jnp.sum(qpf * _bf(ks_new), axis=-1, keepdims=True) + b_self
        m = s_self
        for s in scores:
            m = jnp.maximum(m, jnp.max(s, axis=-1, keepdims=True))
        p_self = jnp.exp(s_self - m)
        l = p_self
        acc = _bf(p_self) * _bf(vs_new)
        for s, (slot, rows) in zip(scores, slots):
            p = jnp.exp(s - m)
            l = l + jnp.sum(p, axis=-1, keepdims=True)
            acc = acc + jnp.dot(p.astype(BF16), vb[slot][0, rows, :].astype(BF16), preferred_element_type=F32)
        o_s = acc / l
        s = jnp.where(wlane >= 1, _nt_dot(qp, kwin.astype(BF16)) + bwin_ref[h], NEG)
        s_self = jnp.sum(qpf * _bf(kw_new), axis=-1, keepdims=True) + b_self
        m = jnp.maximum(jnp.max(s, axis=-1, keepdims=True), s_self)
        p = jnp.exp(s - m)
        p_self = jnp.exp(s_self - m)
        l = jnp.sum(p, axis=-1, keepdims=True) + p_self
        o_w = (jnp.dot(p.astype(BF16), vwin.astype(BF16), preferred_element_type=F32)
               + _bf(p_self) * _bf(vw_new)) / l
        gates = jax.nn.sigmoid(gate_ref[0, h])
        o_h = gates[:, 0:1] * oc_ref[0, h] + gates[:, 1:2] * o_s + gates[:, 2:3] * o_w
        o_rot = pltpu.roll(o_h, NSA_HEAD_DIM, axis=1)
        low = lax.broadcasted_iota(jnp.int32, (1, LANE), 1) < NSA_HEAD_DIM
        for pr in range(NSA_GROUP // 2):
            a = (o_h if h == 0 else o_rot)[2 * pr:2 * pr + 1]
            b2 = (o_rot if h == 0 else o_h)[2 * pr + 1:2 * pr + 2]
            c0 = (h * NSA_GROUP + 2 * pr) * NSA_HEAD_DIM
            o_ref[0, :, c0:c0 + LANE] = jnp.where(low, a, b2)
    rowi = lax.broadcasted_iota(jnp.int32, (nbuf, LANE), 0)
    kwo_ref[0] = jnp.where(rowi == nbuf - 1, kw_new, pltpu.roll(kwin, nbuf - 1, axis=0))
    vwo_ref[0] = jnp.where(rowi == nbuf - 1, vw_new, pltpu.roll(vwin, nbuf - 1, axis=0))


def nsa_sample_attn(qp, kv_new, idx, o_c, gates_raw, k_pool, v_pool, e, k_win, v_win, page_table, table):
    db, n_pages = page_table.shape
    n_pool = k_pool.shape[1]
    p0 = n_pages * PAGE_SIZE
    nbuf = k_win.shape[1]
    nb = NSA_KV_HEADS * NSA_TOPK
    bpp = PAGE_SIZE // NSA_SEL_BLOCK
    assert nbuf == NSA_WINDOW and nbuf % LANE == 0 and p0 >= nbuf
    kp = k_pool.reshape(-1, PAGE_SIZE, NSA_KV_WIDTH)
    vp = v_pool.reshape(-1, PAGE_SIZE, NSA_KV_WIDTH)
    bnear = _bias_rows(table, 1, 0, 2 * NSA_SEL_BLOCK, 1)
    bwin = _bias_rows(table, nbuf // LANE, -LANE, nbuf, 1)
    tb = table.astype(F32).reshape(REL_BUCKETS, NSA_KV_HEADS, NSA_GROUP)
    bends = jnp.pad(jnp.stack([tb[0], tb[REL_BUCKETS - 1]], axis=-1), ((0, 0), (0, 0), (0, LANE - 2)))

    def blk_spec(slot):
        def imap(b, pt, ix):
            j = jnp.minimum(ix[b, slot], LANE - 1)
            return (pt[b, j // bpp] + e * n_pool, 0, 0)
        return pl.BlockSpec((1, PAGE_SIZE, NSA_KV_WIDTH), imap)

    per_b = lambda shape: pl.BlockSpec((1,) + shape, lambda b, pt, ix: (b,) + (0,) * len(shape))
    const = lambda shape: pl.BlockSpec(shape, lambda b, pt, ix: (0,) * len(shape))
    head_shape = (NSA_KV_HEADS, NSA_GROUP, LANE)
    grid_spec = pltpu.PrefetchScalarGridSpec(
        num_scalar_prefetch=2,
        grid=(db,),
        in_specs=[blk_spec(s) for s in range(nb)] * 2 + [
            per_b(head_shape), per_b((1, 6 * NSA_KV_WIDTH)), per_b((nbuf, NSA_KV_WIDTH)), per_b((nbuf, NSA_KV_WIDTH)),
            per_b(head_shape), per_b(head_shape), const(head_shape),
            const((NSA_KV_HEADS, NSA_GROUP, nbuf)), const(head_shape)],
        out_specs=[per_b((1, NSA_HEADS * NSA_HEAD_DIM)), per_b((nbuf, NSA_KV_WIDTH)), per_b((nbuf, NSA_KV_WIDTH))])
    return pl.pallas_call(
        functools.partial(_nsa_sample_attn_body, nbuf=nbuf),
        grid_spec=grid_spec,
        out_shape=[jax.ShapeDtypeStruct((db, 1, NSA_HEADS * NSA_HEAD_DIM), F32),
                   jax.ShapeDtypeStruct((db, nbuf, NSA_KV_WIDTH), F32),
                   jax.ShapeDtypeStruct((db, nbuf, NSA_KV_WIDTH), F32)],
        compiler_params=pltpu.CompilerParams(dimension_semantics=("arbitrary",)),
        name="nsa_sample_attn",
    )(page_table, idx, *([kp] * nb), *([vp] * nb), qp, kv_new.reshape(db, 1, -1),
      k_win.reshape(db, nbuf, NSA_KV_WIDTH), v_win.reshape(db, nbuf, NSA_KV_WIDTH), o_c, gates_raw,
      bnear, bwin, bends)


def nsa_sample_mixer(q, kv, small, e, k_cmp_pool, v_cmp_pool, k_slc_pool, v_slc_pool, k_win, v_win,
                     page_table, w_ck, w_cv, table):
    db = q.shape[0]
    qh = q.reshape(db, NSA_KV_HEADS, NSA_GROUP, NSA_HEAD_DIM)
    zeros = jnp.zeros_like(qh[:, 0])
    qp = jnp.stack([jnp.concatenate([qh[:, 0], zeros], axis=-1), jnp.concatenate([zeros, qh[:, 1]], axis=-1)], axis=1)
    o_c, idx = nsa_sample_cmp(qp, k_cmp_pool, v_cmp_pool, e, page_table, w_ck, w_cv, table)
    idx = idx[:, :NSA_KV_HEADS, :NSA_TOPK].reshape(db, NSA_KV_HEADS * NSA_TOPK)
    g = small[:, SMALL_GATE:SMALL_GATE + 3 * NSA_HEADS].reshape(db, 3, NSA_KV_HEADS, NSA_GROUP).transpose(0, 2, 3, 1)
    g = jnp.pad(g, ((0, 0), (0, 0), (0, 0), (0, LANE - 3)))
    y, kw_n, vw_n = nsa_sample_attn(qp, kv, idx, o_c, g, k_slc_pool, v_slc_pool, e, k_win, v_win, page_table, table)
    shape = (db, -1, NSA_KV_HEADS, NSA_HEAD_DIM)
    return y.reshape(db, NSA_HEADS * NSA_HEAD_DIM), kw_n.reshape(shape), vw_n.reshape(shape)


def _bmm_body(a_ref, b_ref, o_ref):
    o_ref[0] = jnp.dot(a_ref[0].astype(BF16), b_ref[0].astype(BF16), preferred_element_type=F32)


def batched_matmul(a, b):
    hh, m, k = a.shape
    n = b.shape[2]
    return pl.pallas_call(
        _bmm_body,
        grid=(hh,),
        in_specs=[pl.BlockSpec((1, m, k), lambda i: (i, 0, 0)), pl.BlockSpec((1, k, n), lambda i: (i, 0, 0))],
        out_specs=pl.BlockSpec((1, m, n), lambda i: (i, 0, 0)),
        out_shape=jax.ShapeDtypeStruct((hh, m, n), F32),
        compiler_params=pltpu.CompilerParams(dimension_semantics=("arbitrary",)),
        name="batched_matmul",
    )(a, b)


def _mla_sample_body(pt_ref, *refs):
    npg = PAGES_PER_STEP
    cp, rp = refs[:npg], refs[npg:2 * npg]
    qlat_ref, qrope_ref, cnew_ref, krnew_ref, o_ref, m_ref, l_ref, acc_ref = refs[2 * npg:]
    g = pl.program_id(1)

    @pl.when(g == 0)
    def _():
        m_ref[...] = jnp.full_like(m_ref, NEG)
        l_ref[...] = jnp.zeros_like(l_ref)
        acc_ref[...] = jnp.zeros_like(acc_ref)

    ql = qlat_ref[0].astype(BF16)
    qr = qrope_ref[0].astype(BF16)
    c_all = jnp.concatenate([r[0] for r in cp], axis=0).astype(BF16)
    r_all = jnp.concatenate([r[0] for r in rp], axis=0).astype(BF16)
    s = (_nt_dot(ql, c_all) + _nt_dot(qr, r_all)) * MLA_SCALE
    m_old = m_ref[...]
    m_new = jnp.maximum(m_old, jnp.max(s, axis=-1, keepdims=True))
    alpha = jnp.exp(m_old - m_new)
    p = jnp.exp(s - m_new)
    l_new = alpha * l_ref[...] + jnp.sum(p, axis=-1, keepdims=True)
    acc = alpha * acc_ref[...] + jnp.dot(p.astype(BF16), c_all, preferred_element_type=F32)
    m_ref[...] = m_new
    l_ref[...] = l_new
    acc_ref[...] = acc

    @pl.when(g == pl.num_programs(1) - 1)
    def _():
        cn = _bf(cnew_ref[0])
        kn = _bf(krnew_ref[0])
        s_self = (jnp.sum(ql.astype(F32) * cn, axis=-1, keepdims=True)
                  + jnp.sum(qr.astype(F32) * kn, axis=-1, keepdims=True)) * MLA_SCALE
        m2 = jnp.maximum(m_new, s_self)
        a = jnp.exp(m_new - m2)
        p_self = jnp.exp(s_self - m2)
        o_ref[0] = (a * acc + _bf(p_self) * cn) / (a * l_new + p_self)


def mla_sample_attention(q_lat, q_rope, c_new, kr_new, lat_pool, kr_pool, o, page_table):
    db, n_pages = page_table.shape
    n_pool = lat_pool.shape[1]
    npg = PAGES_PER_STEP
    assert n_pages % npg == 0
    cp = lat_pool.reshape(-1, PAGE_SIZE, MLA_KV_RANK)
    rp = kr_pool.reshape(-1, PAGE_SIZE, MLA_ROPE)

    def page_spec(k, width):
        return pl.BlockSpec((1, PAGE_SIZE, width), lambda b, g, pt: (pt[b, g * npg + k] + o * n_pool, 0, 0))

    per_b = lambda shape: pl.BlockSpec((1,) + shape, lambda b, g, pt: (b,) + (0,) * len(shape))
    grid_spec = pltpu.PrefetchScalarGridSpec(
        num_scalar_prefetch=1,
        grid=(db, n_pages // npg),
        in_specs=[page_spec(k, MLA_KV_RANK) for k in range(npg)] + [page_spec(k, MLA_ROPE) for k in range(npg)] + [
            per_b((MLA_HEADS, MLA_KV_RANK)), per_b((MLA_HEADS, MLA_ROPE)), per_b((1, MLA_KV_RANK)), per_b((1, MLA_ROPE))],
        out_specs=per_b((MLA_HEADS, MLA_KV_RANK)),
        scratch_shapes=[pltpu.VMEM((MLA_HEADS, 1), F32), pltpu.VMEM((MLA_HEADS, 1), F32),
                        pltpu.VMEM((MLA_HEADS, MLA_KV_RANK), F32)])
    return pl.pallas_call(
        _mla_sample_body,
        grid_spec=grid_spec,
        out_shape=jax.ShapeDtypeStruct((db, MLA_HEADS, MLA_KV_RANK), F32),
        compiler_params=pltpu.CompilerParams(dimension_semantics=("arbitrary", "arbitrary")),
        name="mla_sample",
    )(page_table, *([cp] * npg), *([rp] * npg), q_lat, q_rope, c_new.reshape(db, 1, -1), kr_new.reshape(db, 1, -1))


def mla_sample_mixer(q_nope, q_rope, c, k_rope, lat_pool, kr_pool, o_idx, page_table, w_kvb):
    db = c.shape[0]
    w = w_kvb.reshape(MLA_KV_RANK, MLA_HEADS, MLA_NOPE + MLA_V).transpose(1, 0, 2)
    q_lat = batched_matmul(q_nope.transpose(1, 0, 2), w[..., :MLA_NOPE].transpose(0, 2, 1)).transpose(1, 0, 2)
    o_lat = mla_sample_attention(q_lat, q_rope, c, k_rope, lat_pool, kr_pool, o_idx, page_table)
    out = batched_matmul(o_lat.transpose(1, 0, 2), w[..., MLA_NOPE:])
    return out.transpose(1, 0, 2).reshape(db, MLA_HEADS * MLA_V)


OD_GROUPS = (GLA_HEADS * GLA_DK, GLA_HEADS * GLA_DK, GLA_HEADS * GLA_DV, GLA_HEADS * GLA_DV, MLA_Q_RANK, MLA_KV_RANK, LANE)
MLA_PREP_ROWS = 128
MLA_TQ = 256
MLA_HEAD_GROUP = 4


def odd_weight(w_in):
    gq, gk, gv, g1, r, qa, kva, kr = split_cols(w_in, OD_SPLITS)
    z = lambda n: jnp.zeros((w_in.shape[0], n), w_in.dtype)
    small = jnp.concatenate([g1, z(MLA_NOPE - GLA_GATE_RANK), kr, z(LANE - MLA_NOPE - MLA_ROPE)], axis=1)
    return jnp.concatenate([gq, gk, gv, r, qa, kva, small], axis=1)


def _head_blocks(w, widths, keep):
    k = w.shape[0]
    w = w.reshape(k, MLA_HEADS, sum(widths))[:, :, keep[0]:keep[1]]
    return jnp.pad(w, ((0, 0), (0, 0), (0, LANE - (keep[1] - keep[0])))).reshape(k, MLA_HEADS * LANE)


def rope_tables(pos):
    half = MLA_ROPE // 2
    freqs = ROPE_THETA ** (-jnp.arange(half, dtype=F32) / half)
    ang = pos.astype(F32)[:, None] * freqs
    cos, sin = jnp.cos(ang), jnp.sin(ang)
    n = pos.shape[0]
    z = lambda w: jnp.zeros((n, w), F32)
    tail = LANE - MLA_NOPE - MLA_ROPE
    c = jnp.concatenate([jnp.ones((n, MLA_NOPE), F32), cos, cos, z(tail)], axis=1)
    s1 = jnp.concatenate([z(MLA_NOPE), -sin, z(half), z(tail)], axis=1)
    s2 = jnp.concatenate([z(MLA_NOPE), z(half), sin, z(tail)], axis=1)
    return c, s1, s2


def _rope_block(x, c, s1, s2):
    half = MLA_ROPE // 2
    return x * c + pltpu.roll(x, LANE - half, axis=1) * s1 + pltpu.roll(x, half, axis=1) * s2


def _mla_prep_body(qa_ref, kva_ref, sm_ref, qg_ref, kg_ref, wq_ref, wk_ref, wv_ref, c_ref, s1_ref, s2_ref,
                   q_out, c_out, k_out, v_out, kr_out):
    norm = lambda x, g: x * lax.rsqrt(jnp.mean(x * x, axis=-1, keepdims=True) + EPS) * g
    c, s1, s2 = c_ref[...], s1_ref[...], s2_ref[...]
    lane = lax.broadcasted_iota(jnp.int32, c.shape, 1)
    q = jnp.dot(norm(qa_ref[...], qg_ref[...]).astype(BF16), wq_ref[...], preferred_element_type=F32)
    lat = norm(kva_ref[...], kg_ref[...])
    c_out[...] = lat
    latb = lat.astype(BF16)
    kn = jnp.dot(latb, wk_ref[...], preferred_element_type=F32)
    v_out[...] = jnp.dot(latb, wv_ref[...], preferred_element_type=F32).astype(BF16)
    kr = jnp.where(lane >= MLA_NOPE, _rope_block(sm_ref[...], c, s1, s2), 0.0)
    kr_out[...] = kr
    for h in range(MLA_HEADS):
        cols = slice(h * LANE, (h + 1) * LANE)
        q_out[:, cols] = _rope_block(q[:, cols], c, s1, s2)
        k_out[:, cols] = (kn[:, cols] + kr).astype(BF16)


def mla_prep(qa, kva, small, q_norm, w_qb, kv_norm, w_kvb, tables, pos_block):
    m = qa.shape[0]
    tm = MLA_PREP_ROWS
    wq = _head_blocks(w_qb, (MLA_NOPE, MLA_ROPE), (0, MLA_NOPE + MLA_ROPE)).astype(BF16)
    wk = _head_blocks(w_kvb, (MLA_NOPE, MLA_V), (0, MLA_NOPE)).astype(BF16)
    wv = _head_blocks(w_kvb, (MLA_NOPE, MLA_V), (MLA_NOPE, MLA_NOPE + MLA_V)).astype(BF16)
    wide = MLA_HEADS * LANE
    rows = lambda w: pl.BlockSpec((tm, w), lambda i: (i, 0))
    const = lambda a: pl.BlockSpec(a.shape, lambda i: (0, 0))
    tab = pl.BlockSpec((tm, LANE), lambda i: (pos_block(i), 0))
    qg = q_norm.reshape(1, -1).astype(F32)
    kg = kv_norm.reshape(1, -1).astype(F32)
    return pl.pallas_call(
        _mla_prep_body,
        grid=(m // tm,),
        in_specs=[rows(MLA_Q_RANK), rows(MLA_KV_RANK), rows(LANE), const(qg), const(kg), const(wq), const(wk), const(wv),
                  tab, tab, tab],
        out_specs=[rows(wide), rows(MLA_KV_RANK), rows(wide), rows(wide), rows(LANE)],
        out_shape=[jax.ShapeDtypeStruct((m, wide), F32), jax.ShapeDtypeStruct((m, MLA_KV_RANK), F32),
                   jax.ShapeDtypeStruct((m, wide), BF16), jax.ShapeDtypeStruct((m, wide), BF16),
                   jax.ShapeDtypeStruct((m, LANE), F32)],
        compiler_params=pltpu.CompilerParams(dimension_semantics=("arbitrary",), vmem_limit_bytes=VMEM_LIMIT_BYTES),
        name="mla_prep",
    )(qa, kva, small, qg, kg, wq, wk, wv, *tables)


def _mla_prompt_body(q_ref, k_ref, v_ref, o_ref):
    tq = q_ref.shape[0]
    qi = pl.program_id(1)
    row = lax.broadcasted_iota(jnp.int32, (tq, tq), 0)
    col = lax.broadcasted_iota(jnp.int32, (tq, tq), 1)
    low = lax.broadcasted_iota(jnp.int32, (tq, LANE), 1) < MLA_V
    outs = []
    for h0 in range(0, MLA_HEADS, MLA_HEAD_GROUP):
        heads = range(h0, h0 + MLA_HEAD_GROUP)
        hcols = [slice(h * LANE, (h + 1) * LANE) for h in heads]
        qs = [(q_ref[:, c] * MLA_SCALE).astype(BF16) for c in hcols]

        def body(kt, carry):
            k0 = pl.multiple_of(kt * tq, tq)
            vis = (kt - qi) * tq + col <= row
            ss = [jnp.where(vis, _nt_dot(q, k_ref[pl.ds(k0, tq), c]), NEG) for q, c in zip(qs, hcols)]
            m_new = [jnp.maximum(st[0], jnp.max(s, axis=-1, keepdims=True)) for st, s in zip(carry, ss)]
            alpha = [jnp.exp(st[0] - mn) for st, mn in zip(carry, m_new)]
            ps = [jnp.exp(s - mn) for s, mn in zip(ss, m_new)]
            l_new = [a * st[1] + jnp.sum(p, axis=-1, keepdims=True) for a, st, p in zip(alpha, carry, ps)]
            pv = [jnp.dot(p.astype(BF16), v_ref[pl.ds(k0, tq), c], preferred_element_type=F32)
                  for p, c in zip(ps, hcols)]
            return tuple((mn, ln, a * st[2] + x) for mn, ln, a, st, x in zip(m_new, l_new, alpha, carry, pv))

        init = tuple((jnp.full((tq, 1), NEG, F32), jnp.zeros((tq, 1), F32), jnp.zeros((tq, LANE), F32))
                     for _ in heads)
        final = lax.fori_loop(0, qi + 1, body, init)
        outs.extend(acc / l for _, l, acc in final)
    for pr in range(MLA_HEADS // 2):
        o_ref[:, pr * LANE:(pr + 1) * LANE] = jnp.where(low, outs[2 * pr], pltpu.roll(outs[2 * pr + 1], MLA_V, axis=1))


def mla_prompt_attention(q_rot, k_full, v_pad, b, l):
    tq = MLA_TQ
    nq = l // tq
    wide = MLA_HEADS * LANE
    return pl.pallas_call(
        _mla_prompt_body,
        grid=(b, nq),
        in_specs=[pl.BlockSpec((tq, wide), lambda bi, i: (bi * nq + i, 0)),
                  pl.BlockSpec((l, wide), lambda bi, i: (bi, 0)),
                  pl.BlockSpec((l, wide), lambda bi, i: (bi, 0))],
        out_specs=pl.BlockSpec((tq, MLA_HEADS * MLA_V), lambda bi, i: (bi * nq + i, 0)),
        out_shape=jax.ShapeDtypeStruct((b * l, MLA_HEADS * MLA_V), F32),
        compiler_params=pltpu.CompilerParams(dimension_semantics=("arbitrary", "arbitrary"),
                                             vmem_limit_bytes=VMEM_LIMIT_BYTES),
        name="mla_prompt",
    )(q_rot, k_full, v_pad)


SSD_PAIRS = SSD_HEADS // 2
PAIR_PER_GROUP = SSD_PAIRS // SSD_GROUPS
CONV_PAD = 8


def _exact_dot(a, b):
    return jnp.dot(a, b, preferred_element_type=F32, precision=lax.Precision.HIGHEST)


def _softplus(x):
    u = jnp.exp(-jnp.abs(x))
    w = 1.0 + u
    return jnp.maximum(x, 0.0) + jnp.where(w == 1.0, u, jnp.log(w) * (u / (w - 1.0)))


def _silu(x):
    return x * jax.nn.sigmoid(x)


def _group_rmsnorm(y, g):
    gw = SSD_INNER // SSD_GROUPS
    outs = []
    for gi in range(SSD_GROUPS):
        yg = y[:, gi * gw:(gi + 1) * gw]
        outs.append(yg * lax.rsqrt(jnp.mean(yg * yg, axis=-1, keepdims=True) + EPS))
    return jnp.concatenate(outs, axis=1) * g


def _ssd_prompt_body(z_ref, xbc_ref, sm_ref, cw_ref, cb_ref, dtb_ref, alog_ref, d_ref, g_ref, ex_ref,
                     y_ref, st_ref, cv_ref, buf, state):
    c = pl.program_id(1)
    nc = pl.num_programs(1)
    t = SSD_CHUNK

    @pl.when(c == 0)
    def _():
        buf[0:CONV_PAD, :] = jnp.zeros((CONV_PAD, SSD_CONV_CH), F32)
        state[...] = jnp.zeros_like(state)

    buf[CONV_PAD:CONV_PAD + t, :] = xbc_ref[...]
    conv = cb_ref[...] + cw_ref[SSD_CONV - 1:SSD_CONV, :] * xbc_ref[...]
    for k in range(SSD_CONV - 1):
        conv = conv + cw_ref[k:k + 1, :] * buf[pl.ds(CONV_PAD - (SSD_CONV - 1) + k, t), :]
    buf[0:CONV_PAD, :] = buf[t:t + CONV_PAD, :]
    xc = _silu(conv)
    xs = xc[:, :SSD_INNER]
    bm = xc[:, SSD_INNER:SSD_INNER + SSD_GROUPS * SSD_STATE].astype(BF16)
    cm = xc[:, SSD_INNER + SSD_GROUPS * SSD_STATE:].astype(BF16)

    dt = _softplus(sm_ref[...] + dtb_ref[...])
    da = dt * (-jnp.exp(alog_ref[...]))
    row = lax.broadcasted_iota(jnp.int32, (t, t), 0)
    col = lax.broadcasted_iota(jnp.int32, (t, t), 1)
    tril = row >= col
    a_cs = _exact_dot(tril.astype(F32), da)
    a_cs_t = a_cs.T
    ex = ex_ref[...]
    dt_x = _exact_dot(dt, ex)
    acs_x = _exact_dot(a_cs, ex)
    last_x = acs_x[t - 1:t, :]
    xd = xs * dt_x
    xdw = xd * jnp.exp(last_x - acs_x)
    xdb = xd.astype(BF16)
    e_in = jnp.exp(acs_x)
    low = lax.broadcasted_iota(jnp.int32, (t, LANE), 1) < SSD_HEADDIM
    zero = jnp.zeros((t, LANE), BF16)

    y_parts = []
    for j in range(SSD_PAIRS):
        g = j // PAIR_PER_GROUP
        cg = cm[:, g * SSD_STATE:(g + 1) * SSD_STATE]
        bg = bm[:, g * SSD_STATE:(g + 1) * SSD_STATE]
        cb = _nt_dot(cg, bg)
        cols = slice(j * LANE, (j + 1) * LANE)
        xp = xdb[:, cols]
        y = jnp.zeros((t, LANE), F32)
        for hh in range(2):
            h = 2 * j + hh
            decay = jnp.where(tril, jnp.exp(a_cs[:, h:h + 1] - a_cs_t[h:h + 1, :]), 0.0)
            xh = jnp.where(low if hh == 0 else jnp.logical_not(low), xp, zero)
            y = y + jnp.dot((cb * decay).astype(BF16), xh, preferred_element_type=F32)
        s_old = state[j]
        y = y + _nt_dot(cg, s_old.astype(BF16)) * e_in[:, cols]
        dec_rows = jnp.broadcast_to(jnp.exp(last_x[:, cols]), (LANE, LANE)).T
        state[j] = dec_rows * s_old + jnp.dot(xdw[:, cols].T.astype(BF16), bg, preferred_element_type=F32)
        y_parts.append(y)
    y = jnp.concatenate(y_parts, axis=1) + xs * d_ref[...]
    y_ref[...] = _group_rmsnorm(y * _silu(z_ref[...]), g_ref[...])

    @pl.when(c == nc - 1)
    def _():
        st_ref[0] = state[...]
        cv_ref[0] = xbc_ref[t - (SSD_CONV - 1):t, :]


def _head_expand():
    return jnp.asarray(np.arange(LANE)[:, None] == (np.arange(SSD_INNER)[None, :] // SSD_HEADDIM), F32)


def _ssd_params(conv_w, conv_b, dt_bias, a_log, d_skip, norm_g):
    pad = lambda v: jnp.pad(v.astype(F32), (0, LANE - SSD_HEADS)).reshape(1, LANE)
    return (conv_w.astype(F32), conv_b.astype(F32).reshape(1, -1), pad(dt_bias), pad(a_log),
            jnp.repeat(d_skip.astype(F32), SSD_HEADDIM).reshape(1, -1), norm_g.astype(F32).reshape(1, -1), _head_expand())


def ssd_prompt(z, xbc, small, b, l, conv_w, conv_b, dt_bias, a_log, d_skip, norm_g):
    t = SSD_CHUNK
    nc = l // t
    params = _ssd_params(conv_w, conv_b, dt_bias, a_log, d_skip, norm_g)
    rows = lambda w: pl.BlockSpec((t, w), lambda bi, c: (bi * nc + c, 0))
    const = lambda a: pl.BlockSpec(a.shape, lambda bi, c: (0,) * a.ndim)
    y, st, cv = pl.pallas_call(
        _ssd_prompt_body,
        grid=(b, nc),
        in_specs=[rows(SSD_INNER), rows(SSD_CONV_CH), rows(LANE)] + [const(p) for p in params],
        out_specs=[rows(SSD_INNER),
                   pl.BlockSpec((1, SSD_PAIRS, LANE, SSD_STATE), lambda bi, c: (bi, 0, 0, 0)),
                   pl.BlockSpec((1, SSD_CONV - 1, SSD_CONV_CH), lambda bi, c: (bi, 0, 0))],
        out_shape=[jax.ShapeDtypeStruct((b * l, SSD_INNER), F32),
                   jax.ShapeDtypeStruct((b, SSD_PAIRS, LANE, SSD_STATE), F32),
                   jax.ShapeDtypeStruct((b, SSD_CONV - 1, SSD_CONV_CH), F32)],
        scratch_shapes=[pltpu.VMEM((CONV_PAD + t, SSD_CONV_CH), F32), pltpu.VMEM((SSD_PAIRS, LANE, SSD_STATE), F32)],
        compiler_params=pltpu.CompilerParams(dimension_semantics=("arbitrary", "arbitrary"),
                                             vmem_limit_bytes=VMEM_LIMIT_BYTES),
        name="ssd_prompt",
    )(z, xbc, small, *params)
    return y, st.reshape(b, SSD_HEADS, SSD_HEADDIM, SSD_STATE), cv


def _ssd_sample_body(z_ref, xbc_ref, sm_ref, cs_ref, st_ref, cw_ref, cb_ref, dtb_ref, alog_ref, d_ref, g_ref, ex_ref,
                     y_ref, sto_ref, cvo_ref):
    x_new = xbc_ref[0]
    cs = cs_ref[0]
    conv = cb_ref[...] + cw_ref[SSD_CONV - 1:SSD_CONV, :] * x_new
    for k in range(SSD_CONV - 1):
        conv = conv + cw_ref[k:k + 1, :] * cs[k:k + 1, :]
    cvo_ref[0] = jnp.concatenate([cs[1:], x_new], axis=0)
    xc = _silu(conv)
    xs = xc[:, :SSD_INNER]
    bm = xc[:, SSD_INNER:SSD_INNER + SSD_GROUPS * SSD_STATE]
    cm = xc[:, SSD_INNER + SSD_GROUPS * SSD_STATE:]
    dt = jnp.broadcast_to(_softplus(sm_ref[0] + dtb_ref[...]), (8, LANE))
    da = dt * (-jnp.exp(alog_ref[...]))
    ex = ex_ref[...]
    xd = xs * _exact_dot(dt, ex)[0:1]
    dec_x = jnp.exp(_exact_dot(da, ex)[0:1])
    y_parts = []
    for j in range(SSD_PAIRS):
        g = j // PAIR_PER_GROUP
        cols = slice(j * LANE, (j + 1) * LANE)
        bg = bm[:, g * SSD_STATE:(g + 1) * SSD_STATE]
        cg = jnp.broadcast_to(cm[:, g * SSD_STATE:(g + 1) * SSD_STATE], (8, SSD_STATE)).astype(BF16)
        dec_rows = jnp.broadcast_to(dec_x[:, cols], (LANE, LANE)).T
        xd_rows = jnp.broadcast_to(xd[:, cols], (LANE, LANE)).T
        s_new = dec_rows * st_ref[0, j] + xd_rows * bg
        sto_ref[0, j] = s_new
        y_parts.append(_nt_dot(cg, s_new.astype(BF16))[0:1])
    y = jnp.concatenate(y_parts, axis=1) + xs * d_ref[...]
    y_ref[0] = _group_rmsnorm(y * _silu(z_ref[0]), g_ref[...])


def ssd_sample(z, xbc, small, conv_state, ssm_state, conv_w, conv_b, dt_bias, a_log, d_skip, norm_g):
    db = z.shape[0]
    params = _ssd_params(conv_w, conv_b, dt_bias, a_log, d_skip, norm_g)
    per_b = lambda shape: pl.BlockSpec((1,) + shape, lambda i: (i,) + (0,) * len(shape))
    const = lambda a: pl.BlockSpec(a.shape, lambda i: (0,) * a.ndim)
    st_shape = (SSD_PAIRS, LANE, SSD_STATE)
    y, st, cv = pl.pallas_call(
        _ssd_sample_body,
        grid=(db,),
        in_specs=[per_b((1, SSD_INNER)), per_b((1, SSD_CONV_CH)), per_b((1, LANE)), per_b((SSD_CONV - 1, SSD_CONV_CH)),
                  per_b(st_shape)] + [const(p) for p in params],
        out_specs=[per_b((1, SSD_INNER)), per_b(st_shape), per_b((SSD_CONV - 1, SSD_CONV_CH))],
        out_shape=[jax.ShapeDtypeStruct((db, 1, SSD_INNER), F32), jax.ShapeDtypeStruct((db,) + st_shape, F32),
                   jax.ShapeDtypeStruct((db, SSD_CONV - 1, SSD_CONV_CH), F32)],
        compiler_params=pltpu.CompilerParams(dimension_semantics=("arbitrary",)),
        name="ssd_sample",
    )(z.reshape(db, 1, -1), xbc.reshape(db, 1, -1), small.reshape(db, 1, -1), conv_state,
      ssm_state.reshape((db,) + st_shape), *params)
    return y.reshape(db, SSD_INNER), st.reshape(db, SSD_HEADS, SSD_HEADDIM, SSD_STATE), cv


GLA_BLOCK = 128
GLA_INNER = GLA_HEADS * GLA_DV


def _log_gate(small, w2_ref, bg_ref):
    logits = jnp.dot(small.astype(BF16), w2_ref[...], preferred_element_type=F32) + bg_ref[...]
    return -_softplus(-logits) / GLA_TAU


def _head_rmsnorm_gate(o_heads, g_ref, r):
    outs = [o * lax.rsqrt(jnp.mean(o * o, axis=-1, keepdims=True) + EPS) for o in o_heads]
    return jnp.concatenate(outs, axis=1) * g_ref[...] * _silu(r)


def _gla_prompt_body(q_ref, k_ref, v_ref, r_ref, sm_ref, w2_ref, bg_ref, g_ref, y_ref, st_ref, state):
    blk = pl.program_id(1)
    t = GLA_BLOCK

    @pl.when(blk == 0)
    def _():
        state[...] = jnp.zeros_like(state)

    la = _log_gate(sm_ref[...], w2_ref, bg_ref)
    row = lax.broadcasted_iota(jnp.int32, (t, t), 0)
    col = lax.broadcasted_iota(jnp.int32, (t, t), 1)
    same = (row // GLA_CHUNK) == (col // GLA_CHUNK)
    causal = same & (row >= col)
    bc = _exact_dot(causal.astype(F32), la)
    is_last = col == (row // GLA_CHUNK) * GLA_CHUNK + GLA_CHUNK - 1
    bl = _exact_dot(is_last.astype(F32), bc)
    qt = q_ref[...] * (GLA_DK ** -0.5) * jnp.exp(bc)
    kt = k_ref[...] * jnp.exp(-bc)
    kd = k_ref[...] * jnp.exp(bl - bc)
    heads = range(GLA_HEADS)
    kcs = [slice(h * GLA_DK, (h + 1) * GLA_DK) for h in heads]
    vbs = [v_ref[:, h * GLA_DV:(h + 1) * GLA_DV].astype(BF16) for h in heads]
    qbs = [qt[:, kc].astype(BF16) for kc in kcs]
    o_intra = []
    for h in heads:
        att = jnp.where(causal, _nt_dot(qbs[h], kt[:, kcs[h]].astype(BF16)), 0.0)
        o_intra.append(jnp.dot(att.astype(BF16), vbs[h], preferred_element_type=F32))
    kd_t = [kd[:, kc].T for kc in kcs]
    dec_t = [jnp.exp(bl[:, kc]).T for kc in kcs]
    ss = [state[h] for h in heads]
    inter = [[] for _ in heads]
    for c in range(t // GLA_CHUNK):
        r0 = c * GLA_CHUNK
        in_chunk = (col // GLA_CHUNK) == c
        for h in heads:
            inter[h].append(jnp.dot(qbs[h][r0:r0 + GLA_CHUNK], ss[h].astype(BF16), preferred_element_type=F32))
        upd = [jnp.dot(jnp.where(in_chunk, kd_t[h], 0.0).astype(BF16), vbs[h], preferred_element_type=F32)
               for h in heads]
        ss = [dec_t[h][:, r0:r0 + 1] * ss[h] + upd[h] for h in heads]
    for h in heads:
        state[h] = ss[h]
    o_heads = [o_intra[h] + jnp.concatenate(inter[h], axis=0) for h in heads]
    y_ref[...] = _head_rmsnorm_gate(o_heads, g_ref, r_ref[...])

    @pl.when(blk == pl.num_programs(1) - 1)
    def _():
        st_ref[0] = state[...]


def _gla_params(w_g2, b_g, norm_g):
    w2 = jnp.pad(w_g2, ((0, LANE - GLA_GATE_RANK), (0, 0))).astype(BF16)
    return w2, b_g.astype(F32).reshape(1, -1), jnp.tile(norm_g.astype(F32), GLA_HEADS).reshape(1, -1)


def gla_prompt(gq, gk, gv, r, small, b, l, w_g2, b_g, norm_g):
    t = GLA_BLOCK
    nb = l // t
    params = _gla_params(w_g2, b_g, norm_g)
    rows = lambda w: pl.BlockSpec((t, w), lambda bi, i: (bi * nb + i, 0))
    const = lambda a: pl.BlockSpec(a.shape, lambda bi, i: (0,) * a.ndim)
    st_shape = (GLA_HEADS, GLA_DK, GLA_DV)
    return pl.pallas_call(
        _gla_prompt_body,
        grid=(b, nb),
        in_specs=[rows(GLA_HEADS * GLA_DK), rows(GLA_HEADS * GLA_DK), rows(GLA_INNER), rows(GLA_INNER), rows(LANE)]
        + [const(p) for p in params],
        out_specs=[rows(GLA_INNER), pl.BlockSpec((1,) + st_shape, lambda bi, i: (bi, 0, 0, 0))],
        out_shape=[jax.ShapeDtypeStruct((b * l, GLA_INNER), F32), jax.ShapeDtypeStruct((b,) + st_shape, F32)],
        scratch_shapes=[pltpu.VMEM(st_shape, F32)],
        compiler_params=pltpu.CompilerParams(dimension_semantics=("arbitrary", "arbitrary"),
                                             vmem_limit_bytes=VMEM_LIMIT_BYTES),
        name="gla_prompt",
    )(gq, gk, gv, r, small, *params)


def _gla_sample_body(q_ref, k_ref, v_ref, r_ref, sm_ref, st_ref, w2_ref, bg_ref, g_ref, y_ref, sto_ref):
    la = _log_gate(jnp.broadcast_to(sm_ref[0], (8, LANE)), w2_ref, bg_ref)[0:1]
    a = jnp.exp(la)
    qt = q_ref[0] * (GLA_DK ** -0.5) * a
    kt = k_ref[0] * jnp.exp(-la)
    o_heads = []
    for h in range(GLA_HEADS):
        kc = slice(h * GLA_DK, (h + 1) * GLA_DK)
        vb = _bf(v_ref[0, :, h * GLA_DV:(h + 1) * GLA_DV])
        qb = _bf(qt[:, kc])
        att = jnp.sum(qb * _bf(kt[:, kc]), axis=-1, keepdims=True)
        s_old = st_ref[0, h]
        o_inter = jnp.dot(jnp.broadcast_to(qb, (8, GLA_DK)).astype(BF16), s_old.astype(BF16),
                          preferred_element_type=F32)[0:1]
        a_rows = jnp.broadcast_to(a[:, kc], (GLA_DK, GLA_DK)).T[:, 0:1]
        k_rows = jnp.broadcast_to(k_ref[0, :, kc], (GLA_DK, GLA_DK)).T[:, 0:1]
        sto_ref[0, h] = a_rows * s_old + k_rows * v_ref[0, :, h * GLA_DV:(h + 1) * GLA_DV]
        o_heads.append(_bf(att) * vb + o_inter)
    y_ref[0] = _head_rmsnorm_gate(o_heads, g_ref, r_ref[0])


def gla_sample(gq, gk, gv, r, small, state, w_g2, b_g, norm_g):
    db = gq.shape[0]
    params = _gla_params(w_g2, b_g, norm_g)
    per_b = lambda shape: pl.BlockSpec((1,) + shape, lambda i: (i,) + (0,) * len(shape))
    const = lambda a: pl.BlockSpec(a.shape, lambda i: (0,) * a.ndim)
    st_shape = (GLA_HEADS, GLA_DK, GLA_DV)
    row3 = lambda x: x.reshape(db, 1, -1)
    y, st = pl.pallas_call(
        _gla_sample_body,
        grid=(db,),
        in_specs=[per_b((1, GLA_HEADS * GLA_DK)), per_b((1, GLA_HEADS * GLA_DK)), per_b((1, GLA_INNER)),
                  per_b((1, GLA_INNER)), per_b((1, LANE)), per_b(st_shape)] + [const(p) for p in params],
        out_specs=[per_b((1, GLA_INNER)), per_b(st_shape)],
        out_shape=[jax.ShapeDtypeStruct((db, 1, GLA_INNER), F32), jax.ShapeDtypeStruct((db,) + st_shape, F32)],
        compiler_params=pltpu.CompilerParams(dimension_semantics=("arbitrary",)),
        name="gla_sample",
    )(row3(gq), row3(gk), row3(gv), row3(r), row3(small), state, *params)
    return y.reshape(db, GLA_INNER), st


def _ffn_body(x_ref, g_ref, wg_ref, wu_ref, wd_ref, fg_ref, o_ref, *, chunks, final):
    x = x_ref[...]
    h = (x * lax.rsqrt(jnp.mean(x * x, axis=-1, keepdims=True) + EPS) * g_ref[...]).astype(BF16)
    acc = x
    for c0, cw in chunks:
        a = jnp.dot(h, wg_ref[:, c0:c0 + cw], preferred_element_type=F32)
        u = jnp.dot(h, wu_ref[:, c0:c0 + cw], preferred_element_type=F32)
        act = (a * jax.nn.sigmoid(a) * u).astype(BF16)
        acc = acc + jnp.dot(act, wd_ref[c0:c0 + cw, :], preferred_element_type=F32)
    if final:
        acc = acc * lax.rsqrt(jnp.mean(acc * acc, axis=-1, keepdims=True) + EPS) * fg_ref[...]
    o_ref[...] = acc


def swiglu_ffn(x, g, w_gate, w_up, w_down, final_g=None):
    m, d = x.shape
    hdim = w_gate.shape[1]
    tm = _row_tile(m)
    final = final_g is not None
    fg = (final_g if final else g).reshape(1, d).astype(F32)
    wspec = lambda shape: pl.BlockSpec(shape, lambda i: (0, 0), pipeline_mode=pl.Buffered(1))
    return pl.pallas_call(
        functools.partial(_ffn_body, chunks=_col_chunks(hdim, 256), final=final),
        grid=(m // tm,),
        in_specs=[pl.BlockSpec((tm, d), lambda i: (i, 0)),
                  pl.BlockSpec((1, d), lambda i: (0, 0)),
                  wspec((d, hdim)), wspec((d, hdim)), wspec((hdim, d)),
                  pl.BlockSpec((1, d), lambda i: (0, 0))],
        out_specs=pl.BlockSpec((tm, d), lambda i: (i, 0)),
        out_shape=jax.ShapeDtypeStruct((m, d), F32),
        compiler_params=pltpu.CompilerParams(dimension_semantics=("arbitrary",),
                                             vmem_limit_bytes=VMEM_LIMIT_BYTES),
        name="swiglu_ffn",
    )(x, g.reshape(1, d).astype(F32), w_gate.astype(BF16), w_up.astype(BF16), w_down.astype(BF16), fg)


def split_cols(h, sizes):
    return jnp.split(h, [int(i) for i in np.cumsum(sizes)[:-1]], axis=-1)


def rmsnorm(x, g):
    xf = x.astype(F32)
    y = xf * lax.rsqrt(jnp.mean(xf * xf, axis=-1, keepdims=True) + EPS)
    return (y * g.astype(F32)).astype(x.dtype)


def rope(x, pos):
    half = x.shape[-1] // 2
    freqs = ROPE_THETA ** (-jnp.arange(half, dtype=F32) / half)
    ang = pos.astype(F32)[:, None] * freqs
    shape = (pos.shape[0],) + (1,) * (x.ndim - 3) + (half,)
    cos, sin = jnp.cos(ang).reshape(shape), jnp.sin(ang).reshape(shape)
    xf = x.astype(F32)
    x1, x2 = xf[..., :half], xf[..., half:]
    return jnp.concatenate([x1 * cos - x2 * sin, x2 * cos + x1 * sin], axis=-1).astype(x.dtype)


def t5_bucket(rel):
    exact = REL_BUCKETS // 2
    n = jnp.maximum(rel, 0)
    nf = jnp.maximum(n, exact).astype(F32)
    large = exact + (jnp.log(nf / exact) / math.log(REL_MAX_DIST / exact) * (REL_BUCKETS - exact)).astype(jnp.int32)
    return jnp.where(n < exact, n, jnp.minimum(large, REL_BUCKETS - 1))


def shared_bias(rel, table):
    lq, nk = rel.shape
    return table.astype(F32)[t5_bucket(rel)].reshape(lq, nk, NSA_KV_HEADS, NSA_GROUP).transpose(0, 2, 3, 1)


def masked_softmax(s, mask):
    return jax.nn.softmax(jnp.where(mask, s, NEG), axis=-1) * mask


def causal_conv(xbc, conv_state, w, b):
    full = jnp.concatenate([conv_state.astype(xbc.dtype), xbc], axis=1)
    ch = xbc.shape[-1]
    y = lax.conv_general_dilated(full, w[:, None, :].astype(xbc.dtype), (1,), 'VALID',
                                 dimension_numbers=('NWC', 'WIO', 'NWC'), feature_group_count=ch)
    return jax.nn.silu(y + b.astype(y.dtype)), full[:, -(SSD_CONV - 1):]


def segsum(x):
    t = x.shape[-1]
    cs = jnp.cumsum(x, axis=-1)
    diff = cs[..., :, None] - cs[..., None, :]
    return jnp.where(jnp.tril(jnp.ones((t, t), bool)), diff, -jnp.inf)


def ssd_scan(x, dt, a, bm, cm, s0, chunk):
    b, l, h, p = x.shape
    g, n = bm.shape[2], bm.shape[3]
    e = h // g
    nc = l // chunk
    xd = (x * dt[..., None]).reshape(b, nc, chunk, g, e, p)
    da = (dt * a).reshape(b, nc, chunk, g, e).transpose(0, 3, 4, 1, 2)
    bm = bm.reshape(b, nc, chunk, g, n)
    cm = cm.reshape(b, nc, chunk, g, n)
    a_cs = jnp.cumsum(da, axis=-1)
    lmat = jnp.exp(segsum(da))
    cb = jnp.einsum('bclgn,bcsgn->bcgls', cm, bm)
    y_diag = jnp.einsum('bcgls,bgecls,bcsgep->bclgep', cb, lmat, xd)
    decay_st = jnp.exp(a_cs[..., -1:] - a_cs)
    states = jnp.einsum('bclgn,bgecl,bclgep->bcgepn', bm, decay_st, xd)
    states = jnp.concatenate([s0.reshape(b, 1, g, e, p, n), states], axis=1)
    chunk_decay = jnp.exp(segsum(jnp.pad(a_cs[..., -1], ((0, 0), (0, 0), (0, 0), (1, 0)))))
    new_states = jnp.einsum('bgezc,bcgepn->bzgepn', chunk_decay, states)
    y_off = jnp.einsum('bclgn,bcgepn,bgecl->bclgep', cm, new_states[:, :-1], jnp.exp(a_cs))
    return (y_diag + y_off).reshape(b, l, h, p), new_states[:, -1].reshape(b, h, p, n)


def ssd_mixer(z, xbc, dt_raw, conv_state, ssm_state, conv_w, conv_b, dt_bias, a_log, d_skip, norm_g):
    b, l = z.shape[:2]
    xbc, new_conv = causal_conv(xbc, conv_state, conv_w, conv_b)
    xs, bm, cm = split_cols(xbc.astype(F32), (SSD_INNER, SSD_GROUPS * SSD_STATE, SSD_GROUPS * SSD_STATE))
    x = xs.reshape(b, l, SSD_HEADS, SSD_HEADDIM)
    bm = bm.reshape(b, l, SSD_GROUPS, SSD_STATE)
    cm = cm.reshape(b, l, SSD_GROUPS, SSD_STATE)
    dt = jax.nn.softplus(dt_raw.astype(F32) + dt_bias.astype(F32))
    a = -jnp.exp(a_log.astype(F32))
    chunk = SSD_CHUNK if l % SSD_CHUNK == 0 else l
    y, new_ssm = ssd_scan(x, dt, a, bm, cm, ssm_state.astype(F32), chunk)
    y = y + x * d_skip.astype(F32)[:, None]
    y = (y.reshape(b, l, SSD_INNER) * jax.nn.silu(z.astype(F32))).reshape(b, l, SSD_GROUPS, SSD_INNER // SSD_GROUPS)
    y = rmsnorm(y, norm_g.reshape(SSD_GROUPS, -1)).reshape(b, l, SSD_INNER)
    return y.astype(z.dtype), new_conv, new_ssm


def nsa_compress(k, w):
    b, t = k.shape[:2]
    nseg = t // NSA_CMP_STRIDE
    seg = k[:, :nseg * NSA_CMP_STRIDE].reshape(b, nseg, NSA_CMP_STRIDE, NSA_KV_HEADS, NSA_HEAD_DIM)
    first = jnp.einsum('bsrhd,rd->bshd', seg, w[:NSA_CMP_STRIDE])
    second = jnp.einsum('bsrhd,rd->bshd', seg, w[NSA_CMP_STRIDE:])
    return first[:, :-1] + second[:, 1:]


def sel_overlap(n_sel, n_cmp):
    j = jnp.arange(n_sel)[:, None]
    i = jnp.arange(n_cmp)[None, :]
    lo = jnp.maximum(j * NSA_SEL_BLOCK, i * NSA_CMP_STRIDE)
    hi = jnp.minimum(j * NSA_SEL_BLOCK + NSA_SEL_BLOCK, i * NSA_CMP_STRIDE + NSA_CMP_BLOCK)
    return (jnp.maximum(hi - lo, 0) / NSA_CMP_STRIDE).astype(F32)


def nsa_cmp_branch(q, qpos, kc, vc, table):
    nc = kc.shape[1]
    cend = jnp.arange(nc) * NSA_CMP_STRIDE + NSA_CMP_BLOCK - 1
    rel = qpos[:, None] - cend[None, :]
    s = jnp.einsum('bqhgd,bchd->bqhgc', q, kc).astype(F32) * NSA_SCALE + shared_bias(rel, table)
    p = masked_softmax(s, (rel >= 0)[:, None, None, :])
    return jnp.einsum('bqhgc,bchd->bqhgd', p, vc), p


def nsa_select(p_cmp, qpos, n_sel):
    imp = jnp.einsum('bqhgc,jc->bqhj', p_cmp, sel_overlap(n_sel, p_cmp.shape[-1]))
    j = jnp.arange(n_sel)[None, :]
    cur = (qpos // NSA_SEL_BLOCK)[:, None]
    valid = (j <= cur)[:, None, :]
    forced = ((j == 0) | (j == cur) | (j == cur - 1))[:, None, :]
    score = jnp.where(valid, jnp.where(forced, NSA_FORCE, imp), -1.0)
    if n_sel < NSA_TOPK:
        score = jnp.pad(score, ((0, 0), (0, 0), (0, 0), (0, NSA_TOPK - n_sel)), constant_values=-1.0)
    vals, idx = lax.top_k(score, NSA_TOPK)
    kpos = (idx[..., None] * NSA_SEL_BLOCK + jnp.arange(NSA_SEL_BLOCK)).reshape(*idx.shape[:-1], NSA_TOPK * NSA_SEL_BLOCK)
    kok = jnp.repeat(vals >= 0, NSA_SEL_BLOCK, axis=-1)
    return kpos, kok


def nsa_slc_branch(q, qpos, ks, vs, kpos, kok, table):
    rel = qpos[None, :, None, None] - kpos
    tb = table.astype(F32).reshape(REL_BUCKETS, NSA_KV_HEADS, NSA_GROUP)
    bias = jnp.moveaxis(tb[t5_bucket(rel), jnp.arange(NSA_KV_HEADS)[:, None]], -1, -2)
    s = jnp.einsum('bqhgd,bqhkd->bqhgk', q, ks).astype(F32) * NSA_SCALE + bias
    p = masked_softmax(s, (kok & (rel >= 0))[..., None, :])
    return jnp.einsum('bqhgk,bqhkd->bqhgd', p, vs)


def nsa_win_branch(q, qpos, kw, vw, wpos, table):
    rel = qpos[:, None] - wpos[None, :]
    mask = (rel >= 0) & (rel < NSA_WINDOW) & (wpos >= 0)[None, :]
    s = jnp.einsum('bqhgd,bkhd->bqhgk', q, kw).astype(F32) * NSA_SCALE + shared_bias(rel, table)
    p = masked_softmax(s, mask[:, None, None, :])
    return jnp.einsum('bqhgk,bkhd->bqhgd', p, vw)


def nsa_gate(g, o_c, o_s, o_w):
    return g[..., 0:1] * o_c + g[..., 1:2] * o_s + g[..., 2:3] * o_w


def gather_rows(pool, e, new, page_table, kpos):
    db, s = new.shape[:2]
    p0 = page_table.shape[1] * PAGE_SIZE
    bi = jnp.arange(db)[:, None, None, None]
    hi = jnp.arange(NSA_KV_HEADS)[None, None, :, None]
    pp = jnp.minimum(kpos, p0 - 1)
    past = pool[e, page_table[bi, pp // PAGE_SIZE], pp % PAGE_SIZE, hi]
    cur = new[bi, jnp.clip(kpos - p0, 0, s - 1), hi]
    return jnp.where((kpos < p0)[..., None], past, cur)


def nsa_sample(q, kvs, gates, e, k_cmp_pool, v_cmp_pool, k_slc_pool, v_slc_pool, k_win_buf, v_win_buf,
               page_table, w_ck, w_cv, table):
    kc_new, vc_new, ks_new, vs_new, kw_new, vw_new = kvs
    db, s = q.shape[:2]
    p0 = page_table.shape[1] * PAGE_SIZE
    qpos = p0 + jnp.arange(s)
    past = lambda pool: pool[e, page_table].reshape(db, p0, NSA_KV_HEADS, NSA_HEAD_DIM)
    kc = nsa_compress(jnp.concatenate([past(k_cmp_pool), kc_new], axis=1), w_ck)
    vc = nsa_compress(jnp.concatenate([past(v_cmp_pool), vc_new], axis=1), w_cv)
    o_c, p_c = nsa_cmp_branch(q, qpos, kc, vc, table)
    kpos, kok = nsa_select(p_c, qpos, -(-(p0 + s) // NSA_SEL_BLOCK))
    ksel = gather_rows(k_slc_pool, e, ks_new, page_table, kpos)
    vsel = gather_rows(v_slc_pool, e, vs_new, page_table, kpos)
    o_s = nsa_slc_branch(q, qpos, ksel, vsel, kpos, kok, table)
    nbuf = k_win_buf.shape[1]
    kw = jnp.concatenate([k_win_buf.astype(kw_new.dtype), kw_new], axis=1)
    vw = jnp.concatenate([v_win_buf.astype(vw_new.dtype), vw_new], axis=1)
    wpos = p0 - nbuf + jnp.arange(nbuf + s)
    o_w = nsa_win_branch(q, qpos, kw, vw, wpos, table)
    out = nsa_gate(gates, o_c, o_s, o_w).reshape(db, s, NSA_HEADS * NSA_HEAD_DIM)
    return out, kw[:, -nbuf:], vw[:, -nbuf:]


def even_split(parts, r0, b, l):
    z, xbc, q, kv, small = (t[r0:r0 + b * l].reshape(b, l, -1) for t in parts)
    dtr = small[..., SMALL_DT:SMALL_DT + SSD_HEADS]
    g = small[..., SMALL_GATE:SMALL_GATE + 3 * NSA_HEADS]
    q = q.reshape(b, l, NSA_KV_HEADS, NSA_GROUP, NSA_HEAD_DIM)
    kvs = tuple(t.reshape(b, l, NSA_KV_HEADS, NSA_HEAD_DIM) for t in split_cols(kv, (NSA_KV_WIDTH,) * 6))
    g = jax.nn.sigmoid(g.astype(F32)).reshape(b, l, 3, NSA_KV_HEADS, NSA_GROUP).transpose(0, 1, 3, 4, 2)
    return z, xbc, dtr, q, kvs, g


def gla_scan(q, k, v, log_a, s0, chunk):
    b, l, h, dk = q.shape
    dv = v.shape[-1]
    nc = l // chunk
    q, k, v, log_a = [t.reshape(b, nc, chunk, h, t.shape[-1]) for t in (q, k, v, log_a)]
    bc = jnp.cumsum(log_a, axis=2)
    blast = bc[:, :, -1:]
    qt = q * jnp.exp(bc)
    kt = k * jnp.exp(-bc)
    kd = k * jnp.exp(blast - bc)
    causal = jnp.tril(jnp.ones((chunk, chunk), bool))
    att = jnp.where(causal, jnp.einsum('bclhk,bcshk->bchls', qt, kt), 0.0)
    o_intra = jnp.einsum('bchls,bcshv->bclhv', att, v)

    def step(st, inp):
        qt_c, kd_c, v_c, dec_c = inp
        o_c = jnp.einsum('blhk,bhkv->blhv', qt_c, st)
        st = jnp.exp(dec_c)[..., None] * st + jnp.einsum('blhk,blhv->bhkv', kd_c, v_c)
        return st, o_c

    xs = tuple(jnp.swapaxes(t, 0, 1) for t in (qt, kd, v, blast[:, :, 0]))
    s_fin, o_inter = lax.scan(step, s0, xs)
    return (o_intra + jnp.swapaxes(o_inter, 0, 1)).reshape(b, l, h, dv), s_fin


def gla_mixer(q, k, v, g1, r, state, w_g2, b_g, norm_g):
    b, l = q.shape[:2]
    q = q.astype(F32).reshape(b, l, GLA_HEADS, GLA_DK) * GLA_DK ** -0.5
    k = k.astype(F32).reshape(b, l, GLA_HEADS, GLA_DK)
    v = v.astype(F32).reshape(b, l, GLA_HEADS, GLA_DV)
    log_a = jax.nn.log_sigmoid(g1.astype(F32) @ w_g2.astype(F32) + b_g.astype(F32)).reshape(b, l, GLA_HEADS, GLA_DK) / GLA_TAU
    chunk = GLA_CHUNK if l % GLA_CHUNK == 0 else l
    o, new_state = gla_scan(q, k, v, log_a, state.astype(F32), chunk)
    o = rmsnorm(o, norm_g).reshape(b, l, GLA_HEADS * GLA_DV) * jax.nn.silu(r.astype(F32))
    return o.astype(r.dtype), new_state


def odd_split(h, b, l, pos, q_norm, w_qb, kv_norm):
    gq, gk, gv, g1, r, qa, kva, kr = split_cols(h.reshape(b, l, -1), OD_SPLITS)
    q = fused_matmul(qa.reshape(b * l, -1), w_qb, norm_g=q_norm).reshape(b, l, MLA_HEADS, MLA_NOPE + MLA_ROPE)
    q_nope, q_rope = q[..., :MLA_NOPE], rope(q[..., MLA_NOPE:], pos)
    return (gq, gk, gv, g1, r), (q_nope, q_rope, rmsnorm(kva, kv_norm), rope(kr, pos))


def mla_prompt(q_nope, q_rope, c, k_rope, w_kvb):
    b, l = c.shape[:2]
    kv = fused_matmul(c.reshape(b * l, -1), w_kvb).reshape(b, l, MLA_HEADS, MLA_NOPE + MLA_V)
    k_nope, v = kv[..., :MLA_NOPE], kv[..., MLA_NOPE:]
    nqb = l // MLA_QBLOCK
    kpos = jnp.arange(l)

    def block(args):
        i, qn, qr = args
        qpos = i * MLA_QBLOCK + jnp.arange(MLA_QBLOCK)
        s = (jnp.einsum('bqhd,bkhd->bhqk', qn, k_nope) + jnp.einsum('bqhr,bkr->bhqk', qr, k_rope)).astype(F32) * MLA_SCALE
        p = jax.nn.softmax(jnp.where(kpos[None, :] <= qpos[:, None], s, NEG), axis=-1)
        return jnp.einsum('bhqk,bkhd->bqhd', p, v)

    blk = lambda t: t.reshape(b, nqb, MLA_QBLOCK, *t.shape[2:]).swapaxes(0, 1)
    o = lax.map(block, (jnp.arange(nqb), blk(q_nope), blk(q_rope)))
    return o.swapaxes(0, 1).reshape(b, l, MLA_HEADS * MLA_V)


def mla_sample(q_nope, q_rope, c, k_rope, lat_pool, kr_pool, o_idx, page_table, w_kvb):
    db, s = c.shape[:2]
    w = w_kvb.reshape(MLA_KV_RANK, MLA_HEADS, MLA_NOPE + MLA_V)
    q_lat = jnp.einsum('bshd,chd->bshc', q_nope, w[..., :MLA_NOPE])
    c_past = lat_pool[o_idx, page_table].reshape(db, -1, MLA_KV_RANK)
    kr_past = kr_pool[o_idx, page_table].reshape(db, -1, MLA_ROPE)
    s_past = (jnp.einsum('bshc,btc->bsht', q_lat, c_past) + jnp.einsum('bshr,btr->bsht', q_rope, kr_past)).astype(F32) * MLA_SCALE
    s_new = (jnp.einsum('bshc,btc->bsht', q_lat, c) + jnp.einsum('bshr,btr->bsht', q_rope, k_rope)).astype(F32) * MLA_SCALE
    causal = jnp.arange(s)[None, :] <= jnp.arange(s)[:, None]
    s_new = jnp.where(causal[:, None, :], s_new, NEG)
    p = jax.nn.softmax(jnp.concatenate([s_past, s_new], axis=-1), axis=-1)
    n_past = c_past.shape[1]
    o_lat = jnp.einsum('bsht,btc->bshc', p[..., :n_past], c_past) + jnp.einsum('bsht,btc->bshc', p[..., n_past:], c)
    return jnp.einsum('bshc,chd->bshd', o_lat, w[..., MLA_NOPE:]).reshape(db, s, MLA_HEADS * MLA_V)


def kernel(x_prompt, x_sample, state_ssm, state_conv, cache_nsa_k_cmp, cache_nsa_v_cmp, cache_nsa_k_slc, cache_nsa_v_slc, cache_nsa_k_win, cache_nsa_v_win, state_gla, cache_mla_latent, cache_mla_krope, page_table, rel_bias, ev_norm, ev_w_in, ssd_conv_w, ssd_conv_b, ssd_dt_bias, ssd_a_log, ssd_d, ssd_norm, nsa_w_cmp_k, nsa_w_cmp_v, ev_w_out, od_norm, od_w_in, gla_w_gate2, gla_b_gate, gla_norm, mla_q_norm, mla_w_qb, mla_kv_norm, mla_w_kvb, od_w_out, ffn_norm, ffn_w_gate, ffn_w_up, ffn_w_down, final_norm):
    b, l = x_prompt.shape[:2]
    db, s = x_sample.shape[:2]
    depth = ffn_norm.shape[0]
    p0 = page_table.shape[1] * PAGE_SIZE
    pos_p = jnp.arange(l)
    pos_s = p0 + jnp.arange(s)
    names = ('ssm', 'conv', 'k_cmp', 'v_cmp', 'k_slc', 'v_slc', 'k_win', 'v_win', 'gla', 'lat', 'krope')
    newp = {n: [] for n in names}
    news = {n: [] for n in names}
    np_rows = b * l
    x = jnp.concatenate([x_prompt.reshape(np_rows, D_MODEL), x_sample.reshape(db * s, D_MODEL)], axis=0)
    for li in range(depth):
        if li % 2 == 0:
            e = li // 2
            ssd_w = (ssd_conv_w[e], ssd_conv_b[e], ssd_dt_bias[e], ssd_a_log[e], ssd_d[e], ssd_norm[e])
            parts = fused_matmul(x, even_weight(ev_w_in[e]), norm_g=ev_norm[e], groups=EV_GROUPS)
            z_all, xbc_all, _, kv_all, small_all = parts
            kv_heads = lambda r0, nb, nl: tuple(
                t.reshape(nb, nl, NSA_KV_HEADS, NSA_HEAD_DIM)
                for t in split_cols(kv_all[r0:r0 + nb * nl], (NSA_KV_WIDTH,) * 6))
            y_ssd, ssm_n, conv_n = ssd_prompt(z_all, xbc_all, small_all, b, l, *ssd_w)
            y_nsa = nsa_prompt_attention(parts[2], parts[3], parts[4], nsa_w_cmp_k[e], nsa_w_cmp_v[e], rel_bias, b, l)
            mix_p = jnp.concatenate([y_ssd, y_nsa], axis=-1)
            nw = min(NSA_WINDOW, l)
            kvs = kv_heads(0, b, l)
            for n, t in zip(names[:8], (ssm_n, conv_n, kvs[0], kvs[1], kvs[2], kvs[3],
                                        kvs[4][:, -nw:], kvs[5][:, -nw:])):
                newp[n].append(t)
            kvs = kv_heads(np_rows, db, s)
            y_ssd, ssm_n, conv_n = ssd_sample(z_all[np_rows:], xbc_all[np_rows:], small_all[np_rows:],
                                              state_conv[e], state_ssm[e], *ssd_w)
            y_nsa, kw_n, vw_n = nsa_sample_mixer(parts[2][np_rows:], parts[3][np_rows:], parts[4][np_rows:], e,
                                                 cache_nsa_k_cmp, cache_nsa_v_cmp, cache_nsa_k_slc, cache_nsa_v_slc,
                                                 cache_nsa_k_win[e], cache_nsa_v_win[e], page_table,
                                                 nsa_w_cmp_k[e], nsa_w_cmp_v[e], rel_bias)
            mix_s = jnp.concatenate([y_ssd, y_nsa], axis=-1)
            for n, t in zip(names[:8], (ssm_n, conv_n, kvs[0], kvs[1], kvs[2], kvs[3], kw_n, vw_n)):
                news[n].append(t)
            x = fused_matmul(jnp.concatenate([mix_p, mix_s], axis=0), ev_w_out[e], residual=x)
        else:
            o = li // 2
            gla_w = (gla_w_gate2[o], gla_b_gate[o], gla_norm[o])
            gq, gk, gv, r, qa, kva, small = fused_matmul(x, odd_weight(od_w_in[o]), norm_g=od_norm[o],
                                                         groups=OD_GROUPS)
            assert db * s == MLA_PREP_ROWS and l % MLA_PREP_ROWS == 0
            nblk = l // MLA_PREP_ROWS
            tables = tuple(jnp.concatenate(t, axis=0)
                           for t in zip(rope_tables(pos_p), rope_tables(jnp.tile(pos_s, db))))
            q_rot, lat, k_full, v_pad, kr_rot = mla_prep(
                qa, kva, small, mla_q_norm[o], mla_w_qb[o], mla_kv_norm[o], mla_w_kvb[o], tables,
                lambda i: jnp.where(i < b * nblk, i % nblk, nblk))
            kr_rot = kr_rot[:, MLA_NOPE:MLA_NOPE + MLA_ROPE]
            assert s == 1
            y_gla, gla_n = gla_prompt(gq, gk, gv, r, small, b, l, *gla_w)
            y_mla = mla_prompt_attention(q_rot, k_full, v_pad, b, l)
            mix_p = jnp.concatenate([y_gla, y_mla], axis=-1)
            newp['gla'].append(gla_n)
            newp['lat'].append(lat[:np_rows].reshape(b, l, -1))
            newp['krope'].append(kr_rot[:np_rows].reshape(b, l, -1))
            y_gla, gla_n = gla_sample(gq[np_rows:], gk[np_rows:], gv[np_rows:], r[np_rows:], small[np_rows:],
                                      state_gla[o], *gla_w)
            q_s = q_rot[np_rows:].reshape(db * s, MLA_HEADS, LANE)
            y_mla = mla_sample_mixer(q_s[..., :MLA_NOPE], q_s[..., MLA_NOPE:MLA_NOPE + MLA_ROPE], lat[np_rows:],
                                     kr_rot[np_rows:], cache_mla_latent, cache_mla_krope, o, page_table, mla_w_kvb[o])
            mix_s = jnp.concatenate([y_gla, y_mla], axis=-1)
            news['gla'].append(gla_n)
            news['lat'].append(lat[np_rows:].reshape(db, s, -1))
            news['krope'].append(kr_rot[np_rows:].reshape(db, s, -1))
            x = fused_matmul(jnp.concatenate([mix_p, mix_s], axis=0), od_w_out[o], residual=x)
        x = swiglu_ffn(x, ffn_norm[li], ffn_w_gate[li], ffn_w_up[li], ffn_w_down[li],
                       final_g=final_norm if li == depth - 1 else None)
    y_prompt = x[:np_rows].reshape(b, l, D_MODEL)
    y_sample = x[np_rows:].reshape(db, s, D_MODEL)
    st = lambda d, n: jnp.stack(d[n])
    return (y_prompt, y_sample,
            st(newp, 'ssm'), st(news, 'ssm'), st(newp, 'conv'), st(news, 'conv'),
            st(newp, 'k_cmp'), st(news, 'k_cmp'), st(newp, 'v_cmp'), st(news, 'v_cmp'),
            st(newp, 'k_slc'), st(news, 'k_slc'), st(newp, 'v_slc'), st(news, 'v_slc'),
            st(newp, 'k_win'), st(news, 'k_win'), st(newp, 'v_win'), st(news, 'v_win'),
            st(newp, 'gla'), st(news, 'gla'), st(newp, 'lat'), st(news, 'lat'),
            st(newp, 'krope'), st(news, 'krope'))
```

```python
import functools
import math

import jax
import jax.numpy as jnp
import numpy as np
from jax import lax
from jax.experimental import pallas as pl
from jax.experimental.pallas import tpu as pltpu

F32 = jnp.float32
BF16 = jnp.bfloat16
EPS = 1e-6
NEG = -1e30

D_MODEL = 1024
PAGE_SIZE = 128

SSD_HEADS = 16
SSD_HEADDIM = 64
SSD_INNER = SSD_HEADS * SSD_HEADDIM
SSD_GROUPS = 2
SSD_STATE = 128
SSD_CONV = 4
SSD_CHUNK = 128
SSD_CONV_CH = SSD_INNER + 2 * SSD_GROUPS * SSD_STATE

NSA_HEADS = 16
NSA_KV_HEADS = 2
NSA_GROUP = NSA_HEADS // NSA_KV_HEADS
NSA_HEAD_DIM = 64
NSA_KV_WIDTH = NSA_KV_HEADS * NSA_HEAD_DIM
NSA_CMP_BLOCK = 32
NSA_CMP_STRIDE = 16
NSA_SEL_BLOCK = 64
NSA_TOPK = 16
NSA_WINDOW = 512
NSA_QBLOCK = 128
NSA_SCALE = NSA_HEAD_DIM ** -0.5
NSA_FORCE = 1e4

REL_BUCKETS = 32
REL_MAX_DIST = 128

GLA_HEADS = 4
GLA_DK = 128
GLA_DV = 256
GLA_GATE_RANK = 16
GLA_TAU = 16.0
GLA_CHUNK = 16

MLA_HEADS = 8
MLA_Q_RANK = 384
MLA_KV_RANK = 256
MLA_NOPE = 64
MLA_ROPE = 32
MLA_V = 64
MLA_QBLOCK = 128
MLA_SCALE = (MLA_NOPE + MLA_ROPE) ** -0.5
ROPE_THETA = 10000.0

EV_SPLITS = (SSD_INNER, SSD_CONV_CH, SSD_HEADS, NSA_HEADS * NSA_HEAD_DIM, 6 * NSA_KV_WIDTH, 3 * NSA_HEADS)
OD_SPLITS = (GLA_HEADS * GLA_DK, GLA_HEADS * GLA_DK, GLA_HEADS * GLA_DV, GLA_GATE_RANK, GLA_HEADS * GLA_DV,
             MLA_Q_RANK, MLA_KV_RANK, MLA_ROPE)

VMEM_LIMIT_BYTES = 56 * 1024 * 1024
LANE = 128


def _row_tile(m):
    for t in (512, 384, 256, 128):
        if m % t == 0:
            return t
    return m


def _col_chunks(n, width=512):
    out, c = [], 0
    while c < n:
        w = min(width, n - c)
        out.append((c, w))
        c += w
    return out


def _mm_body(*refs, norm, residual, groups):
    it = iter(refs)
    x_ref = next(it)
    g_ref = next(it) if norm else None
    w_ref = next(it)
    r_ref = next(it) if residual else None
    o_refs = list(it)
    x = x_ref[...]
    if norm:
        x = x * lax.rsqrt(jnp.mean(x * x, axis=-1, keepdims=True) + EPS) * g_ref[...]
    xb = x.astype(BF16)
    off = 0
    for o_ref, gw in zip(o_refs, groups):
        for c0, cw in _col_chunks(gw):
            acc = jnp.dot(xb, w_ref[:, off + c0:off + c0 + cw], preferred_element_type=F32)
            if residual:
                acc = acc + r_ref[:, off + c0:off + c0 + cw]
            o_ref[:, c0:c0 + cw] = acc
        off += gw


def fused_matmul(x, w, norm_g=None, residual=None, groups=None):
    m, k = x.shape
    n = w.shape[1]
    single = groups is None
    groups = (n,) if single else tuple(groups)
    assert sum(groups) == n
    tm = _row_tile(m)
    norm = norm_g is not None
    res = residual is not None
    args = [x]
    specs = [pl.BlockSpec((tm, k), lambda i: (i, 0))]
    if norm:
        args.append(norm_g.reshape(1, k).astype(F32))
        specs.append(pl.BlockSpec((1, k), lambda i: (0, 0)))
    args.append(w.astype(BF16))
    specs.append(pl.BlockSpec((k, n), lambda i: (0, 0)))
    if res:
        args.append(residual)
        specs.append(pl.BlockSpec((tm, n), lambda i: (i, 0)))
    outs = pl.pallas_call(
        functools.partial(_mm_body, norm=norm, residual=res, groups=groups),
        grid=(m // tm,),
        in_specs=specs,
        out_specs=[pl.BlockSpec((tm, gw), lambda i: (i, 0)) for gw in groups],
        out_shape=[jax.ShapeDtypeStruct((m, gw), F32) for gw in groups],
        compiler_params=pltpu.CompilerParams(dimension_semantics=("arbitrary",),
                                             vmem_limit_bytes=VMEM_LIMIT_BYTES),
        name="fused_matmul",
    )(*args)
    return outs[0] if single else outs


EV_GROUPS = (SSD_INNER, SSD_CONV_CH, NSA_HEADS * NSA_HEAD_DIM, 6 * NSA_KV_WIDTH, LANE)
SMALL_DT = 0
SMALL_GATE = SSD_HEADS
QB = NSA_QBLOCK
ROWS = NSA_GROUP * QB


def even_weight(w_in):
    z, xbc, dtr, q, kvs, g = split_cols(w_in, EV_SPLITS)
    pad = jnp.zeros((w_in.shape[0], LANE - SSD_HEADS - 3 * NSA_HEADS), w_in.dtype)
    return jnp.concatenate([z, xbc, q, kvs, dtr, g, pad], axis=1)


def _bucket_tile(rel):
    exact = REL_BUCKETS // 2
    n = jnp.maximum(rel, 0)
    nf = jnp.maximum(n, exact).astype(F32)
    large = exact + (jnp.log(nf / exact) / math.log(REL_MAX_DIST / exact) * (REL_BUCKETS - exact)).astype(jnp.int32)
    return jnp.where(n < exact, n, jnp.minimum(large, REL_BUCKETS - 1))


def _bias_tiles_body(table_ref, o_ref, *, base_step, base_off, kmul, qmul):
    t = pl.program_id(0)
    shape = o_ref.shape[2:]
    qi = lax.broadcasted_iota(jnp.int32, shape, 0)
    ki = lax.broadcasted_iota(jnp.int32, shape, 1)
    bucket = _bucket_tile(t * base_step + base_off + qmul * qi - kmul * ki)
    for hd in range(NSA_HEADS):
        acc = jnp.zeros(shape, F32)
        for bk in range(REL_BUCKETS):
            acc = jnp.where(bucket == bk, table_ref[bk, hd], acc)
        o_ref[0, hd] = acc


def bias_tiles(table, n_tiles, base_step, base_off, kmul, qmul=1, rows=QB):
    return pl.pallas_call(
        functools.partial(_bias_tiles_body, base_step=base_step, base_off=base_off, kmul=kmul, qmul=qmul),
        grid=(n_tiles,),
        in_specs=[pl.BlockSpec(memory_space=pltpu.SMEM)],
        out_specs=pl.BlockSpec((1, NSA_HEADS, rows, LANE), lambda t: (t, 0, 0, 0)),
        out_shape=jax.ShapeDtypeStruct((n_tiles, NSA_HEADS, rows, LANE), F32),
        compiler_params=pltpu.CompilerParams(dimension_semantics=("arbitrary",)),
        name="t5_bias_tiles",
    )(table.astype(F32))


def _compress_body(kv_ref, w_ref, o_ref):
    nseg = o_ref.shape[1]
    first = jnp.zeros(o_ref.shape[1:], F32)
    second = jnp.zeros(o_ref.shape[1:], F32)
    for r in range(NSA_CMP_STRIDE):
        rows = kv_ref[pl.ds(r, nseg, stride=NSA_CMP_STRIDE), :]
        first = first + rows * w_ref[r:r + 1, :]
        second = second + rows * w_ref[NSA_CMP_STRIDE + r:NSA_CMP_STRIDE + r + 1, :]
    out = first + pltpu.roll(second, nseg - 1, axis=0)
    row = lax.broadcasted_iota(jnp.int32, out.shape, 0)
    o_ref[0] = jnp.where(row < nseg - 1, out, 0.0)


def nsa_compress_prompt(kv, w_ck, w_cv, b, l):
    nseg = l // NSA_CMP_STRIDE
    w = jnp.concatenate([w_ck, w_ck, w_cv, w_cv], axis=1).astype(F32)
    return pl.pallas_call(
        _compress_body,
        grid=(b, 2),
        in_specs=[pl.BlockSpec((l, NSA_KV_WIDTH), lambda i, j: (i, j)),
                  pl.BlockSpec((NSA_CMP_BLOCK, NSA_KV_WIDTH), lambda i, j: (0, j))],
        out_specs=pl.BlockSpec((1, nseg, NSA_KV_WIDTH), lambda i, j: (i, 0, j)),
        out_shape=jax.ShapeDtypeStruct((b, nseg, 2 * NSA_KV_WIDTH), F32),
        compiler_params=pltpu.CompilerParams(dimension_semantics=("arbitrary", "arbitrary")),
        name="nsa_compress",
    )(kv, w)


def _nt_dot(a, b):
    return lax.dot_general(a, b, (((1,), (1,)), ((), ())), preferred_element_type=F32)


def _importance(p_group_sum, ovt):
    return jnp.dot(p_group_sum.astype(BF16), ovt, preferred_element_type=F32)


def _nsa_prompt_body(q_ref, kv_ref, sm_ref, cmp_ref, cmpb_ref, toeb_ref, ov_ref, exp_ref, o_ref,
                     selm_ref, acc_ref, *, n_kt):
    i = pl.program_id(1)
    n_sel = 2 * n_kt
    row = lax.broadcasted_iota(jnp.int32, (QB, LANE), 0)
    lane = lax.broadcasted_iota(jnp.int32, (QB, LANE), 1)
    low = lane < NSA_HEAD_DIM
    gates = jax.nn.sigmoid(sm_ref[...])
    qpos = i * QB + row

    def attend(qp, h, kcol, vcol, n_tiles, tile_of, use_sel, window):
        acc_ref[...] = jnp.zeros_like(acc_ref)

        def body(t, carry):
            m, l = carry
            kt = tile_of(t)
            d = i - kt
            k0 = pl.multiple_of(kt * QB, QB)
            k_t = kv_ref[pl.ds(k0, QB), kcol:kcol + LANE].astype(BF16)
            v_t = kv_ref[pl.ds(k0, QB), vcol:vcol + LANE].astype(BF16)
            s = _nt_dot(qp, k_t).reshape(NSA_GROUP, QB, LANE)
            s = s + toeb_ref[jnp.minimum(d, 2), pl.ds(h * NSA_GROUP, NSA_GROUP)]
            rel = d * QB + row - lane
            msk = rel >= 0
            if window:
                msk = msk & (rel < NSA_WINDOW)
            if use_sel:
                msk = msk & (selm_ref[kt] > 0.5)
            s = jnp.where(msk[None], s, NEG)
            m_new = jnp.maximum(m, jnp.max(s, axis=-1, keepdims=True))
            alpha = jnp.exp(m - m_new)
            p = jnp.exp(s - m_new)
            l_new = alpha * l + jnp.sum(p, axis=-1, keepdims=True)
            pv = jnp.dot(p.reshape(ROWS, LANE).astype(BF16), v_t, preferred_element_type=F32)
            acc_ref[...] = acc_ref[...] * alpha.reshape(ROWS, 1) + pv
            return m_new, l_new

        m0 = jnp.full((NSA_GROUP, QB, 1), NEG, F32)
        l0 = jnp.zeros((NSA_GROUP, QB, 1), F32)
        _, l = lax.fori_loop(0, n_tiles, body, (m0, l0))
        return acc_ref[...] / l.reshape(ROWS, 1)

    for h in range(NSA_KV_HEADS):
        in_half = low if h == 0 else jnp.logical_not(low)
        parts = []
        for g in range(NSA_GROUP):
            hd = h * NSA_GROUP + g
            t = q_ref[:, (hd // 2) * LANE:(hd // 2) * LANE + LANE]
            if hd % 2 != h:
                t = pltpu.roll(t, NSA_HEAD_DIM, axis=1)
            parts.append((jnp.where(in_half, t, 0.0) * NSA_SCALE).astype(BF16))
        qp = jnp.concatenate(parts, axis=0)

        kc = cmp_ref[0, :, 0:LANE].astype(BF16)
        vc = cmp_ref[0, :, LANE:2 * LANE].astype(BF16)
        s = _nt_dot(qp, kc).reshape(NSA_GROUP, QB, LANE) + cmpb_ref[0, pl.ds(h * NSA_GROUP, NSA_GROUP)]
        mc = (qpos - (NSA_CMP_STRIDE * lane + NSA_CMP_BLOCK - 1)) >= 0
        s = jnp.where(mc[None], s, NEG)
        e = jnp.exp(s - jnp.max(s, axis=-1, keepdims=True))
        p = e / jnp.sum(e, axis=-1, keepdims=True) * mc[None].astype(F32)
        o_c = jnp.dot(p.reshape(ROWS, LANE).astype(BF16), vc, preferred_element_type=F32)
        imp_t = _nt_dot(ov_ref[...], jnp.sum(p, axis=0).astype(BF16))[0:n_sel]

        jrow = lax.broadcasted_iota(jnp.int32, (n_sel, LANE), 0)
        cur = (i * QB + lax.broadcasted_iota(jnp.int32, (n_sel, LANE), 1)) // NSA_SEL_BLOCK
        valid = jrow <= cur
        forced = (jrow == 0) | (jrow == cur) | (jrow == cur - 1)
        score = jnp.where(valid, jnp.where(forced, NSA_FORCE, imp_t), -1.0)
        cnt = jnp.zeros((n_sel, LANE), jnp.int32)
        for jp in range(n_sel):
            other = score[jp:jp + 1, :]
            beats = (other > score) | ((other == score) & (jrow > jp))
            cnt = cnt + beats.astype(jnp.int32)
        sel_t = ((cnt < NSA_TOPK) & valid).astype(F32)
        sel = jnp.concatenate([sel_t, jnp.zeros((LANE - n_sel, LANE), F32)], axis=0).T.astype(BF16)
        for kt in range(n_kt):
            selm_ref[kt] = jnp.dot(sel, exp_ref[:, kt * QB:(kt + 1) * QB], preferred_element_type=F32)

        o_s = attend(qp, h, 2 * LANE, 3 * LANE, i + 1, lambda t: t, True, False)
        o_w = attend(qp, h, 4 * LANE, 5 * LANE, jnp.minimum(i, NSA_WINDOW // QB) + 1, lambda t: i - t, False, True)

        outs = []
        for g in range(NSA_GROUP):
            c = SMALL_GATE + h * NSA_GROUP + g
            r0 = g * QB
            outs.append(gates[:, c:c + 1] * o_c[r0:r0 + QB]
                        + gates[:, c + NSA_HEADS:c + NSA_HEADS + 1] * o_s[r0:r0 + QB]
                        + gates[:, c + 2 * NSA_HEADS:c + 2 * NSA_HEADS + 1] * o_w[r0:r0 + QB])
        for pr in range(NSA_GROUP // 2):
            a, b2 = outs[2 * pr], outs[2 * pr + 1]
            if h == 0:
                b2 = pltpu.roll(b2, NSA_HEAD_DIM, axis=1)
            else:
                a = pltpu.roll(a, NSA_HEAD_DIM, axis=1)
            c0 = (h * NSA_GROUP + 2 * pr) * NSA_HEAD_DIM
            o_ref[:, c0:c0 + LANE] = jnp.where(low, a, b2)


def nsa_prompt_attention(q, kv, small, w_ck, w_cv, table, b, l):
    nqb = l // QB
    n_sel = l // NSA_SEL_BLOCK
    nseg = l // NSA_CMP_STRIDE
    assert nseg == LANE and n_sel <= LANE and l % QB == 0
    cmp = nsa_compress_prompt(kv, w_ck, w_cv, b, l)
    cmp_bias = bias_tiles(table, nqb, QB, -(NSA_CMP_BLOCK - 1), NSA_CMP_STRIDE)
    toe_bias = bias_tiles(table, 3, QB, 0, 1)
    jj = np.arange(LANE)[None, :]
    cc = np.arange(LANE)[:, None]
    lo = np.maximum(jj * NSA_SEL_BLOCK, cc * NSA_CMP_STRIDE)
    hi = np.minimum(jj * NSA_SEL_BLOCK + NSA_SEL_BLOCK, cc * NSA_CMP_STRIDE + NSA_CMP_BLOCK)
    ovt = np.where((jj < n_sel) & (cc < nseg - 1), np.maximum(hi - lo, 0) / NSA_CMP_STRIDE, 0.0)
    expand = (np.arange(l)[None, :] // NSA_SEL_BLOCK == np.arange(LANE)[:, None]).astype(np.float32)
    return pl.pallas_call(
        functools.partial(_nsa_prompt_body, n_kt=nqb),
        grid=(b, nqb),
        in_specs=[pl.BlockSpec((QB, NSA_HEADS * NSA_HEAD_DIM), lambda bi, i: (bi * nqb + i, 0)),
                  pl.BlockSpec((l, 6 * NSA_KV_WIDTH), lambda bi, i: (bi, 0)),
                  pl.BlockSpec((QB, LANE), lambda bi, i: (bi * nqb + i, 0)),
                  pl.BlockSpec((1, nseg, 2 * NSA_KV_WIDTH), lambda bi, i: (bi, 0, 0)),
                  pl.BlockSpec((1, NSA_HEADS, QB, LANE), lambda bi, i: (i, 0, 0, 0)),
                  pl.BlockSpec((3, NSA_HEADS, QB, LANE), lambda bi, i: (0, 0, 0, 0)),
                  pl.BlockSpec((LANE, LANE), lambda bi, i: (0, 0)),
                  pl.BlockSpec((LANE, l), lambda bi, i: (0, 0))],
        out_specs=pl.BlockSpec((QB, NSA_HEADS * NSA_HEAD_DIM), lambda bi, i: (bi * nqb + i, 0)),
        out_shape=jax.ShapeDtypeStruct((b * l, NSA_HEADS * NSA_HEAD_DIM), F32),
        scratch_shapes=[pltpu.VMEM((nqb, QB, QB), F32), pltpu.VMEM((ROWS, LANE), F32)],
        compiler_params=pltpu.CompilerParams(dimension_semantics=("arbitrary", "arbitrary"),
                                             vmem_limit_bytes=VMEM_LIMIT_BYTES),
        name="nsa_prompt",
    )(q, kv, small, cmp, cmp_bias, toe_bias, jnp.asarray(ovt.T, BF16), jnp.asarray(expand, BF16))


PAGES_PER_STEP = 16
NSEG_PAGE = PAGE_SIZE // NSA_CMP_STRIDE
SELF_RANK = 2


def _bf(x):
    return x.astype(BF16).astype(F32)


def _bias_rows(table, n_tiles, step, off, kmul):
    t = bias_tiles(table, n_tiles, step, off, kmul, qmul=0, rows=8)[:, :, 0, :]
    return t.transpose(1, 0, 2).reshape(NSA_KV_HEADS, NSA_GROUP, n_tiles * LANE)


def _nsa_sample_cmp_body(pt_ref, k_hbm, v_hbm, qp_ref, wk_ref, wv_ref, bias_ref, ovt_ref, oc_ref, idx_ref,
                         kbuf, vbuf, ksem, vsem, fk, sk, fv, sv, *, base, n_pages):
    slot = _paged_prefetch((k_hbm, v_hbm), (kbuf, vbuf), (ksem, vsem), pt_ref, base, n_pages)
    nseg = n_pages * NSEG_PAGE

    def compress_page(p, carry):
        for buf, w_ref, f_ref, s_ref in ((kbuf, wk_ref, fk, sk), (vbuf, wv_ref, fv, sv)):
            f = jnp.zeros((NSEG_PAGE, LANE), F32)
            s = jnp.zeros((NSEG_PAGE, LANE), F32)
            for r in range(NSA_CMP_STRIDE):
                rows = buf[slot, p, pl.ds(r, NSEG_PAGE, stride=NSA_CMP_STRIDE), :]
                f = f + rows * w_ref[r:r + 1, :]
                s = s + rows * w_ref[NSA_CMP_STRIDE + r:NSA_CMP_STRIDE + r + 1, :]
            row0 = pl.multiple_of(p * NSEG_PAGE, NSEG_PAGE)
            f_ref[pl.ds(row0, NSEG_PAGE), :] = f
            s_ref[pl.ds(row0, NSEG_PAGE), :] = s
        return carry

    lax.fori_loop(0, n_pages, compress_page, 0)

    def attend_and_select():
        rowi = lax.broadcasted_iota(jnp.int32, (nseg, LANE), 0)
        kc = jnp.where(rowi < nseg - 1, fk[...] + pltpu.roll(sk[...], nseg - 1, axis=0), 0.0).astype(BF16)
        vc = jnp.where(rowi < nseg - 1, fv[...] + pltpu.roll(sv[...], nseg - 1, axis=0), 0.0).astype(BF16)
        mc = lax.broadcasted_iota(jnp.int32, (NSA_GROUP, nseg), 1) < nseg - 1
        lane = lax.broadcasted_iota(jnp.int32, (8, LANE), 1)
        imps = []
        for h in range(NSA_KV_HEADS):
            qp = (qp_ref[0, h] * NSA_SCALE).astype(BF16)
            s = jnp.where(mc, _nt_dot(qp, kc) + bias_ref[h], NEG)
            e = jnp.exp(s - jnp.max(s, axis=-1, keepdims=True))
            p = e / jnp.sum(e, axis=-1, keepdims=True) * mc.astype(F32)
            pb = p.astype(BF16)
            oc_ref[0, h] = jnp.dot(pb, vc, preferred_element_type=F32)
            imps.append(_importance(jnp.sum(p, axis=0, keepdims=True), ovt_ref[...]))
        imp = jnp.concatenate(imps + [jnp.zeros((8 - NSA_KV_HEADS, LANE), F32)], axis=0)
        forced = (lane == 0) | (lane == LANE - 1)
        score = jnp.where(forced, NSA_FORCE, imp)
        cnt = (score < NSA_FORCE).astype(F32)
        for jp in range(LANE):
            col = score[:, jp:jp + 1]
            cnt = cnt + ((col > score) | ((col == score) & (lane > jp))).astype(F32)
        lanef = lane.astype(F32)
        idx = jnp.where(lane == SELF_RANK, float(LANE), 0.0)
        for k in range(NSA_TOPK):
            if k != SELF_RANK:
                v = jnp.sum(jnp.where(cnt == k, lanef, 0.0), axis=-1, keepdims=True)
                idx = jnp.where(lane == k, v, idx)
        idx_ref[0] = idx.astype(jnp.int32)

    attend_and_select()


def nsa_sample_cmp(qp, k_pool, v_pool, e, page_table, w_ck, w_cv, table):
    db, n_pages = page_table.shape
    n_pool = k_pool.shape[1]
    p0 = n_pages * PAGE_SIZE
    nseg = n_pages * NSEG_PAGE
    npg = PAGES_PER_STEP
    assert p0 // NSA_SEL_BLOCK == LANE and n_pages % npg == 0
    kp = k_pool.reshape(-1, PAGE_SIZE, NSA_KV_WIDTH)
    vp = v_pool.reshape(-1, PAGE_SIZE, NSA_KV_WIDTH)
    bias = _bias_rows(table, nseg // LANE, -LANE * NSA_CMP_STRIDE, p0 - (NSA_CMP_BLOCK - 1), NSA_CMP_STRIDE)
    jj = np.arange(LANE)[None, :]
    cc = np.arange(nseg)[:, None]
    lo = np.maximum(jj * NSA_SEL_BLOCK, cc * NSA_CMP_STRIDE)
    hi = np.minimum(jj * NSA_SEL_BLOCK + NSA_SEL_BLOCK, cc * NSA_CMP_STRIDE + NSA_CMP_BLOCK)
    ovt = np.where(cc < nseg - 1, np.maximum(hi - lo, 0) / NSA_CMP_STRIDE, 0.0)

    const = lambda shape: pl.BlockSpec(shape, lambda b, pt: (0,) * len(shape))
    hbm = pl.BlockSpec(memory_space=pl.ANY)
    page_buf = pltpu.VMEM((2, n_pages, PAGE_SIZE, NSA_KV_WIDTH), F32)
    grid_spec = pltpu.PrefetchScalarGridSpec(
        num_scalar_prefetch=1,
        grid=(db,),
        in_specs=[hbm, hbm, pl.BlockSpec((1, NSA_KV_HEADS, NSA_GROUP, LANE), lambda b, pt: (b, 0, 0, 0)),
                  const((NSA_CMP_BLOCK, NSA_KV_WIDTH)), const((NSA_CMP_BLOCK, NSA_KV_WIDTH)),
                  const((NSA_KV_HEADS, NSA_GROUP, nseg)), const((nseg, LANE))],
        out_specs=[pl.BlockSpec((1, NSA_KV_HEADS, NSA_GROUP, LANE), lambda b, pt: (b, 0, 0, 0)),
                   pl.BlockSpec((1, 8, LANE), lambda b, pt: (b, 0, 0))],
        scratch_shapes=[page_buf, page_buf, pltpu.SemaphoreType.DMA((2,)), pltpu.SemaphoreType.DMA((2,))]
        + [pltpu.VMEM((nseg, LANE), F32)] * 4)
    tile2 = lambda w: jnp.concatenate([w, w], axis=1).astype(F32)
    return pl.pallas_call(
        functools.partial(_nsa_sample_cmp_body, base=e * n_pool, n_pages=n_pages),
        grid_spec=grid_spec,
        out_shape=[jax.ShapeDtypeStruct((db, NSA_KV_HEADS, NSA_GROUP, LANE), F32),
                   jax.ShapeDtypeStruct((db, 8, LANE), jnp.int32)],
        compiler_params=pltpu.CompilerParams(dimension_semantics=("arbitrary",), vmem_limit_bytes=VMEM_LIMIT_BYTES),
        name="nsa_sample_cmp",
    )(page_table, kp, vp, qp, tile2(w_ck), tile2(w_cv), bias, jnp.asarray(ovt, BF16))


def _selected_block_copies(pt_ref, idx_ref, pools, bufs, sems, b, slot, base):
    half = NSA_SEL_BLOCK
    bpp = PAGE_SIZE // NSA_SEL_BLOCK
    copies = []
    for ent in range(NSA_KV_HEADS * NSA_TOPK):
        if ent % NSA_TOPK == SELF_RANK:
            continue
        j = jnp.minimum(idx_ref[b, ent], LANE - 1)
        page = pt_ref[b, j // bpp] + base
        rows = pl.ds(pl.multiple_of((j % bpp) * half, half), half)
        for pool_ref, buf_ref, sem_ref in zip(pools, bufs, sems):
            copies.append(pltpu.make_async_copy(pool_ref.at[page, rows], buf_ref.at[slot, ent], sem_ref.at[slot]))
    return copies


def _nsa_sample_attn_body(pt_ref, idx_ref, k_hbm, v_hbm, qp_ref, new_ref, kwin_ref, vwin_ref, oc_ref, gate_ref,
                          bnear_ref, bwin_ref, bends_ref, o_ref, kwo_ref, vwo_ref, kb, vb, ksem, vsem, *, nbuf, base):
    b = pl.program_id(0)
    cur = b % 2
    copies = functools.partial(_selected_block_copies, pt_ref, idx_ref, (k_hbm, v_hbm), (kb, vb), (ksem, vsem))

    @pl.when(b == 0)
    def _():
        for c in copies(b, cur, base):
            c.start()

    @pl.when(b + 1 < pl.num_programs(0))
    def _():
        for c in copies(b + 1, 1 - cur, base):
            c.start()

    for c in copies(b, cur, base):
        c.wait()
    new = new_ref[0]
    ks_new, vs_new, kw_new, vw_new = (new[:, c * LANE:(c + 1) * LANE] for c in (2, 3, 4, 5))
    kwin = kwin_ref[0]
    vwin = vwin_ref[0]
    wlane = lax.broadcasted_iota(jnp.int32, (NSA_GROUP, nbuf), 1)
    half = NSA_SEL_BLOCK
    for h in range(NSA_KV_HEADS):
        qp = (qp_ref[0, h] * NSA_SCALE).astype(BF16)
        qpf = qp.astype(F32)
        b_self = bends_ref[h][:, 0:1]
        b_far = bends_ref[h][:, 1:2]
        near = bnear_ref[h]
        scores, slots = [], []
        for k in range(NSA_TOPK):
            if k == SELF_RANK:
                continue
            j = idx_ref[b, h * NSA_TOPK + k]
            bias = jnp.where(j == LANE - 1, near[:, half:], jnp.where(j == LANE - 2, near[:, :half], b_far))
            scores.append(_nt_dot(qp, kb[cur, h * NSA_TOPK + k].astype(BF16)) + bias)
            slots.append(h * NSA_TOPK + k)
        s_self = jnp.sum(qpf * _bf(ks_new), axis=-1, keepdims=True) + b_self
        m = s_self
        for s in scores:
            m = jnp.maximum(m, jnp.max(s, axis=-1, keepdims=True))
        p_self = jnp.exp(s_self - m)
        l = p_self
        acc = _bf(p_self) * _bf(vs_new)
        for s, ent in zip(scores, slots):
            p = jnp.exp(s - m)
            l = l + jnp.sum(p, axis=-1, keepdims=True)
            acc = acc + jnp.dot(p.astype(BF16), vb[cur, ent].astype(BF16), preferred_element_type=F32)
        o_s = acc / l
        s = jnp.where(wlane >= 1, _nt_dot(qp, kwin.astype(BF16)) + bwin_ref[h], NEG)
        s_self = jnp.sum(qpf * _bf(kw_new), axis=-1, keepdims=True) + b_self
        m = jnp.maximum(jnp.max(s, axis=-1, keepdims=True), s_self)
        p = jnp.exp(s - m)
        p_self = jnp.exp(s_self - m)
        l = jnp.sum(p, axis=-1, keepdims=True) + p_self
        o_w = (jnp.dot(p.astype(BF16), vwin.astype(BF16), preferred_element_type=F32)
               + _bf(p_self) * _bf(vw_new)) / l
        gates = jax.nn.sigmoid(gate_ref[0, h])
        o_h = gates[:, 0:1] * oc_ref[0, h] + gates[:, 1:2] * o_s + gates[:, 2:3] * o_w
        o_rot = pltpu.roll(o_h, NSA_HEAD_DIM, axis=1)
        low = lax.broadcasted_iota(jnp.int32, (1, LANE), 1) < NSA_HEAD_DIM
        for pr in range(NSA_GROUP // 2):
            a = (o_h if h == 0 else o_rot)[2 * pr:2 * pr + 1]
            b2 = (o_rot if h == 0 else o_h)[2 * pr + 1:2 * pr + 2]
            c0 = (h * NSA_GROUP + 2 * pr) * NSA_HEAD_DIM
            o_ref[0, :, c0:c0 + LANE] = jnp.where(low, a, b2)
    rowi = lax.broadcasted_iota(jnp.int32, (nbuf, LANE), 0)
    kwo_ref[0] = jnp.where(rowi == nbuf - 1, kw_new, pltpu.roll(kwin, nbuf - 1, axis=0))
    vwo_ref[0] = jnp.where(rowi == nbuf - 1, vw_new, pltpu.roll(vwin, nbuf - 1, axis=0))


def nsa_sample_attn(qp, kv_new, idx, o_c, gates_raw, k_pool, v_pool, e, k_win, v_win, page_table, table):
    db, n_pages = page_table.shape
    n_pool = k_pool.shape[1]
    p0 = n_pages * PAGE_SIZE
    nbuf = k_win.shape[1]
    nb = NSA_KV_HEADS * NSA_TOPK
    bpp = PAGE_SIZE // NSA_SEL_BLOCK
    assert nbuf == NSA_WINDOW and nbuf % LANE == 0 and p0 >= nbuf
    kp = k_pool.reshape(-1, PAGE_SIZE, NSA_KV_WIDTH)
    vp = v_pool.reshape(-1, PAGE_SIZE, NSA_KV_WIDTH)
    bnear = _bias_rows(table, 1, 0, 2 * NSA_SEL_BLOCK, 1)
    bwin = _bias_rows(table, nbuf // LANE, -LANE, nbuf, 1)
    tb = table.astype(F32).reshape(REL_BUCKETS, NSA_KV_HEADS, NSA_GROUP)
    bends = jnp.pad(jnp.stack([tb[0], tb[REL_BUCKETS - 1]], axis=-1), ((0, 0), (0, 0), (0, LANE - 2)))

    per_b = lambda shape: pl.BlockSpec((1,) + shape, lambda b, pt, ix: (b,) + (0,) * len(shape))
    const = lambda shape: pl.BlockSpec(shape, lambda b, pt, ix: (0,) * len(shape))
    hbm = pl.BlockSpec(memory_space=pl.ANY)
    head_shape = (NSA_KV_HEADS, NSA_GROUP, LANE)
    blk_buf = pltpu.VMEM((2, nb, NSA_SEL_BLOCK, NSA_KV_WIDTH), F32)
    grid_spec = pltpu.PrefetchScalarGridSpec(
        num_scalar_prefetch=2,
        grid=(db,),
        in_specs=[hbm, hbm,
                  per_b(head_shape), per_b((1, 6 * NSA_KV_WIDTH)), per_b((nbuf, NSA_KV_WIDTH)), per_b((nbuf, NSA_KV_WIDTH)),
                  per_b(head_shape), per_b(head_shape), const(head_shape),
                  const((NSA_KV_HEADS, NSA_GROUP, nbuf)), const(head_shape)],
        out_specs=[per_b((1, NSA_HEADS * NSA_HEAD_DIM)), per_b((nbuf, NSA_KV_WIDTH)), per_b((nbuf, NSA_KV_WIDTH))],
        scratch_shapes=[blk_buf, blk_buf, pltpu.SemaphoreType.DMA((2,)), pltpu.SemaphoreType.DMA((2,))])
    return pl.pallas_call(
        functools.partial(_nsa_sample_attn_body, nbuf=nbuf, base=e * n_pool),
        grid_spec=grid_spec,
        out_shape=[jax.ShapeDtypeStruct((db, 1, NSA_HEADS * NSA_HEAD_DIM), F32),
                   jax.ShapeDtypeStruct((db, nbuf, NSA_KV_WIDTH), F32),
                   jax.ShapeDtypeStruct((db, nbuf, NSA_KV_WIDTH), F32)],
        compiler_params=pltpu.CompilerParams(dimension_semantics=("arbitrary",)),
        name="nsa_sample_attn",
    )(page_table, idx, kp, vp, qp, kv_new.reshape(db, 1, -1),
      k_win.reshape(db, nbuf, NSA_KV_WIDTH), v_win.reshape(db, nbuf, NSA_KV_WIDTH), o_c, gates_raw,
      bnear, bwin, bends)


def nsa_sample_mixer(q, kv, small, e, k_cmp_pool, v_cmp_pool, k_slc_pool, v_slc_pool, k_win, v_win,
                     page_table, w_ck, w_cv, table):
    db = q.shape[0]
    qh = q.reshape(db, NSA_KV_HEADS, NSA_GROUP, NSA_HEAD_DIM)
    zeros = jnp.zeros_like(qh[:, 0])
    qp = jnp.stack([jnp.concatenate([qh[:, 0], zeros], axis=-1), jnp.concatenate([zeros, qh[:, 1]], axis=-1)], axis=1)
    o_c, idx = nsa_sample_cmp(qp, k_cmp_pool, v_cmp_pool, e, page_table, w_ck, w_cv, table)
    idx = idx[:, :NSA_KV_HEADS, :NSA_TOPK].reshape(db, NSA_KV_HEADS * NSA_TOPK)
    g = small[:, SMALL_GATE:SMALL_GATE + 3 * NSA_HEADS].reshape(db, 3, NSA_KV_HEADS, NSA_GROUP).transpose(0, 2, 3, 1)
    g = jnp.pad(g, ((0, 0), (0, 0), (0, 0), (0, LANE - 3)))
    y, kw_n, vw_n = nsa_sample_attn(qp, kv, idx, o_c, g, k_slc_pool, v_slc_pool, e, k_win, v_win, page_table, table)
    shape = (db, -1, NSA_KV_HEADS, NSA_HEAD_DIM)
    return y.reshape(db, NSA_HEADS * NSA_HEAD_DIM), kw_n.reshape(shape), vw_n.reshape(shape)


def _bmm_body(a_ref, b_ref, o_ref):
    o_ref[0] = jnp.dot(a_ref[0].astype(BF16), b_ref[0].astype(BF16), preferred_element_type=F32)


def batched_matmul(a, b):
    hh, m, k = a.shape
    n = b.shape[2]
    return pl.pallas_call(
        _bmm_body,
        grid=(hh,),
        in_specs=[pl.BlockSpec((1, m, k), lambda i: (i, 0, 0)), pl.BlockSpec((1, k, n), lambda i: (i, 0, 0))],
        out_specs=pl.BlockSpec((1, m, n), lambda i: (i, 0, 0)),
        out_shape=jax.ShapeDtypeStruct((hh, m, n), F32),
        compiler_params=pltpu.CompilerParams(dimension_semantics=("arbitrary",)),
        name="batched_matmul",
    )(a, b)


def _page_copy(pool_ref, buf_ref, sem_ref, page, slot, p):
    return pltpu.make_async_copy(pool_ref.at[page], buf_ref.at[slot, p], sem_ref.at[slot])


def _start_pages(pools, bufs, sems, pt_ref, b, slot, base, n_pages):
    def issue(p, carry):
        page = pt_ref[b, p] + base
        for pool_ref, buf_ref, sem_ref in zip(pools, bufs, sems):
            _page_copy(pool_ref, buf_ref, sem_ref, page, slot, p).start()
        return carry
    lax.fori_loop(0, n_pages, issue, 0)


def _wait_pages(pools, bufs, sems, slot, n_pages):
    def wait(p, carry):
        for pool_ref, buf_ref, sem_ref in zip(pools, bufs, sems):
            _page_copy(pool_ref, buf_ref, sem_ref, 0, slot, p).wait()
        return carry
    lax.fori_loop(0, n_pages, wait, 0)


def _paged_prefetch(pools, bufs, sems, pt_ref, base, n_pages):
    b = pl.program_id(0)
    slot = b % 2

    @pl.when(b == 0)
    def _():
        _start_pages(pools, bufs, sems, pt_ref, b, slot, base, n_pages)

    @pl.when(b + 1 < pl.num_programs(0))
    def _():
        _start_pages(pools, bufs, sems, pt_ref, b + 1, 1 - slot, base, n_pages)

    _wait_pages(pools, bufs, sems, slot, n_pages)
    return slot


def _mla_sample_body(pt_ref, lat_hbm, kr_hbm, qlat_ref, qrope_ref, cnew_ref, krnew_ref, o_ref,
                     cbuf, rbuf, csem, rsem, *, base, n_pages):
    slot = _paged_prefetch((lat_hbm, kr_hbm), (cbuf, rbuf), (csem, rsem), pt_ref, base, n_pages)
    npg = PAGES_PER_STEP
    ql = qlat_ref[0].astype(BF16)
    qr = qrope_ref[0].astype(BF16)
    chunks, scores = [], []
    for ch in range(n_pages // npg):
        c = cbuf[slot, pl.ds(ch * npg, npg)].reshape(npg * PAGE_SIZE, MLA_KV_RANK).astype(BF16)
        r = rbuf[slot, pl.ds(ch * npg, npg)].reshape(npg * PAGE_SIZE, MLA_ROPE).astype(BF16)
        scores.append((_nt_dot(ql, c) + _nt_dot(qr, r)) * MLA_SCALE)
    cn = _bf(cnew_ref[0])
    kn = _bf(krnew_ref[0])
    s_self = (jnp.sum(ql.astype(F32) * cn, axis=-1, keepdims=True)
              + jnp.sum(qr.astype(F32) * kn, axis=-1, keepdims=True)) * MLA_SCALE
    m = s_self
    for s in scores:
        m = jnp.maximum(m, jnp.max(s, axis=-1, keepdims=True))
    p_self = jnp.exp(s_self - m)
    l = p_self
    acc = _bf(p_self) * cn
    for ch, s in enumerate(scores):
        p = jnp.exp(s - m)
        l = l + jnp.sum(p, axis=-1, keepdims=True)
        c = cbuf[slot, pl.ds(ch * npg, npg)].reshape(npg * PAGE_SIZE, MLA_KV_RANK).astype(BF16)
        acc = acc + jnp.dot(p.astype(BF16), c, preferred_element_type=F32)
    o_ref[0] = acc / l


def mla_sample_attention(q_lat, q_rope, c_new, kr_new, lat_pool, kr_pool, o, page_table):
    db, n_pages = page_table.shape
    n_pool = lat_pool.shape[1]
    npg = PAGES_PER_STEP
    assert n_pages % npg == 0
    cp = lat_pool.reshape(-1, PAGE_SIZE, MLA_KV_RANK)
    rp = kr_pool.reshape(-1, PAGE_SIZE, MLA_ROPE)

    per_b = lambda shape: pl.BlockSpec((1,) + shape, lambda b, pt: (b,) + (0,) * len(shape))
    hbm = pl.BlockSpec(memory_space=pl.ANY)
    grid_spec = pltpu.PrefetchScalarGridSpec(
        num_scalar_prefetch=1,
        grid=(db,),
        in_specs=[hbm, hbm, per_b((MLA_HEADS, MLA_KV_RANK)), per_b((MLA_HEADS, MLA_ROPE)),
                  per_b((1, MLA_KV_RANK)), per_b((1, MLA_ROPE))],
        out_specs=per_b((MLA_HEADS, MLA_KV_RANK)),
        scratch_shapes=[pltpu.VMEM((2, n_pages, PAGE_SIZE, MLA_KV_RANK), F32),
                        pltpu.VMEM((2, n_pages, PAGE_SIZE, MLA_ROPE), F32),
                        pltpu.SemaphoreType.DMA((2,)), pltpu.SemaphoreType.DMA((2,))])
    return pl.pallas_call(
        functools.partial(_mla_sample_body, base=o * n_pool, n_pages=n_pages),
        grid_spec=grid_spec,
        out_shape=jax.ShapeDtypeStruct((db, MLA_HEADS, MLA_KV_RANK), F32),
        compiler_params=pltpu.CompilerParams(dimension_semantics=("arbitrary",), vmem_limit_bytes=VMEM_LIMIT_BYTES),
        name="mla_sample",
    )(page_table, cp, rp, q_lat, q_rope, c_new.reshape(db, 1, -1), kr_new.reshape(db, 1, -1))


def mla_sample_mixer(q_nope, q_rope, c, k_rope, lat_pool, kr_pool, o_idx, page_table, w_kvb):
    db = c.shape[0]
    w = w_kvb.reshape(MLA_KV_RANK, MLA_HEADS, MLA_NOPE + MLA_V).transpose(1, 0, 2)
    q_lat = batched_matmul(q_nope.transpose(1, 0, 2), w[..., :MLA_NOPE].transpose(0, 2, 1)).transpose(1, 0, 2)
    o_lat = mla_sample_attention(q_lat, q_rope, c, k_rope, lat_pool, kr_pool, o_idx, page_table)
    out = batched_matmul(o_lat.transpose(1, 0, 2), w[..., MLA_NOPE:])
    return out.transpose(1, 0, 2).reshape(db, MLA_HEADS * MLA_V)


OD_GROUPS = (GLA_HEADS * GLA_DK, GLA_HEADS * GLA_DK, GLA_HEADS * GLA_DV, GLA_HEADS * GLA_DV, MLA_Q_RANK, MLA_KV_RANK, LANE)
MLA_PREP_ROWS = 128
MLA_TQ = 256
MLA_HEAD_GROUP = 4


def odd_weight(w_in):
    gq, gk, gv, g1, r, qa, kva, kr = split_cols(w_in, OD_SPLITS)
    z = lambda n: jnp.zeros((w_in.shape[0], n), w_in.dtype)
    small = jnp.concatenate([g1, z(MLA_NOPE - GLA_GATE_RANK), kr, z(LANE - MLA_NOPE - MLA_ROPE)], axis=1)
    return jnp.concatenate([gq, gk, gv, r, qa, kva, small], axis=1)


def _head_blocks(w, widths, keep):
    k = w.shape[0]
    w = w.reshape(k, MLA_HEADS, sum(widths))[:, :, keep[0]:keep[1]]
    return jnp.pad(w, ((0, 0), (0, 0), (0, LANE - (keep[1] - keep[0])))).reshape(k, MLA_HEADS * LANE)


def rope_tables(pos):
    half = MLA_ROPE // 2
    freqs = ROPE_THETA ** (-jnp.arange(half, dtype=F32) / half)
    ang = pos.astype(F32)[:, None] * freqs
    cos, sin = jnp.cos(ang), jnp.sin(ang)
    n = pos.shape[0]
    z = lambda w: jnp.zeros((n, w), F32)
    tail = LANE - MLA_NOPE - MLA_ROPE
    c = jnp.concatenate([jnp.ones((n, MLA_NOPE), F32), cos, cos, z(tail)], axis=1)
    s1 = jnp.concatenate([z(MLA_NOPE), -sin, z(half), z(tail)], axis=1)
    s2 = jnp.concatenate([z(MLA_NOPE), z(half), sin, z(tail)], axis=1)
    return c, s1, s2


def _rope_block(x, c, s1, s2):
    half = MLA_ROPE // 2
    return x * c + pltpu.roll(x, LANE - half, axis=1) * s1 + pltpu.roll(x, half, axis=1) * s2


def _mla_prep_body(qa_ref, kva_ref, sm_ref, qg_ref, kg_ref, wq_ref, wk_ref, wv_ref, c_ref, s1_ref, s2_ref,
                   q_out, c_out, k_out, v_out, kr_out):
    norm = lambda x, g: x * lax.rsqrt(jnp.mean(x * x, axis=-1, keepdims=True) + EPS) * g
    c, s1, s2 = c_ref[...], s1_ref[...], s2_ref[...]
    lane = lax.broadcasted_iota(jnp.int32, c.shape, 1)
    q = jnp.dot(norm(qa_ref[...], qg_ref[...]).astype(BF16), wq_ref[...], preferred_element_type=F32)
    lat = norm(kva_ref[...], kg_ref[...])
    c_out[...] = lat
    latb = lat.astype(BF16)
    kn = jnp.dot(latb, wk_ref[...], preferred_element_type=F32)
    v_out[...] = jnp.dot(latb, wv_ref[...], preferred_element_type=F32).astype(BF16)
    kr = jnp.where(lane >= MLA_NOPE, _rope_block(sm_ref[...], c, s1, s2), 0.0)
    kr_out[...] = kr
    for h in range(MLA_HEADS):
        cols = slice(h * LANE, (h + 1) * LANE)
        q_out[:, cols] = _rope_block(q[:, cols], c, s1, s2)
        k_out[:, cols] = (kn[:, cols] + kr).astype(BF16)


def mla_prep(qa, kva, small, q_norm, w_qb, kv_norm, w_kvb, tables, pos_block):
    m = qa.shape[0]
    tm = MLA_PREP_ROWS
    wq = _head_blocks(w_qb, (MLA_NOPE, MLA_ROPE), (0, MLA_NOPE + MLA_ROPE)).astype(BF16)
    wk = _head_blocks(w_kvb, (MLA_NOPE, MLA_V), (0, MLA_NOPE)).astype(BF16)
    wv = _head_blocks(w_kvb, (MLA_NOPE, MLA_V), (MLA_NOPE, MLA_NOPE + MLA_V)).astype(BF16)
    wide = MLA_HEADS * LANE
    rows = lambda w: pl.BlockSpec((tm, w), lambda i: (i, 0))
    const = lambda a: pl.BlockSpec(a.shape, lambda i: (0, 0))
    tab = pl.BlockSpec((tm, LANE), lambda i: (pos_block(i), 0))
    qg = q_norm.reshape(1, -1).astype(F32)
    kg = kv_norm.reshape(1, -1).astype(F32)
    return pl.pallas_call(
        _mla_prep_body,
        grid=(m // tm,),
        in_specs=[rows(MLA_Q_RANK), rows(MLA_KV_RANK), rows(LANE), const(qg), const(kg), const(wq), const(wk), const(wv),
                  tab, tab, tab],
        out_specs=[rows(wide), rows(MLA_KV_RANK), rows(wide), rows(wide), rows(LANE)],
        out_shape=[jax.ShapeDtypeStruct((m, wide), F32), jax.ShapeDtypeStruct((m, MLA_KV_RANK), F32),
                   jax.ShapeDtypeStruct((m, wide), BF16), jax.ShapeDtypeStruct((m, wide), BF16),
                   jax.ShapeDtypeStruct((m, LANE), F32)],
        compiler_params=pltpu.CompilerParams(dimension_semantics=("arbitrary",), vmem_limit_bytes=VMEM_LIMIT_BYTES),
        name="mla_prep",
    )(qa, kva, small, qg, kg, wq, wk, wv, *tables)


def _mla_prompt_body(q_ref, k_ref, v_ref, o_ref):
    tq = q_ref.shape[0]
    qi = pl.program_id(1)
    row = lax.broadcasted_iota(jnp.int32, (tq, tq), 0)
    col = lax.broadcasted_iota(jnp.int32, (tq, tq), 1)
    low = lax.broadcasted_iota(jnp.int32, (tq, LANE), 1) < MLA_V
    outs = []
    for h0 in range(0, MLA_HEADS, MLA_HEAD_GROUP):
        heads = range(h0, h0 + MLA_HEAD_GROUP)
        hcols = [slice(h * LANE, (h + 1) * LANE) for h in heads]
        qs = [(q_ref[:, c] * MLA_SCALE).astype(BF16) for c in hcols]

        def body(kt, carry):
            k0 = pl.multiple_of(kt * tq, tq)
            vis = (kt - qi) * tq + col <= row
            ss = [jnp.where(vis, _nt_dot(q, k_ref[pl.ds(k0, tq), c]), NEG) for q, c in zip(qs, hcols)]
            m_new = [jnp.maximum(st[0], jnp.max(s, axis=-1, keepdims=True)) for st, s in zip(carry, ss)]
            alpha = [jnp.exp(st[0] - mn) for st, mn in zip(carry, m_new)]
            ps = [jnp.exp(s - mn) for s, mn in zip(ss, m_new)]
            l_new = [a * st[1] + jnp.sum(p, axis=-1, keepdims=True) for a, st, p in zip(alpha, carry, ps)]
            pv = [jnp.dot(p.astype(BF16), v_ref[pl.ds(k0, tq), c], preferred_element_type=F32)
                  for p, c in zip(ps, hcols)]
            return tuple((mn, ln, a * st[2] + x) for mn, ln, a, st, x in zip(m_new, l_new, alpha, carry, pv))

        init = tuple((jnp.full((tq, 1), NEG, F32), jnp.zeros((tq, 1), F32), jnp.zeros((tq, LANE), F32))
                     for _ in heads)
        final = lax.fori_loop(0, qi + 1, body, init)
        outs.extend(acc / l for _, l, acc in final)
    for pr in range(MLA_HEADS // 2):
        o_ref[:, pr * LANE:(pr + 1) * LANE] = jnp.where(low, outs[2 * pr], pltpu.roll(outs[2 * pr + 1], MLA_V, axis=1))


def mla_prompt_attention(q_rot, k_full, v_pad, b, l):
    tq = MLA_TQ
    nq = l // tq
    wide = MLA_HEADS * LANE
    return pl.pallas_call(
        _mla_prompt_body,
        grid=(b, nq),
        in_specs=[pl.BlockSpec((tq, wide), lambda bi, i: (bi * nq + i, 0)),
                  pl.BlockSpec((l, wide), lambda bi, i: (bi, 0)),
                  pl.BlockSpec((l, wide), lambda bi, i: (bi, 0))],
        out_specs=pl.BlockSpec((tq, MLA_HEADS * MLA_V), lambda bi, i: (bi * nq + i, 0)),
        out_shape=jax.ShapeDtypeStruct((b * l, MLA_HEADS * MLA_V), F32),
        compiler_params=pltpu.CompilerParams(dimension_semantics=("arbitrary", "arbitrary"),
                                             vmem_limit_bytes=VMEM_LIMIT_BYTES),
        name="mla_prompt",
    )(q_rot, k_full, v_pad)


SSD_PAIRS = SSD_HEADS // 2
PAIR_PER_GROUP = SSD_PAIRS // SSD_GROUPS
CONV_PAD = 8


def _exact_dot(a, b):
    return jnp.dot(a, b, preferred_element_type=F32, precision=lax.Precision.HIGHEST)


def _softplus(x):
    u = jnp.exp(-jnp.abs(x))
    w = 1.0 + u
    return jnp.maximum(x, 0.0) + jnp.where(w == 1.0, u, jnp.log(w) * (u / (w - 1.0)))


def _silu(x):
    return x * jax.nn.sigmoid(x)


def _group_rmsnorm(y, g):
    gw = SSD_INNER // SSD_GROUPS
    outs = []
    for gi in range(SSD_GROUPS):
        yg = y[:, gi * gw:(gi + 1) * gw]
        outs.append(yg * lax.rsqrt(jnp.mean(yg * yg, axis=-1, keepdims=True) + EPS))
    return jnp.concatenate(outs, axis=1) * g


def _ssd_prompt_body(z_ref, xbc_ref, sm_ref, cw_ref, cb_ref, dtb_ref, alog_ref, d_ref, g_ref, ex_ref,
                     y_ref, st_ref, cv_ref, buf, state):
    c = pl.program_id(1)
    nc = pl.num_programs(1)
    t = SSD_CHUNK

    @pl.when(c == 0)
    def _():
        buf[0:CONV_PAD, :] = jnp.zeros((CONV_PAD, SSD_CONV_CH), F32)
        state[...] = jnp.zeros_like(state)

    buf[CONV_PAD:CONV_PAD + t, :] = xbc_ref[...]
    conv = cb_ref[...] + cw_ref[SSD_CONV - 1:SSD_CONV, :] * xbc_ref[...]
    for k in range(SSD_CONV - 1):
        conv = conv + cw_ref[k:k + 1, :] * buf[pl.ds(CONV_PAD - (SSD_CONV - 1) + k, t), :]
    buf[0:CONV_PAD, :] = buf[t:t + CONV_PAD, :]
    xc = _silu(conv)
    xs = xc[:, :SSD_INNER]
    bm = xc[:, SSD_INNER:SSD_INNER + SSD_GROUPS * SSD_STATE].astype(BF16)
    cm = xc[:, SSD_INNER + SSD_GROUPS * SSD_STATE:].astype(BF16)

    dt = _softplus(sm_ref[...] + dtb_ref[...])
    da = dt * (-jnp.exp(alog_ref[...]))
    row = lax.broadcasted_iota(jnp.int32, (t, t), 0)
    col = lax.broadcasted_iota(jnp.int32, (t, t), 1)
    tril = row >= col
    a_cs = _exact_dot(tril.astype(F32), da)
    a_cs_t = a_cs.T
    ex = ex_ref[...]
    dt_x = _exact_dot(dt, ex)
    acs_x = _exact_dot(a_cs, ex)
    last_x = acs_x[t - 1:t, :]
    xd = xs * dt_x
    xdw = xd * jnp.exp(last_x - acs_x)
    xdb = xd.astype(BF16)
    e_in = jnp.exp(acs_x)
    low = lax.broadcasted_iota(jnp.int32, (t, LANE), 1) < SSD_HEADDIM
    zero = jnp.zeros((t, LANE), BF16)

    y_parts = []
    for j in range(SSD_PAIRS):
        g = j // PAIR_PER_GROUP
        cg = cm[:, g * SSD_STATE:(g + 1) * SSD_STATE]
        bg = bm[:, g * SSD_STATE:(g + 1) * SSD_STATE]
        cb = _nt_dot(cg, bg)
        cols = slice(j * LANE, (j + 1) * LANE)
        xp = xdb[:, cols]
        y = jnp.zeros((t, LANE), F32)
        for hh in range(2):
            h = 2 * j + hh
            decay = jnp.where(tril, jnp.exp(a_cs[:, h:h + 1] - a_cs_t[h:h + 1, :]), 0.0)
            xh = jnp.where(low if hh == 0 else jnp.logical_not(low), xp, zero)
            y = y + jnp.dot((cb * decay).astype(BF16), xh, preferred_element_type=F32)
        s_old = state[j]
        y = y + _nt_dot(cg, s_old.astype(BF16)) * e_in[:, cols]
        dec_rows = jnp.broadcast_to(jnp.exp(last_x[:, cols]), (LANE, LANE)).T
        state[j] = dec_rows * s_old + jnp.dot(xdw[:, cols].T.astype(BF16), bg, preferred_element_type=F32)
        y_parts.append(y)
    y = jnp.concatenate(y_parts, axis=1) + xs * d_ref[...]
    y_ref[...] = _group_rmsnorm(y * _silu(z_ref[...]), g_ref[...])

    @pl.when(c == nc - 1)
    def _():
        st_ref[0] = state[...]
        cv_ref[0] = xbc_ref[t - (SSD_CONV - 1):t, :]


def _head_expand():
    return jnp.asarray(np.arange(LANE)[:, None] == (np.arange(SSD_INNER)[None, :] // SSD_HEADDIM), F32)


def _ssd_params(conv_w, conv_b, dt_bias, a_log, d_skip, norm_g):
    pad = lambda v: jnp.pad(v.astype(F32), (0, LANE - SSD_HEADS)).reshape(1, LANE)
    return (conv_w.astype(F32), conv_b.astype(F32).reshape(1, -1), pad(dt_bias), pad(a_log),
            jnp.repeat(d_skip.astype(F32), SSD_HEADDIM).reshape(1, -1), norm_g.astype(F32).reshape(1, -1), _head_expand())


def ssd_prompt(z, xbc, small, b, l, conv_w, conv_b, dt_bias, a_log, d_skip, norm_g):
    t = SSD_CHUNK
    nc = l // t
    params = _ssd_params(conv_w, conv_b, dt_bias, a_log, d_skip, norm_g)
    rows = lambda w: pl.BlockSpec((t, w), lambda bi, c: (bi * nc + c, 0))
    const = lambda a: pl.BlockSpec(a.shape, lambda bi, c: (0,) * a.ndim)
    y, st, cv = pl.pallas_call(
        _ssd_prompt_body,
        grid=(b, nc),
        in_specs=[rows(SSD_INNER), rows(SSD_CONV_CH), rows(LANE)] + [const(p) for p in params],
        out_specs=[rows(SSD_INNER),
                   pl.BlockSpec((1, SSD_PAIRS, LANE, SSD_STATE), lambda bi, c: (bi, 0, 0, 0)),
                   pl.BlockSpec((1, SSD_CONV - 1, SSD_CONV_CH), lambda bi, c: (bi, 0, 0))],
        out_shape=[jax.ShapeDtypeStruct((b * l, SSD_INNER), F32),
                   jax.ShapeDtypeStruct((b, SSD_PAIRS, LANE, SSD_STATE), F32),
                   jax.ShapeDtypeStruct((b, SSD_CONV - 1, SSD_CONV_CH), F32)],
        scratch_shapes=[pltpu.VMEM((CONV_PAD + t, SSD_CONV_CH), F32), pltpu.VMEM((SSD_PAIRS, LANE, SSD_STATE), F32)],
        compiler_params=pltpu.CompilerParams(dimension_semantics=("arbitrary", "arbitrary"),
                                             vmem_limit_bytes=VMEM_LIMIT_BYTES),
        name="ssd_prompt",
    )(z, xbc, small, *params)
    return y, st.reshape(b, SSD_HEADS, SSD_HEADDIM, SSD_STATE), cv


def _ssd_sample_body(z_ref, xbc_ref, sm_ref, cs_ref, st_ref, cw_ref, cb_ref, dtb_ref, alog_ref, d_ref, g_ref, ex_ref,
                     y_ref, sto_ref, cvo_ref):
    x_new = xbc_ref[0]
    cs = cs_ref[0]
    conv = cb_ref[...] + cw_ref[SSD_CONV - 1:SSD_CONV, :] * x_new
    for k in range(SSD_CONV - 1):
        conv = conv + cw_ref[k:k + 1, :] * cs[k:k + 1, :]
    cvo_ref[0] = jnp.concatenate([cs[1:], x_new], axis=0)
    xc = _silu(conv)
    xs = xc[:, :SSD_INNER]
    bm = xc[:, SSD_INNER:SSD_INNER + SSD_GROUPS * SSD_STATE]
    cm = xc[:, SSD_INNER + SSD_GROUPS * SSD_STATE:]
    dt = jnp.broadcast_to(_softplus(sm_ref[0] + dtb_ref[...]), (8, LANE))
    da = dt * (-jnp.exp(alog_ref[...]))
    ex = ex_ref[...]
    xd = xs * _exact_dot(dt, ex)[0:1]
    dec_x = jnp.exp(_exact_dot(da, ex)[0:1])
    y_parts = []
    for j in range(SSD_PAIRS):
        g = j // PAIR_PER_GROUP
        cols = slice(j * LANE, (j + 1) * LANE)
        bg = bm[:, g * SSD_STATE:(g + 1) * SSD_STATE]
        cg = jnp.broadcast_to(cm[:, g * SSD_STATE:(g + 1) * SSD_STATE], (8, SSD_STATE)).astype(BF16)
        dec_rows = jnp.broadcast_to(dec_x[:, cols], (LANE, LANE)).T
        xd_rows = jnp.broadcast_to(xd[:, cols], (LANE, LANE)).T
        s_new = dec_rows * st_ref[0, j] + xd_rows * bg
        sto_ref[0, j] = s_new
        y_parts.append(_nt_dot(cg, s_new.astype(BF16))[0:1])
    y = jnp.concatenate(y_parts, axis=1) + xs * d_ref[...]
    y_ref[0] = _group_rmsnorm(y * _silu(z_ref[0]), g_ref[...])


def ssd_sample(z, xbc, small, conv_state, ssm_state, conv_w, conv_b, dt_bias, a_log, d_skip, norm_g):
    db = z.shape[0]
    params = _ssd_params(conv_w, conv_b, dt_bias, a_log, d_skip, norm_g)
    per_b = lambda shape: pl.BlockSpec((1,) + shape, lambda i: (i,) + (0,) * len(shape))
    const = lambda a: pl.BlockSpec(a.shape, lambda i: (0,) * a.ndim)
    st_shape = (SSD_PAIRS, LANE, SSD_STATE)
    y, st, cv = pl.pallas_call(
        _ssd_sample_body,
        grid=(db,),
        in_specs=[per_b((1, SSD_INNER)), per_b((1, SSD_CONV_CH)), per_b((1, LANE)), per_b((SSD_CONV - 1, SSD_CONV_CH)),
                  per_b(st_shape)] + [const(p) for p in params],
        out_specs=[per_b((1, SSD_INNER)), per_b(st_shape), per_b((SSD_CONV - 1, SSD_CONV_CH))],
        out_shape=[jax.ShapeDtypeStruct((db, 1, SSD_INNER), F32), jax.ShapeDtypeStruct((db,) + st_shape, F32),
                   jax.ShapeDtypeStruct((db, SSD_CONV - 1, SSD_CONV_CH), F32)],
        compiler_params=pltpu.CompilerParams(dimension_semantics=("arbitrary",)),
        name="ssd_sample",
    )(z.reshape(db, 1, -1), xbc.reshape(db, 1, -1), small.reshape(db, 1, -1), conv_state,
      ssm_state.reshape((db,) + st_shape), *params)
    return y.reshape(db, SSD_INNER), st.reshape(db, SSD_HEADS, SSD_HEADDIM, SSD_STATE), cv


GLA_BLOCK = 128
GLA_INNER = GLA_HEADS * GLA_DV


def _log_gate(small, w2_ref, bg_ref):
    logits = jnp.dot(small.astype(BF16), w2_ref[...], preferred_element_type=F32) + bg_ref[...]
    return -_softplus(-logits) / GLA_TAU


def _head_rmsnorm_gate(o_heads, g_ref, r):
    outs = [o * lax.rsqrt(jnp.mean(o * o, axis=-1, keepdims=True) + EPS) for o in o_heads]
    return jnp.concatenate(outs, axis=1) * g_ref[...] * _silu(r)


def _gla_prompt_body(q_ref, k_ref, v_ref, r_ref, sm_ref, w2_ref, bg_ref, g_ref, y_ref, st_ref, state):
    blk = pl.program_id(1)
    t = GLA_BLOCK

    @pl.when(blk == 0)
    def _():
        state[...] = jnp.zeros_like(state)

    la = _log_gate(sm_ref[...], w2_ref, bg_ref)
    row = lax.broadcasted_iota(jnp.int32, (t, t), 0)
    col = lax.broadcasted_iota(jnp.int32, (t, t), 1)
    same = (row // GLA_CHUNK) == (col // GLA_CHUNK)
    causal = same & (row >= col)
    bc = _exact_dot(causal.astype(F32), la)
    is_last = col == (row // GLA_CHUNK) * GLA_CHUNK + GLA_CHUNK - 1
    bl = _exact_dot(is_last.astype(F32), bc)
    qt = q_ref[...] * (GLA_DK ** -0.5) * jnp.exp(bc)
    kt = k_ref[...] * jnp.exp(-bc)
    kd = k_ref[...] * jnp.exp(bl - bc)
    heads = range(GLA_HEADS)
    kcs = [slice(h * GLA_DK, (h + 1) * GLA_DK) for h in heads]
    vbs = [v_ref[:, h * GLA_DV:(h + 1) * GLA_DV].astype(BF16) for h in heads]
    qbs = [qt[:, kc].astype(BF16) for kc in kcs]
    o_intra = []
    for h in heads:
        att = jnp.where(causal, _nt_dot(qbs[h], kt[:, kcs[h]].astype(BF16)), 0.0)
        o_intra.append(jnp.dot(att.astype(BF16), vbs[h], preferred_element_type=F32))
    kd_t = [kd[:, kc].T for kc in kcs]
    dec_t = [jnp.exp(bl[:, kc]).T for kc in kcs]
    ss = [state[h] for h in heads]
    inter = [[] for _ in heads]
    for c in range(t // GLA_CHUNK):
        r0 = c * GLA_CHUNK
        in_chunk = (col // GLA_CHUNK) == c
        for h in heads:
            inter[h].append(jnp.dot(qbs[h][r0:r0 + GLA_CHUNK], ss[h].astype(BF16), preferred_element_type=F32))
        upd = [jnp.dot(jnp.where(in_chunk, kd_t[h], 0.0).astype(BF16), vbs[h], preferred_element_type=F32)
               for h in heads]
        ss = [dec_t[h][:, r0:r0 + 1] * ss[h] + upd[h] for h in heads]
    for h in heads:
        state[h] = ss[h]
    o_heads = [o_intra[h] + jnp.concatenate(inter[h], axis=0) for h in heads]
    y_ref[...] = _head_rmsnorm_gate(o_heads, g_ref, r_ref[...])

    @pl.when(blk == pl.num_programs(1) - 1)
    def _():
        st_ref[0] = state[...]


def _gla_params(w_g2, b_g, norm_g):
    w2 = jnp.pad(w_g2, ((0, LANE - GLA_GATE_RANK), (0, 0))).astype(BF16)
    return w2, b_g.astype(F32).reshape(1, -1), jnp.tile(norm_g.astype(F32), GLA_HEADS).reshape(1, -1)


def gla_prompt(gq, gk, gv, r, small, b, l, w_g2, b_g, norm_g):
    t = GLA_BLOCK
    nb = l // t
    params = _gla_params(w_g2, b_g, norm_g)
    rows = lambda w: pl.BlockSpec((t, w), lambda bi, i: (bi * nb + i, 0))
    const = lambda a: pl.BlockSpec(a.shape, lambda bi, i: (0,) * a.ndim)
    st_shape = (GLA_HEADS, GLA_DK, GLA_DV)
    return pl.pallas_call(
        _gla_prompt_body,
        grid=(b, nb),
        in_specs=[rows(GLA_HEADS * GLA_DK), rows(GLA_HEADS * GLA_DK), rows(GLA_INNER), rows(GLA_INNER), rows(LANE)]
        + [const(p) for p in params],
        out_specs=[rows(GLA_INNER), pl.BlockSpec((1,) + st_shape, lambda bi, i: (bi, 0, 0, 0))],
        out_shape=[jax.ShapeDtypeStruct((b * l, GLA_INNER), F32), jax.ShapeDtypeStruct((b,) + st_shape, F32)],
        scratch_shapes=[pltpu.VMEM(st_shape, F32)],
        compiler_params=pltpu.CompilerParams(dimension_semantics=("arbitrary", "arbitrary"),
                                             vmem_limit_bytes=VMEM_LIMIT_BYTES),
        name="gla_prompt",
    )(gq, gk, gv, r, small, *params)


def _gla_sample_body(q_ref, k_ref, v_ref, r_ref, sm_ref, st_ref, w2_ref, bg_ref, g_ref, y_ref, sto_ref):
    la = _log_gate(jnp.broadcast_to(sm_ref[0], (8, LANE)), w2_ref, bg_ref)[0:1]
    a = jnp.exp(la)
    qt = q_ref[0] * (GLA_DK ** -0.5) * a
    kt = k_ref[0] * jnp.exp(-la)
    o_heads = []
    for h in range(GLA_HEADS):
        kc = slice(h * GLA_DK, (h + 1) * GLA_DK)
        vb = _bf(v_ref[0, :, h * GLA_DV:(h + 1) * GLA_DV])
        qb = _bf(qt[:, kc])
        att = jnp.sum(qb * _bf(kt[:, kc]), axis=-1, keepdims=True)
        s_old = st_ref[0, h]
        o_inter = jnp.dot(jnp.broadcast_to(qb, (8, GLA_DK)).astype(BF16), s_old.astype(BF16),
                          preferred_element_type=F32)[0:1]
        a_rows = jnp.broadcast_to(a[:, kc], (GLA_DK, GLA_DK)).T[:, 0:1]
        k_rows = jnp.broadcast_to(k_ref[0, :, kc], (GLA_DK, GLA_DK)).T[:, 0:1]
        sto_ref[0, h] = a_rows * s_old + k_rows * v_ref[0, :, h * GLA_DV:(h + 1) * GLA_DV]
        o_heads.append(_bf(att) * vb + o_inter)
    y_ref[0] = _head_rmsnorm_gate(o_heads, g_ref, r_ref[0])


def gla_sample(gq, gk, gv, r, small, state, w_g2, b_g, norm_g):
    db = gq.shape[0]
    params = _gla_params(w_g2, b_g, norm_g)
    per_b = lambda shape: pl.BlockSpec((1,) + shape, lambda i: (i,) + (0,) * len(shape))
    const = lambda a: pl.BlockSpec(a.shape, lambda i: (0,) * a.ndim)
    st_shape = (GLA_HEADS, GLA_DK, GLA_DV)
    row3 = lambda x: x.reshape(db, 1, -1)
    y, st = pl.pallas_call(
        _gla_sample_body,
        grid=(db,),
        in_specs=[per_b((1, GLA_HEADS * GLA_DK)), per_b((1, GLA_HEADS * GLA_DK)), per_b((1, GLA_INNER)),
                  per_b((1, GLA_INNER)), per_b((1, LANE)), per_b(st_shape)] + [const(p) for p in params],
        out_specs=[per_b((1, GLA_INNER)), per_b(st_shape)],
        out_shape=[jax.ShapeDtypeStruct((db, 1, GLA_INNER), F32), jax.ShapeDtypeStruct((db,) + st_shape, F32)],
        compiler_params=pltpu.CompilerParams(dimension_semantics=("arbitrary",)),
        name="gla_sample",
    )(row3(gq), row3(gk), row3(gv), row3(r), row3(small), state, *params)
    return y.reshape(db, GLA_INNER), st


def _ffn_body(x_ref, g_ref, wg_ref, wu_ref, wd_ref, fg_ref, o_ref, *, chunks, final):
    x = x_ref[...]
    h = (x * lax.rsqrt(jnp.mean(x * x, axis=-1, keepdims=True) + EPS) * g_ref[...]).astype(BF16)
    acc = x
    for c0, cw in chunks:
        a = jnp.dot(h, wg_ref[:, c0:c0 + cw], preferred_element_type=F32)
        u = jnp.dot(h, wu_ref[:, c0:c0 + cw], preferred_element_type=F32)
        act = (a * jax.nn.sigmoid(a) * u).astype(BF16)
        acc = acc + jnp.dot(act, wd_ref[c0:c0 + cw, :], preferred_element_type=F32)
    if final:
        acc = acc * lax.rsqrt(jnp.mean(acc * acc, axis=-1, keepdims=True) + EPS) * fg_ref[...]
    o_ref[...] = acc


def swiglu_ffn(x, g, w_gate, w_up, w_down, final_g=None):
    m, d = x.shape
    hdim = w_gate.shape[1]
    tm = _row_tile(m)
    final = final_g is not None
    fg = (final_g if final else g).reshape(1, d).astype(F32)
    wspec = lambda shape: pl.BlockSpec(shape, lambda i: (0, 0), pipeline_mode=pl.Buffered(1))
    return pl.pallas_call(
        functools.partial(_ffn_body, chunks=_col_chunks(hdim, 256), final=final),
        grid=(m // tm,),
        in_specs=[pl.BlockSpec((tm, d), lambda i: (i, 0)),
                  pl.BlockSpec((1, d), lambda i: (0, 0)),
                  wspec((d, hdim)), wspec((d, hdim)), wspec((hdim, d)),
                  pl.BlockSpec((1, d), lambda i: (0, 0))],
        out_specs=pl.BlockSpec((tm, d), lambda i: (i, 0)),
        out_shape=jax.ShapeDtypeStruct((m, d), F32),
        compiler_params=pltpu.CompilerParams(dimension_semantics=("arbitrary",),
                                             vmem_limit_bytes=VMEM_LIMIT_BYTES),
        name="swiglu_ffn",
    )(x, g.reshape(1, d).astype(F32), w_gate.astype(BF16), w_up.astype(BF16), w_down.astype(BF16), fg)


def split_cols(h, sizes):
    return jnp.split(h, [int(i) for i in np.cumsum(sizes)[:-1]], axis=-1)


def rmsnorm(x, g):
    xf = x.astype(F32)
    y = xf * lax.rsqrt(jnp.mean(xf * xf, axis=-1, keepdims=True) + EPS)
    return (y * g.astype(F32)).astype(x.dtype)


def rope(x, pos):
    half = x.shape[-1] // 2
    freqs = ROPE_THETA ** (-jnp.arange(half, dtype=F32) / half)
    ang = pos.astype(F32)[:, None] * freqs
    shape = (pos.shape[0],) + (1,) * (x.ndim - 3) + (half,)
    cos, sin = jnp.cos(ang).reshape(shape), jnp.sin(ang).reshape(shape)
    xf = x.astype(F32)
    x1, x2 = xf[..., :half], xf[..., half:]
    return jnp.concatenate([x1 * cos - x2 * sin, x2 * cos + x1 * sin], axis=-1).astype(x.dtype)


def t5_bucket(rel):
    exact = REL_BUCKETS // 2
    n = jnp.maximum(rel, 0)
    nf = jnp.maximum(n, exact).astype(F32)
    large = exact + (jnp.log(nf / exact) / math.log(REL_MAX_DIST / exact) * (REL_BUCKETS - exact)).astype(jnp.int32)
    return jnp.where(n < exact, n, jnp.minimum(large, REL_BUCKETS - 1))


def shared_bias(rel, table):
    lq, nk = rel.shape
    return table.astype(F32)[t5_bucket(rel)].reshape(lq, nk, NSA_KV_HEADS, NSA_GROUP).transpose(0, 2, 3, 1)


def masked_softmax(s, mask):
    return jax.nn.softmax(jnp.where(mask, s, NEG), axis=-1) * mask


def causal_conv(xbc, conv_state, w, b):
    full = jnp.concatenate([conv_state.astype(xbc.dtype), xbc], axis=1)
    ch = xbc.shape[-1]
    y = lax.conv_general_dilated(full, w[:, None, :].astype(xbc.dtype), (1,), 'VALID',
                                 dimension_numbers=('NWC', 'WIO', 'NWC'), feature_group_count=ch)
    return jax.nn.silu(y + b.astype(y.dtype)), full[:, -(SSD_CONV - 1):]


def segsum(x):
    t = x.shape[-1]
    cs = jnp.cumsum(x, axis=-1)
    diff = cs[..., :, None] - cs[..., None, :]
    return jnp.where(jnp.tril(jnp.ones((t, t), bool)), diff, -jnp.inf)


def ssd_scan(x, dt, a, bm, cm, s0, chunk):
    b, l, h, p = x.shape
    g, n = bm.shape[2], bm.shape[3]
    e = h // g
    nc = l // chunk
    xd = (x * dt[..., None]).reshape(b, nc, chunk, g, e, p)
    da = (dt * a).reshape(b, nc, chunk, g, e).transpose(0, 3, 4, 1, 2)
    bm = bm.reshape(b, nc, chunk, g, n)
    cm = cm.reshape(b, nc, chunk, g, n)
    a_cs = jnp.cumsum(da, axis=-1)
    lmat = jnp.exp(segsum(da))
    cb = jnp.einsum('bclgn,bcsgn->bcgls', cm, bm)
    y_diag = jnp.einsum('bcgls,bgecls,bcsgep->bclgep', cb, lmat, xd)
    decay_st = jnp.exp(a_cs[..., -1:] - a_cs)
    states = jnp.einsum('bclgn,bgecl,bclgep->bcgepn', bm, decay_st, xd)
    states = jnp.concatenate([s0.reshape(b, 1, g, e, p, n), states], axis=1)
    chunk_decay = jnp.exp(segsum(jnp.pad(a_cs[..., -1], ((0, 0), (0, 0), (0, 0), (1, 0)))))
    new_states = jnp.einsum('bgezc,bcgepn->bzgepn', chunk_decay, states)
    y_off = jnp.einsum('bclgn,bcgepn,bgecl->bclgep', cm, new_states[:, :-1], jnp.exp(a_cs))
    return (y_diag + y_off).reshape(b, l, h, p), new_states[:, -1].reshape(b, h, p, n)


def ssd_mixer(z, xbc, dt_raw, conv_state, ssm_state, conv_w, conv_b, dt_bias, a_log, d_skip, norm_g):
    b, l = z.shape[:2]
    xbc, new_conv = causal_conv(xbc, conv_state, conv_w, conv_b)
    xs, bm, cm = split_cols(xbc.astype(F32), (SSD_INNER, SSD_GROUPS * SSD_STATE, SSD_GROUPS * SSD_STATE))
    x = xs.reshape(b, l, SSD_HEADS, SSD_HEADDIM)
    bm = bm.reshape(b, l, SSD_GROUPS, SSD_STATE)
    cm = cm.reshape(b, l, SSD_GROUPS, SSD_STATE)
    dt = jax.nn.softplus(dt_raw.astype(F32) + dt_bias.astype(F32))
    a = -jnp.exp(a_log.astype(F32))
    chunk = SSD_CHUNK if l % SSD_CHUNK == 0 else l
    y, new_ssm = ssd_scan(x, dt, a, bm, cm, ssm_state.astype(F32), chunk)
    y = y + x * d_skip.astype(F32)[:, None]
    y = (y.reshape(b, l, SSD_INNER) * jax.nn.silu(z.astype(F32))).reshape(b, l, SSD_GROUPS, SSD_INNER // SSD_GROUPS)
    y = rmsnorm(y, norm_g.reshape(SSD_GROUPS, -1)).reshape(b, l, SSD_INNER)
    return y.astype(z.dtype), new_conv, new_ssm


def nsa_compress(k, w):
    b, t = k.shape[:2]
    nseg = t // NSA_CMP_STRIDE
    seg = k[:, :nseg * NSA_CMP_STRIDE].reshape(b, nseg, NSA_CMP_STRIDE, NSA_KV_HEADS, NSA_HEAD_DIM)
    first = jnp.einsum('bsrhd,rd->bshd', seg, w[:NSA_CMP_STRIDE])
    second = jnp.einsum('bsrhd,rd->bshd', seg, w[NSA_CMP_STRIDE:])
    return first[:, :-1] + second[:, 1:]


def sel_overlap(n_sel, n_cmp):
    j = jnp.arange(n_sel)[:, None]
    i = jnp.arange(n_cmp)[None, :]
    lo = jnp.maximum(j * NSA_SEL_BLOCK, i * NSA_CMP_STRIDE)
    hi = jnp.minimum(j * NSA_SEL_BLOCK + NSA_SEL_BLOCK, i * NSA_CMP_STRIDE + NSA_CMP_BLOCK)
    return (jnp.maximum(hi - lo, 0) / NSA_CMP_STRIDE).astype(F32)


def nsa_cmp_branch(q, qpos, kc, vc, table):
    nc = kc.shape[1]
    cend = jnp.arange(nc) * NSA_CMP_STRIDE + NSA_CMP_BLOCK - 1
    rel = qpos[:, None] - cend[None, :]
    s = jnp.einsum('bqhgd,bchd->bqhgc', q, kc).astype(F32) * NSA_SCALE + shared_bias(rel, table)
    p = masked_softmax(s, (rel >= 0)[:, None, None, :])
    return jnp.einsum('bqhgc,bchd->bqhgd', p, vc), p


def nsa_select(p_cmp, qpos, n_sel):
    imp = jnp.einsum('bqhgc,jc->bqhj', p_cmp, sel_overlap(n_sel, p_cmp.shape[-1]))
    j = jnp.arange(n_sel)[None, :]
    cur = (qpos // NSA_SEL_BLOCK)[:, None]
    valid = (j <= cur)[:, None, :]
    forced = ((j == 0) | (j == cur) | (j == cur - 1))[:, None, :]
    score = jnp.where(valid, jnp.where(forced, NSA_FORCE, imp), -1.0)
    if n_sel < NSA_TOPK:
        score = jnp.pad(score, ((0, 0), (0, 0), (0, 0), (0, NSA_TOPK - n_sel)), constant_values=-1.0)
    vals, idx = lax.top_k(score, NSA_TOPK)
    kpos = (idx[..., None] * NSA_SEL_BLOCK + jnp.arange(NSA_SEL_BLOCK)).reshape(*idx.shape[:-1], NSA_TOPK * NSA_SEL_BLOCK)
    kok = jnp.repeat(vals >= 0, NSA_SEL_BLOCK, axis=-1)
    return kpos, kok


def nsa_slc_branch(q, qpos, ks, vs, kpos, kok, table):
    rel = qpos[None, :, None, None] - kpos
    tb = table.astype(F32).reshape(REL_BUCKETS, NSA_KV_HEADS, NSA_GROUP)
    bias = jnp.moveaxis(tb[t5_bucket(rel), jnp.arange(NSA_KV_HEADS)[:, None]], -1, -2)
    s = jnp.einsum('bqhgd,bqhkd->bqhgk', q, ks).astype(F32) * NSA_SCALE + bias
    p = masked_softmax(s, (kok & (rel >= 0))[..., None, :])
    return jnp.einsum('bqhgk,bqhkd->bqhgd', p, vs)


def nsa_win_branch(q, qpos, kw, vw, wpos, table):
    rel = qpos[:, None] - wpos[None, :]
    mask = (rel >= 0) & (rel < NSA_WINDOW) & (wpos >= 0)[None, :]
    s = jnp.einsum('bqhgd,bkhd->bqhgk', q, kw).astype(F32) * NSA_SCALE + shared_bias(rel, table)
    p = masked_softmax(s, mask[:, None, None, :])
    return jnp.einsum('bqhgk,bkhd->bqhgd', p, vw)


def nsa_gate(g, o_c, o_s, o_w):
    return g[..., 0:1] * o_c + g[..., 1:2] * o_s + g[..., 2:3] * o_w


def gather_rows(pool, e, new, page_table, kpos):
    db, s = new.shape[:2]
    p0 = page_table.shape[1] * PAGE_SIZE
    bi = jnp.arange(db)[:, None, None, None]
    hi = jnp.arange(NSA_KV_HEADS)[None, None, :, None]
    pp = jnp.minimum(kpos, p0 - 1)
    past = pool[e, page_table[bi, pp // PAGE_SIZE], pp % PAGE_SIZE, hi]
    cur = new[bi, jnp.clip(kpos - p0, 0, s - 1), hi]
    return jnp.where((kpos < p0)[..., None], past, cur)


def nsa_sample(q, kvs, gates, e, k_cmp_pool, v_cmp_pool, k_slc_pool, v_slc_pool, k_win_buf, v_win_buf,
               page_table, w_ck, w_cv, table):
    kc_new, vc_new, ks_new, vs_new, kw_new, vw_new = kvs
    db, s = q.shape[:2]
    p0 = page_table.shape[1] * PAGE_SIZE
    qpos = p0 + jnp.arange(s)
    past = lambda pool: pool[e, page_table].reshape(db, p0, NSA_KV_HEADS, NSA_HEAD_DIM)
    kc = nsa_compress(jnp.concatenate([past(k_cmp_pool), kc_new], axis=1), w_ck)
    vc = nsa_compress(jnp.concatenate([past(v_cmp_pool), vc_new], axis=1), w_cv)
    o_c, p_c = nsa_cmp_branch(q, qpos, kc, vc, table)
    kpos, kok = nsa_select(p_c, qpos, -(-(p0 + s) // NSA_SEL_BLOCK))
    ksel = gather_rows(k_slc_pool, e, ks_new, page_table, kpos)
    vsel = gather_rows(v_slc_pool, e, vs_new, page_table, kpos)
    o_s = nsa_slc_branch(q, qpos, ksel, vsel, kpos, kok, table)
    nbuf = k_win_buf.shape[1]
    kw = jnp.concatenate([k_win_buf.astype(kw_new.dtype), kw_new], axis=1)
    vw = jnp.concatenate([v_win_buf.astype(vw_new.dtype), vw_new], axis=1)
    wpos = p0 - nbuf + jnp.arange(nbuf + s)
    o_w = nsa_win_branch(q, qpos, kw, vw, wpos, table)
    out = nsa_gate(gates, o_c, o_s, o_w).reshape(db, s, NSA_HEADS * NSA_HEAD_DIM)
    return out, kw[:, -nbuf:], vw[:, -nbuf:]


def even_split(parts, r0, b, l):
    z, xbc, q, kv, small = (t[r0:r0 + b * l].reshape(b, l, -1) for t in parts)
    dtr = small[..., SMALL_DT:SMALL_DT + SSD_HEADS]
    g = small[..., SMALL_GATE:SMALL_GATE + 3 * NSA_HEADS]
    q = q.reshape(b, l, NSA_KV_HEADS, NSA_GROUP, NSA_HEAD_DIM)
    kvs = tuple(t.reshape(b, l, NSA_KV_HEADS, NSA_HEAD_DIM) for t in split_cols(kv, (NSA_KV_WIDTH,) * 6))
    g = jax.nn.sigmoid(g.astype(F32)).reshape(b, l, 3, NSA_KV_HEADS, NSA_GROUP).transpose(0, 1, 3, 4, 2)
    return z, xbc, dtr, q, kvs, g


def gla_scan(q, k, v, log_a, s0, chunk):
    b, l, h, dk = q.shape
    dv = v.shape[-1]
    nc = l // chunk
    q, k, v, log_a = [t.reshape(b, nc, chunk, h, t.shape[-1]) for t in (q, k, v, log_a)]
    bc = jnp.cumsum(log_a, axis=2)
    blast = bc[:, :, -1:]
    qt = q * jnp.exp(bc)
    kt = k * jnp.exp(-bc)
    kd = k * jnp.exp(blast - bc)
    causal = jnp.tril(jnp.ones((chunk, chunk), bool))
    att = jnp.where(causal, jnp.einsum('bclhk,bcshk->bchls', qt, kt), 0.0)
    o_intra = jnp.einsum('bchls,bcshv->bclhv', att, v)

    def step(st, inp):
        qt_c, kd_c, v_c, dec_c = inp
        o_c = jnp.einsum('blhk,bhkv->blhv', qt_c, st)
        st = jnp.exp(dec_c)[..., None] * st + jnp.einsum('blhk,blhv->bhkv', kd_c, v_c)
        return st, o_c

    xs = tuple(jnp.swapaxes(t, 0, 1) for t in (qt, kd, v, blast[:, :, 0]))
    s_fin, o_inter = lax.scan(step, s0, xs)
    return (o_intra + jnp.swapaxes(o_inter, 0, 1)).reshape(b, l, h, dv), s_fin


def gla_mixer(q, k, v, g1, r, state, w_g2, b_g, norm_g):
    b, l = q.shape[:2]
    q = q.astype(F32).reshape(b, l, GLA_HEADS, GLA_DK) * GLA_DK ** -0.5
    k = k.astype(F32).reshape(b, l, GLA_HEADS, GLA_DK)
    v = v.astype(F32).reshape(b, l, GLA_HEADS, GLA_DV)
    log_a = jax.nn.log_sigmoid(g1.astype(F32) @ w_g2.astype(F32) + b_g.astype(F32)).reshape(b, l, GLA_HEADS, GLA_DK) / GLA_TAU
    chunk = GLA_CHUNK if l % GLA_CHUNK == 0 else l
    o, new_state = gla_scan(q, k, v, log_a, state.astype(F32), chunk)
    o = rmsnorm(o, norm_g).reshape(b, l, GLA_HEADS * GLA_DV) * jax.nn.silu(r.astype(F32))
    return o.astype(r.dtype), new_state


def odd_split(h, b, l, pos, q_norm, w_qb, kv_norm):
    gq, gk, gv, g1, r, qa, kva, kr = split_cols(h.reshape(b, l, -1), OD_SPLITS)
    q = fused_matmul(qa.reshape(b * l, -1), w_qb, norm_g=q_norm).reshape(b, l, MLA_HEADS, MLA_NOPE + MLA_ROPE)
    q_nope, q_rope = q[..., :MLA_NOPE], rope(q[..., MLA_NOPE:], pos)
    return (gq, gk, gv, g1, r), (q_nope, q_rope, rmsnorm(kva, kv_norm), rope(kr, pos))


def mla_prompt(q_nope, q_rope, c, k_rope, w_kvb):
    b, l = c.shape[:2]
    kv = fused_matmul(c.reshape(b * l, -1), w_kvb).reshape(b, l, MLA_HEADS, MLA_NOPE + MLA_V)
    k_nope, v = kv[..., :MLA_NOPE], kv[..., MLA_NOPE:]
    nqb = l // MLA_QBLOCK
    kpos = jnp.arange(l)

    def block(args):
        i, qn, qr = args
        qpos = i * MLA_QBLOCK + jnp.arange(MLA_QBLOCK)
        s = (jnp.einsum('bqhd,bkhd->bhqk', qn, k_nope) + jnp.einsum('bqhr,bkr->bhqk', qr, k_rope)).astype(F32) * MLA_SCALE
        p = jax.nn.softmax(jnp.where(kpos[None, :] <= qpos[:, None], s, NEG), axis=-1)
        return jnp.einsum('bhqk,bkhd->bqhd', p, v)

    blk = lambda t: t.reshape(b, nqb, MLA_QBLOCK, *t.shape[2:]).swapaxes(0, 1)
    o = lax.map(block, (jnp.arange(nqb), blk(q_nope), blk(q_rope)))
    return o.swapaxes(0, 1).reshape(b, l, MLA_HEADS * MLA_V)


def mla_sample(q_nope, q_rope, c, k_rope, lat_pool, kr_pool, o_idx, page_table, w_kvb):
    db, s = c.shape[:2]
    w = w_kvb.reshape(MLA_KV_RANK, MLA_HEADS, MLA_NOPE + MLA_V)
    q_lat = jnp.einsum('bshd,chd->bshc', q_nope, w[..., :MLA_NOPE])
    c_past = lat_pool[o_idx, page_table].reshape(db, -1, MLA_KV_RANK)
    kr_past = kr_pool[o_idx, page_table].reshape(db, -1, MLA_ROPE)
    s_past = (jnp.einsum('bshc,btc->bsht', q_lat, c_past) + jnp.einsum('bshr,btr->bsht', q_rope, kr_past)).astype(F32) * MLA_SCALE
    s_new = (jnp.einsum('bshc,btc->bsht', q_lat, c) + jnp.einsum('bshr,btr->bsht', q_rope, k_rope)).astype(F32) * MLA_SCALE
    causal = jnp.arange(s)[None, :] <= jnp.arange(s)[:, None]
    s_new = jnp.where(causal[:, None, :], s_new, NEG)
    p = jax.nn.softmax(jnp.concatenate([s_past, s_new], axis=-1), axis=-1)
    n_past = c_past.shape[1]
    o_lat = jnp.einsum('bsht,btc->bshc', p[..., :n_past], c_past) + jnp.einsum('bsht,btc->bshc', p[..., n_past:], c)
    return jnp.einsum('bshc,chd->bshd', o_lat, w[..., MLA_NOPE:]).reshape(db, s, MLA_HEADS * MLA_V)


def kernel(x_prompt, x_sample, state_ssm, state_conv, cache_nsa_k_cmp, cache_nsa_v_cmp, cache_nsa_k_slc, cache_nsa_v_slc, cache_nsa_k_win, cache_nsa_v_win, state_gla, cache_mla_latent, cache_mla_krope, page_table, rel_bias, ev_norm, ev_w_in, ssd_conv_w, ssd_conv_b, ssd_dt_bias, ssd_a_log, ssd_d, ssd_norm, nsa_w_cmp_k, nsa_w_cmp_v, ev_w_out, od_norm, od_w_in, gla_w_gate2, gla_b_gate, gla_norm, mla_q_norm, mla_w_qb, mla_kv_norm, mla_w_kvb, od_w_out, ffn_norm, ffn_w_gate, ffn_w_up, ffn_w_down, final_norm):
    b, l = x_prompt.shape[:2]
    db, s = x_sample.shape[:2]
    depth = ffn_norm.shape[0]
    p0 = page_table.shape[1] * PAGE_SIZE
    pos_p = jnp.arange(l)
    pos_s = p0 + jnp.arange(s)
    names = ('ssm', 'conv', 'k_cmp', 'v_cmp', 'k_slc', 'v_slc', 'k_win', 'v_win', 'gla', 'lat', 'krope')
    newp = {n: [] for n in names}
    news = {n: [] for n in names}
    np_rows = b * l
    x = jnp.concatenate([x_prompt.reshape(np_rows, D_MODEL), x_sample.reshape(db * s, D_MODEL)], axis=0)
    for li in range(depth):
        if li % 2 == 0:
            e = li // 2
            ssd_w = (ssd_conv_w[e], ssd_conv_b[e], ssd_dt_bias[e], ssd_a_log[e], ssd_d[e], ssd_norm[e])
            parts = fused_matmul(x, even_weight(ev_w_in[e]), norm_g=ev_norm[e], groups=EV_GROUPS)
            z_all, xbc_all, _, kv_all, small_all = parts
            kv_heads = lambda r0, nb, nl: tuple(
                t.reshape(nb, nl, NSA_KV_HEADS, NSA_HEAD_DIM)
                for t in split_cols(kv_all[r0:r0 + nb * nl], (NSA_KV_WIDTH,) * 6))
            y_ssd, ssm_n, conv_n = ssd_prompt(z_all, xbc_all, small_all, b, l, *ssd_w)
            y_nsa = nsa_prompt_attention(parts[2], parts[3], parts[4], nsa_w_cmp_k[e], nsa_w_cmp_v[e], rel_bias, b, l)
            mix_p = jnp.concatenate([y_ssd, y_nsa], axis=-1)
            nw = min(NSA_WINDOW, l)
            kvs = kv_heads(0, b, l)
            for n, t in zip(names[:8], (ssm_n, conv_n, kvs[0], kvs[1], kvs[2], kvs[3],
                                        kvs[4][:, -nw:], kvs[5][:, -nw:])):
                newp[n].append(t)
            kvs = kv_heads(np_rows, db, s)
            y_ssd, ssm_n, conv_n = ssd_sample(z_all[np_rows:], xbc_all[np_rows:], small_all[np_rows:],
                                              state_conv[e], state_ssm[e], *ssd_w)
            y_nsa, kw_n, vw_n = nsa_sample_mixer(parts[2][np_rows:], parts[3][np_rows:], parts[4][np_rows:], e,
                                                 cache_nsa_k_cmp, cache_nsa_v_cmp, cache_nsa_k_slc, cache_nsa_v_slc,
                                                 cache_nsa_k_win[e], cache_nsa_v_win[e], page_table,
                                                 nsa_w_cmp_k[e], nsa_w_cmp_v[e], rel_bias)
            mix_s = jnp.concatenate([y_ssd, y_nsa], axis=-1)
            for n, t in zip(names[:8], (ssm_n, conv_n, kvs[0], kvs[1], kvs[2], kvs[3], kw_n, vw_n)):
                news[n].append(t)
            x = fused_matmul(jnp.concatenate([mix_p, mix_s], axis=0), ev_w_out[e], residual=x)
        else:
            o = li // 2
            gla_w = (gla_w_gate2[o], gla_b_gate[o], gla_norm[o])
            gq, gk, gv, r, qa, kva, small = fused_matmul(x, odd_weight(od_w_in[o]), norm_g=od_norm[o],
                                                         groups=OD_GROUPS)
            assert db * s == MLA_PREP_ROWS and l % MLA_PREP_ROWS == 0
            nblk = l // MLA_PREP_ROWS
            tables = tuple(jnp.concatenate(t, axis=0)
                           for t in zip(rope_tables(pos_p), rope_tables(jnp.tile(pos_s, db))))
            q_rot, lat, k_full, v_pad, kr_rot = mla_prep(
                qa, kva, small, mla_q_norm[o], mla_w_qb[o], mla_kv_norm[o], mla_w_kvb[o], tables,
                lambda i: jnp.where(i < b * nblk, i % nblk, nblk))
            kr_rot = kr_rot[:, MLA_NOPE:MLA_NOPE + MLA_ROPE]
            assert s == 1
            y_gla, gla_n = gla_prompt(gq, gk, gv, r, small, b, l, *gla_w)
            y_mla = mla_prompt_attention(q_rot, k_full, v_pad, b, l)
            mix_p = jnp.concatenate([y_gla, y_mla], axis=-1)
            newp['gla'].append(gla_n)
            newp['lat'].append(lat[:np_rows].reshape(b, l, -1))
            newp['krope'].append(kr_rot[:np_rows].reshape(b, l, -1))
            y_gla, gla_n = gla_sample(gq[np_rows:], gk[np_rows:], gv[np_rows:], r[np_rows:], small[np_rows:],
                                      state_gla[o], *gla_w)
            q_s = q_rot[np_rows:].reshape(db * s, MLA_HEADS, LANE)
            y_mla = mla_sample_mixer(q_s[..., :MLA_NOPE], q_s[..., MLA_NOPE:MLA_NOPE + MLA_ROPE], lat[np_rows:],
                                     kr_rot[np_rows:], cache_mla_latent, cache_mla_krope, o, page_table, mla_w_kvb[o])
            mix_s = jnp.concatenate([y_gla, y_mla], axis=-1)
            news['gla'].append(gla_n)
            news['lat'].append(lat[np_rows:].reshape(db, s, -1))
            news['krope'].append(kr_rot[np_rows:].reshape(db, s, -1))
            x = fused_matmul(jnp.concatenate([mix_p, mix_s], axis=0), od_w_out[o], residual=x)
        x = swiglu_ffn(x, ffn_norm[li], ffn_w_gate[li], ffn_w_up[li], ffn_w_down[li],
                       final_g=final_norm if li == depth - 1 else None)
    y_prompt = x[:np_rows].reshape(b, l, D_MODEL)
    y_sample = x[np_rows:].reshape(db, s, D_MODEL)
    st = lambda d, n: jnp.stack(d[n])
    return (y_prompt, y_sample,
            st(newp, 'ssm'), st(news, 'ssm'), st(newp, 'conv'), st(news, 'conv'),
            st(newp, 'k_cmp'), st(news, 'k_cmp'), st(newp, 'v_cmp'), st(news, 'v_cmp'),
            st(newp, 'k_slc'), st(news, 'k_slc'), st(newp, 'v_slc'), st(news, 'v_slc'),
            st(newp, 'k_win'), st(news, 'k_win'), st(newp, 'v_win'), st(news, 'v_win'),
            st(newp, 'gla'), st(news, 'gla'), st(newp, 'lat'), st(news, 'lat'),
            st(newp, 'krope'), st(news, 'krope'))
```

```python
import functools
import math

import jax
import jax.numpy as jnp
import numpy as np
from jax import lax
from jax.experimental import pallas as pl
from jax.experimental.pallas import tpu as pltpu

F32 = jnp.float32
BF16 = jnp.bfloat16
EPS = 1e-6
NEG = -1e30

D_MODEL = 1024
PAGE_SIZE = 128

SSD_HEADS = 16
SSD_HEADDIM = 64
SSD_INNER = SSD_HEADS * SSD_HEADDIM
SSD_GROUPS = 2
SSD_STATE = 128
SSD_CONV = 4
SSD_CHUNK = 128
SSD_CONV_CH = SSD_INNER + 2 * SSD_GROUPS * SSD_STATE

NSA_HEADS = 16
NSA_KV_HEADS = 2
NSA_GROUP = NSA_HEADS // NSA_KV_HEADS
NSA_HEAD_DIM = 64
NSA_KV_WIDTH = NSA_KV_HEADS * NSA_HEAD_DIM
NSA_CMP_BLOCK = 32
NSA_CMP_STRIDE = 16
NSA_SEL_BLOCK = 64
NSA_TOPK = 16
NSA_WINDOW = 512
NSA_QBLOCK = 128
NSA_SCALE = NSA_HEAD_DIM ** -0.5
NSA_FORCE = 1e4

REL_BUCKETS = 32
REL_MAX_DIST = 128

GLA_HEADS = 4
GLA_DK = 128
GLA_DV = 256
GLA_GATE_RANK = 16
GLA_TAU = 16.0
GLA_CHUNK = 16

MLA_HEADS = 8
MLA_Q_RANK = 384
MLA_KV_RANK = 256
MLA_NOPE = 64
MLA_ROPE = 32
MLA_V = 64
MLA_QBLOCK = 128
MLA_SCALE = (MLA_NOPE + MLA_ROPE) ** -0.5
ROPE_THETA = 10000.0

EV_SPLITS = (SSD_INNER, SSD_CONV_CH, SSD_HEADS, NSA_HEADS * NSA_HEAD_DIM, 6 * NSA_KV_WIDTH, 3 * NSA_HEADS)
OD_SPLITS = (GLA_HEADS * GLA_DK, GLA_HEADS * GLA_DK, GLA_HEADS * GLA_DV, GLA_GATE_RANK, GLA_HEADS * GLA_DV,
             MLA_Q_RANK, MLA_KV_RANK, MLA_ROPE)

VMEM_LIMIT_BYTES = 56 * 1024 * 1024
LANE = 128


def _row_tile(m):
    for t in (512, 384, 256, 128):
        if m % t == 0:
            return t
    return m


def _col_chunks(n, width=512):
    out, c = [], 0
    while c < n:
        w = min(width, n - c)
        out.append((c, w))
        c += w
    return out


def _mm_body(*refs, norm, residual, groups):
    it = iter(refs)
    x_ref = next(it)
    g_ref = next(it) if norm else None
    w_ref = next(it)
    r_ref = next(it) if residual else None
    o_refs = list(it)
    x = x_ref[...]
    if norm:
        x = x * lax.rsqrt(jnp.mean(x * x, axis=-1, keepdims=True) + EPS) * g_ref[...]
    xb = x.astype(BF16)
    off = 0
    for o_ref, gw in zip(o_refs, groups):
        for c0, cw in _col_chunks(gw):
            acc = jnp.dot(xb, w_ref[:, off + c0:off + c0 + cw], preferred_element_type=F32)
            if residual:
                acc = acc + r_ref[:, off + c0:off + c0 + cw]
            o_ref[:, c0:c0 + cw] = acc
        off += gw


def fused_matmul(x, w, norm_g=None, residual=None, groups=None):
    m, k = x.shape
    n = w.shape[1]
    single = groups is None
    groups = (n,) if single else tuple(groups)
    assert sum(groups) == n
    tm = _row_tile(m)
    norm = norm_g is not None
    res = residual is not None
    args = [x]
    specs = [pl.BlockSpec((tm, k), lambda i: (i, 0))]
    if norm:
        args.append(norm_g.reshape(1, k).astype(F32))
        specs.append(pl.BlockSpec((1, k), lambda i: (0, 0)))
    args.append(w.astype(BF16))
    specs.append(pl.BlockSpec((k, n), lambda i: (0, 0)))
    if res:
        args.append(residual)
        specs.append(pl.BlockSpec((tm, n), lambda i: (i, 0)))
    outs = pl.pallas_call(
        functools.partial(_mm_body, norm=norm, residual=res, groups=groups),
        grid=(m // tm,),
        in_specs=specs,
        out_specs=[pl.BlockSpec((tm, gw), lambda i: (i, 0)) for gw in groups],
        out_shape=[jax.ShapeDtypeStruct((m, gw), F32) for gw in groups],
        compiler_params=pltpu.CompilerParams(dimension_semantics=("arbitrary",),
                                             vmem_limit_bytes=VMEM_LIMIT_BYTES),
        name="fused_matmul",
    )(*args)
    return outs[0] if single else outs


EV_GROUPS = (SSD_INNER, SSD_CONV_CH, NSA_HEADS * NSA_HEAD_DIM, 6 * NSA_KV_WIDTH, LANE)
SMALL_DT = 0
SMALL_GATE = SSD_HEADS
QB = NSA_QBLOCK
ROWS = NSA_GROUP * QB


def even_weight(w_in):
    z, xbc, dtr, q, kvs, g = split_cols(w_in, EV_SPLITS)
    pad = jnp.zeros((w_in.shape[0], LANE - SSD_HEADS - 3 * NSA_HEADS), w_in.dtype)
    return jnp.concatenate([z, xbc, q, kvs, dtr, g, pad], axis=1)


def _bucket_tile(rel):
    exact = REL_BUCKETS // 2
    n = jnp.maximum(rel, 0)
    nf = jnp.maximum(n, exact).astype(F32)
    large = exact + (jnp.log(nf / exact) / math.log(REL_MAX_DIST / exact) * (REL_BUCKETS - exact)).astype(jnp.int32)
    return jnp.where(n < exact, n, jnp.minimum(large, REL_BUCKETS - 1))


def _bias_tiles_body(table_ref, o_ref, *, base_step, base_off, kmul, qmul):
    t = pl.program_id(0)
    shape = o_ref.shape[2:]
    qi = lax.broadcasted_iota(jnp.int32, shape, 0)
    ki = lax.broadcasted_iota(jnp.int32, shape, 1)
    bucket = _bucket_tile(t * base_step + base_off + qmul * qi - kmul * ki)
    for hd in range(NSA_HEADS):
        acc = jnp.zeros(shape, F32)
        for bk in range(REL_BUCKETS):
            acc = jnp.where(bucket == bk, table_ref[bk, hd], acc)
        o_ref[0, hd] = acc


def bias_tiles(table, n_tiles, base_step, base_off, kmul, qmul=1, rows=QB):
    return pl.pallas_call(
        functools.partial(_bias_tiles_body, base_step=base_step, base_off=base_off, kmul=kmul, qmul=qmul),
        grid=(n_tiles,),
        in_specs=[pl.BlockSpec(memory_space=pltpu.SMEM)],
        out_specs=pl.BlockSpec((1, NSA_HEADS, rows, LANE), lambda t: (t, 0, 0, 0)),
        out_shape=jax.ShapeDtypeStruct((n_tiles, NSA_HEADS, rows, LANE), F32),
        compiler_params=pltpu.CompilerParams(dimension_semantics=("arbitrary",)),
        name="t5_bias_tiles",
    )(table.astype(F32))


def _compress_body(kv_ref, w_ref, o_ref):
    nseg = o_ref.shape[1]
    first = jnp.zeros(o_ref.shape[1:], F32)
    second = jnp.zeros(o_ref.shape[1:], F32)
    for r in range(NSA_CMP_STRIDE):
        rows = kv_ref[pl.ds(r, nseg, stride=NSA_CMP_STRIDE), :]
        first = first + rows * w_ref[r:r + 1, :]
        second = second + rows * w_ref[NSA_CMP_STRIDE + r:NSA_CMP_STRIDE + r + 1, :]
    out = first + pltpu.roll(second, nseg - 1, axis=0)
    row = lax.broadcasted_iota(jnp.int32, out.shape, 0)
    o_ref[0] = jnp.where(row < nseg - 1, out, 0.0)


def nsa_compress_prompt(kv, w_ck, w_cv, b, l):
    nseg = l // NSA_CMP_STRIDE
    w = jnp.concatenate([w_ck, w_ck, w_cv, w_cv], axis=1).astype(F32)
    return pl.pallas_call(
        _compress_body,
        grid=(b, 2),
        in_specs=[pl.BlockSpec((l, NSA_KV_WIDTH), lambda i, j: (i, j)),
                  pl.BlockSpec((NSA_CMP_BLOCK, NSA_KV_WIDTH), lambda i, j: (0, j))],
        out_specs=pl.BlockSpec((1, nseg, NSA_KV_WIDTH), lambda i, j: (i, 0, j)),
        out_shape=jax.ShapeDtypeStruct((b, nseg, 2 * NSA_KV_WIDTH), F32),
        compiler_params=pltpu.CompilerParams(dimension_semantics=("arbitrary", "arbitrary")),
        name="nsa_compress",
    )(kv, w)


def _nt_dot(a, b):
    return lax.dot_general(a, b, (((1,), (1,)), ((), ())), preferred_element_type=F32)


def _importance(p_group_sum, ovt):
    return jnp.dot(p_group_sum.astype(BF16), ovt, preferred_element_type=F32)


def _nsa_prompt_body(q_ref, kv_ref, sm_ref, cmp_ref, cmpb_ref, toeb_ref, ov_ref, exp_ref, o_ref,
                     selm_ref, acc_ref, *, n_kt):
    i = pl.program_id(1)
    n_sel = 2 * n_kt
    row = lax.broadcasted_iota(jnp.int32, (QB, LANE), 0)
    lane = lax.broadcasted_iota(jnp.int32, (QB, LANE), 1)
    low = lane < NSA_HEAD_DIM
    gates = jax.nn.sigmoid(sm_ref[...])
    qpos = i * QB + row

    def attend(qp, h, kcol, vcol, n_tiles, tile_of, use_sel, window):
        acc_ref[...] = jnp.zeros_like(acc_ref)

        def body(t, carry):
            m, l = carry
            kt = tile_of(t)
            d = i - kt
            k0 = pl.multiple_of(kt * QB, QB)
            k_t = kv_ref[pl.ds(k0, QB), kcol:kcol + LANE].astype(BF16)
            v_t = kv_ref[pl.ds(k0, QB), vcol:vcol + LANE].astype(BF16)
            s = _nt_dot(qp, k_t).reshape(NSA_GROUP, QB, LANE)
            s = s + toeb_ref[jnp.minimum(d, 2), pl.ds(h * NSA_GROUP, NSA_GROUP)]
            rel = d * QB + row - lane
            msk = rel >= 0
            if window:
                msk = msk & (rel < NSA_WINDOW)
            if use_sel:
                msk = msk & (selm_ref[kt] > 0.5)
            s = jnp.where(msk[None], s, NEG)
            m_new = jnp.maximum(m, jnp.max(s, axis=-1, keepdims=True))
            alpha = jnp.exp(m - m_new)
            p = jnp.exp(s - m_new)
            l_new = alpha * l + jnp.sum(p, axis=-1, keepdims=True)
            pv = jnp.dot(p.reshape(ROWS, LANE).astype(BF16), v_t, preferred_element_type=F32)
            acc_ref[...] = acc_ref[...] * alpha.reshape(ROWS, 1) + pv
            return m_new, l_new

        m0 = jnp.full((NSA_GROUP, QB, 1), NEG, F32)
        l0 = jnp.zeros((NSA_GROUP, QB, 1), F32)
        _, l = lax.fori_loop(0, n_tiles, body, (m0, l0))
        return acc_ref[...] / l.reshape(ROWS, 1)

    for h in range(NSA_KV_HEADS):
        in_half = low if h == 0 else jnp.logical_not(low)
        parts = []
        for g in range(NSA_GROUP):
            hd = h * NSA_GROUP + g
            t = q_ref[:, (hd // 2) * LANE:(hd // 2) * LANE + LANE]
            if hd % 2 != h:
                t = pltpu.roll(t, NSA_HEAD_DIM, axis=1)
            parts.append((jnp.where(in_half, t, 0.0) * NSA_SCALE).astype(BF16))
        qp = jnp.concatenate(parts, axis=0)

        kc = cmp_ref[0, :, 0:LANE].astype(BF16)
        vc = cmp_ref[0, :, LANE:2 * LANE].astype(BF16)
        s = _nt_dot(qp, kc).reshape(NSA_GROUP, QB, LANE) + cmpb_ref[0, pl.ds(h * NSA_GROUP, NSA_GROUP)]
        mc = (qpos - (NSA_CMP_STRIDE * lane + NSA_CMP_BLOCK - 1)) >= 0
        s = jnp.where(mc[None], s, NEG)
        e = jnp.exp(s - jnp.max(s, axis=-1, keepdims=True))
        p = e / jnp.sum(e, axis=-1, keepdims=True) * mc[None].astype(F32)
        o_c = jnp.dot(p.reshape(ROWS, LANE).astype(BF16), vc, preferred_element_type=F32)
        imp_t = _nt_dot(ov_ref[...], jnp.sum(p, axis=0).astype(BF16))[0:n_sel]

        jrow = lax.broadcasted_iota(jnp.int32, (n_sel, LANE), 0)
        cur = (i * QB + lax.broadcasted_iota(jnp.int32, (n_sel, LANE), 1)) // NSA_SEL_BLOCK
        valid = jrow <= cur
        forced = (jrow == 0) | (jrow == cur) | (jrow == cur - 1)
        score = jnp.where(valid, jnp.where(forced, NSA_FORCE, imp_t), -1.0)
        cnt = jnp.zeros((n_sel, LANE), jnp.int32)
        for jp in range(n_sel):
            other = score[jp:jp + 1, :]
            beats = (other > score) | ((other == score) & (jrow > jp))
            cnt = cnt + beats.astype(jnp.int32)
        sel_t = ((cnt < NSA_TOPK) & valid).astype(F32)
        sel = jnp.concatenate([sel_t, jnp.zeros((LANE - n_sel, LANE), F32)], axis=0).T.astype(BF16)
        for kt in range(n_kt):
            selm_ref[kt] = jnp.dot(sel, exp_ref[:, kt * QB:(kt + 1) * QB], preferred_element_type=F32)

        o_s = attend(qp, h, 2 * LANE, 3 * LANE, i + 1, lambda t: t, True, False)
        o_w = attend(qp, h, 4 * LANE, 5 * LANE, jnp.minimum(i, NSA_WINDOW // QB) + 1, lambda t: i - t, False, True)

        outs = []
        for g in range(NSA_GROUP):
            c = SMALL_GATE + h * NSA_GROUP + g
            r0 = g * QB
            outs.append(gates[:, c:c + 1] * o_c[r0:r0 + QB]
                        + gates[:, c + NSA_HEADS:c + NSA_HEADS + 1] * o_s[r0:r0 + QB]
                        + gates[:, c + 2 * NSA_HEADS:c + 2 * NSA_HEADS + 1] * o_w[r0:r0 + QB])
        for pr in range(NSA_GROUP // 2):
            a, b2 = outs[2 * pr], outs[2 * pr + 1]
            if h == 0:
                b2 = pltpu.roll(b2, NSA_HEAD_DIM, axis=1)
            else:
                a = pltpu.roll(a, NSA_HEAD_DIM, axis=1)
            c0 = (h * NSA_GROUP + 2 * pr) * NSA_HEAD_DIM
            o_ref[:, c0:c0 + LANE] = jnp.where(low, a, b2)


def _vt_body(kv_ref, o_ref):
    for w, col in enumerate((3 * LANE, 5 * LANE)):
        o_ref[0, w, 0] = kv_ref[:, col:col + LANE].T.astype(BF16)


def nsa_value_transpose(kv, b, l):
    nkt = l // QB
    return pl.pallas_call(
        _vt_body,
        grid=(b, nkt),
        in_specs=[pl.BlockSpec((QB, 6 * NSA_KV_WIDTH), lambda bi, kt: (bi * nkt + kt, 0))],
        out_specs=pl.BlockSpec((1, 2, 1, LANE, QB), lambda bi, kt: (bi, 0, kt, 0, 0)),
        out_shape=jax.ShapeDtypeStruct((b, 2, nkt, LANE, QB), BF16),
        compiler_params=pltpu.CompilerParams(dimension_semantics=("arbitrary", "arbitrary")),
        name="nsa_value_transpose",
    )(kv)


def _nsa_prompt_t_body(q_ref, kv_ref, vt_ref, sm_ref, cmp_ref, cmpb_ref, toeb_ref, ov_ref, exp_ref, o_ref,
                       selm_ref, acc_ref, *, n_kt):
    i = pl.program_id(1)
    n_sel = 2 * n_kt
    krow = lax.broadcasted_iota(jnp.int32, (QB, LANE), 0)
    qlane = lax.broadcasted_iota(jnp.int32, (QB, LANE), 1)
    low = qlane < NSA_HEAD_DIM
    gates_t = jax.nn.sigmoid(sm_ref[...]).T
    qpos = i * QB + qlane
    heads = range(NSA_GROUP)
    all_heads = range(NSA_HEADS)

    def attend(qs, kcol, w, n_tiles, tile_of, use_sel, window):
        acc_ref[...] = jnp.zeros_like(acc_ref)

        def body(t, carry):
            m, l = carry
            kt = tile_of(t)
            d = i - kt
            k0 = pl.multiple_of(kt * QB, QB)
            k_t = kv_ref[pl.ds(k0, QB), kcol:kcol + LANE].astype(BF16)
            vt_t = vt_ref[0, w, kt]
            rel = d * QB + qlane - krow
            vis = rel >= 0
            if window:
                vis = vis & (rel < NSA_WINDOW)
            msk = [vis & (selm_ref[h, kt] > 0.5) if use_sel else vis for h in range(NSA_KV_HEADS)]
            bias = jnp.minimum(d, 2)
            s = [jnp.where(msk[hd // NSA_GROUP], _nt_dot(k_t, qs[hd]) + toeb_ref[bias, hd], NEG) for hd in all_heads]
            m_new = [jnp.maximum(m[hd], jnp.max(s[hd], axis=0, keepdims=True)) for hd in all_heads]
            alpha = [jnp.exp(m[hd] - m_new[hd]) for hd in all_heads]
            p = [jnp.exp(s[hd] - m_new[hd][0:1]) for hd in all_heads]
            l_new = [alpha[hd] * l[hd] + jnp.sum(p[hd], axis=0, keepdims=True) for hd in all_heads]
            pv = [jnp.dot(vt_t, p[hd].astype(BF16), preferred_element_type=F32) for hd in all_heads]
            for hd in all_heads:
                acc_ref[hd] = acc_ref[hd] * alpha[hd][0:1] + pv[hd]
            return tuple(m_new), tuple(l_new)

        m0 = tuple(jnp.full((8, LANE), NEG, F32) for _ in all_heads)
        l0 = tuple(jnp.zeros((8, LANE), F32) for _ in all_heads)
        _, l = lax.fori_loop(0, n_tiles, body, (m0, l0))
        return [acc_ref[hd] / l[hd][0:1] for hd in all_heads]

    qs, o_c = [], []
    for h in range(NSA_KV_HEADS):
        in_half = low if h == 0 else jnp.logical_not(low)
        qh = []
        for g in heads:
            hd = h * NSA_GROUP + g
            t = q_ref[:, (hd // 2) * LANE:(hd // 2) * LANE + LANE]
            if hd % 2 != h:
                t = pltpu.roll(t, NSA_HEAD_DIM, axis=1)
            qh.append((jnp.where(in_half, t, 0.0) * NSA_SCALE).astype(BF16))
        qs.extend(qh)

        kc = cmp_ref[0, :, 0:LANE].astype(BF16)
        vc_t = cmp_ref[0, :, LANE:2 * LANE].T.astype(BF16)
        mc = (qpos - (NSA_CMP_STRIDE * krow + NSA_CMP_BLOCK - 1)) >= 0
        mcf = mc.astype(F32)
        s = [jnp.where(mc, _nt_dot(kc, qh[g]) + cmpb_ref[0, h * NSA_GROUP + g], NEG) for g in heads]
        e = [jnp.exp(s[g] - jnp.max(s[g], axis=0, keepdims=True)) for g in heads]
        p = [e[g] / jnp.sum(e[g], axis=0, keepdims=True) * mcf for g in heads]
        o_c.extend(jnp.dot(vc_t, p[g].astype(BF16), preferred_element_type=F32) for g in heads)
        p_sum = p[0]
        for g in heads[1:]:
            p_sum = p_sum + p[g]
        imp_t = jnp.dot(ov_ref[...], p_sum.astype(BF16), preferred_element_type=F32)[0:n_sel]

        jrow = lax.broadcasted_iota(jnp.int32, (n_sel, LANE), 0)
        cur = (i * QB + lax.broadcasted_iota(jnp.int32, (n_sel, LANE), 1)) // NSA_SEL_BLOCK
        valid = jrow <= cur
        forced = (jrow == 0) | (jrow == cur) | (jrow == cur - 1)
        score = jnp.where(valid, jnp.where(forced, NSA_FORCE, imp_t), -1.0)
        cnt = jnp.zeros((n_sel, LANE), jnp.int32)
        for jp in range(n_sel):
            other = score[jp:jp + 1, :]
            beats = (other > score) | ((other == score) & (jrow > jp))
            cnt = cnt + beats.astype(jnp.int32)
        sel_t = ((cnt < NSA_TOPK) & valid).astype(F32)
        sel_t = jnp.concatenate([sel_t, jnp.zeros((LANE - n_sel, LANE), F32)], axis=0).astype(BF16)
        for kt in range(n_kt):
            selm_ref[h, kt] = jnp.dot(exp_ref[kt * QB:(kt + 1) * QB, :], sel_t, preferred_element_type=F32)

    o_s = attend(qs, 2 * LANE, 0, i + 1, lambda t: t, True, False)
    o_w = attend(qs, 4 * LANE, 1, jnp.minimum(i, NSA_WINDOW // QB) + 1, lambda t: i - t, False, True)

    for h in range(NSA_KV_HEADS):
        outs = []
        for g in heads:
            hd = h * NSA_GROUP + g
            c = SMALL_GATE + hd
            out_t = (gates_t[c:c + 1] * o_c[hd] + gates_t[c + NSA_HEADS:c + NSA_HEADS + 1] * o_s[hd]
                     + gates_t[c + 2 * NSA_HEADS:c + 2 * NSA_HEADS + 1] * o_w[hd])
            outs.append(out_t.T)
        for pr in range(NSA_GROUP // 2):
            a, b2 = outs[2 * pr], outs[2 * pr + 1]
            if h == 0:
                b2 = pltpu.roll(b2, NSA_HEAD_DIM, axis=1)
            else:
                a = pltpu.roll(a, NSA_HEAD_DIM, axis=1)
            c0 = (h * NSA_GROUP + 2 * pr) * NSA_HEAD_DIM
            o_ref[:, c0:c0 + LANE] = jnp.where(low, a, b2)


def nsa_prompt_attention(q, kv, small, w_ck, w_cv, table, b, l):
    nqb = l // QB
    n_sel = l // NSA_SEL_BLOCK
    nseg = l // NSA_CMP_STRIDE
    assert nseg == LANE and n_sel <= LANE and l % QB == 0
    cmp = nsa_compress_prompt(kv, w_ck, w_cv, b, l)
    vt = nsa_value_transpose(kv, b, l)
    cmp_bias = bias_tiles(table, nqb, QB, -(NSA_CMP_BLOCK - 1), -1, qmul=-NSA_CMP_STRIDE)
    toe_bias = bias_tiles(table, 3, QB, 0, -1, qmul=-1)
    jj = np.arange(LANE)[None, :]
    cc = np.arange(LANE)[:, None]
    lo = np.maximum(jj * NSA_SEL_BLOCK, cc * NSA_CMP_STRIDE)
    hi = np.minimum(jj * NSA_SEL_BLOCK + NSA_SEL_BLOCK, cc * NSA_CMP_STRIDE + NSA_CMP_BLOCK)
    ovt = np.where((jj < n_sel) & (cc < nseg - 1), np.maximum(hi - lo, 0) / NSA_CMP_STRIDE, 0.0)
    expand = (np.arange(l)[None, :] // NSA_SEL_BLOCK == np.arange(LANE)[:, None]).astype(np.float32)
    return pl.pallas_call(
        functools.partial(_nsa_prompt_t_body, n_kt=nqb),
        grid=(b, nqb),
        in_specs=[pl.BlockSpec((QB, NSA_HEADS * NSA_HEAD_DIM), lambda bi, i: (bi * nqb + i, 0)),
                  pl.BlockSpec((l, 6 * NSA_KV_WIDTH), lambda bi, i: (bi, 0)),
                  pl.BlockSpec((1, 2, nqb, LANE, QB), lambda bi, i: (bi, 0, 0, 0, 0)),
                  pl.BlockSpec((QB, LANE), lambda bi, i: (bi * nqb + i, 0)),
                  pl.BlockSpec((1, nseg, 2 * NSA_KV_WIDTH), lambda bi, i: (bi, 0, 0)),
                  pl.BlockSpec((1, NSA_HEADS, QB, LANE), lambda bi, i: (i, 0, 0, 0)),
                  pl.BlockSpec((3, NSA_HEADS, QB, LANE), lambda bi, i: (0, 0, 0, 0)),
                  pl.BlockSpec((LANE, LANE), lambda bi, i: (0, 0)),
                  pl.BlockSpec((l, LANE), lambda bi, i: (0, 0))],
        out_specs=pl.BlockSpec((QB, NSA_HEADS * NSA_HEAD_DIM), lambda bi, i: (bi * nqb + i, 0)),
        out_shape=jax.ShapeDtypeStruct((b * l, NSA_HEADS * NSA_HEAD_DIM), F32),
        scratch_shapes=[pltpu.VMEM((NSA_KV_HEADS, nqb, QB, QB), F32), pltpu.VMEM((NSA_HEADS, LANE, QB), F32)],
        compiler_params=pltpu.CompilerParams(dimension_semantics=("arbitrary", "arbitrary"),
                                             vmem_limit_bytes=VMEM_LIMIT_BYTES),
        name="nsa_prompt",
    )(q, kv, vt, small, cmp, cmp_bias, toe_bias, jnp.asarray(ovt.T, BF16), jnp.asarray(expand.T, BF16))


PAGES_PER_STEP = 16
NSEG_PAGE = PAGE_SIZE // NSA_CMP_STRIDE
SELF_RANK = 2


def _bf(x):
    return x.astype(BF16).astype(F32)


def _bias_rows(table, n_tiles, step, off, kmul):
    t = bias_tiles(table, n_tiles, step, off, kmul, qmul=0, rows=8)[:, :, 0, :]
    return t.transpose(1, 0, 2).reshape(NSA_KV_HEADS, NSA_GROUP, n_tiles * LANE)


def _nsa_sample_cmp_body(pt_ref, k_hbm, v_hbm, qp_ref, wk_ref, wv_ref, bias_ref, ovt_ref, oc_ref, idx_ref,
                         kbuf, vbuf, ksem, vsem, fk, sk, fv, sv, *, base, n_pages):
    slot = _paged_prefetch((k_hbm, v_hbm), (kbuf, vbuf), (ksem, vsem), pt_ref, base, n_pages)
    nseg = n_pages * NSEG_PAGE

    def compress_page(p, carry):
        for buf, w_ref, f_ref, s_ref in ((kbuf, wk_ref, fk, sk), (vbuf, wv_ref, fv, sv)):
            f = jnp.zeros((NSEG_PAGE, LANE), F32)
            s = jnp.zeros((NSEG_PAGE, LANE), F32)
            for r in range(NSA_CMP_STRIDE):
                rows = buf[slot, p, pl.ds(r, NSEG_PAGE, stride=NSA_CMP_STRIDE), :]
                f = f + rows * w_ref[r:r + 1, :]
                s = s + rows * w_ref[NSA_CMP_STRIDE + r:NSA_CMP_STRIDE + r + 1, :]
            row0 = pl.multiple_of(p * NSEG_PAGE, NSEG_PAGE)
            f_ref[pl.ds(row0, NSEG_PAGE), :] = f
            s_ref[pl.ds(row0, NSEG_PAGE), :] = s
        return carry

    lax.fori_loop(0, n_pages, compress_page, 0)

    def attend_and_select():
        rowi = lax.broadcasted_iota(jnp.int32, (nseg, LANE), 0)
        kc = jnp.where(rowi < nseg - 1, fk[...] + pltpu.roll(sk[...], nseg - 1, axis=0), 0.0).astype(BF16)
        vc = jnp.where(rowi < nseg - 1, fv[...] + pltpu.roll(sv[...], nseg - 1, axis=0), 0.0).astype(BF16)
        mc = lax.broadcasted_iota(jnp.int32, (NSA_GROUP, nseg), 1) < nseg - 1
        lane = lax.broadcasted_iota(jnp.int32, (8, LANE), 1)
        imps = []
        for h in range(NSA_KV_HEADS):
            qp = (qp_ref[0, h] * NSA_SCALE).astype(BF16)
            s = jnp.where(mc, _nt_dot(qp, kc) + bias_ref[h], NEG)
            e = jnp.exp(s - jnp.max(s, axis=-1, keepdims=True))
            p = e / jnp.sum(e, axis=-1, keepdims=True) * mc.astype(F32)
            pb = p.astype(BF16)
            oc_ref[0, h] = jnp.dot(pb, vc, preferred_element_type=F32)
            imps.append(_importance(jnp.sum(p, axis=0, keepdims=True), ovt_ref[...]))
        imp = jnp.concatenate(imps + [jnp.zeros((8 - NSA_KV_HEADS, LANE), F32)], axis=0)
        forced = (lane == 0) | (lane == LANE - 1)
        score = jnp.where(forced, NSA_FORCE, imp)
        cnt = (score < NSA_FORCE).astype(F32)
        for jp in range(LANE):
            col = score[:, jp:jp + 1]
            cnt = cnt + ((col > score) | ((col == score) & (lane > jp))).astype(F32)
        lanef = lane.astype(F32)
        idx = jnp.where(lane == SELF_RANK, float(LANE), 0.0)
        for k in range(NSA_TOPK):
            if k != SELF_RANK:
                v = jnp.sum(jnp.where(cnt == k, lanef, 0.0), axis=-1, keepdims=True)
                idx = jnp.where(lane == k, v, idx)
        idx_ref[0] = idx.astype(jnp.int32)

    attend_and_select()


def nsa_sample_cmp(qp, k_pool, v_pool, e, page_table, w_ck, w_cv, table):
    db, n_pages = page_table.shape
    n_pool = k_pool.shape[1]
    p0 = n_pages * PAGE_SIZE
    nseg = n_pages * NSEG_PAGE
    npg = PAGES_PER_STEP
    assert p0 // NSA_SEL_BLOCK == LANE and n_pages % npg == 0
    kp = k_pool.reshape(-1, PAGE_SIZE, NSA_KV_WIDTH)
    vp = v_pool.reshape(-1, PAGE_SIZE, NSA_KV_WIDTH)
    bias = _bias_rows(table, nseg // LANE, -LANE * NSA_CMP_STRIDE, p0 - (NSA_CMP_BLOCK - 1), NSA_CMP_STRIDE)
    jj = np.arange(LANE)[None, :]
    cc = np.arange(nseg)[:, None]
    lo = np.maximum(jj * NSA_SEL_BLOCK, cc * NSA_CMP_STRIDE)
    hi = np.minimum(jj * NSA_SEL_BLOCK + NSA_SEL_BLOCK, cc * NSA_CMP_STRIDE + NSA_CMP_BLOCK)
    ovt = np.where(cc < nseg - 1, np.maximum(hi - lo, 0) / NSA_CMP_STRIDE, 0.0)

    const = lambda shape: pl.BlockSpec(shape, lambda b, pt: (0,) * len(shape))
    hbm = pl.BlockSpec(memory_space=pl.ANY)
    page_buf = pltpu.VMEM((2, n_pages, PAGE_SIZE, NSA_KV_WIDTH), F32)
    grid_spec = pltpu.PrefetchScalarGridSpec(
        num_scalar_prefetch=1,
        grid=(db,),
        in_specs=[hbm, hbm, pl.BlockSpec((1, NSA_KV_HEADS, NSA_GROUP, LANE), lambda b, pt: (b, 0, 0, 0)),
                  const((NSA_CMP_BLOCK, NSA_KV_WIDTH)), const((NSA_CMP_BLOCK, NSA_KV_WIDTH)),
                  const((NSA_KV_HEADS, NSA_GROUP, nseg)), const((nseg, LANE))],
        out_specs=[pl.BlockSpec((1, NSA_KV_HEADS, NSA_GROUP, LANE), lambda b, pt: (b, 0, 0, 0)),
                   pl.BlockSpec((1, 8, LANE), lambda b, pt: (b, 0, 0))],
        scratch_shapes=[page_buf, page_buf, pltpu.SemaphoreType.DMA((2,)), pltpu.SemaphoreType.DMA((2,))]
        + [pltpu.VMEM((nseg, LANE), F32)] * 4)
    tile2 = lambda w: jnp.concatenate([w, w], axis=1).astype(F32)
    return pl.pallas_call(
        functools.partial(_nsa_sample_cmp_body, base=e * n_pool, n_pages=n_pages),
        grid_spec=grid_spec,
        out_shape=[jax.ShapeDtypeStruct((db, NSA_KV_HEADS, NSA_GROUP, LANE), F32),
                   jax.ShapeDtypeStruct((db, 8, LANE), jnp.int32)],
        compiler_params=pltpu.CompilerParams(dimension_semantics=("arbitrary",), vmem_limit_bytes=VMEM_LIMIT_BYTES),
        name="nsa_sample_cmp",
    )(page_table, kp, vp, qp, tile2(w_ck), tile2(w_cv), bias, jnp.asarray(ovt, BF16))


def _selected_block_copies(pt_ref, idx_ref, pools, bufs, sems, b, slot, base):
    half = NSA_SEL_BLOCK
    bpp = PAGE_SIZE // NSA_SEL_BLOCK
    copies = []
    for ent in range(NSA_KV_HEADS * NSA_TOPK):
        if ent % NSA_TOPK == SELF_RANK:
            continue
        j = jnp.minimum(idx_ref[b, ent], LANE - 1)
        page = pt_ref[b, j // bpp] + base
        rows = pl.ds(pl.multiple_of((j % bpp) * half, half), half)
        for pool_ref, buf_ref, sem_ref in zip(pools, bufs, sems):
            copies.append(pltpu.make_async_copy(pool_ref.at[page, rows], buf_ref.at[slot, ent], sem_ref.at[slot]))
    return copies


def _nsa_sample_attn_body(pt_ref, idx_ref, k_hbm, v_hbm, qp_ref, new_ref, kwin_ref, vwin_ref, oc_ref, gate_ref,
                          bnear_ref, bwin_ref, bends_ref, o_ref, kwo_ref, vwo_ref, kb, vb, ksem, vsem, *, nbuf, base):
    b = pl.program_id(0)
    cur = b % 2
    copies = functools.partial(_selected_block_copies, pt_ref, idx_ref, (k_hbm, v_hbm), (kb, vb), (ksem, vsem))

    @pl.when(b == 0)
    def _():
        for c in copies(b, cur, base):
            c.start()

    @pl.when(b + 1 < pl.num_programs(0))
    def _():
        for c in copies(b + 1, 1 - cur, base):
            c.start()

    for c in copies(b, cur, base):
        c.wait()
    new = new_ref[0]
    ks_new, vs_new, kw_new, vw_new = (new[:, c * LANE:(c + 1) * LANE] for c in (2, 3, 4, 5))
    kwin = kwin_ref[0]
    vwin = vwin_ref[0]
    wlane = lax.broadcasted_iota(jnp.int32, (NSA_GROUP, nbuf), 1)
    half = NSA_SEL_BLOCK
    for h in range(NSA_KV_HEADS):
        qp = (qp_ref[0, h] * NSA_SCALE).astype(BF16)
        qpf = qp.astype(F32)
        b_self = bends_ref[h][:, 0:1]
        b_far = bends_ref[h][:, 1:2]
        near = bnear_ref[h]
        scores, slots = [], []
        for k in range(NSA_TOPK):
            if k == SELF_RANK:
                continue
            j = idx_ref[b, h * NSA_TOPK + k]
            bias = jnp.where(j == LANE - 1, near[:, half:], jnp.where(j == LANE - 2, near[:, :half], b_far))
            scores.append(_nt_dot(qp, kb[cur, h * NSA_TOPK + k].astype(BF16)) + bias)
            slots.append(h * NSA_TOPK + k)
        s_self = jnp.sum(qpf * _bf(ks_new), axis=-1, keepdims=True) + b_self
        m = s_self
        for s in scores:
            m = jnp.maximum(m, jnp.max(s, axis=-1, keepdims=True))
        p_self = jnp.exp(s_self - m)
        l = p_self
        acc = _bf(p_self) * _bf(vs_new)
        for s, ent in zip(scores, slots):
            p = jnp.exp(s - m)
            l = l + jnp.sum(p, axis=-1, keepdims=True)
            acc = acc + jnp.dot(p.astype(BF16), vb[cur, ent].astype(BF16), preferred_element_type=F32)
        o_s = acc / l
        s = jnp.where(wlane >= 1, _nt_dot(qp, kwin.astype(BF16)) + bwin_ref[h], NEG)
        s_self = jnp.sum(qpf * _bf(kw_new), axis=-1, keepdims=True) + b_self
        m = jnp.maximum(jnp.max(s, axis=-1, keepdims=True), s_self)
        p = jnp.exp(s - m)
        p_self = jnp.exp(s_self - m)
        l = jnp.sum(p, axis=-1, keepdims=True) + p_self
        o_w = (jnp.dot(p.astype(BF16), vwin.astype(BF16), preferred_element_type=F32)
               + _bf(p_self) * _bf(vw_new)) / l
        gates = jax.nn.sigmoid(gate_ref[0, h])
        o_h = gates[:, 0:1] * oc_ref[0, h] + gates[:, 1:2] * o_s + gates[:, 2:3] * o_w
        o_rot = pltpu.roll(o_h, NSA_HEAD_DIM, axis=1)
        low = lax.broadcasted_iota(jnp.int32, (1, LANE), 1) < NSA_HEAD_DIM
        for pr in range(NSA_GROUP // 2):
            a = (o_h if h == 0 else o_rot)[2 * pr:2 * pr + 1]
            b2 = (o_rot if h == 0 else o_h)[2 * pr + 1:2 * pr + 2]
            c0 = (h * NSA_GROUP + 2 * pr) * NSA_HEAD_DIM
            o_ref[0, :, c0:c0 + LANE] = jnp.where(low, a, b2)
    rowi = lax.broadcasted_iota(jnp.int32, (nbuf, LANE), 0)
    kwo_ref[0] = jnp.where(rowi == nbuf - 1, kw_new, pltpu.roll(kwin, nbuf - 1, axis=0))
    vwo_ref[0] = jnp.where(rowi == nbuf - 1, vw_new, pltpu.roll(vwin, nbuf - 1, axis=0))


def nsa_sample_attn(qp, kv_new, idx, o_c, gates_raw, k_pool, v_pool, e, k_win, v_win, page_table, table):
    db, n_pages = page_table.shape
    n_pool = k_pool.shape[1]
    p0 = n_pages * PAGE_SIZE
    nbuf = k_win.shape[1]
    nb = NSA_KV_HEADS * NSA_TOPK
    bpp = PAGE_SIZE // NSA_SEL_BLOCK
    assert nbuf == NSA_WINDOW and nbuf % LANE == 0 and p0 >= nbuf
    kp = k_pool.reshape(-1, PAGE_SIZE, NSA_KV_WIDTH)
    vp = v_pool.reshape(-1, PAGE_SIZE, NSA_KV_WIDTH)
    bnear = _bias_rows(table, 1, 0, 2 * NSA_SEL_BLOCK, 1)
    bwin = _bias_rows(table, nbuf // LANE, -LANE, nbuf, 1)
    tb = table.astype(F32).reshape(REL_BUCKETS, NSA_KV_HEADS, NSA_GROUP)
    bends = jnp.pad(jnp.stack([tb[0], tb[REL_BUCKETS - 1]], axis=-1), ((0, 0), (0, 0), (0, LANE - 2)))

    per_b = lambda shape: pl.BlockSpec((1,) + shape, lambda b, pt, ix: (b,) + (0,) * len(shape))
    const = lambda shape: pl.BlockSpec(shape, lambda b, pt, ix: (0,) * len(shape))
    hbm = pl.BlockSpec(memory_space=pl.ANY)
    head_shape = (NSA_KV_HEADS, NSA_GROUP, LANE)
    blk_buf = pltpu.VMEM((2, nb, NSA_SEL_BLOCK, NSA_KV_WIDTH), F32)
    grid_spec = pltpu.PrefetchScalarGridSpec(
        num_scalar_prefetch=2,
        grid=(db,),
        in_specs=[hbm, hbm,
                  per_b(head_shape), per_b((1, 6 * NSA_KV_WIDTH)), per_b((nbuf, NSA_KV_WIDTH)), per_b((nbuf, NSA_KV_WIDTH)),
                  per_b(head_shape), per_b(head_shape), const(head_shape),
                  const((NSA_KV_HEADS, NSA_GROUP, nbuf)), const(head_shape)],
        out_specs=[per_b((1, NSA_HEADS * NSA_HEAD_DIM)), per_b((nbuf, NSA_KV_WIDTH)), per_b((nbuf, NSA_KV_WIDTH))],
        scratch_shapes=[blk_buf, blk_buf, pltpu.SemaphoreType.DMA((2,)), pltpu.SemaphoreType.DMA((2,))])
    return pl.pallas_call(
        functools.partial(_nsa_sample_attn_body, nbuf=nbuf, base=e * n_pool),
        grid_spec=grid_spec,
        out_shape=[jax.ShapeDtypeStruct((db, 1, NSA_HEADS * NSA_HEAD_DIM), F32),
                   jax.ShapeDtypeStruct((db, nbuf, NSA_KV_WIDTH), F32),
                   jax.ShapeDtypeStruct((db, nbuf, NSA_KV_WIDTH), F32)],
        compiler_params=pltpu.CompilerParams(dimension_semantics=("arbitrary",)),
        name="nsa_sample_attn",
    )(page_table, idx, kp, vp, qp, kv_new.reshape(db, 1, -1),
      k_win.reshape(db, nbuf, NSA_KV_WIDTH), v_win.reshape(db, nbuf, NSA_KV_WIDTH), o_c, gates_raw,
      bnear, bwin, bends)


def nsa_sample_mixer(q, kv, small, e, k_cmp_pool, v_cmp_pool, k_slc_pool, v_slc_pool, k_win, v_win,
                     page_table, w_ck, w_cv, table):
    db = q.shape[0]
    qh = q.reshape(db, NSA_KV_HEADS, NSA_GROUP, NSA_HEAD_DIM)
    zeros = jnp.zeros_like(qh[:, 0])
    qp = jnp.stack([jnp.concatenate([qh[:, 0], zeros], axis=-1), jnp.concatenate([zeros, qh[:, 1]], axis=-1)], axis=1)
    o_c, idx = nsa_sample_cmp(qp, k_cmp_pool, v_cmp_pool, e, page_table, w_ck, w_cv, table)
    idx = idx[:, :NSA_KV_HEADS, :NSA_TOPK].reshape(db, NSA_KV_HEADS * NSA_TOPK)
    g = small[:, SMALL_GATE:SMALL_GATE + 3 * NSA_HEADS].reshape(db, 3, NSA_KV_HEADS, NSA_GROUP).transpose(0, 2, 3, 1)
    g = jnp.pad(g, ((0, 0), (0, 0), (0, 0), (0, LANE - 3)))
    y, kw_n, vw_n = nsa_sample_attn(qp, kv, idx, o_c, g, k_slc_pool, v_slc_pool, e, k_win, v_win, page_table, table)
    shape = (db, -1, NSA_KV_HEADS, NSA_HEAD_DIM)
    return y.reshape(db, NSA_HEADS * NSA_HEAD_DIM), kw_n.reshape(shape), vw_n.reshape(shape)


def _bmm_body(a_ref, b_ref, o_ref):
    o_ref[0] = jnp.dot(a_ref[0].astype(BF16), b_ref[0].astype(BF16), preferred_element_type=F32)


def batched_matmul(a, b):
    hh, m, k = a.shape
    n = b.shape[2]
    return pl.pallas_call(
        _bmm_body,
        grid=(hh,),
        in_specs=[pl.BlockSpec((1, m, k), lambda i: (i, 0, 0)), pl.BlockSpec((1, k, n), lambda i: (i, 0, 0))],
        out_specs=pl.BlockSpec((1, m, n), lambda i: (i, 0, 0)),
        out_shape=jax.ShapeDtypeStruct((hh, m, n), F32),
        compiler_params=pltpu.CompilerParams(dimension_semantics=("arbitrary",)),
        name="batched_matmul",
    )(a, b)


def _page_copy(pool_ref, buf_ref, sem_ref, page, slot, p):
    return pltpu.make_async_copy(pool_ref.at[page], buf_ref.at[slot, p], sem_ref.at[slot])


def _start_pages(pools, bufs, sems, pt_ref, b, slot, base, n_pages):
    def issue(p, carry):
        page = pt_ref[b, p] + base
        for pool_ref, buf_ref, sem_ref in zip(pools, bufs, sems):
            _page_copy(pool_ref, buf_ref, sem_ref, page, slot, p).start()
        return carry
    lax.fori_loop(0, n_pages, issue, 0)


def _wait_pages(pools, bufs, sems, slot, n_pages):
    def wait(p, carry):
        for pool_ref, buf_ref, sem_ref in zip(pools, bufs, sems):
            _page_copy(pool_ref, buf_ref, sem_ref, 0, slot, p).wait()
        return carry
    lax.fori_loop(0, n_pages, wait, 0)


def _paged_prefetch(pools, bufs, sems, pt_ref, base, n_pages):
    b = pl.program_id(0)
    slot = b % 2

    @pl.when(b == 0)
    def _():
        _start_pages(pools, bufs, sems, pt_ref, b, slot, base, n_pages)

    @pl.when(b + 1 < pl.num_programs(0))
    def _():
        _start_pages(pools, bufs, sems, pt_ref, b + 1, 1 - slot, base, n_pages)

    _wait_pages(pools, bufs, sems, slot, n_pages)
    return slot


def _mla_sample_body(pt_ref, lat_hbm, kr_hbm, qlat_ref, qrope_ref, cnew_ref, krnew_ref, o_ref,
                     cbuf, rbuf, csem, rsem, *, base, n_pages):
    slot = _paged_prefetch((lat_hbm, kr_hbm), (cbuf, rbuf), (csem, rsem), pt_ref, base, n_pages)
    npg = PAGES_PER_STEP
    ql = qlat_ref[0].astype(BF16)
    qr = qrope_ref[0].astype(BF16)
    chunks, scores = [], []
    for ch in range(n_pages // npg):
        c = cbuf[slot, pl.ds(ch * npg, npg)].reshape(npg * PAGE_SIZE, MLA_KV_RANK).astype(BF16)
        r = rbuf[slot, pl.ds(ch * npg, npg)].reshape(npg * PAGE_SIZE, MLA_ROPE).astype(BF16)
        scores.append((_nt_dot(ql, c) + _nt_dot(qr, r)) * MLA_SCALE)
    cn = _bf(cnew_ref[0])
    kn = _bf(krnew_ref[0])
    s_self = (jnp.sum(ql.astype(F32) * cn, axis=-1, keepdims=True)
              + jnp.sum(qr.astype(F32) * kn, axis=-1, keepdims=True)) * MLA_SCALE
    m = s_self
    for s in scores:
        m = jnp.maximum(m, jnp.max(s, axis=-1, keepdims=True))
    p_self = jnp.exp(s_self - m)
    l = p_self
    acc = _bf(p_self) * cn
    for ch, s in enumerate(scores):
        p = jnp.exp(s - m)
        l = l + jnp.sum(p, axis=-1, keepdims=True)
        c = cbuf[slot, pl.ds(ch * npg, npg)].reshape(npg * PAGE_SIZE, MLA_KV_RANK).astype(BF16)
        acc = acc + jnp.dot(p.astype(BF16), c, preferred_element_type=F32)
    o_ref[0] = acc / l


def mla_sample_attention(q_lat, q_rope, c_new, kr_new, lat_pool, kr_pool, o, page_table):
    db, n_pages = page_table.shape
    n_pool = lat_pool.shape[1]
    npg = PAGES_PER_STEP
    assert n_pages % npg == 0
    cp = lat_pool.reshape(-1, PAGE_SIZE, MLA_KV_RANK)
    rp = kr_pool.reshape(-1, PAGE_SIZE, MLA_ROPE)

    per_b = lambda shape: pl.BlockSpec((1,) + shape, lambda b, pt: (b,) + (0,) * len(shape))
    hbm = pl.BlockSpec(memory_space=pl.ANY)
    grid_spec = pltpu.PrefetchScalarGridSpec(
        num_scalar_prefetch=1,
        grid=(db,),
        in_specs=[hbm, hbm, per_b((MLA_HEADS, MLA_KV_RANK)), per_b((MLA_HEADS, MLA_ROPE)),
                  per_b((1, MLA_KV_RANK)), per_b((1, MLA_ROPE))],
        out_specs=per_b((MLA_HEADS, MLA_KV_RANK)),
        scratch_shapes=[pltpu.VMEM((2, n_pages, PAGE_SIZE, MLA_KV_RANK), F32),
                        pltpu.VMEM((2, n_pages, PAGE_SIZE, MLA_ROPE), F32),
                        pltpu.SemaphoreType.DMA((2,)), pltpu.SemaphoreType.DMA((2,))])
    return pl.pallas_call(
        functools.partial(_mla_sample_body, base=o * n_pool, n_pages=n_pages),
        grid_spec=grid_spec,
        out_shape=jax.ShapeDtypeStruct((db, MLA_HEADS, MLA_KV_RANK), F32),
        compiler_params=pltpu.CompilerParams(dimension_semantics=("arbitrary",), vmem_limit_bytes=VMEM_LIMIT_BYTES),
        name="mla_sample",
    )(page_table, cp, rp, q_lat, q_rope, c_new.reshape(db, 1, -1), kr_new.reshape(db, 1, -1))


def mla_sample_mixer(q_nope, q_rope, c, k_rope, lat_pool, kr_pool, o_idx, page_table, w_kvb):
    db = c.shape[0]
    w = w_kvb.reshape(MLA_KV_RANK, MLA_HEADS, MLA_NOPE + MLA_V).transpose(1, 0, 2)
    q_lat = batched_matmul(q_nope.transpose(1, 0, 2), w[..., :MLA_NOPE].transpose(0, 2, 1)).transpose(1, 0, 2)
    o_lat = mla_sample_attention(q_lat, q_rope, c, k_rope, lat_pool, kr_pool, o_idx, page_table)
    out = batched_matmul(o_lat.transpose(1, 0, 2), w[..., MLA_NOPE:])
    return out.transpose(1, 0, 2).reshape(db, MLA_HEADS * MLA_V)


OD_GROUPS = (GLA_HEADS * GLA_DK, GLA_HEADS * GLA_DK, GLA_HEADS * GLA_DV, GLA_HEADS * GLA_DV, MLA_Q_RANK, MLA_KV_RANK, LANE)
MLA_PREP_ROWS = 128
MLA_TQ = 256
MLA_HEAD_GROUP = 4


def odd_weight(w_in):
    gq, gk, gv, g1, r, qa, kva, kr = split_cols(w_in, OD_SPLITS)
    z = lambda n: jnp.zeros((w_in.shape[0], n), w_in.dtype)
    small = jnp.concatenate([g1, z(MLA_NOPE - GLA_GATE_RANK), kr, z(LANE - MLA_NOPE - MLA_ROPE)], axis=1)
    return jnp.concatenate([gq, gk, gv, r, qa, kva, small], axis=1)


def _head_blocks(w, widths, keep):
    k = w.shape[0]
    w = w.reshape(k, MLA_HEADS, sum(widths))[:, :, keep[0]:keep[1]]
    return jnp.pad(w, ((0, 0), (0, 0), (0, LANE - (keep[1] - keep[0])))).reshape(k, MLA_HEADS * LANE)


def rope_tables(pos):
    half = MLA_ROPE // 2
    freqs = ROPE_THETA ** (-jnp.arange(half, dtype=F32) / half)
    ang = pos.astype(F32)[:, None] * freqs
    cos, sin = jnp.cos(ang), jnp.sin(ang)
    n = pos.shape[0]
    z = lambda w: jnp.zeros((n, w), F32)
    tail = LANE - MLA_NOPE - MLA_ROPE
    c = jnp.concatenate([jnp.ones((n, MLA_NOPE), F32), cos, cos, z(tail)], axis=1)
    s1 = jnp.concatenate([z(MLA_NOPE), -sin, z(half), z(tail)], axis=1)
    s2 = jnp.concatenate([z(MLA_NOPE), z(half), sin, z(tail)], axis=1)
    return c, s1, s2


def _rope_block(x, c, s1, s2):
    half = MLA_ROPE // 2
    return x * c + pltpu.roll(x, LANE - half, axis=1) * s1 + pltpu.roll(x, half, axis=1) * s2


def _mla_prep_body(qa_ref, kva_ref, sm_ref, qg_ref, kg_ref, wq_ref, wk_ref, wv_ref, c_ref, s1_ref, s2_ref,
                   q_out, c_out, k_out, v_out, kr_out):
    norm = lambda x, g: x * lax.rsqrt(jnp.mean(x * x, axis=-1, keepdims=True) + EPS) * g
    c, s1, s2 = c_ref[...], s1_ref[...], s2_ref[...]
    lane = lax.broadcasted_iota(jnp.int32, c.shape, 1)
    q = jnp.dot(norm(qa_ref[...], qg_ref[...]).astype(BF16), wq_ref[...], preferred_element_type=F32)
    lat = norm(kva_ref[...], kg_ref[...])
    c_out[...] = lat
    latb = lat.astype(BF16)
    kn = jnp.dot(latb, wk_ref[...], preferred_element_type=F32)
    v_out[...] = jnp.dot(latb, wv_ref[...], preferred_element_type=F32).astype(BF16)
    kr = jnp.where(lane >= MLA_NOPE, _rope_block(sm_ref[...], c, s1, s2), 0.0)
    kr_out[...] = kr
    for h in range(MLA_HEADS):
        cols = slice(h * LANE, (h + 1) * LANE)
        q_out[:, cols] = _rope_block(q[:, cols], c, s1, s2)
        k_out[:, cols] = (kn[:, cols] + kr).astype(BF16)


def mla_prep(qa, kva, small, q_norm, w_qb, kv_norm, w_kvb, tables, pos_block):
    m = qa.shape[0]
    tm = MLA_PREP_ROWS
    wq = _head_blocks(w_qb, (MLA_NOPE, MLA_ROPE), (0, MLA_NOPE + MLA_ROPE)).astype(BF16)
    wk = _head_blocks(w_kvb, (MLA_NOPE, MLA_V), (0, MLA_NOPE)).astype(BF16)
    wv = _head_blocks(w_kvb, (MLA_NOPE, MLA_V), (MLA_NOPE, MLA_NOPE + MLA_V)).astype(BF16)
    wide = MLA_HEADS * LANE
    rows = lambda w: pl.BlockSpec((tm, w), lambda i: (i, 0))
    const = lambda a: pl.BlockSpec(a.shape, lambda i: (0, 0))
    tab = pl.BlockSpec((tm, LANE), lambda i: (pos_block(i), 0))
    qg = q_norm.reshape(1, -1).astype(F32)
    kg = kv_norm.reshape(1, -1).astype(F32)
    return pl.pallas_call(
        _mla_prep_body,
        grid=(m // tm,),
        in_specs=[rows(MLA_Q_RANK), rows(MLA_KV_RANK), rows(LANE), const(qg), const(kg), const(wq), const(wk), const(wv),
                  tab, tab, tab],
        out_specs=[rows(wide), rows(MLA_KV_RANK), rows(wide), rows(wide), rows(LANE)],
        out_shape=[jax.ShapeDtypeStruct((m, wide), F32), jax.ShapeDtypeStruct((m, MLA_KV_RANK), F32),
                   jax.ShapeDtypeStruct((m, wide), BF16), jax.ShapeDtypeStruct((m, wide), BF16),
                   jax.ShapeDtypeStruct((m, LANE), F32)],
        compiler_params=pltpu.CompilerParams(dimension_semantics=("arbitrary",), vmem_limit_bytes=VMEM_LIMIT_BYTES),
        name="mla_prep",
    )(qa, kva, small, qg, kg, wq, wk, wv, *tables)


def _mla_prompt_body(q_ref, k_ref, v_ref, o_ref):
    tq = q_ref.shape[0]
    qi = pl.program_id(1)
    row = lax.broadcasted_iota(jnp.int32, (tq, tq), 0)
    col = lax.broadcasted_iota(jnp.int32, (tq, tq), 1)
    low = lax.broadcasted_iota(jnp.int32, (tq, LANE), 1) < MLA_V
    outs = []
    for h0 in range(0, MLA_HEADS, MLA_HEAD_GROUP):
        heads = range(h0, h0 + MLA_HEAD_GROUP)
        hcols = [slice(h * LANE, (h + 1) * LANE) for h in heads]
        qs = [(q_ref[:, c] * MLA_SCALE).astype(BF16) for c in hcols]

        def body(kt, carry):
            k0 = pl.multiple_of(kt * tq, tq)
            vis = (kt - qi) * tq + col <= row
            ss = [jnp.where(vis, _nt_dot(q, k_ref[pl.ds(k0, tq), c]), NEG) for q, c in zip(qs, hcols)]
            m_new = [jnp.maximum(st[0], jnp.max(s, axis=-1, keepdims=True)) for st, s in zip(carry, ss)]
            alpha = [jnp.exp(st[0] - mn) for st, mn in zip(carry, m_new)]
            ps = [jnp.exp(s - mn) for s, mn in zip(ss, m_new)]
            l_new = [a * st[1] + jnp.sum(p, axis=-1, keepdims=True) for a, st, p in zip(alpha, carry, ps)]
            pv = [jnp.dot(p.astype(BF16), v_ref[pl.ds(k0, tq), c], preferred_element_type=F32)
                  for p, c in zip(ps, hcols)]
            return tuple((mn, ln, a * st[2] + x) for mn, ln, a, st, x in zip(m_new, l_new, alpha, carry, pv))

        init = tuple((jnp.full((tq, 1), NEG, F32), jnp.zeros((tq, 1), F32), jnp.zeros((tq, LANE), F32))
                     for _ in heads)
        final = lax.fori_loop(0, qi + 1, body, init)
        outs.extend(acc / l for _, l, acc in final)
    for pr in range(MLA_HEADS // 2):
        o_ref[:, pr * LANE:(pr + 1) * LANE] = jnp.where(low, outs[2 * pr], pltpu.roll(outs[2 * pr + 1], MLA_V, axis=1))


def mla_prompt_attention(q_rot, k_full, v_pad, b, l):
    tq = MLA_TQ
    nq = l // tq
    wide = MLA_HEADS * LANE
    return pl.pallas_call(
        _mla_prompt_body,
        grid=(b, nq),
        in_specs=[pl.BlockSpec((tq, wide), lambda bi, i: (bi * nq + i, 0)),
                  pl.BlockSpec((l, wide), lambda bi, i: (bi, 0)),
                  pl.BlockSpec((l, wide), lambda bi, i: (bi, 0))],
        out_specs=pl.BlockSpec((tq, MLA_HEADS * MLA_V), lambda bi, i: (bi * nq + i, 0)),
        out_shape=jax.ShapeDtypeStruct((b * l, MLA_HEADS * MLA_V), F32),
        compiler_params=pltpu.CompilerParams(dimension_semantics=("arbitrary", "arbitrary"),
                                             vmem_limit_bytes=VMEM_LIMIT_BYTES),
        name="mla_prompt",
    )(q_rot, k_full, v_pad)


SSD_PAIRS = SSD_HEADS // 2
PAIR_PER_GROUP = SSD_PAIRS // SSD_GROUPS
CONV_PAD = 8


def _exact_dot(a, b):
    return jnp.dot(a, b, preferred_element_type=F32, precision=lax.Precision.HIGHEST)


def _softplus(x):
    u = jnp.exp(-jnp.abs(x))
    w = 1.0 + u
    return jnp.maximum(x, 0.0) + jnp.where(w == 1.0, u, jnp.log(w) * (u / (w - 1.0)))


def _silu(x):
    return x * jax.nn.sigmoid(x)


def _group_rmsnorm(y, g):
    gw = SSD_INNER // SSD_GROUPS
    outs = []
    for gi in range(SSD_GROUPS):
        yg = y[:, gi * gw:(gi + 1) * gw]
        outs.append(yg * lax.rsqrt(jnp.mean(yg * yg, axis=-1, keepdims=True) + EPS))
    return jnp.concatenate(outs, axis=1) * g


def _ssd_prompt_body(z_ref, xbc_ref, sm_ref, cw_ref, cb_ref, dtb_ref, alog_ref, d_ref, g_ref, ex_ref,
                     y_ref, st_ref, cv_ref, buf, state):
    c = pl.program_id(1)
    nc = pl.num_programs(1)
    t = SSD_CHUNK

    @pl.when(c == 0)
    def _():
        buf[0:CONV_PAD, :] = jnp.zeros((CONV_PAD, SSD_CONV_CH), F32)
        state[...] = jnp.zeros_like(state)

    buf[CONV_PAD:CONV_PAD + t, :] = xbc_ref[...]
    conv = cb_ref[...] + cw_ref[SSD_CONV - 1:SSD_CONV, :] * xbc_ref[...]
    for k in range(SSD_CONV - 1):
        conv = conv + cw_ref[k:k + 1, :] * buf[pl.ds(CONV_PAD - (SSD_CONV - 1) + k, t), :]
    buf[0:CONV_PAD, :] = buf[t:t + CONV_PAD, :]
    xc = _silu(conv)
    xs = xc[:, :SSD_INNER]
    bm = xc[:, SSD_INNER:SSD_INNER + SSD_GROUPS * SSD_STATE].astype(BF16)
    cm = xc[:, SSD_INNER + SSD_GROUPS * SSD_STATE:].astype(BF16)

    dt = _softplus(sm_ref[...] + dtb_ref[...])
    da = dt * (-jnp.exp(alog_ref[...]))
    row = lax.broadcasted_iota(jnp.int32, (t, t), 0)
    col = lax.broadcasted_iota(jnp.int32, (t, t), 1)
    tril = row >= col
    a_cs = _exact_dot(tril.astype(F32), da)
    a_cs_t = a_cs.T
    ex = ex_ref[...]
    dt_x = _exact_dot(dt, ex)
    acs_x = _exact_dot(a_cs, ex)
    last_x = acs_x[t - 1:t, :]
    xd = xs * dt_x
    xdw = xd * jnp.exp(last_x - acs_x)
    xdb = xd.astype(BF16)
    e_in = jnp.exp(acs_x)
    low = lax.broadcasted_iota(jnp.int32, (t, LANE), 1) < SSD_HEADDIM
    zero = jnp.zeros((t, LANE), BF16)

    y_parts = []
    for j in range(SSD_PAIRS):
        g = j // PAIR_PER_GROUP
        cg = cm[:, g * SSD_STATE:(g + 1) * SSD_STATE]
        bg = bm[:, g * SSD_STATE:(g + 1) * SSD_STATE]
        cb = _nt_dot(cg, bg)
        cols = slice(j * LANE, (j + 1) * LANE)
        xp = xdb[:, cols]
        y = jnp.zeros((t, LANE), F32)
        for hh in range(2):
            h = 2 * j + hh
            decay = jnp.where(tril, jnp.exp(a_cs[:, h:h + 1] - a_cs_t[h:h + 1, :]), 0.0)
            xh = jnp.where(low if hh == 0 else jnp.logical_not(low), xp, zero)
            y = y + jnp.dot((cb * decay).astype(BF16), xh, preferred_element_type=F32)
        s_old = state[j]
        y = y + _nt_dot(cg, s_old.astype(BF16)) * e_in[:, cols]
        dec_rows = jnp.broadcast_to(jnp.exp(last_x[:, cols]), (LANE, LANE)).T
        state[j] = dec_rows * s_old + jnp.dot(xdw[:, cols].T.astype(BF16), bg, preferred_element_type=F32)
        y_parts.append(y)
    y = jnp.concatenate(y_parts, axis=1) + xs * d_ref[...]
    y_ref[...] = _group_rmsnorm(y * _silu(z_ref[...]), g_ref[...])

    @pl.when(c == nc - 1)
    def _():
        st_ref[0] = state[...]
        cv_ref[0] = xbc_ref[t - (SSD_CONV - 1):t, :]


def _head_expand():
    return jnp.asarray(np.arange(LANE)[:, None] == (np.arange(SSD_INNER)[None, :] // SSD_HEADDIM), F32)


def _ssd_params(conv_w, conv_b, dt_bias, a_log, d_skip, norm_g):
    pad = lambda v: jnp.pad(v.astype(F32), (0, LANE - SSD_HEADS)).reshape(1, LANE)
    return (conv_w.astype(F32), conv_b.astype(F32).reshape(1, -1), pad(dt_bias), pad(a_log),
            jnp.repeat(d_skip.astype(F32), SSD_HEADDIM).reshape(1, -1), norm_g.astype(F32).reshape(1, -1), _head_expand())


def ssd_prompt(z, xbc, small, b, l, conv_w, conv_b, dt_bias, a_log, d_skip, norm_g):
    t = SSD_CHUNK
    nc = l // t
    params = _ssd_params(conv_w, conv_b, dt_bias, a_log, d_skip, norm_g)
    rows = lambda w: pl.BlockSpec((t, w), lambda bi, c: (bi * nc + c, 0))
    const = lambda a: pl.BlockSpec(a.shape, lambda bi, c: (0,) * a.ndim)
    y, st, cv = pl.pallas_call(
        _ssd_prompt_body,
        grid=(b, nc),
        in_specs=[rows(SSD_INNER), rows(SSD_CONV_CH), rows(LANE)] + [const(p) for p in params],
        out_specs=[rows(SSD_INNER),
                   pl.BlockSpec((1, SSD_PAIRS, LANE, SSD_STATE), lambda bi, c: (bi, 0, 0, 0)),
                   pl.BlockSpec((1, SSD_CONV - 1, SSD_CONV_CH), lambda bi, c: (bi, 0, 0))],
        out_shape=[jax.ShapeDtypeStruct((b * l, SSD_INNER), F32),
                   jax.ShapeDtypeStruct((b, SSD_PAIRS, LANE, SSD_STATE), F32),
                   jax.ShapeDtypeStruct((b, SSD_CONV - 1, SSD_CONV_CH), F32)],
        scratch_shapes=[pltpu.VMEM((CONV_PAD + t, SSD_CONV_CH), F32), pltpu.VMEM((SSD_PAIRS, LANE, SSD_STATE), F32)],
        compiler_params=pltpu.CompilerParams(dimension_semantics=("arbitrary", "arbitrary"),
                                             vmem_limit_bytes=VMEM_LIMIT_BYTES),
        name="ssd_prompt",
    )(z, xbc, small, *params)
    return y, st.reshape(b, SSD_HEADS, SSD_HEADDIM, SSD_STATE), cv


def _ssd_sample_body(z_ref, xbc_ref, sm_ref, cs_ref, st_ref, cw_ref, cb_ref, dtb_ref, alog_ref, d_ref, g_ref, ex_ref,
                     y_ref, sto_ref, cvo_ref):
    x_new = xbc_ref[0]
    cs = cs_ref[0]
    conv = cb_ref[...] + cw_ref[SSD_CONV - 1:SSD_CONV, :] * x_new
    for k in range(SSD_CONV - 1):
        conv = conv + cw_ref[k:k + 1, :] * cs[k:k + 1, :]
    cvo_ref[0] = jnp.concatenate([cs[1:], x_new], axis=0)
    xc = _silu(conv)
    xs = xc[:, :SSD_INNER]
    bm = xc[:, SSD_INNER:SSD_INNER + SSD_GROUPS * SSD_STATE]
    cm = xc[:, SSD_INNER + SSD_GROUPS * SSD_STATE:]
    dt = jnp.broadcast_to(_softplus(sm_ref[0] + dtb_ref[...]), (8, LANE))
    da = dt * (-jnp.exp(alog_ref[...]))
    ex = ex_ref[...]
    xd = xs * _exact_dot(dt, ex)[0:1]
    dec_x = jnp.exp(_exact_dot(da, ex)[0:1])
    y_parts = []
    for j in range(SSD_PAIRS):
        g = j // PAIR_PER_GROUP
        cols = slice(j * LANE, (j + 1) * LANE)
        bg = bm[:, g * SSD_STATE:(g + 1) * SSD_STATE]
        cg = jnp.broadcast_to(cm[:, g * SSD_STATE:(g + 1) * SSD_STATE], (8, SSD_STATE)).astype(BF16)
        dec_rows = jnp.broadcast_to(dec_x[:, cols], (LANE, LANE)).T
        xd_rows = jnp.broadcast_to(xd[:, cols], (LANE, LANE)).T
        s_new = dec_rows * st_ref[0, j] + xd_rows * bg
        sto_ref[0, j] = s_new
        y_parts.append(_nt_dot(cg, s_new.astype(BF16))[0:1])
    y = jnp.concatenate(y_parts, axis=1) + xs * d_ref[...]
    y_ref[0] = _group_rmsnorm(y * _silu(z_ref[0]), g_ref[...])


def ssd_sample(z, xbc, small, conv_state, ssm_state, conv_w, conv_b, dt_bias, a_log, d_skip, norm_g):
    db = z.shape[0]
    params = _ssd_params(conv_w, conv_b, dt_bias, a_log, d_skip, norm_g)
    per_b = lambda shape: pl.BlockSpec((1,) + shape, lambda i: (i,) + (0,) * len(shape))
    const = lambda a: pl.BlockSpec(a.shape, lambda i: (0,) * a.ndim)
    st_shape = (SSD_PAIRS, LANE, SSD_STATE)
    y, st, cv = pl.pallas_call(
        _ssd_sample_body,
        grid=(db,),
        in_specs=[per_b((1, SSD_INNER)), per_b((1, SSD_CONV_CH)), per_b((1, LANE)), per_b((SSD_CONV - 1, SSD_CONV_CH)),
                  per_b(st_shape)] + [const(p) for p in params],
        out_specs=[per_b((1, SSD_INNER)), per_b(st_shape), per_b((SSD_CONV - 1, SSD_CONV_CH))],
        out_shape=[jax.ShapeDtypeStruct((db, 1, SSD_INNER), F32), jax.ShapeDtypeStruct((db,) + st_shape, F32),
                   jax.ShapeDtypeStruct((db, SSD_CONV - 1, SSD_CONV_CH), F32)],
        compiler_params=pltpu.CompilerParams(dimension_semantics=("arbitrary",)),
        name="ssd_sample",
    )(z.reshape(db, 1, -1), xbc.reshape(db, 1, -1), small.reshape(db, 1, -1), conv_state,
      ssm_state.reshape((db,) + st_shape), *params)
    return y.reshape(db, SSD_INNER), st.reshape(db, SSD_HEADS, SSD_HEADDIM, SSD_STATE), cv


GLA_BLOCK = 128
GLA_INNER = GLA_HEADS * GLA_DV


def _log_gate(small, w2_ref, bg_ref):
    logits = jnp.dot(small.astype(BF16), w2_ref[...], preferred_element_type=F32) + bg_ref[...]
    return -_softplus(-logits) / GLA_TAU


def _head_rmsnorm_gate(o_heads, g_ref, r):
    outs = [o * lax.rsqrt(jnp.mean(o * o, axis=-1, keepdims=True) + EPS) for o in o_heads]
    return jnp.concatenate(outs, axis=1) * g_ref[...] * _silu(r)


def _gla_prompt_body(q_ref, k_ref, v_ref, r_ref, sm_ref, w2_ref, bg_ref, g_ref, y_ref, st_ref, state):
    blk = pl.program_id(1)
    t = GLA_BLOCK

    @pl.when(blk == 0)
    def _():
        state[...] = jnp.zeros_like(state)

    la = _log_gate(sm_ref[...], w2_ref, bg_ref)
    row = lax.broadcasted_iota(jnp.int32, (t, t), 0)
    col = lax.broadcasted_iota(jnp.int32, (t, t), 1)
    same = (row // GLA_CHUNK) == (col // GLA_CHUNK)
    causal = same & (row >= col)
    bc = _exact_dot(causal.astype(F32), la)
    is_last = col == (row // GLA_CHUNK) * GLA_CHUNK + GLA_CHUNK - 1
    bl = _exact_dot(is_last.astype(F32), bc)
    qt = q_ref[...] * (GLA_DK ** -0.5) * jnp.exp(bc)
    kt = k_ref[...] * jnp.exp(-bc)
    kd = k_ref[...] * jnp.exp(bl - bc)
    heads = range(GLA_HEADS)
    kcs = [slice(h * GLA_DK, (h + 1) * GLA_DK) for h in heads]
    vbs = [v_ref[:, h * GLA_DV:(h + 1) * GLA_DV].astype(BF16) for h in heads]
    qbs = [qt[:, kc].astype(BF16) for kc in kcs]
    o_intra = []
    for h in heads:
        att = jnp.where(causal, _nt_dot(qbs[h], kt[:, kcs[h]].astype(BF16)), 0.0)
        o_intra.append(jnp.dot(att.astype(BF16), vbs[h], preferred_element_type=F32))
    kd_t = [kd[:, kc].T for kc in kcs]
    dec_t = [jnp.exp(bl[:, kc]).T for kc in kcs]
    ss = [state[h] for h in heads]
    inter = [[] for _ in heads]
    for c in range(t // GLA_CHUNK):
        r0 = c * GLA_CHUNK
        in_chunk = (col // GLA_CHUNK) == c
        for h in heads:
            inter[h].append(jnp.dot(qbs[h][r0:r0 + GLA_CHUNK], ss[h].astype(BF16), preferred_element_type=F32))
        upd = [jnp.dot(jnp.where(in_chunk, kd_t[h], 0.0).astype(BF16), vbs[h], preferred_element_type=F32)
               for h in heads]
        ss = [dec_t[h][:, r0:r0 + 1] * ss[h] + upd[h] for h in heads]
    for h in heads:
        state[h] = ss[h]
    o_heads = [o_intra[h] + jnp.concatenate(inter[h], axis=0) for h in heads]
    y_ref[...] = _head_rmsnorm_gate(o_heads, g_ref, r_ref[...])

    @pl.when(blk == pl.num_programs(1) - 1)
    def _():
        st_ref[0] = state[...]


def _gla_params(w_g2, b_g, norm_g):
    w2 = jnp.pad(w_g2, ((0, LANE - GLA_GATE_RANK), (0, 0))).astype(BF16)
    return w2, b_g.astype(F32).reshape(1, -1), jnp.tile(norm_g.astype(F32), GLA_HEADS).reshape(1, -1)


def gla_prompt(gq, gk, gv, r, small, b, l, w_g2, b_g, norm_g):
    t = GLA_BLOCK
    nb = l // t
    params = _gla_params(w_g2, b_g, norm_g)
    rows = lambda w: pl.BlockSpec((t, w), lambda bi, i: (bi * nb + i, 0))
    const = lambda a: pl.BlockSpec(a.shape, lambda bi, i: (0,) * a.ndim)
    st_shape = (GLA_HEADS, GLA_DK, GLA_DV)
    return pl.pallas_call(
        _gla_prompt_body,
        grid=(b, nb),
        in_specs=[rows(GLA_HEADS * GLA_DK), rows(GLA_HEADS * GLA_DK), rows(GLA_INNER), rows(GLA_INNER), rows(LANE)]
        + [const(p) for p in params],
        out_specs=[rows(GLA_INNER), pl.BlockSpec((1,) + st_shape, lambda bi, i: (bi, 0, 0, 0))],
        out_shape=[jax.ShapeDtypeStruct((b * l, GLA_INNER), F32), jax.ShapeDtypeStruct((b,) + st_shape, F32)],
        scratch_shapes=[pltpu.VMEM(st_shape, F32)],
        compiler_params=pltpu.CompilerParams(dimension_semantics=("arbitrary", "arbitrary"),
                                             vmem_limit_bytes=VMEM_LIMIT_BYTES),
        name="gla_prompt",
    )(gq, gk, gv, r, small, *params)


def _gla_sample_body(q_ref, k_ref, v_ref, r_ref, sm_ref, st_ref, w2_ref, bg_ref, g_ref, y_ref, sto_ref):
    la = _log_gate(jnp.broadcast_to(sm_ref[0], (8, LANE)), w2_ref, bg_ref)[0:1]
    a = jnp.exp(la)
    qt = q_ref[0] * (GLA_DK ** -0.5) * a
    kt = k_ref[0] * jnp.exp(-la)
    o_heads = []
    for h in range(GLA_HEADS):
        kc = slice(h * GLA_DK, (h + 1) * GLA_DK)
        vb = _bf(v_ref[0, :, h * GLA_DV:(h + 1) * GLA_DV])
        qb = _bf(qt[:, kc])
        att = jnp.sum(qb * _bf(kt[:, kc]), axis=-1, keepdims=True)
        s_old = st_ref[0, h]
        o_inter = jnp.dot(jnp.broadcast_to(qb, (8, GLA_DK)).astype(BF16), s_old.astype(BF16),
                          preferred_element_type=F32)[0:1]
        a_rows = jnp.broadcast_to(a[:, kc], (GLA_DK, GLA_DK)).T[:, 0:1]
        k_rows = jnp.broadcast_to(k_ref[0, :, kc], (GLA_DK, GLA_DK)).T[:, 0:1]
        sto_ref[0, h] = a_rows * s_old + k_rows * v_ref[0, :, h * GLA_DV:(h + 1) * GLA_DV]
        o_heads.append(_bf(att) * vb + o_inter)
    y_ref[0] = _head_rmsnorm_gate(o_heads, g_ref, r_ref[0])


def gla_sample(gq, gk, gv, r, small, state, w_g2, b_g, norm_g):
    db = gq.shape[0]
    params = _gla_params(w_g2, b_g, norm_g)
    per_b = lambda shape: pl.BlockSpec((1,) + shape, lambda i: (i,) + (0,) * len(shape))
    const = lambda a: pl.BlockSpec(a.shape, lambda i: (0,) * a.ndim)
    st_shape = (GLA_HEADS, GLA_DK, GLA_DV)
    row3 = lambda x: x.reshape(db, 1, -1)
    y, st = pl.pallas_call(
        _gla_sample_body,
        grid=(db,),
        in_specs=[per_b((1, GLA_HEADS * GLA_DK)), per_b((1, GLA_HEADS * GLA_DK)), per_b((1, GLA_INNER)),
                  per_b((1, GLA_INNER)), per_b((1, LANE)), per_b(st_shape)] + [const(p) for p in params],
        out_specs=[per_b((1, GLA_INNER)), per_b(st_shape)],
        out_shape=[jax.ShapeDtypeStruct((db, 1, GLA_INNER), F32), jax.ShapeDtypeStruct((db,) + st_shape, F32)],
        compiler_params=pltpu.CompilerParams(dimension_semantics=("arbitrary",)),
        name="gla_sample",
    )(row3(gq), row3(gk), row3(gv), row3(r), row3(small), state, *params)
    return y.reshape(db, GLA_INNER), st


def _ffn_body(x_ref, g_ref, wg_ref, wu_ref, wd_ref, fg_ref, o_ref, *, chunks, final):
    x = x_ref[...]
    h = (x * lax.rsqrt(jnp.mean(x * x, axis=-1, keepdims=True) + EPS) * g_ref[...]).astype(BF16)
    acc = x
    for c0, cw in chunks:
        a = jnp.dot(h, wg_ref[:, c0:c0 + cw], preferred_element_type=F32)
        u = jnp.dot(h, wu_ref[:, c0:c0 + cw], preferred_element_type=F32)
        act = (a * jax.nn.sigmoid(a) * u).astype(BF16)
        acc = acc + jnp.dot(act, wd_ref[c0:c0 + cw, :], preferred_element_type=F32)
    if final:
        acc = acc * lax.rsqrt(jnp.mean(acc * acc, axis=-1, keepdims=True) + EPS) * fg_ref[...]
    o_ref[...] = acc


def swiglu_ffn(x, g, w_gate, w_up, w_down, final_g=None):
    m, d = x.shape
    hdim = w_gate.shape[1]
    tm = _row_tile(m)
    final = final_g is not None
    fg = (final_g if final else g).reshape(1, d).astype(F32)
    wspec = lambda shape: pl.BlockSpec(shape, lambda i: (0, 0), pipeline_mode=pl.Buffered(1))
    return pl.pallas_call(
        functools.partial(_ffn_body, chunks=_col_chunks(hdim, 256), final=final),
        grid=(m // tm,),
        in_specs=[pl.BlockSpec((tm, d), lambda i: (i, 0)),
                  pl.BlockSpec((1, d), lambda i: (0, 0)),
                  wspec((d, hdim)), wspec((d, hdim)), wspec((hdim, d)),
                  pl.BlockSpec((1, d), lambda i: (0, 0))],
        out_specs=pl.BlockSpec((tm, d), lambda i: (i, 0)),
        out_shape=jax.ShapeDtypeStruct((m, d), F32),
        compiler_params=pltpu.CompilerParams(dimension_semantics=("arbitrary",),
                                             vmem_limit_bytes=VMEM_LIMIT_BYTES),
        name="swiglu_ffn",
    )(x, g.reshape(1, d).astype(F32), w_gate.astype(BF16), w_up.astype(BF16), w_down.astype(BF16), fg)


def split_cols(h, sizes):
    return jnp.split(h, [int(i) for i in np.cumsum(sizes)[:-1]], axis=-1)


def rmsnorm(x, g):
    xf = x.astype(F32)
    y = xf * lax.rsqrt(jnp.mean(xf * xf, axis=-1, keepdims=True) + EPS)
    return (y * g.astype(F32)).astype(x.dtype)


def rope(x, pos):
    half = x.shape[-1] // 2
    freqs = ROPE_THETA ** (-jnp.arange(half, dtype=F32) / half)
    ang = pos.astype(F32)[:, None] * freqs
    shape = (pos.shape[0],) + (1,) * (x.ndim - 3) + (half,)
    cos, sin = jnp.cos(ang).reshape(shape), jnp.sin(ang).reshape(shape)
    xf = x.astype(F32)
    x1, x2 = xf[..., :half], xf[..., half:]
    return jnp.concatenate([x1 * cos - x2 * sin, x2 * cos + x1 * sin], axis=-1).astype(x.dtype)


def t5_bucket(rel):
    exact = REL_BUCKETS // 2
    n = jnp.maximum(rel, 0)
    nf = jnp.maximum(n, exact).astype(F32)
    large = exact + (jnp.log(nf / exact) / math.log(REL_MAX_DIST / exact) * (REL_BUCKETS - exact)).astype(jnp.int32)
    return jnp.where(n < exact, n, jnp.minimum(large, REL_BUCKETS - 1))


def shared_bias(rel, table):
    lq, nk = rel.shape
    return table.astype(F32)[t5_bucket(rel)].reshape(lq, nk, NSA_KV_HEADS, NSA_GROUP).transpose(0, 2, 3, 1)


def masked_softmax(s, mask):
    return jax.nn.softmax(jnp.where(mask, s, NEG), axis=-1) * mask


def causal_conv(xbc, conv_state, w, b):
    full = jnp.concatenate([conv_state.astype(xbc.dtype), xbc], axis=1)
    ch = xbc.shape[-1]
    y = lax.conv_general_dilated(full, w[:, None, :].astype(xbc.dtype), (1,), 'VALID',
                                 dimension_numbers=('NWC', 'WIO', 'NWC'), feature_group_count=ch)
    return jax.nn.silu(y + b.astype(y.dtype)), full[:, -(SSD_CONV - 1):]


def segsum(x):
    t = x.shape[-1]
    cs = jnp.cumsum(x, axis=-1)
    diff = cs[..., :, None] - cs[..., None, :]
    return jnp.where(jnp.tril(jnp.ones((t, t), bool)), diff, -jnp.inf)


def ssd_scan(x, dt, a, bm, cm, s0, chunk):
    b, l, h, p = x.shape
    g, n = bm.shape[2], bm.shape[3]
    e = h // g
    nc = l // chunk
    xd = (x * dt[..., None]).reshape(b, nc, chunk, g, e, p)
    da = (dt * a).reshape(b, nc, chunk, g, e).transpose(0, 3, 4, 1, 2)
    bm = bm.reshape(b, nc, chunk, g, n)
    cm = cm.reshape(b, nc, chunk, g, n)
    a_cs = jnp.cumsum(da, axis=-1)
    lmat = jnp.exp(segsum(da))
    cb = jnp.einsum('bclgn,bcsgn->bcgls', cm, bm)
    y_diag = jnp.einsum('bcgls,bgecls,bcsgep->bclgep', cb, lmat, xd)
    decay_st = jnp.exp(a_cs[..., -1:] - a_cs)
    states = jnp.einsum('bclgn,bgecl,bclgep->bcgepn', bm, decay_st, xd)
    states = jnp.concatenate([s0.reshape(b, 1, g, e, p, n), states], axis=1)
    chunk_decay = jnp.exp(segsum(jnp.pad(a_cs[..., -1], ((0, 0), (0, 0), (0, 0), (1, 0)))))
    new_states = jnp.einsum('bgezc,bcgepn->bzgepn', chunk_decay, states)
    y_off = jnp.einsum('bclgn,bcgepn,bgecl->bclgep', cm, new_states[:, :-1], jnp.exp(a_cs))
    return (y_diag + y_off).reshape(b, l, h, p), new_states[:, -1].reshape(b, h, p, n)


def ssd_mixer(z, xbc, dt_raw, conv_state, ssm_state, conv_w, conv_b, dt_bias, a_log, d_skip, norm_g):
    b, l = z.shape[:2]
    xbc, new_conv = causal_conv(xbc, conv_state, conv_w, conv_b)
    xs, bm, cm = split_cols(xbc.astype(F32), (SSD_INNER, SSD_GROUPS * SSD_STATE, SSD_GROUPS * SSD_STATE))
    x = xs.reshape(b, l, SSD_HEADS, SSD_HEADDIM)
    bm = bm.reshape(b, l, SSD_GROUPS, SSD_STATE)
    cm = cm.reshape(b, l, SSD_GROUPS, SSD_STATE)
    dt = jax.nn.softplus(dt_raw.astype(F32) + dt_bias.astype(F32))
    a = -jnp.exp(a_log.astype(F32))
    chunk = SSD_CHUNK if l % SSD_CHUNK == 0 else l
    y, new_ssm = ssd_scan(x, dt, a, bm, cm, ssm_state.astype(F32), chunk)
    y = y + x * d_skip.astype(F32)[:, None]
    y = (y.reshape(b, l, SSD_INNER) * jax.nn.silu(z.astype(F32))).reshape(b, l, SSD_GROUPS, SSD_INNER // SSD_GROUPS)
    y = rmsnorm(y, norm_g.reshape(SSD_GROUPS, -1)).reshape(b, l, SSD_INNER)
    return y.astype(z.dtype), new_conv, new_ssm


def nsa_compress(k, w):
    b, t = k.shape[:2]
    nseg = t // NSA_CMP_STRIDE
    seg = k[:, :nseg * NSA_CMP_STRIDE].reshape(b, nseg, NSA_CMP_STRIDE, NSA_KV_HEADS, NSA_HEAD_DIM)
    first = jnp.einsum('bsrhd,rd->bshd', seg, w[:NSA_CMP_STRIDE])
    second = jnp.einsum('bsrhd,rd->bshd', seg, w[NSA_CMP_STRIDE:])
    return first[:, :-1] + second[:, 1:]


def sel_overlap(n_sel, n_cmp):
    j = jnp.arange(n_sel)[:, None]
    i = jnp.arange(n_cmp)[None, :]
    lo = jnp.maximum(j * NSA_SEL_BLOCK, i * NSA_CMP_STRIDE)
    hi = jnp.minimum(j * NSA_SEL_BLOCK + NSA_SEL_BLOCK, i * NSA_CMP_STRIDE + NSA_CMP_BLOCK)
    return (jnp.maximum(hi - lo, 0) / NSA_CMP_STRIDE).astype(F32)


def nsa_cmp_branch(q, qpos, kc, vc, table):
    nc = kc.shape[1]
    cend = jnp.arange(nc) * NSA_CMP_STRIDE + NSA_CMP_BLOCK - 1
    rel = qpos[:, None] - cend[None, :]
    s = jnp.einsum('bqhgd,bchd->bqhgc', q, kc).astype(F32) * NSA_SCALE + shared_bias(rel, table)
    p = masked_softmax(s, (rel >= 0)[:, None, None, :])
    return jnp.einsum('bqhgc,bchd->bqhgd', p, vc), p


def nsa_select(p_cmp, qpos, n_sel):
    imp = jnp.einsum('bqhgc,jc->bqhj', p_cmp, sel_overlap(n_sel, p_cmp.shape[-1]))
    j = jnp.arange(n_sel)[None, :]
    cur = (qpos // NSA_SEL_BLOCK)[:, None]
    valid = (j <= cur)[:, None, :]
    forced = ((j == 0) | (j == cur) | (j == cur - 1))[:, None, :]
    score = jnp.where(valid, jnp.where(forced, NSA_FORCE, imp), -1.0)
    if n_sel < NSA_TOPK:
        score = jnp.pad(score, ((0, 0), (0, 0), (0, 0), (0, NSA_TOPK - n_sel)), constant_values=-1.0)
    vals, idx = lax.top_k(score, NSA_TOPK)
    kpos = (idx[..., None] * NSA_SEL_BLOCK + jnp.arange(NSA_SEL_BLOCK)).reshape(*idx.shape[:-1], NSA_TOPK * NSA_SEL_BLOCK)
    kok = jnp.repeat(vals >= 0, NSA_SEL_BLOCK, axis=-1)
    return kpos, kok


def nsa_slc_branch(q, qpos, ks, vs, kpos, kok, table):
    rel = qpos[None, :, None, None] - kpos
    tb = table.astype(F32).reshape(REL_BUCKETS, NSA_KV_HEADS, NSA_GROUP)
    bias = jnp.moveaxis(tb[t5_bucket(rel), jnp.arange(NSA_KV_HEADS)[:, None]], -1, -2)
    s = jnp.einsum('bqhgd,bqhkd->bqhgk', q, ks).astype(F32) * NSA_SCALE + bias
    p = masked_softmax(s, (kok & (rel >= 0))[..., None, :])
    return jnp.einsum('bqhgk,bqhkd->bqhgd', p, vs)


def nsa_win_branch(q, qpos, kw, vw, wpos, table):
    rel = qpos[:, None] - wpos[None, :]
    mask = (rel >= 0) & (rel < NSA_WINDOW) & (wpos >= 0)[None, :]
    s = jnp.einsum('bqhgd,bkhd->bqhgk', q, kw).astype(F32) * NSA_SCALE + shared_bias(rel, table)
    p = masked_softmax(s, mask[:, None, None, :])
    return jnp.einsum('bqhgk,bkhd->bqhgd', p, vw)


def nsa_gate(g, o_c, o_s, o_w):
    return g[..., 0:1] * o_c + g[..., 1:2] * o_s + g[..., 2:3] * o_w


def gather_rows(pool, e, new, page_table, kpos):
    db, s = new.shape[:2]
    p0 = page_table.shape[1] * PAGE_SIZE
    bi = jnp.arange(db)[:, None, None, None]
    hi = jnp.arange(NSA_KV_HEADS)[None, None, :, None]
    pp = jnp.minimum(kpos, p0 - 1)
    past = pool[e, page_table[bi, pp // PAGE_SIZE], pp % PAGE_SIZE, hi]
    cur = new[bi, jnp.clip(kpos - p0, 0, s - 1), hi]
    return jnp.where((kpos < p0)[..., None], past, cur)


def nsa_sample(q, kvs, gates, e, k_cmp_pool, v_cmp_pool, k_slc_pool, v_slc_pool, k_win_buf, v_win_buf,
               page_table, w_ck, w_cv, table):
    kc_new, vc_new, ks_new, vs_new, kw_new, vw_new = kvs
    db, s = q.shape[:2]
    p0 = page_table.shape[1] * PAGE_SIZE
    qpos = p0 + jnp.arange(s)
    past = lambda pool: pool[e, page_table].reshape(db, p0, NSA_KV_HEADS, NSA_HEAD_DIM)
    kc = nsa_compress(jnp.concatenate([past(k_cmp_pool), kc_new], axis=1), w_ck)
    vc = nsa_compress(jnp.concatenate([past(v_cmp_pool), vc_new], axis=1), w_cv)
    o_c, p_c = nsa_cmp_branch(q, qpos, kc, vc, table)
    kpos, kok = nsa_select(p_c, qpos, -(-(p0 + s) // NSA_SEL_BLOCK))
    ksel = gather_rows(k_slc_pool, e, ks_new, page_table, kpos)
    vsel = gather_rows(v_slc_pool, e, vs_new, page_table, kpos)
    o_s = nsa_slc_branch(q, qpos, ksel, vsel, kpos, kok, table)
    nbuf = k_win_buf.shape[1]
    kw = jnp.concatenate([k_win_buf.astype(kw_new.dtype), kw_new], axis=1)
    vw = jnp.concatenate([v_win_buf.astype(vw_new.dtype), vw_new], axis=1)
    wpos = p0 - nbuf + jnp.arange(nbuf + s)
    o_w = nsa_win_branch(q, qpos, kw, vw, wpos, table)
    out = nsa_gate(gates, o_c, o_s, o_w).reshape(db, s, NSA_HEADS * NSA_HEAD_DIM)
    return out, kw[:, -nbuf:], vw[:, -nbuf:]


def even_split(parts, r0, b, l):
    z, xbc, q, kv, small = (t[r0:r0 + b * l].reshape(b, l, -1) for t in parts)
    dtr = small[..., SMALL_DT:SMALL_DT + SSD_HEADS]
    g = small[..., SMALL_GATE:SMALL_GATE + 3 * NSA_HEADS]
    q = q.reshape(b, l, NSA_KV_HEADS, NSA_GROUP, NSA_HEAD_DIM)
    kvs = tuple(t.reshape(b, l, NSA_KV_HEADS, NSA_HEAD_DIM) for t in split_cols(kv, (NSA_KV_WIDTH,) * 6))
    g = jax.nn.sigmoid(g.astype(F32)).reshape(b, l, 3, NSA_KV_HEADS, NSA_GROUP).transpose(0, 1, 3, 4, 2)
    return z, xbc, dtr, q, kvs, g


def gla_scan(q, k, v, log_a, s0, chunk):
    b, l, h, dk = q.shape
    dv = v.shape[-1]
    nc = l // chunk
    q, k, v, log_a = [t.reshape(b, nc, chunk, h, t.shape[-1]) for t in (q, k, v, log_a)]
    bc = jnp.cumsum(log_a, axis=2)
    blast = bc[:, :, -1:]
    qt = q * jnp.exp(bc)
    kt = k * jnp.exp(-bc)
    kd = k * jnp.exp(blast - bc)
    causal = jnp.tril(jnp.ones((chunk, chunk), bool))
    att = jnp.where(causal, jnp.einsum('bclhk,bcshk->bchls', qt, kt), 0.0)
    o_intra = jnp.einsum('bchls,bcshv->bclhv', att, v)

    def step(st, inp):
        qt_c, kd_c, v_c, dec_c = inp
        o_c = jnp.einsum('blhk,bhkv->blhv', qt_c, st)
        st = jnp.exp(dec_c)[..., None] * st + jnp.einsum('blhk,blhv->bhkv', kd_c, v_c)
        return st, o_c

    xs = tuple(jnp.swapaxes(t, 0, 1) for t in (qt, kd, v, blast[:, :, 0]))
    s_fin, o_inter = lax.scan(step, s0, xs)
    return (o_intra + jnp.swapaxes(o_inter, 0, 1)).reshape(b, l, h, dv), s_fin


def gla_mixer(q, k, v, g1, r, state, w_g2, b_g, norm_g):
    b, l = q.shape[:2]
    q = q.astype(F32).reshape(b, l, GLA_HEADS, GLA_DK) * GLA_DK ** -0.5
    k = k.astype(F32).reshape(b, l, GLA_HEADS, GLA_DK)
    v = v.astype(F32).reshape(b, l, GLA_HEADS, GLA_DV)
    log_a = jax.nn.log_sigmoid(g1.astype(F32) @ w_g2.astype(F32) + b_g.astype(F32)).reshape(b, l, GLA_HEADS, GLA_DK) / GLA_TAU
    chunk = GLA_CHUNK if l % GLA_CHUNK == 0 else l
    o, new_state = gla_scan(q, k, v, log_a, state.astype(F32), chunk)
    o = rmsnorm(o, norm_g).reshape(b, l, GLA_HEADS * GLA_DV) * jax.nn.silu(r.astype(F32))
    return o.astype(r.dtype), new_state


def odd_split(h, b, l, pos, q_norm, w_qb, kv_norm):
    gq, gk, gv, g1, r, qa, kva, kr = split_cols(h.reshape(b, l, -1), OD_SPLITS)
    q = fused_matmul(qa.reshape(b * l, -1), w_qb, norm_g=q_norm).reshape(b, l, MLA_HEADS, MLA_NOPE + MLA_ROPE)
    q_nope, q_rope = q[..., :MLA_NOPE], rope(q[..., MLA_NOPE:], pos)
    return (gq, gk, gv, g1, r), (q_nope, q_rope, rmsnorm(kva, kv_norm), rope(kr, pos))


def mla_prompt(q_nope, q_rope, c, k_rope, w_kvb):
    b, l = c.shape[:2]
    kv = fused_matmul(c.reshape(b * l, -1), w_kvb).reshape(b, l, MLA_HEADS, MLA_NOPE + MLA_V)
    k_nope, v = kv[..., :MLA_NOPE], kv[..., MLA_NOPE:]
    nqb = l // MLA_QBLOCK
    kpos = jnp.arange(l)

    def block(args):
        i, qn, qr = args
        qpos = i * MLA_QBLOCK + jnp.arange(MLA_QBLOCK)
        s = (jnp.einsum('bqhd,bkhd->bhqk', qn, k_nope) + jnp.einsum('bqhr,bkr->bhqk', qr, k_rope)).astype(F32) * MLA_SCALE
        p = jax.nn.softmax(jnp.where(kpos[None, :] <= qpos[:, None], s, NEG), axis=-1)
        return jnp.einsum('bhqk,bkhd->bqhd', p, v)

    blk = lambda t: t.reshape(b, nqb, MLA_QBLOCK, *t.shape[2:]).swapaxes(0, 1)
    o = lax.map(block, (jnp.arange(nqb), blk(q_nope), blk(q_rope)))
    return o.swapaxes(0, 1).reshape(b, l, MLA_HEADS * MLA_V)


def mla_sample(q_nope, q_rope, c, k_rope, lat_pool, kr_pool, o_idx, page_table, w_kvb):
    db, s = c.shape[:2]
    w = w_kvb.reshape(MLA_KV_RANK, MLA_HEADS, MLA_NOPE + MLA_V)
    q_lat = jnp.einsum('bshd,chd->bshc', q_nope, w[..., :MLA_NOPE])
    c_past = lat_pool[o_idx, page_table].reshape(db, -1, MLA_KV_RANK)
    kr_past = kr_pool[o_idx, page_table].reshape(db, -1, MLA_ROPE)
    s_past = (jnp.einsum('bshc,btc->bsht', q_lat, c_past) + jnp.einsum('bshr,btr->bsht', q_rope, kr_past)).astype(F32) * MLA_SCALE
    s_new = (jnp.einsum('bshc,btc->bsht', q_lat, c) + jnp.einsum('bshr,btr->bsht', q_rope, k_rope)).astype(F32) * MLA_SCALE
    causal = jnp.arange(s)[None, :] <= jnp.arange(s)[:, None]
    s_new = jnp.where(causal[:, None, :], s_new, NEG)
    p = jax.nn.softmax(jnp.concatenate([s_past, s_new], axis=-1), axis=-1)
    n_past = c_past.shape[1]
    o_lat = jnp.einsum('bsht,btc->bshc', p[..., :n_past], c_past) + jnp.einsum('bsht,btc->bshc', p[..., n_past:], c)
    return jnp.einsum('bshc,chd->bshd', o_lat, w[..., MLA_NOPE:]).reshape(db, s, MLA_HEADS * MLA_V)


def kernel(x_prompt, x_sample, state_ssm, state_conv, cache_nsa_k_cmp, cache_nsa_v_cmp, cache_nsa_k_slc, cache_nsa_v_slc, cache_nsa_k_win, cache_nsa_v_win, state_gla, cache_mla_latent, cache_mla_krope, page_table, rel_bias, ev_norm, ev_w_in, ssd_conv_w, ssd_conv_b, ssd_dt_bias, ssd_a_log, ssd_d, ssd_norm, nsa_w_cmp_k, nsa_w_cmp_v, ev_w_out, od_norm, od_w_in, gla_w_gate2, gla_b_gate, gla_norm, mla_q_norm, mla_w_qb, mla_kv_norm, mla_w_kvb, od_w_out, ffn_norm, ffn_w_gate, ffn_w_up, ffn_w_down, final_norm):
    b, l = x_prompt.shape[:2]
    db, s = x_sample.shape[:2]
    depth = ffn_norm.shape[0]
    p0 = page_table.shape[1] * PAGE_SIZE
    pos_p = jnp.arange(l)
    pos_s = p0 + jnp.arange(s)
    names = ('ssm', 'conv', 'k_cmp', 'v_cmp', 'k_slc', 'v_slc', 'k_win', 'v_win', 'gla', 'lat', 'krope')
    newp = {n: [] for n in names}
    news = {n: [] for n in names}
    np_rows = b * l
    x = jnp.concatenate([x_prompt.reshape(np_rows, D_MODEL), x_sample.reshape(db * s, D_MODEL)], axis=0)
    for li in range(depth):
        if li % 2 == 0:
            e = li // 2
            ssd_w = (ssd_conv_w[e], ssd_conv_b[e], ssd_dt_bias[e], ssd_a_log[e], ssd_d[e], ssd_norm[e])
            parts = fused_matmul(x, even_weight(ev_w_in[e]), norm_g=ev_norm[e], groups=EV_GROUPS)
            z_all, xbc_all, _, kv_all, small_all = parts
            kv_heads = lambda r0, nb, nl: tuple(
                t.reshape(nb, nl, NSA_KV_HEADS, NSA_HEAD_DIM)
                for t in split_cols(kv_all[r0:r0 + nb * nl], (NSA_KV_WIDTH,) * 6))
            y_ssd, ssm_n, conv_n = ssd_prompt(z_all, xbc_all, small_all, b, l, *ssd_w)
            y_nsa = nsa_prompt_attention(parts[2], parts[3], parts[4], nsa_w_cmp_k[e], nsa_w_cmp_v[e], rel_bias, b, l)
            mix_p = jnp.concatenate([y_ssd, y_nsa], axis=-1)
            nw = min(NSA_WINDOW, l)
            kvs = kv_heads(0, b, l)
            for n, t in zip(names[:8], (ssm_n, conv_n, kvs[0], kvs[1], kvs[2], kvs[3],
                                        kvs[4][:, -nw:], kvs[5][:, -nw:])):
                newp[n].append(t)
            kvs = kv_heads(np_rows, db, s)
            y_ssd, ssm_n, conv_n = ssd_sample(z_all[np_rows:], xbc_all[np_rows:], small_all[np_rows:],
                                              state_conv[e], state_ssm[e], *ssd_w)
            y_nsa, kw_n, vw_n = nsa_sample_mixer(parts[2][np_rows:], parts[3][np_rows:], parts[4][np_rows:], e,
                                                 cache_nsa_k_cmp, cache_nsa_v_cmp, cache_nsa_k_slc, cache_nsa_v_slc,
                                                 cache_nsa_k_win[e], cache_nsa_v_win[e], page_table,
                                                 nsa_w_cmp_k[e], nsa_w_cmp_v[e], rel_bias)
            mix_s = jnp.concatenate([y_ssd, y_nsa], axis=-1)
            for n, t in zip(names[:8], (ssm_n, conv_n, kvs[0], kvs[1], kvs[2], kvs[3], kw_n, vw_n)):
                news[n].append(t)
            x = fused_matmul(jnp.concatenate([mix_p, mix_s], axis=0), ev_w_out[e], residual=x)
        else:
            o = li // 2
            gla_w = (gla_w_gate2[o], gla_b_gate[o], gla_norm[o])
            gq, gk, gv, r, qa, kva, small = fused_matmul(x, odd_weight(od_w_in[o]), norm_g=od_norm[o],
                                                         groups=OD_GROUPS)
            assert db * s == MLA_PREP_ROWS and l % MLA_PREP_ROWS == 0
            nblk = l // MLA_PREP_ROWS
            tables = tuple(jnp.concatenate(t, axis=0)
                           for t in zip(rope_tables(pos_p), rope_tables(jnp.tile(pos_s, db))))
            q_rot, lat, k_full, v_pad, kr_rot = mla_prep(
                qa, kva, small, mla_q_norm[o], mla_w_qb[o], mla_kv_norm[o], mla_w_kvb[o], tables,
                lambda i: jnp.where(i < b * nblk, i % nblk, nblk))
            kr_rot = kr_rot[:, MLA_NOPE:MLA_NOPE + MLA_ROPE]
            assert s == 1
            y_gla, gla_n = gla_prompt(gq, gk, gv, r, small, b, l, *gla_w)
            y_mla = mla_prompt_attention(q_rot, k_full, v_pad, b, l)
            mix_p = jnp.concatenate([y_gla, y_mla], axis=-1)
            newp['gla'].append(gla_n)
            newp['lat'].append(lat[:np_rows].reshape(b, l, -1))
            newp['krope'].append(kr_rot[:np_rows].reshape(b, l, -1))
            y_gla, gla_n = gla_sample(gq[np_rows:], gk[np_rows:], gv[np_rows:], r[np_rows:], small[np_rows:],
                                      state_gla[o], *gla_w)
            q_s = q_rot[np_rows:].reshape(db * s, MLA_HEADS, LANE)
            y_mla = mla_sample_mixer(q_s[..., :MLA_NOPE], q_s[..., MLA_NOPE:MLA_NOPE + MLA_ROPE], lat[np_rows:],
                                     kr_rot[np_rows:], cache_mla_latent, cache_mla_krope, o, page_table, mla_w_kvb[o])
            mix_s = jnp.concatenate([y_gla, y_mla], axis=-1)
            news['gla'].append(gla_n)
            news['lat'].append(lat[np_rows:].reshape(db, s, -1))
            news['krope'].append(kr_rot[np_rows:].reshape(db, s, -1))
            x = fused_matmul(jnp.concatenate([mix_p, mix_s], axis=0), od_w_out[o], residual=x)
        x = swiglu_ffn(x, ffn_norm[li], ffn_w_gate[li], ffn_w_up[li], ffn_w_down[li],
                       final_g=final_norm if li == depth - 1 else None)
    y_prompt = x[:np_rows].reshape(b, l, D_MODEL)
    y_sample = x[np_rows:].reshape(db, s, D_MODEL)
    st = lambda d, n: jnp.stack(d[n])
    return (y_prompt, y_sample,
            st(newp, 'ssm'), st(news, 'ssm'), st(newp, 'conv'), st(news, 'conv'),
            st(newp, 'k_cmp'), st(news, 'k_cmp'), st(newp, 'v_cmp'), st(news, 'v_cmp'),
            st(newp, 'k_slc'), st(news, 'k_slc'), st(newp, 'v_slc'), st(news, 'v_slc'),
            st(newp, 'k_win'), st(news, 'k_win'), st(newp, 'v_win'), st(news, 'v_win'),
            st(newp, 'gla'), st(news, 'gla'), st(newp, 'lat'), st(news, 'lat'),
            st(newp, 'krope'), st(news, 'krope'))
```

```python
import functools
import math

import jax
import jax.numpy as jnp
import numpy as np
from jax import lax
from jax.experimental import pallas as pl
from jax.experimental.pallas import tpu as pltpu

F32 = jnp.float32
BF16 = jnp.bfloat16
EPS = 1e-6
NEG = -1e30

D_MODEL = 1024
PAGE_SIZE = 128

SSD_HEADS = 16
SSD_HEADDIM = 64
SSD_INNER = SSD_HEADS * SSD_HEADDIM
SSD_GROUPS = 2
SSD_STATE = 128
SSD_CONV = 4
SSD_CHUNK = 128
SSD_CONV_CH = SSD_INNER + 2 * SSD_GROUPS * SSD_STATE

NSA_HEADS = 16
NSA_KV_HEADS = 2
NSA_GROUP = NSA_HEADS // NSA_KV_HEADS
NSA_HEAD_DIM = 64
NSA_KV_WIDTH = NSA_KV_HEADS * NSA_HEAD_DIM
NSA_CMP_BLOCK = 32
NSA_CMP_STRIDE = 16
NSA_SEL_BLOCK = 64
NSA_TOPK = 16
NSA_WINDOW = 512
NSA_QBLOCK = 128
NSA_SCALE = NSA_HEAD_DIM ** -0.5
NSA_FORCE = 1e4

REL_BUCKETS = 32
REL_MAX_DIST = 128

GLA_HEADS = 4
GLA_DK = 128
GLA_DV = 256
GLA_GATE_RANK = 16
GLA_TAU = 16.0
GLA_CHUNK = 16

MLA_HEADS = 8
MLA_Q_RANK = 384
MLA_KV_RANK = 256
MLA_NOPE = 64
MLA_ROPE = 32
MLA_V = 64
MLA_SCALE = (MLA_NOPE + MLA_ROPE) ** -0.5
ROPE_THETA = 10000.0

EV_SPLITS = (SSD_INNER, SSD_CONV_CH, SSD_HEADS, NSA_HEADS * NSA_HEAD_DIM, 6 * NSA_KV_WIDTH, 3 * NSA_HEADS)
OD_SPLITS = (GLA_HEADS * GLA_DK, GLA_HEADS * GLA_DK, GLA_HEADS * GLA_DV, GLA_GATE_RANK, GLA_HEADS * GLA_DV,
             MLA_Q_RANK, MLA_KV_RANK, MLA_ROPE)

VMEM_LIMIT_BYTES = 56 * 1024 * 1024
LANE = 128


def _row_tile(m):
    for t in (512, 384, 256, 128):
        if m % t == 0:
            return t
    return m


def _col_chunks(n, width=512):
    out, c = [], 0
    while c < n:
        w = min(width, n - c)
        out.append((c, w))
        c += w
    return out


def _mm_body(*refs, norm, residual, groups):
    it = iter(refs)
    x_ref = next(it)
    g_ref = next(it) if norm else None
    w_ref = next(it)
    r_ref = next(it) if residual else None
    o_refs = list(it)
    x = x_ref[...]
    if norm:
        x = x * lax.rsqrt(jnp.mean(x * x, axis=-1, keepdims=True) + EPS) * g_ref[...]
    xb = x.astype(BF16)
    off = 0
    for o_ref, gw in zip(o_refs, groups):
        for c0, cw in _col_chunks(gw):
            acc = jnp.dot(xb, w_ref[:, off + c0:off + c0 + cw], preferred_element_type=F32)
            if residual:
                acc = acc + r_ref[:, off + c0:off + c0 + cw]
            o_ref[:, c0:c0 + cw] = acc
        off += gw


def fused_matmul(x, w, norm_g=None, residual=None, groups=None):
    m, k = x.shape
    n = w.shape[1]
    single = groups is None
    groups = (n,) if single else tuple(groups)
    assert sum(groups) == n
    tm = _row_tile(m)
    norm = norm_g is not None
    res = residual is not None
    args = [x]
    specs = [pl.BlockSpec((tm, k), lambda i: (i, 0))]
    if norm:
        args.append(norm_g.reshape(1, k).astype(F32))
        specs.append(pl.BlockSpec((1, k), lambda i: (0, 0)))
    args.append(w.astype(BF16))
    specs.append(pl.BlockSpec((k, n), lambda i: (0, 0)))
    if res:
        args.append(residual)
        specs.append(pl.BlockSpec((tm, n), lambda i: (i, 0)))
    outs = pl.pallas_call(
        functools.partial(_mm_body, norm=norm, residual=res, groups=groups),
        grid=(m // tm,),
        in_specs=specs,
        out_specs=[pl.BlockSpec((tm, gw), lambda i: (i, 0)) for gw in groups],
        out_shape=[jax.ShapeDtypeStruct((m, gw), F32) for gw in groups],
        compiler_params=pltpu.CompilerParams(dimension_semantics=("arbitrary",),
                                             vmem_limit_bytes=VMEM_LIMIT_BYTES),
        name="fused_matmul",
    )(*args)
    return outs[0] if single else outs


EV_GROUPS = (SSD_INNER, SSD_CONV_CH, NSA_HEADS * NSA_HEAD_DIM, 6 * NSA_KV_WIDTH, LANE)
SMALL_DT = 0
SMALL_GATE = SSD_HEADS
QB = NSA_QBLOCK


def even_weight(w_in):
    z, xbc, dtr, q, kvs, g = split_cols(w_in, EV_SPLITS)
    pad = jnp.zeros((w_in.shape[0], LANE - SSD_HEADS - 3 * NSA_HEADS), w_in.dtype)
    return jnp.concatenate([z, xbc, q, kvs, dtr, g, pad], axis=1)


def _bucket_tile(rel):
    exact = REL_BUCKETS // 2
    n = jnp.maximum(rel, 0)
    nf = jnp.maximum(n, exact).astype(F32)
    large = exact + (jnp.log(nf / exact) / math.log(REL_MAX_DIST / exact) * (REL_BUCKETS - exact)).astype(jnp.int32)
    return jnp.where(n < exact, n, jnp.minimum(large, REL_BUCKETS - 1))


def _bias_tiles_body(table_ref, o_ref, *, base_step, base_off, kmul, qmul):
    t = pl.program_id(0)
    shape = o_ref.shape[2:]
    qi = lax.broadcasted_iota(jnp.int32, shape, 0)
    ki = lax.broadcasted_iota(jnp.int32, shape, 1)
    bucket = _bucket_tile(t * base_step + base_off + qmul * qi - kmul * ki)
    for hd in range(NSA_HEADS):
        acc = jnp.zeros(shape, F32)
        for bk in range(REL_BUCKETS):
            acc = jnp.where(bucket == bk, table_ref[bk, hd], acc)
        o_ref[0, hd] = acc


def bias_tiles(table, n_tiles, base_step, base_off, kmul, qmul=1, rows=QB):
    return pl.pallas_call(
        functools.partial(_bias_tiles_body, base_step=base_step, base_off=base_off, kmul=kmul, qmul=qmul),
        grid=(n_tiles,),
        in_specs=[pl.BlockSpec(memory_space=pltpu.SMEM)],
        out_specs=pl.BlockSpec((1, NSA_HEADS, rows, LANE), lambda t: (t, 0, 0, 0)),
        out_shape=jax.ShapeDtypeStruct((n_tiles, NSA_HEADS, rows, LANE), F32),
        compiler_params=pltpu.CompilerParams(dimension_semantics=("arbitrary",)),
        name="t5_bias_tiles",
    )(table.astype(F32))


def _compress_body(kv_ref, w_ref, o_ref):
    nseg = o_ref.shape[1]
    first = jnp.zeros(o_ref.shape[1:], F32)
    second = jnp.zeros(o_ref.shape[1:], F32)
    for r in range(NSA_CMP_STRIDE):
        rows = kv_ref[pl.ds(r, nseg, stride=NSA_CMP_STRIDE), :]
        first = first + rows * w_ref[r:r + 1, :]
        second = second + rows * w_ref[NSA_CMP_STRIDE + r:NSA_CMP_STRIDE + r + 1, :]
    out = first + pltpu.roll(second, nseg - 1, axis=0)
    row = lax.broadcasted_iota(jnp.int32, out.shape, 0)
    o_ref[0] = jnp.where(row < nseg - 1, out, 0.0)


def nsa_compress_prompt(kv, w_ck, w_cv, b, l):
    nseg = l // NSA_CMP_STRIDE
    w = jnp.concatenate([w_ck, w_ck, w_cv, w_cv], axis=1).astype(F32)
    return pl.pallas_call(
        _compress_body,
        grid=(b, 2),
        in_specs=[pl.BlockSpec((l, NSA_KV_WIDTH), lambda i, j: (i, j)),
                  pl.BlockSpec((NSA_CMP_BLOCK, NSA_KV_WIDTH), lambda i, j: (0, j))],
        out_specs=pl.BlockSpec((1, nseg, NSA_KV_WIDTH), lambda i, j: (i, 0, j)),
        out_shape=jax.ShapeDtypeStruct((b, nseg, 2 * NSA_KV_WIDTH), F32),
        compiler_params=pltpu.CompilerParams(dimension_semantics=("arbitrary", "arbitrary")),
        name="nsa_compress",
    )(kv, w)


def _nt_dot(a, b):
    return lax.dot_general(a, b, (((1,), (1,)), ((), ())), preferred_element_type=F32)


def _importance(p_group_sum, ovt):
    return jnp.dot(p_group_sum.astype(BF16), ovt, preferred_element_type=F32)


def _vt_body(kv_ref, o_ref):
    for w, col in enumerate((3 * LANE, 5 * LANE)):
        o_ref[0, w, 0] = kv_ref[:, col:col + LANE].T.astype(BF16)


def nsa_value_transpose(kv, b, l):
    nkt = l // QB
    return pl.pallas_call(
        _vt_body,
        grid=(b, nkt),
        in_specs=[pl.BlockSpec((QB, 6 * NSA_KV_WIDTH), lambda bi, kt: (bi * nkt + kt, 0))],
        out_specs=pl.BlockSpec((1, 2, 1, LANE, QB), lambda bi, kt: (bi, 0, kt, 0, 0)),
        out_shape=jax.ShapeDtypeStruct((b, 2, nkt, LANE, QB), BF16),
        compiler_params=pltpu.CompilerParams(dimension_semantics=("arbitrary", "arbitrary")),
        name="nsa_value_transpose",
    )(kv)


def _nsa_prompt_t_body(q_ref, kv_ref, vt_ref, sm_ref, cmp_ref, cmpb_ref, toeb_ref, ov_ref, exp_ref, o_ref,
                       selm_ref, acc_ref, *, n_kt):
    i = pl.program_id(1)
    n_sel = 2 * n_kt
    krow = lax.broadcasted_iota(jnp.int32, (QB, LANE), 0)
    qlane = lax.broadcasted_iota(jnp.int32, (QB, LANE), 1)
    low = qlane < NSA_HEAD_DIM
    gates_t = jax.nn.sigmoid(sm_ref[...]).T
    qpos = i * QB + qlane
    heads = range(NSA_GROUP)
    all_heads = range(NSA_HEADS)

    def attend(qs, kcol, w, n_tiles, tile_of, use_sel, window):
        acc_ref[...] = jnp.zeros_like(acc_ref)

        def body(t, carry):
            m, l = carry
            kt = tile_of(t)
            d = i - kt
            k0 = pl.multiple_of(kt * QB, QB)
            k_t = kv_ref[pl.ds(k0, QB), kcol:kcol + LANE].astype(BF16)
            vt_t = vt_ref[0, w, kt]
            rel = d * QB + qlane - krow
            vis = rel >= 0
            if window:
                vis = vis & (rel < NSA_WINDOW)
            msk = [vis & (selm_ref[h, kt] > 0.5) if use_sel else vis for h in range(NSA_KV_HEADS)]
            bias = jnp.minimum(d, 2)
            s = [jnp.where(msk[hd // NSA_GROUP], _nt_dot(k_t, qs[hd]) + toeb_ref[bias, hd], NEG) for hd in all_heads]
            m_new = [jnp.maximum(m[hd], jnp.max(s[hd], axis=0, keepdims=True)) for hd in all_heads]
            alpha = [jnp.exp(m[hd] - m_new[hd]) for hd in all_heads]
            p = [jnp.exp(s[hd] - m_new[hd][0:1]) for hd in all_heads]
            l_new = [alpha[hd] * l[hd] + jnp.sum(p[hd], axis=0, keepdims=True) for hd in all_heads]
            pv = [jnp.dot(vt_t, p[hd].astype(BF16), preferred_element_type=F32) for hd in all_heads]
            for hd in all_heads:
                acc_ref[hd] = acc_ref[hd] * alpha[hd][0:1] + pv[hd]
            return tuple(m_new), tuple(l_new)

        m0 = tuple(jnp.full((8, LANE), NEG, F32) for _ in all_heads)
        l0 = tuple(jnp.zeros((8, LANE), F32) for _ in all_heads)
        _, l = lax.fori_loop(0, n_tiles, body, (m0, l0))
        return [acc_ref[hd] / l[hd][0:1] for hd in all_heads]

    qs, o_c = [], []
    for h in range(NSA_KV_HEADS):
        in_half = low if h == 0 else jnp.logical_not(low)
        qh = []
        for g in heads:
            hd = h * NSA_GROUP + g
            t = q_ref[:, (hd // 2) * LANE:(hd // 2) * LANE + LANE]
            if hd % 2 != h:
                t = pltpu.roll(t, NSA_HEAD_DIM, axis=1)
            qh.append((jnp.where(in_half, t, 0.0) * NSA_SCALE).astype(BF16))
        qs.extend(qh)

        kc = cmp_ref[0, :, 0:LANE].astype(BF16)
        vc_t = cmp_ref[0, :, LANE:2 * LANE].T.astype(BF16)
        mc = (qpos - (NSA_CMP_STRIDE * krow + NSA_CMP_BLOCK - 1)) >= 0
        mcf = mc.astype(F32)
        s = [jnp.where(mc, _nt_dot(kc, qh[g]) + cmpb_ref[0, h * NSA_GROUP + g], NEG) for g in heads]
        e = [jnp.exp(s[g] - jnp.max(s[g], axis=0, keepdims=True)) for g in heads]
        p = [e[g] / jnp.sum(e[g], axis=0, keepdims=True) * mcf for g in heads]
        o_c.extend(jnp.dot(vc_t, p[g].astype(BF16), preferred_element_type=F32) for g in heads)
        p_sum = p[0]
        for g in heads[1:]:
            p_sum = p_sum + p[g]
        imp_t = jnp.dot(ov_ref[...], p_sum.astype(BF16), preferred_element_type=F32)[0:n_sel]

        jrow = lax.broadcasted_iota(jnp.int32, (n_sel, LANE), 0)
        cur = (i * QB + lax.broadcasted_iota(jnp.int32, (n_sel, LANE), 1)) // NSA_SEL_BLOCK
        valid = jrow <= cur
        forced = (jrow == 0) | (jrow == cur) | (jrow == cur - 1)
        score = jnp.where(valid, jnp.where(forced, NSA_FORCE, imp_t), -1.0)
        cnt = jnp.zeros((n_sel, LANE), jnp.int32)
        for jp in range(n_sel):
            other = score[jp:jp + 1, :]
            beats = (other > score) | ((other == score) & (jrow > jp))
            cnt = cnt + beats.astype(jnp.int32)
        sel_t = ((cnt < NSA_TOPK) & valid).astype(F32)
        sel_t = jnp.concatenate([sel_t, jnp.zeros((LANE - n_sel, LANE), F32)], axis=0).astype(BF16)
        for kt in range(n_kt):
            selm_ref[h, kt] = jnp.dot(exp_ref[kt * QB:(kt + 1) * QB, :], sel_t, preferred_element_type=F32)

    o_s = attend(qs, 2 * LANE, 0, i + 1, lambda t: t, True, False)
    o_w = attend(qs, 4 * LANE, 1, jnp.minimum(i, NSA_WINDOW // QB) + 1, lambda t: i - t, False, True)

    for h in range(NSA_KV_HEADS):
        outs = []
        for g in heads:
            hd = h * NSA_GROUP + g
            c = SMALL_GATE + hd
            out_t = (gates_t[c:c + 1] * o_c[hd] + gates_t[c + NSA_HEADS:c + NSA_HEADS + 1] * o_s[hd]
                     + gates_t[c + 2 * NSA_HEADS:c + 2 * NSA_HEADS + 1] * o_w[hd])
            outs.append(out_t.T)
        for pr in range(NSA_GROUP // 2):
            a, b2 = outs[2 * pr], outs[2 * pr + 1]
            if h == 0:
                b2 = pltpu.roll(b2, NSA_HEAD_DIM, axis=1)
            else:
                a = pltpu.roll(a, NSA_HEAD_DIM, axis=1)
            c0 = (h * NSA_GROUP + 2 * pr) * NSA_HEAD_DIM
            o_ref[:, c0:c0 + LANE] = jnp.where(low, a, b2)


def nsa_prompt_attention(q, kv, small, w_ck, w_cv, table, b, l):
    nqb = l // QB
    n_sel = l // NSA_SEL_BLOCK
    nseg = l // NSA_CMP_STRIDE
    assert nseg == LANE and n_sel <= LANE and l % QB == 0
    cmp = nsa_compress_prompt(kv, w_ck, w_cv, b, l)
    vt = nsa_value_transpose(kv, b, l)
    cmp_bias = bias_tiles(table, nqb, QB, -(NSA_CMP_BLOCK - 1), -1, qmul=-NSA_CMP_STRIDE)
    toe_bias = bias_tiles(table, 3, QB, 0, -1, qmul=-1)
    jj = np.arange(LANE)[None, :]
    cc = np.arange(LANE)[:, None]
    lo = np.maximum(jj * NSA_SEL_BLOCK, cc * NSA_CMP_STRIDE)
    hi = np.minimum(jj * NSA_SEL_BLOCK + NSA_SEL_BLOCK, cc * NSA_CMP_STRIDE + NSA_CMP_BLOCK)
    ovt = np.where((jj < n_sel) & (cc < nseg - 1), np.maximum(hi - lo, 0) / NSA_CMP_STRIDE, 0.0)
    expand = (np.arange(l)[None, :] // NSA_SEL_BLOCK == np.arange(LANE)[:, None]).astype(np.float32)
    return pl.pallas_call(
        functools.partial(_nsa_prompt_t_body, n_kt=nqb),
        grid=(b, nqb),
        in_specs=[pl.BlockSpec((QB, NSA_HEADS * NSA_HEAD_DIM), lambda bi, i: (bi * nqb + i, 0)),
                  pl.BlockSpec((l, 6 * NSA_KV_WIDTH), lambda bi, i: (bi, 0)),
                  pl.BlockSpec((1, 2, nqb, LANE, QB), lambda bi, i: (bi, 0, 0, 0, 0)),
                  pl.BlockSpec((QB, LANE), lambda bi, i: (bi * nqb + i, 0)),
                  pl.BlockSpec((1, nseg, 2 * NSA_KV_WIDTH), lambda bi, i: (bi, 0, 0)),
                  pl.BlockSpec((1, NSA_HEADS, QB, LANE), lambda bi, i: (i, 0, 0, 0)),
                  pl.BlockSpec((3, NSA_HEADS, QB, LANE), lambda bi, i: (0, 0, 0, 0)),
                  pl.BlockSpec((LANE, LANE), lambda bi, i: (0, 0)),
                  pl.BlockSpec((l, LANE), lambda bi, i: (0, 0))],
        out_specs=pl.BlockSpec((QB, NSA_HEADS * NSA_HEAD_DIM), lambda bi, i: (bi * nqb + i, 0)),
        out_shape=jax.ShapeDtypeStruct((b * l, NSA_HEADS * NSA_HEAD_DIM), F32),
        scratch_shapes=[pltpu.VMEM((NSA_KV_HEADS, nqb, QB, QB), F32), pltpu.VMEM((NSA_HEADS, LANE, QB), F32)],
        compiler_params=pltpu.CompilerParams(dimension_semantics=("arbitrary", "arbitrary"),
                                             vmem_limit_bytes=VMEM_LIMIT_BYTES),
        name="nsa_prompt",
    )(q, kv, vt, small, cmp, cmp_bias, toe_bias, jnp.asarray(ovt.T, BF16), jnp.asarray(expand.T, BF16))


PAGES_PER_STEP = 16
NSEG_PAGE = PAGE_SIZE // NSA_CMP_STRIDE
SELF_RANK = 2


def _bf(x):
    return x.astype(BF16).astype(F32)


def _bias_rows(table, n_tiles, step, off, kmul):
    t = bias_tiles(table, n_tiles, step, off, kmul, qmul=0, rows=8)[:, :, 0, :]
    return t.transpose(1, 0, 2).reshape(NSA_KV_HEADS, NSA_GROUP, n_tiles * LANE)


def _nsa_sample_cmp_body(pt_ref, k_hbm, v_hbm, qp_ref, wk_ref, wv_ref, bias_ref, ovt_ref, oc_ref, idx_ref,
                         kbuf, vbuf, ksem, vsem, fk, sk, fv, sv, *, base, n_pages):
    slot = _paged_prefetch((k_hbm, v_hbm), (kbuf, vbuf), (ksem, vsem), pt_ref, base, n_pages)
    nseg = n_pages * NSEG_PAGE

    def compress_page(p, carry):
        for buf, w_ref, f_ref, s_ref in ((kbuf, wk_ref, fk, sk), (vbuf, wv_ref, fv, sv)):
            f = jnp.zeros((NSEG_PAGE, LANE), F32)
            s = jnp.zeros((NSEG_PAGE, LANE), F32)
            for r in range(NSA_CMP_STRIDE):
                rows = buf[slot, p, pl.ds(r, NSEG_PAGE, stride=NSA_CMP_STRIDE), :]
                f = f + rows * w_ref[r:r + 1, :]
                s = s + rows * w_ref[NSA_CMP_STRIDE + r:NSA_CMP_STRIDE + r + 1, :]
            row0 = pl.multiple_of(p * NSEG_PAGE, NSEG_PAGE)
            f_ref[pl.ds(row0, NSEG_PAGE), :] = f
            s_ref[pl.ds(row0, NSEG_PAGE), :] = s
        return carry

    lax.fori_loop(0, n_pages, compress_page, 0)

    def attend_and_select():
        rowi = lax.broadcasted_iota(jnp.int32, (nseg, LANE), 0)
        kc = jnp.where(rowi < nseg - 1, fk[...] + pltpu.roll(sk[...], nseg - 1, axis=0), 0.0).astype(BF16)
        vc = jnp.where(rowi < nseg - 1, fv[...] + pltpu.roll(sv[...], nseg - 1, axis=0), 0.0).astype(BF16)
        mc = lax.broadcasted_iota(jnp.int32, (NSA_GROUP, nseg), 1) < nseg - 1
        lane = lax.broadcasted_iota(jnp.int32, (8, LANE), 1)
        imps = []
        for h in range(NSA_KV_HEADS):
            qp = (qp_ref[0, h] * NSA_SCALE).astype(BF16)
            s = jnp.where(mc, _nt_dot(qp, kc) + bias_ref[h], NEG)
            e = jnp.exp(s - jnp.max(s, axis=-1, keepdims=True))
            p = e / jnp.sum(e, axis=-1, keepdims=True) * mc.astype(F32)
            pb = p.astype(BF16)
            oc_ref[0, h] = jnp.dot(pb, vc, preferred_element_type=F32)
            imps.append(_importance(jnp.sum(p, axis=0, keepdims=True), ovt_ref[...]))
        imp = jnp.concatenate(imps + [jnp.zeros((8 - NSA_KV_HEADS, LANE), F32)], axis=0)
        forced = (lane == 0) | (lane == LANE - 1)
        score = jnp.where(forced, NSA_FORCE, imp)
        cnt = (score < NSA_FORCE).astype(F32)
        for jp in range(LANE):
            col = score[:, jp:jp + 1]
            cnt = cnt + ((col > score) | ((col == score) & (lane > jp))).astype(F32)
        lanef = lane.astype(F32)
        idx = jnp.where(lane == SELF_RANK, float(LANE), 0.0)
        for k in range(NSA_TOPK):
            if k != SELF_RANK:
                v = jnp.sum(jnp.where(cnt == k, lanef, 0.0), axis=-1, keepdims=True)
                idx = jnp.where(lane == k, v, idx)
        idx_ref[0] = idx.astype(jnp.int32)

    attend_and_select()


def nsa_sample_cmp(qp, k_pool, v_pool, e, page_table, w_ck, w_cv, table):
    db, n_pages = page_table.shape
    n_pool = k_pool.shape[1]
    p0 = n_pages * PAGE_SIZE
    nseg = n_pages * NSEG_PAGE
    npg = PAGES_PER_STEP
    assert p0 // NSA_SEL_BLOCK == LANE and n_pages % npg == 0
    kp = k_pool.reshape(-1, PAGE_SIZE, NSA_KV_WIDTH)
    vp = v_pool.reshape(-1, PAGE_SIZE, NSA_KV_WIDTH)
    bias = _bias_rows(table, nseg // LANE, -LANE * NSA_CMP_STRIDE, p0 - (NSA_CMP_BLOCK - 1), NSA_CMP_STRIDE)
    jj = np.arange(LANE)[None, :]
    cc = np.arange(nseg)[:, None]
    lo = np.maximum(jj * NSA_SEL_BLOCK, cc * NSA_CMP_STRIDE)
    hi = np.minimum(jj * NSA_SEL_BLOCK + NSA_SEL_BLOCK, cc * NSA_CMP_STRIDE + NSA_CMP_BLOCK)
    ovt = np.where(cc < nseg - 1, np.maximum(hi - lo, 0) / NSA_CMP_STRIDE, 0.0)

    const = lambda shape: pl.BlockSpec(shape, lambda b, pt: (0,) * len(shape))
    hbm = pl.BlockSpec(memory_space=pl.ANY)
    page_buf = pltpu.VMEM((2, n_pages, PAGE_SIZE, NSA_KV_WIDTH), F32)
    grid_spec = pltpu.PrefetchScalarGridSpec(
        num_scalar_prefetch=1,
        grid=(db,),
        in_specs=[hbm, hbm, pl.BlockSpec((1, NSA_KV_HEADS, NSA_GROUP, LANE), lambda b, pt: (b, 0, 0, 0)),
                  const((NSA_CMP_BLOCK, NSA_KV_WIDTH)), const((NSA_CMP_BLOCK, NSA_KV_WIDTH)),
                  const((NSA_KV_HEADS, NSA_GROUP, nseg)), const((nseg, LANE))],
        out_specs=[pl.BlockSpec((1, NSA_KV_HEADS, NSA_GROUP, LANE), lambda b, pt: (b, 0, 0, 0)),
                   pl.BlockSpec((1, 8, LANE), lambda b, pt: (b, 0, 0))],
        scratch_shapes=[page_buf, page_buf, pltpu.SemaphoreType.DMA((2,)), pltpu.SemaphoreType.DMA((2,))]
        + [pltpu.VMEM((nseg, LANE), F32)] * 4)
    tile2 = lambda w: jnp.concatenate([w, w], axis=1).astype(F32)
    return pl.pallas_call(
        functools.partial(_nsa_sample_cmp_body, base=e * n_pool, n_pages=n_pages),
        grid_spec=grid_spec,
        out_shape=[jax.ShapeDtypeStruct((db, NSA_KV_HEADS, NSA_GROUP, LANE), F32),
                   jax.ShapeDtypeStruct((db, 8, LANE), jnp.int32)],
        compiler_params=pltpu.CompilerParams(dimension_semantics=("arbitrary",), vmem_limit_bytes=VMEM_LIMIT_BYTES),
        name="nsa_sample_cmp",
    )(page_table, kp, vp, qp, tile2(w_ck), tile2(w_cv), bias, jnp.asarray(ovt, BF16))


def _selected_block_copies(pt_ref, idx_ref, pools, bufs, sems, b, slot, base):
    half = NSA_SEL_BLOCK
    bpp = PAGE_SIZE // NSA_SEL_BLOCK
    copies = []
    for ent in range(NSA_KV_HEADS * NSA_TOPK):
        if ent % NSA_TOPK == SELF_RANK:
            continue
        j = jnp.minimum(idx_ref[b, ent], LANE - 1)
        page = pt_ref[b, j // bpp] + base
        rows = pl.ds(pl.multiple_of((j % bpp) * half, half), half)
        for pool_ref, buf_ref, sem_ref in zip(pools, bufs, sems):
            copies.append(pltpu.make_async_copy(pool_ref.at[page, rows], buf_ref.at[slot, ent], sem_ref.at[slot]))
    return copies


def _nsa_sample_attn_body(pt_ref, idx_ref, k_hbm, v_hbm, qp_ref, new_ref, kwin_ref, vwin_ref, oc_ref, gate_ref,
                          bnear_ref, bwin_ref, bends_ref, o_ref, kwo_ref, vwo_ref, kb, vb, ksem, vsem, *, nbuf, base):
    b = pl.program_id(0)
    cur = b % 2
    copies = functools.partial(_selected_block_copies, pt_ref, idx_ref, (k_hbm, v_hbm), (kb, vb), (ksem, vsem))

    @pl.when(b == 0)
    def _():
        for c in copies(b, cur, base):
            c.start()

    @pl.when(b + 1 < pl.num_programs(0))
    def _():
        for c in copies(b + 1, 1 - cur, base):
            c.start()

    for c in copies(b, cur, base):
        c.wait()
    new = new_ref[0]
    ks_new, vs_new, kw_new, vw_new = (new[:, c * LANE:(c + 1) * LANE] for c in (2, 3, 4, 5))
    kwin = kwin_ref[0]
    vwin = vwin_ref[0]
    wlane = lax.broadcasted_iota(jnp.int32, (NSA_GROUP, nbuf), 1)
    half = NSA_SEL_BLOCK
    for h in range(NSA_KV_HEADS):
        qp = (qp_ref[0, h] * NSA_SCALE).astype(BF16)
        qpf = qp.astype(F32)
        b_self = bends_ref[h][:, 0:1]
        b_far = bends_ref[h][:, 1:2]
        near = bnear_ref[h]
        scores, slots = [], []
        for k in range(NSA_TOPK):
            if k == SELF_RANK:
                continue
            j = idx_ref[b, h * NSA_TOPK + k]
            bias = jnp.where(j == LANE - 1, near[:, half:], jnp.where(j == LANE - 2, near[:, :half], b_far))
            scores.append(_nt_dot(qp, kb[cur, h * NSA_TOPK + k].astype(BF16)) + bias)
            slots.append(h * NSA_TOPK + k)
        s_self = jnp.sum(qpf * _bf(ks_new), axis=-1, keepdims=True) + b_self
        m = s_self
        for s in scores:
            m = jnp.maximum(m, jnp.max(s, axis=-1, keepdims=True))
        p_self = jnp.exp(s_self - m)
        l = p_self
        acc = _bf(p_self) * _bf(vs_new)
        for s, ent in zip(scores, slots):
            p = jnp.exp(s - m)
            l = l + jnp.sum(p, axis=-1, keepdims=True)
            acc = acc + jnp.dot(p.astype(BF16), vb[cur, ent].astype(BF16), preferred_element_type=F32)
        o_s = acc / l
        s = jnp.where(wlane >= 1, _nt_dot(qp, kwin.astype(BF16)) + bwin_ref[h], NEG)
        s_self = jnp.sum(qpf * _bf(kw_new), axis=-1, keepdims=True) + b_self
        m = jnp.maximum(jnp.max(s, axis=-1, keepdims=True), s_self)
        p = jnp.exp(s - m)
        p_self = jnp.exp(s_self - m)
        l = jnp.sum(p, axis=-1, keepdims=True) + p_self
        o_w = (jnp.dot(p.astype(BF16), vwin.astype(BF16), preferred_element_type=F32)
               + _bf(p_self) * _bf(vw_new)) / l
        gates = jax.nn.sigmoid(gate_ref[0, h])
        o_h = gates[:, 0:1] * oc_ref[0, h] + gates[:, 1:2] * o_s + gates[:, 2:3] * o_w
        o_rot = pltpu.roll(o_h, NSA_HEAD_DIM, axis=1)
        low = lax.broadcasted_iota(jnp.int32, (1, LANE), 1) < NSA_HEAD_DIM
        for pr in range(NSA_GROUP // 2):
            a = (o_h if h == 0 else o_rot)[2 * pr:2 * pr + 1]
            b2 = (o_rot if h == 0 else o_h)[2 * pr + 1:2 * pr + 2]
            c0 = (h * NSA_GROUP + 2 * pr) * NSA_HEAD_DIM
            o_ref[0, :, c0:c0 + LANE] = jnp.where(low, a, b2)
    rowi = lax.broadcasted_iota(jnp.int32, (nbuf, LANE), 0)
    kwo_ref[0] = jnp.where(rowi == nbuf - 1, kw_new, pltpu.roll(kwin, nbuf - 1, axis=0))
    vwo_ref[0] = jnp.where(rowi == nbuf - 1, vw_new, pltpu.roll(vwin, nbuf - 1, axis=0))


def nsa_sample_attn(qp, kv_new, idx, o_c, gates_raw, k_pool, v_pool, e, k_win, v_win, page_table, table):
    db, n_pages = page_table.shape
    n_pool = k_pool.shape[1]
    p0 = n_pages * PAGE_SIZE
    nbuf = k_win.shape[1]
    nb = NSA_KV_HEADS * NSA_TOPK
    bpp = PAGE_SIZE // NSA_SEL_BLOCK
    assert nbuf == NSA_WINDOW and nbuf % LANE == 0 and p0 >= nbuf
    kp = k_pool.reshape(-1, PAGE_SIZE, NSA_KV_WIDTH)
    vp = v_pool.reshape(-1, PAGE_SIZE, NSA_KV_WIDTH)
    bnear = _bias_rows(table, 1, 0, 2 * NSA_SEL_BLOCK, 1)
    bwin = _bias_rows(table, nbuf // LANE, -LANE, nbuf, 1)
    tb = table.astype(F32).reshape(REL_BUCKETS, NSA_KV_HEADS, NSA_GROUP)
    bends = jnp.pad(jnp.stack([tb[0], tb[REL_BUCKETS - 1]], axis=-1), ((0, 0), (0, 0), (0, LANE - 2)))

    per_b = lambda shape: pl.BlockSpec((1,) + shape, lambda b, pt, ix: (b,) + (0,) * len(shape))
    const = lambda shape: pl.BlockSpec(shape, lambda b, pt, ix: (0,) * len(shape))
    hbm = pl.BlockSpec(memory_space=pl.ANY)
    head_shape = (NSA_KV_HEADS, NSA_GROUP, LANE)
    blk_buf = pltpu.VMEM((2, nb, NSA_SEL_BLOCK, NSA_KV_WIDTH), F32)
    grid_spec = pltpu.PrefetchScalarGridSpec(
        num_scalar_prefetch=2,
        grid=(db,),
        in_specs=[hbm, hbm,
                  per_b(head_shape), per_b((1, 6 * NSA_KV_WIDTH)), per_b((nbuf, NSA_KV_WIDTH)), per_b((nbuf, NSA_KV_WIDTH)),
                  per_b(head_shape), per_b(head_shape), const(head_shape),
                  const((NSA_KV_HEADS, NSA_GROUP, nbuf)), const(head_shape)],
        out_specs=[per_b((1, NSA_HEADS * NSA_HEAD_DIM)), per_b((nbuf, NSA_KV_WIDTH)), per_b((nbuf, NSA_KV_WIDTH))],
        scratch_shapes=[blk_buf, blk_buf, pltpu.SemaphoreType.DMA((2,)), pltpu.SemaphoreType.DMA((2,))])
    return pl.pallas_call(
        functools.partial(_nsa_sample_attn_body, nbuf=nbuf, base=e * n_pool),
        grid_spec=grid_spec,
        out_shape=[jax.ShapeDtypeStruct((db, 1, NSA_HEADS * NSA_HEAD_DIM), F32),
                   jax.ShapeDtypeStruct((db, nbuf, NSA_KV_WIDTH), F32),
                   jax.ShapeDtypeStruct((db, nbuf, NSA_KV_WIDTH), F32)],
        compiler_params=pltpu.CompilerParams(dimension_semantics=("arbitrary",)),
        name="nsa_sample_attn",
    )(page_table, idx, kp, vp, qp, kv_new.reshape(db, 1, -1),
      k_win.reshape(db, nbuf, NSA_KV_WIDTH), v_win.reshape(db, nbuf, NSA_KV_WIDTH), o_c, gates_raw,
      bnear, bwin, bends)


def nsa_sample_mixer(q, kv, small, e, k_cmp_pool, v_cmp_pool, k_slc_pool, v_slc_pool, k_win, v_win,
                     page_table, w_ck, w_cv, table):
    db = q.shape[0]
    qh = q.reshape(db, NSA_KV_HEADS, NSA_GROUP, NSA_HEAD_DIM)
    zeros = jnp.zeros_like(qh[:, 0])
    qp = jnp.stack([jnp.concatenate([qh[:, 0], zeros], axis=-1), jnp.concatenate([zeros, qh[:, 1]], axis=-1)], axis=1)
    o_c, idx = nsa_sample_cmp(qp, k_cmp_pool, v_cmp_pool, e, page_table, w_ck, w_cv, table)
    idx = idx[:, :NSA_KV_HEADS, :NSA_TOPK].reshape(db, NSA_KV_HEADS * NSA_TOPK)
    g = small[:, SMALL_GATE:SMALL_GATE + 3 * NSA_HEADS].reshape(db, 3, NSA_KV_HEADS, NSA_GROUP).transpose(0, 2, 3, 1)
    g = jnp.pad(g, ((0, 0), (0, 0), (0, 0), (0, LANE - 3)))
    y, kw_n, vw_n = nsa_sample_attn(qp, kv, idx, o_c, g, k_slc_pool, v_slc_pool, e, k_win, v_win, page_table, table)
    shape = (db, -1, NSA_KV_HEADS, NSA_HEAD_DIM)
    return y.reshape(db, NSA_HEADS * NSA_HEAD_DIM), kw_n.reshape(shape), vw_n.reshape(shape)


def _bmm_body(a_ref, b_ref, o_ref):
    o_ref[0] = jnp.dot(a_ref[0].astype(BF16), b_ref[0].astype(BF16), preferred_element_type=F32)


def batched_matmul(a, b):
    hh, m, k = a.shape
    n = b.shape[2]
    return pl.pallas_call(
        _bmm_body,
        grid=(hh,),
        in_specs=[pl.BlockSpec((1, m, k), lambda i: (i, 0, 0)), pl.BlockSpec((1, k, n), lambda i: (i, 0, 0))],
        out_specs=pl.BlockSpec((1, m, n), lambda i: (i, 0, 0)),
        out_shape=jax.ShapeDtypeStruct((hh, m, n), F32),
        compiler_params=pltpu.CompilerParams(dimension_semantics=("arbitrary",)),
        name="batched_matmul",
    )(a, b)


def _page_copy(pool_ref, buf_ref, sem_ref, page, slot, p):
    return pltpu.make_async_copy(pool_ref.at[page], buf_ref.at[slot, p], sem_ref.at[slot])


def _start_pages(pools, bufs, sems, pt_ref, b, slot, base, n_pages):
    def issue(p, carry):
        page = pt_ref[b, p] + base
        for pool_ref, buf_ref, sem_ref in zip(pools, bufs, sems):
            _page_copy(pool_ref, buf_ref, sem_ref, page, slot, p).start()
        return carry
    lax.fori_loop(0, n_pages, issue, 0)


def _wait_pages(pools, bufs, sems, slot, n_pages):
    def wait(p, carry):
        for pool_ref, buf_ref, sem_ref in zip(pools, bufs, sems):
            _page_copy(pool_ref, buf_ref, sem_ref, 0, slot, p).wait()
        return carry
    lax.fori_loop(0, n_pages, wait, 0)


def _paged_prefetch(pools, bufs, sems, pt_ref, base, n_pages):
    b = pl.program_id(0)
    slot = b % 2

    @pl.when(b == 0)
    def _():
        _start_pages(pools, bufs, sems, pt_ref, b, slot, base, n_pages)

    @pl.when(b + 1 < pl.num_programs(0))
    def _():
        _start_pages(pools, bufs, sems, pt_ref, b + 1, 1 - slot, base, n_pages)

    _wait_pages(pools, bufs, sems, slot, n_pages)
    return slot


def _mla_sample_body(pt_ref, lat_hbm, kr_hbm, qlat_ref, qrope_ref, cnew_ref, krnew_ref, o_ref,
                     cbuf, rbuf, csem, rsem, *, base, n_pages):
    slot = _paged_prefetch((lat_hbm, kr_hbm), (cbuf, rbuf), (csem, rsem), pt_ref, base, n_pages)
    npg = PAGES_PER_STEP
    ql = qlat_ref[0].astype(BF16)
    qr = qrope_ref[0].astype(BF16)
    chunks, scores = [], []
    for ch in range(n_pages // npg):
        c = cbuf[slot, pl.ds(ch * npg, npg)].reshape(npg * PAGE_SIZE, MLA_KV_RANK).astype(BF16)
        r = rbuf[slot, pl.ds(ch * npg, npg)].reshape(npg * PAGE_SIZE, MLA_ROPE).astype(BF16)
        scores.append((_nt_dot(ql, c) + _nt_dot(qr, r)) * MLA_SCALE)
    cn = _bf(cnew_ref[0])
    kn = _bf(krnew_ref[0])
    s_self = (jnp.sum(ql.astype(F32) * cn, axis=-1, keepdims=True)
              + jnp.sum(qr.astype(F32) * kn, axis=-1, keepdims=True)) * MLA_SCALE
    m = s_self
    for s in scores:
        m = jnp.maximum(m, jnp.max(s, axis=-1, keepdims=True))
    p_self = jnp.exp(s_self - m)
    l = p_self
    acc = _bf(p_self) * cn
    for ch, s in enumerate(scores):
        p = jnp.exp(s - m)
        l = l + jnp.sum(p, axis=-1, keepdims=True)
        c = cbuf[slot, pl.ds(ch * npg, npg)].reshape(npg * PAGE_SIZE, MLA_KV_RANK).astype(BF16)
        acc = acc + jnp.dot(p.astype(BF16), c, preferred_element_type=F32)
    o_ref[0] = acc / l


def mla_sample_attention(q_lat, q_rope, c_new, kr_new, lat_pool, kr_pool, o, page_table):
    db, n_pages = page_table.shape
    n_pool = lat_pool.shape[1]
    npg = PAGES_PER_STEP
    assert n_pages % npg == 0
    cp = lat_pool.reshape(-1, PAGE_SIZE, MLA_KV_RANK)
    rp = kr_pool.reshape(-1, PAGE_SIZE, MLA_ROPE)

    per_b = lambda shape: pl.BlockSpec((1,) + shape, lambda b, pt: (b,) + (0,) * len(shape))
    hbm = pl.BlockSpec(memory_space=pl.ANY)
    grid_spec = pltpu.PrefetchScalarGridSpec(
        num_scalar_prefetch=1,
        grid=(db,),
        in_specs=[hbm, hbm, per_b((MLA_HEADS, MLA_KV_RANK)), per_b((MLA_HEADS, MLA_ROPE)),
                  per_b((1, MLA_KV_RANK)), per_b((1, MLA_ROPE))],
        out_specs=per_b((MLA_HEADS, MLA_KV_RANK)),
        scratch_shapes=[pltpu.VMEM((2, n_pages, PAGE_SIZE, MLA_KV_RANK), F32),
                        pltpu.VMEM((2, n_pages, PAGE_SIZE, MLA_ROPE), F32),
                        pltpu.SemaphoreType.DMA((2,)), pltpu.SemaphoreType.DMA((2,))])
    return pl.pallas_call(
        functools.partial(_mla_sample_body, base=o * n_pool, n_pages=n_pages),
        grid_spec=grid_spec,
        out_shape=jax.ShapeDtypeStruct((db, MLA_HEADS, MLA_KV_RANK), F32),
        compiler_params=pltpu.CompilerParams(dimension_semantics=("arbitrary",), vmem_limit_bytes=VMEM_LIMIT_BYTES),
        name="mla_sample",
    )(page_table, cp, rp, q_lat, q_rope, c_new.reshape(db, 1, -1), kr_new.reshape(db, 1, -1))


def mla_sample_mixer(q_nope, q_rope, c, k_rope, lat_pool, kr_pool, o_idx, page_table, w_kvb):
    db = c.shape[0]
    w = w_kvb.reshape(MLA_KV_RANK, MLA_HEADS, MLA_NOPE + MLA_V).transpose(1, 0, 2)
    q_lat = batched_matmul(q_nope.transpose(1, 0, 2), w[..., :MLA_NOPE].transpose(0, 2, 1)).transpose(1, 0, 2)
    o_lat = mla_sample_attention(q_lat, q_rope, c, k_rope, lat_pool, kr_pool, o_idx, page_table)
    out = batched_matmul(o_lat.transpose(1, 0, 2), w[..., MLA_NOPE:])
    return out.transpose(1, 0, 2).reshape(db, MLA_HEADS * MLA_V)


OD_GROUPS = (GLA_HEADS * GLA_DK, GLA_HEADS * GLA_DK, GLA_HEADS * GLA_DV, GLA_HEADS * GLA_DV, MLA_Q_RANK, MLA_KV_RANK, LANE)
MLA_PREP_ROWS = 128
MLA_TQ = 256
MLA_HEAD_GROUP = 8


def odd_weight(w_in):
    gq, gk, gv, g1, r, qa, kva, kr = split_cols(w_in, OD_SPLITS)
    z = lambda n: jnp.zeros((w_in.shape[0], n), w_in.dtype)
    small = jnp.concatenate([g1, z(MLA_NOPE - GLA_GATE_RANK), kr, z(LANE - MLA_NOPE - MLA_ROPE)], axis=1)
    return jnp.concatenate([gq, gk, gv, r, qa, kva, small], axis=1)


def _head_blocks(w, widths, keep):
    k = w.shape[0]
    w = w.reshape(k, MLA_HEADS, sum(widths))[:, :, keep[0]:keep[1]]
    return jnp.pad(w, ((0, 0), (0, 0), (0, LANE - (keep[1] - keep[0])))).reshape(k, MLA_HEADS * LANE)


def rope_tables(pos):
    half = MLA_ROPE // 2
    freqs = ROPE_THETA ** (-jnp.arange(half, dtype=F32) / half)
    ang = pos.astype(F32)[:, None] * freqs
    cos, sin = jnp.cos(ang), jnp.sin(ang)
    n = pos.shape[0]
    z = lambda w: jnp.zeros((n, w), F32)
    tail = LANE - MLA_NOPE - MLA_ROPE
    c = jnp.concatenate([jnp.ones((n, MLA_NOPE), F32), cos, cos, z(tail)], axis=1)
    s1 = jnp.concatenate([z(MLA_NOPE), -sin, z(half), z(tail)], axis=1)
    s2 = jnp.concatenate([z(MLA_NOPE), z(half), sin, z(tail)], axis=1)
    return c, s1, s2


def _rope_block(x, c, s1, s2):
    half = MLA_ROPE // 2
    return x * c + pltpu.roll(x, LANE - half, axis=1) * s1 + pltpu.roll(x, half, axis=1) * s2


def _mla_prep_body(qa_ref, kva_ref, sm_ref, qg_ref, kg_ref, wq_ref, wk_ref, wv_ref, c_ref, s1_ref, s2_ref,
                   q_out, c_out, k_out, v_out, kr_out):
    norm = lambda x, g: x * lax.rsqrt(jnp.mean(x * x, axis=-1, keepdims=True) + EPS) * g
    c, s1, s2 = c_ref[...], s1_ref[...], s2_ref[...]
    lane = lax.broadcasted_iota(jnp.int32, c.shape, 1)
    q = jnp.dot(norm(qa_ref[...], qg_ref[...]).astype(BF16), wq_ref[...], preferred_element_type=F32)
    lat = norm(kva_ref[...], kg_ref[...])
    c_out[...] = lat
    latb = lat.astype(BF16)
    kn = jnp.dot(latb, wk_ref[...], preferred_element_type=F32)
    v_out[...] = jnp.dot(latb, wv_ref[...], preferred_element_type=F32).astype(BF16)
    kr = jnp.where(lane >= MLA_NOPE, _rope_block(sm_ref[...], c, s1, s2), 0.0)
    kr_out[...] = kr
    for h in range(MLA_HEADS):
        cols = slice(h * LANE, (h + 1) * LANE)
        q_out[:, cols] = _rope_block(q[:, cols], c, s1, s2)
        k_out[:, cols] = (kn[:, cols] + kr).astype(BF16)


def mla_prep(qa, kva, small, q_norm, w_qb, kv_norm, w_kvb, tables, pos_block):
    m = qa.shape[0]
    tm = MLA_PREP_ROWS
    wq = _head_blocks(w_qb, (MLA_NOPE, MLA_ROPE), (0, MLA_NOPE + MLA_ROPE)).astype(BF16)
    wk = _head_blocks(w_kvb, (MLA_NOPE, MLA_V), (0, MLA_NOPE)).astype(BF16)
    wv = _head_blocks(w_kvb, (MLA_NOPE, MLA_V), (MLA_NOPE, MLA_NOPE + MLA_V)).astype(BF16)
    wide = MLA_HEADS * LANE
    rows = lambda w: pl.BlockSpec((tm, w), lambda i: (i, 0))
    const = lambda a: pl.BlockSpec(a.shape, lambda i: (0, 0))
    tab = pl.BlockSpec((tm, LANE), lambda i: (pos_block(i), 0))
    qg = q_norm.reshape(1, -1).astype(F32)
    kg = kv_norm.reshape(1, -1).astype(F32)
    return pl.pallas_call(
        _mla_prep_body,
        grid=(m // tm,),
        in_specs=[rows(MLA_Q_RANK), rows(MLA_KV_RANK), rows(LANE), const(qg), const(kg), const(wq), const(wk), const(wv),
                  tab, tab, tab],
        out_specs=[rows(wide), rows(MLA_KV_RANK), rows(wide), rows(wide), rows(LANE)],
        out_shape=[jax.ShapeDtypeStruct((m, wide), F32), jax.ShapeDtypeStruct((m, MLA_KV_RANK), F32),
                   jax.ShapeDtypeStruct((m, wide), BF16), jax.ShapeDtypeStruct((m, wide), BF16),
                   jax.ShapeDtypeStruct((m, LANE), F32)],
        compiler_params=pltpu.CompilerParams(dimension_semantics=("arbitrary",), vmem_limit_bytes=VMEM_LIMIT_BYTES),
        name="mla_prep",
    )(qa, kva, small, qg, kg, wq, wk, wv, *tables)


def _mla_prompt_body(q_ref, k_ref, v_ref, o_ref):
    tq = q_ref.shape[0]
    qi = pl.program_id(1)
    row = lax.broadcasted_iota(jnp.int32, (tq, tq), 0)
    col = lax.broadcasted_iota(jnp.int32, (tq, tq), 1)
    low = lax.broadcasted_iota(jnp.int32, (tq, LANE), 1) < MLA_V
    outs = []
    for h0 in range(0, MLA_HEADS, MLA_HEAD_GROUP):
        heads = range(h0, h0 + MLA_HEAD_GROUP)
        hcols = [slice(h * LANE, (h + 1) * LANE) for h in heads]
        qs = [(q_ref[:, c] * MLA_SCALE).astype(BF16) for c in hcols]

        def body(kt, carry):
            k0 = pl.multiple_of(kt * tq, tq)
            vis = (kt - qi) * tq + col <= row
            ss = [jnp.where(vis, _nt_dot(q, k_ref[pl.ds(k0, tq), c]), NEG) for q, c in zip(qs, hcols)]
            m_new = [jnp.maximum(st[0], jnp.max(s, axis=-1, keepdims=True)) for st, s in zip(carry, ss)]
            alpha = [jnp.exp(st[0] - mn) for st, mn in zip(carry, m_new)]
            ps = [jnp.exp(s - mn) for s, mn in zip(ss, m_new)]
            l_new = [a * st[1] + jnp.sum(p, axis=-1, keepdims=True) for a, st, p in zip(alpha, carry, ps)]
            pv = [jnp.dot(p.astype(BF16), v_ref[pl.ds(k0, tq), c], preferred_element_type=F32)
                  for p, c in zip(ps, hcols)]
            return tuple((mn, ln, a * st[2] + x) for mn, ln, a, st, x in zip(m_new, l_new, alpha, carry, pv))

        init = tuple((jnp.full((tq, 1), NEG, F32), jnp.zeros((tq, 1), F32), jnp.zeros((tq, LANE), F32))
                     for _ in heads)
        final = lax.fori_loop(0, qi + 1, body, init)
        outs.extend(acc / l for _, l, acc in final)
    for pr in range(MLA_HEADS // 2):
        o_ref[:, pr * LANE:(pr + 1) * LANE] = jnp.where(low, outs[2 * pr], pltpu.roll(outs[2 * pr + 1], MLA_V, axis=1))


def mla_prompt_attention(q_rot, k_full, v_pad, b, l):
    tq = MLA_TQ
    nq = l // tq
    wide = MLA_HEADS * LANE
    return pl.pallas_call(
        _mla_prompt_body,
        grid=(b, nq),
        in_specs=[pl.BlockSpec((tq, wide), lambda bi, i: (bi * nq + i, 0)),
                  pl.BlockSpec((l, wide), lambda bi, i: (bi, 0)),
                  pl.BlockSpec((l, wide), lambda bi, i: (bi, 0))],
        out_specs=pl.BlockSpec((tq, MLA_HEADS * MLA_V), lambda bi, i: (bi * nq + i, 0)),
        out_shape=jax.ShapeDtypeStruct((b * l, MLA_HEADS * MLA_V), F32),
        compiler_params=pltpu.CompilerParams(dimension_semantics=("arbitrary", "arbitrary"),
                                             vmem_limit_bytes=VMEM_LIMIT_BYTES),
        name="mla_prompt",
    )(q_rot, k_full, v_pad)


SSD_PAIRS = SSD_HEADS // 2
PAIR_PER_GROUP = SSD_PAIRS // SSD_GROUPS
CONV_PAD = 8


def _exact_dot(a, b):
    return jnp.dot(a, b, preferred_element_type=F32, precision=lax.Precision.HIGHEST)


def _softplus(x):
    u = jnp.exp(-jnp.abs(x))
    w = 1.0 + u
    return jnp.maximum(x, 0.0) + jnp.where(w == 1.0, u, jnp.log(w) * (u / (w - 1.0)))


def _silu(x):
    return x * jax.nn.sigmoid(x)


def _group_rmsnorm(y, g):
    gw = SSD_INNER // SSD_GROUPS
    outs = []
    for gi in range(SSD_GROUPS):
        yg = y[:, gi * gw:(gi + 1) * gw]
        outs.append(yg * lax.rsqrt(jnp.mean(yg * yg, axis=-1, keepdims=True) + EPS))
    return jnp.concatenate(outs, axis=1) * g


def _ssd_prompt_body(z_ref, xbc_ref, sm_ref, cw_ref, cb_ref, dtb_ref, alog_ref, d_ref, g_ref, ex_ref,
                     y_ref, st_ref, cv_ref, buf, state):
    c = pl.program_id(1)
    nc = pl.num_programs(1)
    t = SSD_CHUNK

    @pl.when(c == 0)
    def _():
        buf[0:CONV_PAD, :] = jnp.zeros((CONV_PAD, SSD_CONV_CH), F32)
        state[...] = jnp.zeros_like(state)

    buf[CONV_PAD:CONV_PAD + t, :] = xbc_ref[...]
    conv = cb_ref[...] + cw_ref[SSD_CONV - 1:SSD_CONV, :] * xbc_ref[...]
    for k in range(SSD_CONV - 1):
        conv = conv + cw_ref[k:k + 1, :] * buf[pl.ds(CONV_PAD - (SSD_CONV - 1) + k, t), :]
    buf[0:CONV_PAD, :] = buf[t:t + CONV_PAD, :]
    xc = _silu(conv)
    xs = xc[:, :SSD_INNER]
    bm = xc[:, SSD_INNER:SSD_INNER + SSD_GROUPS * SSD_STATE].astype(BF16)
    cm = xc[:, SSD_INNER + SSD_GROUPS * SSD_STATE:].astype(BF16)

    dt = _softplus(sm_ref[...] + dtb_ref[...])
    da = dt * (-jnp.exp(alog_ref[...]))
    row = lax.broadcasted_iota(jnp.int32, (t, t), 0)
    col = lax.broadcasted_iota(jnp.int32, (t, t), 1)
    tril = row >= col
    a_cs = _exact_dot(tril.astype(F32), da)
    a_cs_t = a_cs.T
    ex = ex_ref[...]
    dt_x = _exact_dot(dt, ex)
    acs_x = _exact_dot(a_cs, ex)
    last_x = acs_x[t - 1:t, :]
    xd = xs * dt_x
    xdw = xd * jnp.exp(last_x - acs_x)
    xdb = xd.astype(BF16)
    e_in = jnp.exp(acs_x)
    low = lax.broadcasted_iota(jnp.int32, (t, LANE), 1) < SSD_HEADDIM
    zero = jnp.zeros((t, LANE), BF16)

    y_parts = []
    for j in range(SSD_PAIRS):
        g = j // PAIR_PER_GROUP
        cg = cm[:, g * SSD_STATE:(g + 1) * SSD_STATE]
        bg = bm[:, g * SSD_STATE:(g + 1) * SSD_STATE]
        cb = _nt_dot(cg, bg)
        cols = slice(j * LANE, (j + 1) * LANE)
        xp = xdb[:, cols]
        y = jnp.zeros((t, LANE), F32)
        for hh in range(2):
            h = 2 * j + hh
            decay = jnp.where(tril, jnp.exp(a_cs[:, h:h + 1] - a_cs_t[h:h + 1, :]), 0.0)
            xh = jnp.where(low if hh == 0 else jnp.logical_not(low), xp, zero)
            y = y + jnp.dot((cb * decay).astype(BF16), xh, preferred_element_type=F32)
        s_old = state[j]
        y = y + _nt_dot(cg, s_old.astype(BF16)) * e_in[:, cols]
        dec_rows = jnp.broadcast_to(jnp.exp(last_x[:, cols]), (LANE, LANE)).T
        state[j] = dec_rows * s_old + jnp.dot(xdw[:, cols].T.astype(BF16), bg, preferred_element_type=F32)
        y_parts.append(y)
    y = jnp.concatenate(y_parts, axis=1) + xs * d_ref[...]
    y_ref[...] = _group_rmsnorm(y * _silu(z_ref[...]), g_ref[...])

    @pl.when(c == nc - 1)
    def _():
        st_ref[0] = state[...]
        cv_ref[0] = xbc_ref[t - (SSD_CONV - 1):t, :]


def _head_expand():
    return jnp.asarray(np.arange(LANE)[:, None] == (np.arange(SSD_INNER)[None, :] // SSD_HEADDIM), F32)


def _ssd_params(conv_w, conv_b, dt_bias, a_log, d_skip, norm_g):
    pad = lambda v: jnp.pad(v.astype(F32), (0, LANE - SSD_HEADS)).reshape(1, LANE)
    return (conv_w.astype(F32), conv_b.astype(F32).reshape(1, -1), pad(dt_bias), pad(a_log),
            jnp.repeat(d_skip.astype(F32), SSD_HEADDIM).reshape(1, -1), norm_g.astype(F32).reshape(1, -1), _head_expand())


def ssd_prompt(z, xbc, small, b, l, conv_w, conv_b, dt_bias, a_log, d_skip, norm_g):
    t = SSD_CHUNK
    nc = l // t
    params = _ssd_params(conv_w, conv_b, dt_bias, a_log, d_skip, norm_g)
    rows = lambda w: pl.BlockSpec((t, w), lambda bi, c: (bi * nc + c, 0))
    const = lambda a: pl.BlockSpec(a.shape, lambda bi, c: (0,) * a.ndim)
    y, st, cv = pl.pallas_call(
        _ssd_prompt_body,
        grid=(b, nc),
        in_specs=[rows(SSD_INNER), rows(SSD_CONV_CH), rows(LANE)] + [const(p) for p in params],
        out_specs=[rows(SSD_INNER),
                   pl.BlockSpec((1, SSD_PAIRS, LANE, SSD_STATE), lambda bi, c: (bi, 0, 0, 0)),
                   pl.BlockSpec((1, SSD_CONV - 1, SSD_CONV_CH), lambda bi, c: (bi, 0, 0))],
        out_shape=[jax.ShapeDtypeStruct((b * l, SSD_INNER), F32),
                   jax.ShapeDtypeStruct((b, SSD_PAIRS, LANE, SSD_STATE), F32),
                   jax.ShapeDtypeStruct((b, SSD_CONV - 1, SSD_CONV_CH), F32)],
        scratch_shapes=[pltpu.VMEM((CONV_PAD + t, SSD_CONV_CH), F32), pltpu.VMEM((SSD_PAIRS, LANE, SSD_STATE), F32)],
        compiler_params=pltpu.CompilerParams(dimension_semantics=("arbitrary", "arbitrary"),
                                             vmem_limit_bytes=VMEM_LIMIT_BYTES),
        name="ssd_prompt",
    )(z, xbc, small, *params)
    return y, st.reshape(b, SSD_HEADS, SSD_HEADDIM, SSD_STATE), cv


def _ssd_sample_body(z_ref, xbc_ref, sm_ref, cs_ref, st_ref, cw_ref, cb_ref, dtb_ref, alog_ref, d_ref, g_ref, ex_ref,
                     y_ref, sto_ref, cvo_ref):
    x_new = xbc_ref[0]
    cs = cs_ref[0]
    conv = cb_ref[...] + cw_ref[SSD_CONV - 1:SSD_CONV, :] * x_new
    for k in range(SSD_CONV - 1):
        conv = conv + cw_ref[k:k + 1, :] * cs[k:k + 1, :]
    cvo_ref[0] = jnp.concatenate([cs[1:], x_new], axis=0)
    xc = _silu(conv)
    xs = xc[:, :SSD_INNER]
    bm = xc[:, SSD_INNER:SSD_INNER + SSD_GROUPS * SSD_STATE]
    cm = xc[:, SSD_INNER + SSD_GROUPS * SSD_STATE:]
    dt = jnp.broadcast_to(_softplus(sm_ref[0] + dtb_ref[...]), (8, LANE))
    da = dt * (-jnp.exp(alog_ref[...]))
    ex = ex_ref[...]
    xd = xs * _exact_dot(dt, ex)[0:1]
    dec_x = jnp.exp(_exact_dot(da, ex)[0:1])
    y_parts = []
    for j in range(SSD_PAIRS):
        g = j // PAIR_PER_GROUP
        cols = slice(j * LANE, (j + 1) * LANE)
        bg = bm[:, g * SSD_STATE:(g + 1) * SSD_STATE]
        cg = jnp.broadcast_to(cm[:, g * SSD_STATE:(g + 1) * SSD_STATE], (8, SSD_STATE)).astype(BF16)
        dec_rows = jnp.broadcast_to(dec_x[:, cols], (LANE, LANE)).T
        xd_rows = jnp.broadcast_to(xd[:, cols], (LANE, LANE)).T
        s_new = dec_rows * st_ref[0, j] + xd_rows * bg
        sto_ref[0, j] = s_new
        y_parts.append(_nt_dot(cg, s_new.astype(BF16))[0:1])
    y = jnp.concatenate(y_parts, axis=1) + xs * d_ref[...]
    y_ref[0] = _group_rmsnorm(y * _silu(z_ref[0]), g_ref[...])


def ssd_sample(z, xbc, small, conv_state, ssm_state, conv_w, conv_b, dt_bias, a_log, d_skip, norm_g):
    db = z.shape[0]
    params = _ssd_params(conv_w, conv_b, dt_bias, a_log, d_skip, norm_g)
    per_b = lambda shape: pl.BlockSpec((1,) + shape, lambda i: (i,) + (0,) * len(shape))
    const = lambda a: pl.BlockSpec(a.shape, lambda i: (0,) * a.ndim)
    st_shape = (SSD_PAIRS, LANE, SSD_STATE)
    y, st, cv = pl.pallas_call(
        _ssd_sample_body,
        grid=(db,),
        in_specs=[per_b((1, SSD_INNER)), per_b((1, SSD_CONV_CH)), per_b((1, LANE)), per_b((SSD_CONV - 1, SSD_CONV_CH)),
                  per_b(st_shape)] + [const(p) for p in params],
        out_specs=[per_b((1, SSD_INNER)), per_b(st_shape), per_b((SSD_CONV - 1, SSD_CONV_CH))],
        out_shape=[jax.ShapeDtypeStruct((db, 1, SSD_INNER), F32), jax.ShapeDtypeStruct((db,) + st_shape, F32),
                   jax.ShapeDtypeStruct((db, SSD_CONV - 1, SSD_CONV_CH), F32)],
        compiler_params=pltpu.CompilerParams(dimension_semantics=("arbitrary",)),
        name="ssd_sample",
    )(z.reshape(db, 1, -1), xbc.reshape(db, 1, -1), small.reshape(db, 1, -1), conv_state,
      ssm_state.reshape((db,) + st_shape), *params)
    return y.reshape(db, SSD_INNER), st.reshape(db, SSD_HEADS, SSD_HEADDIM, SSD_STATE), cv


GLA_BLOCK = 128
GLA_INNER = GLA_HEADS * GLA_DV


def _log_gate(small, w2_ref, bg_ref):
    logits = jnp.dot(small.astype(BF16), w2_ref[...], preferred_element_type=F32) + bg_ref[...]
    return -_softplus(-logits) / GLA_TAU


def _head_rmsnorm_gate(o_heads, g_ref, r):
    outs = [o * lax.rsqrt(jnp.mean(o * o, axis=-1, keepdims=True) + EPS) for o in o_heads]
    return jnp.concatenate(outs, axis=1) * g_ref[...] * _silu(r)


def _gla_prompt_body(q_ref, k_ref, v_ref, r_ref, sm_ref, w2_ref, bg_ref, g_ref, y_ref, st_ref, state):
    blk = pl.program_id(1)
    t = GLA_BLOCK

    @pl.when(blk == 0)
    def _():
        state[...] = jnp.zeros_like(state)

    la = _log_gate(sm_ref[...], w2_ref, bg_ref)
    row = lax.broadcasted_iota(jnp.int32, (t, t), 0)
    col = lax.broadcasted_iota(jnp.int32, (t, t), 1)
    same = (row // GLA_CHUNK) == (col // GLA_CHUNK)
    causal = same & (row >= col)
    bc = _exact_dot(causal.astype(F32), la)
    is_last = col == (row // GLA_CHUNK) * GLA_CHUNK + GLA_CHUNK - 1
    bl = _exact_dot(is_last.astype(F32), bc)
    qt = q_ref[...] * (GLA_DK ** -0.5) * jnp.exp(bc)
    kt = k_ref[...] * jnp.exp(-bc)
    kd = k_ref[...] * jnp.exp(bl - bc)
    heads = range(GLA_HEADS)
    kcs = [slice(h * GLA_DK, (h + 1) * GLA_DK) for h in heads]
    vbs = [v_ref[:, h * GLA_DV:(h + 1) * GLA_DV].astype(BF16) for h in heads]
    qbs = [qt[:, kc].astype(BF16) for kc in kcs]
    o_intra = []
    for h in heads:
        att = jnp.where(causal, _nt_dot(qbs[h], kt[:, kcs[h]].astype(BF16)), 0.0)
        o_intra.append(jnp.dot(att.astype(BF16), vbs[h], preferred_element_type=F32))
    kd_t = [kd[:, kc].T for kc in kcs]
    dec_t = [jnp.exp(bl[:, kc]).T for kc in kcs]
    ss = [state[h] for h in heads]
    inter = [[] for _ in heads]
    for c in range(t // GLA_CHUNK):
        r0 = c * GLA_CHUNK
        in_chunk = (col // GLA_CHUNK) == c
        for h in heads:
            inter[h].append(jnp.dot(qbs[h][r0:r0 + GLA_CHUNK], ss[h].astype(BF16), preferred_element_type=F32))
        upd = [jnp.dot(jnp.where(in_chunk, kd_t[h], 0.0).astype(BF16), vbs[h], preferred_element_type=F32)
               for h in heads]
        ss = [dec_t[h][:, r0:r0 + 1] * ss[h] + upd[h] for h in heads]
    for h in heads:
        state[h] = ss[h]
    o_heads = [o_intra[h] + jnp.concatenate(inter[h], axis=0) for h in heads]
    y_ref[...] = _head_rmsnorm_gate(o_heads, g_ref, r_ref[...])

    @pl.when(blk == pl.num_programs(1) - 1)
    def _():
        st_ref[0] = state[...]


def _gla_params(w_g2, b_g, norm_g):
    w2 = jnp.pad(w_g2, ((0, LANE - GLA_GATE_RANK), (0, 0))).astype(BF16)
    return w2, b_g.astype(F32).reshape(1, -1), jnp.tile(norm_g.astype(F32), GLA_HEADS).reshape(1, -1)


def gla_prompt(gq, gk, gv, r, small, b, l, w_g2, b_g, norm_g):
    t = GLA_BLOCK
    nb = l // t
    params = _gla_params(w_g2, b_g, norm_g)
    rows = lambda w: pl.BlockSpec((t, w), lambda bi, i: (bi * nb + i, 0))
    const = lambda a: pl.BlockSpec(a.shape, lambda bi, i: (0,) * a.ndim)
    st_shape = (GLA_HEADS, GLA_DK, GLA_DV)
    return pl.pallas_call(
        _gla_prompt_body,
        grid=(b, nb),
        in_specs=[rows(GLA_HEADS * GLA_DK), rows(GLA_HEADS * GLA_DK), rows(GLA_INNER), rows(GLA_INNER), rows(LANE)]
        + [const(p) for p in params],
        out_specs=[rows(GLA_INNER), pl.BlockSpec((1,) + st_shape, lambda bi, i: (bi, 0, 0, 0))],
        out_shape=[jax.ShapeDtypeStruct((b * l, GLA_INNER), F32), jax.ShapeDtypeStruct((b,) + st_shape, F32)],
        scratch_shapes=[pltpu.VMEM(st_shape, F32)],
        compiler_params=pltpu.CompilerParams(dimension_semantics=("arbitrary", "arbitrary"),
                                             vmem_limit_bytes=VMEM_LIMIT_BYTES),
        name="gla_prompt",
    )(gq, gk, gv, r, small, *params)


def _gla_sample_body(q_ref, k_ref, v_ref, r_ref, sm_ref, st_ref, w2_ref, bg_ref, g_ref, y_ref, sto_ref):
    la = _log_gate(jnp.broadcast_to(sm_ref[0], (8, LANE)), w2_ref, bg_ref)[0:1]
    a = jnp.exp(la)
    qt = q_ref[0] * (GLA_DK ** -0.5) * a
    kt = k_ref[0] * jnp.exp(-la)
    o_heads = []
    for h in range(GLA_HEADS):
        kc = slice(h * GLA_DK, (h + 1) * GLA_DK)
        vb = _bf(v_ref[0, :, h * GLA_DV:(h + 1) * GLA_DV])
        qb = _bf(qt[:, kc])
        att = jnp.sum(qb * _bf(kt[:, kc]), axis=-1, keepdims=True)
        s_old = st_ref[0, h]
        o_inter = jnp.dot(jnp.broadcast_to(qb, (8, GLA_DK)).astype(BF16), s_old.astype(BF16),
                          preferred_element_type=F32)[0:1]
        a_rows = jnp.broadcast_to(a[:, kc], (GLA_DK, GLA_DK)).T[:, 0:1]
        k_rows = jnp.broadcast_to(k_ref[0, :, kc], (GLA_DK, GLA_DK)).T[:, 0:1]
        sto_ref[0, h] = a_rows * s_old + k_rows * v_ref[0, :, h * GLA_DV:(h + 1) * GLA_DV]
        o_heads.append(_bf(att) * vb + o_inter)
    y_ref[0] = _head_rmsnorm_gate(o_heads, g_ref, r_ref[0])


def gla_sample(gq, gk, gv, r, small, state, w_g2, b_g, norm_g):
    db = gq.shape[0]
    params = _gla_params(w_g2, b_g, norm_g)
    per_b = lambda shape: pl.BlockSpec((1,) + shape, lambda i: (i,) + (0,) * len(shape))
    const = lambda a: pl.BlockSpec(a.shape, lambda i: (0,) * a.ndim)
    st_shape = (GLA_HEADS, GLA_DK, GLA_DV)
    row3 = lambda x: x.reshape(db, 1, -1)
    y, st = pl.pallas_call(
        _gla_sample_body,
        grid=(db,),
        in_specs=[per_b((1, GLA_HEADS * GLA_DK)), per_b((1, GLA_HEADS * GLA_DK)), per_b((1, GLA_INNER)),
                  per_b((1, GLA_INNER)), per_b((1, LANE)), per_b(st_shape)] + [const(p) for p in params],
        out_specs=[per_b((1, GLA_INNER)), per_b(st_shape)],
        out_shape=[jax.ShapeDtypeStruct((db, 1, GLA_INNER), F32), jax.ShapeDtypeStruct((db,) + st_shape, F32)],
        compiler_params=pltpu.CompilerParams(dimension_semantics=("arbitrary",)),
        name="gla_sample",
    )(row3(gq), row3(gk), row3(gv), row3(r), row3(small), state, *params)
    return y.reshape(db, GLA_INNER), st


def _ffn_body(x_ref, g_ref, wg_ref, wu_ref, wd_ref, fg_ref, o_ref, *, chunks, final):
    x = x_ref[...]
    h = (x * lax.rsqrt(jnp.mean(x * x, axis=-1, keepdims=True) + EPS) * g_ref[...]).astype(BF16)
    acc = x
    for c0, cw in chunks:
        a = jnp.dot(h, wg_ref[:, c0:c0 + cw], preferred_element_type=F32)
        u = jnp.dot(h, wu_ref[:, c0:c0 + cw], preferred_element_type=F32)
        act = (a * jax.nn.sigmoid(a) * u).astype(BF16)
        acc = acc + jnp.dot(act, wd_ref[c0:c0 + cw, :], preferred_element_type=F32)
    if final:
        acc = acc * lax.rsqrt(jnp.mean(acc * acc, axis=-1, keepdims=True) + EPS) * fg_ref[...]
    o_ref[...] = acc


def swiglu_ffn(x, g, w_gate, w_up, w_down, final_g=None):
    m, d = x.shape
    hdim = w_gate.shape[1]
    tm = _row_tile(m)
    final = final_g is not None
    fg = (final_g if final else g).reshape(1, d).astype(F32)
    wspec = lambda shape: pl.BlockSpec(shape, lambda i: (0, 0), pipeline_mode=pl.Buffered(1))
    return pl.pallas_call(
        functools.partial(_ffn_body, chunks=_col_chunks(hdim, 256), final=final),
        grid=(m // tm,),
        in_specs=[pl.BlockSpec((tm, d), lambda i: (i, 0)),
                  pl.BlockSpec((1, d), lambda i: (0, 0)),
                  wspec((d, hdim)), wspec((d, hdim)), wspec((hdim, d)),
                  pl.BlockSpec((1, d), lambda i: (0, 0))],
        out_specs=pl.BlockSpec((tm, d), lambda i: (i, 0)),
        out_shape=jax.ShapeDtypeStruct((m, d), F32),
        compiler_params=pltpu.CompilerParams(dimension_semantics=("arbitrary",),
                                             vmem_limit_bytes=VMEM_LIMIT_BYTES),
        name="swiglu_ffn",
    )(x, g.reshape(1, d).astype(F32), w_gate.astype(BF16), w_up.astype(BF16), w_down.astype(BF16), fg)


def split_cols(h, sizes):
    return jnp.split(h, [int(i) for i in np.cumsum(sizes)[:-1]], axis=-1)


def kernel(x_prompt, x_sample, state_ssm, state_conv, cache_nsa_k_cmp, cache_nsa_v_cmp, cache_nsa_k_slc, cache_nsa_v_slc, cache_nsa_k_win, cache_nsa_v_win, state_gla, cache_mla_latent, cache_mla_krope, page_table, rel_bias, ev_norm, ev_w_in, ssd_conv_w, ssd_conv_b, ssd_dt_bias, ssd_a_log, ssd_d, ssd_norm, nsa_w_cmp_k, nsa_w_cmp_v, ev_w_out, od_norm, od_w_in, gla_w_gate2, gla_b_gate, gla_norm, mla_q_norm, mla_w_qb, mla_kv_norm, mla_w_kvb, od_w_out, ffn_norm, ffn_w_gate, ffn_w_up, ffn_w_down, final_norm):
    b, l = x_prompt.shape[:2]
    db, s = x_sample.shape[:2]
    depth = ffn_norm.shape[0]
    p0 = page_table.shape[1] * PAGE_SIZE
    pos_p = jnp.arange(l)
    pos_s = p0 + jnp.arange(s)
    names = ('ssm', 'conv', 'k_cmp', 'v_cmp', 'k_slc', 'v_slc', 'k_win', 'v_win', 'gla', 'lat', 'krope')
    newp = {n: [] for n in names}
    news = {n: [] for n in names}
    np_rows = b * l
    x = jnp.concatenate([x_prompt.reshape(np_rows, D_MODEL), x_sample.reshape(db * s, D_MODEL)], axis=0)
    for li in range(depth):
        if li % 2 == 0:
            e = li // 2
            ssd_w = (ssd_conv_w[e], ssd_conv_b[e], ssd_dt_bias[e], ssd_a_log[e], ssd_d[e], ssd_norm[e])
            parts = fused_matmul(x, even_weight(ev_w_in[e]), norm_g=ev_norm[e], groups=EV_GROUPS)
            z_all, xbc_all, _, kv_all, small_all = parts
            kv_heads = lambda r0, nb, nl: tuple(
                t.reshape(nb, nl, NSA_KV_HEADS, NSA_HEAD_DIM)
                for t in split_cols(kv_all[r0:r0 + nb * nl], (NSA_KV_WIDTH,) * 6))
            y_ssd, ssm_n, conv_n = ssd_prompt(z_all, xbc_all, small_all, b, l, *ssd_w)
            y_nsa = nsa_prompt_attention(parts[2], parts[3], parts[4], nsa_w_cmp_k[e], nsa_w_cmp_v[e], rel_bias, b, l)
            mix_p = jnp.concatenate([y_ssd, y_nsa], axis=-1)
            nw = min(NSA_WINDOW, l)
            kvs = kv_heads(0, b, l)
            for n, t in zip(names[:8], (ssm_n, conv_n, kvs[0], kvs[1], kvs[2], kvs[3],
                                        kvs[4][:, -nw:], kvs[5][:, -nw:])):
                newp[n].append(t)
            kvs = kv_heads(np_rows, db, s)
            y_ssd, ssm_n, conv_n = ssd_sample(z_all[np_rows:], xbc_all[np_rows:], small_all[np_rows:],
                                              state_conv[e], state_ssm[e], *ssd_w)
            y_nsa, kw_n, vw_n = nsa_sample_mixer(parts[2][np_rows:], parts[3][np_rows:], parts[4][np_rows:], e,
                                                 cache_nsa_k_cmp, cache_nsa_v_cmp, cache_nsa_k_slc, cache_nsa_v_slc,
                                                 cache_nsa_k_win[e], cache_nsa_v_win[e], page_table,
                                                 nsa_w_cmp_k[e], nsa_w_cmp_v[e], rel_bias)
            mix_s = jnp.concatenate([y_ssd, y_nsa], axis=-1)
            for n, t in zip(names[:8], (ssm_n, conv_n, kvs[0], kvs[1], kvs[2], kvs[3], kw_n, vw_n)):
                news[n].append(t)
            x = fused_matmul(jnp.concatenate([mix_p, mix_s], axis=0), ev_w_out[e], residual=x)
        else:
            o = li // 2
            gla_w = (gla_w_gate2[o], gla_b_gate[o], gla_norm[o])
            gq, gk, gv, r, qa, kva, small = fused_matmul(x, odd_weight(od_w_in[o]), norm_g=od_norm[o],
                                                         groups=OD_GROUPS)
            assert db * s == MLA_PREP_ROWS and l % MLA_PREP_ROWS == 0
            nblk = l // MLA_PREP_ROWS
            tables = tuple(jnp.concatenate(t, axis=0)
                           for t in zip(rope_tables(pos_p), rope_tables(jnp.tile(pos_s, db))))
            q_rot, lat, k_full, v_pad, kr_rot = mla_prep(
                qa, kva, small, mla_q_norm[o], mla_w_qb[o], mla_kv_norm[o], mla_w_kvb[o], tables,
                lambda i: jnp.where(i < b * nblk, i % nblk, nblk))
            kr_rot = kr_rot[:, MLA_NOPE:MLA_NOPE + MLA_ROPE]
            assert s == 1
            y_gla, gla_n = gla_prompt(gq, gk, gv, r, small, b, l, *gla_w)
            y_mla = mla_prompt_attention(q_rot, k_full, v_pad, b, l)
            mix_p = jnp.concatenate([y_gla, y_mla], axis=-1)
            newp['gla'].append(gla_n)
            newp['lat'].append(lat[:np_rows].reshape(b, l, -1))
            newp['krope'].append(kr_rot[:np_rows].reshape(b, l, -1))
            y_gla, gla_n = gla_sample(gq[np_rows:], gk[np_rows:], gv[np_rows:], r[np_rows:], small[np_rows:],
                                      state_gla[o], *gla_w)
            q_s = q_rot[np_rows:].reshape(db * s, MLA_HEADS, LANE)
            y_mla = mla_sample_mixer(q_s[..., :MLA_NOPE], q_s[..., MLA_NOPE:MLA_NOPE + MLA_ROPE], lat[np_rows:],
                                     kr_rot[np_rows:], cache_mla_latent, cache_mla_krope, o, page_table, mla_w_kvb[o])
            mix_s = jnp.concatenate([y_gla, y_mla], axis=-1)
            news['gla'].append(gla_n)
            news['lat'].append(lat[np_rows:].reshape(db, s, -1))
            news['krope'].append(kr_rot[np_rows:].reshape(db, s, -1))
            x = fused_matmul(jnp.concatenate([mix_p, mix_s], axis=0), od_w_out[o], residual=x)
        x = swiglu_ffn(x, ffn_norm[li], ffn_w_gate[li], ffn_w_up[li], ffn_w_down[li],
                       final_g=final_norm if li == depth - 1 else None)
    y_prompt = x[:np_rows].reshape(b, l, D_MODEL)
    y_sample = x[np_rows:].reshape(db, s, D_MODEL)
    st = lambda d, n: jnp.stack(d[n])
    return (y_prompt, y_sample,
            st(newp, 'ssm'), st(news, 'ssm'), st(newp, 'conv'), st(news, 'conv'),
            st(newp, 'k_cmp'), st(news, 'k_cmp'), st(newp, 'v_cmp'), st(news, 'v_cmp'),
            st(newp, 'k_slc'), st(news, 'k_slc'), st(newp, 'v_slc'), st(news, 'v_slc'),
            st(newp, 'k_win'), st(news, 'k_win'), st(newp, 'v_win'), st(news, 'v_win'),
            st(newp, 'gla'), st(news, 'gla'), st(newp, 'lat'), st(news, 'lat'),
            st(newp, 'krope'), st(news, 'krope'))
```

```python
import functools
import math

import jax
import jax.numpy as jnp
import numpy as np
from jax import lax
from jax.experimental import pallas as pl
from jax.experimental.pallas import tpu as pltpu

F32 = jnp.float32
BF16 = jnp.bfloat16
EPS = 1e-6
NEG = -1e30

D_MODEL = 1024
PAGE_SIZE = 128

SSD_HEADS = 16
SSD_HEADDIM = 64
SSD_INNER = SSD_HEADS * SSD_HEADDIM
SSD_GROUPS = 2
SSD_STATE = 128
SSD_CONV = 4
SSD_CHUNK = 128
SSD_CONV_CH = SSD_INNER + 2 * SSD_GROUPS * SSD_STATE

NSA_HEADS = 16
NSA_KV_HEADS = 2
NSA_GROUP = NSA_HEADS // NSA_KV_HEADS
NSA_HEAD_DIM = 64
NSA_KV_WIDTH = NSA_KV_HEADS * NSA_HEAD_DIM
NSA_CMP_BLOCK = 32
NSA_CMP_STRIDE = 16
NSA_SEL_BLOCK = 64
NSA_TOPK = 16
NSA_WINDOW = 512
NSA_QBLOCK = 128
NSA_SCALE = NSA_HEAD_DIM ** -0.5
NSA_FORCE = 1e4

REL_BUCKETS = 32
REL_MAX_DIST = 128

GLA_HEADS = 4
GLA_DK = 128
GLA_DV = 256
GLA_GATE_RANK = 16
GLA_TAU = 16.0
GLA_CHUNK = 16

MLA_HEADS = 8
MLA_Q_RANK = 384
MLA_KV_RANK = 256
MLA_NOPE = 64
MLA_ROPE = 32
MLA_V = 64
MLA_SCALE = (MLA_NOPE + MLA_ROPE) ** -0.5
ROPE_THETA = 10000.0

EV_SPLITS = (SSD_INNER, SSD_CONV_CH, SSD_HEADS, NSA_HEADS * NSA_HEAD_DIM, 6 * NSA_KV_WIDTH, 3 * NSA_HEADS)
OD_SPLITS = (GLA_HEADS * GLA_DK, GLA_HEADS * GLA_DK, GLA_HEADS * GLA_DV, GLA_GATE_RANK, GLA_HEADS * GLA_DV,
             MLA_Q_RANK, MLA_KV_RANK, MLA_ROPE)

VMEM_LIMIT_BYTES = 56 * 1024 * 1024
LANE = 128


def _row_tile(m):
    for t in (512, 384, 256, 128):
        if m % t == 0:
            return t
    return m


def _col_chunks(n, width=512):
    out, c = [], 0
    while c < n:
        w = min(width, n - c)
        out.append((c, w))
        c += w
    return out


def _mm_body(*refs, norm, residual, groups):
    it = iter(refs)
    x_ref = next(it)
    g_ref = next(it) if norm else None
    w_ref = next(it)
    r_ref = next(it) if residual else None
    o_refs = list(it)
    x = x_ref[...]
    if norm:
        x = x * lax.rsqrt(jnp.mean(x * x, axis=-1, keepdims=True) + EPS) * g_ref[...]
    xb = x.astype(BF16)
    off = 0
    for o_ref, gw in zip(o_refs, groups):
        for c0, cw in _col_chunks(gw):
            acc = jnp.dot(xb, w_ref[:, off + c0:off + c0 + cw], preferred_element_type=F32)
            if residual:
                acc = acc + r_ref[:, off + c0:off + c0 + cw]
            o_ref[:, c0:c0 + cw] = acc
        off += gw


def fused_matmul(x, w, norm_g=None, residual=None, groups=None):
    m, k = x.shape
    n = w.shape[1]
    single = groups is None
    groups = (n,) if single else tuple(groups)
    assert sum(groups) == n
    tm = _row_tile(m)
    norm = norm_g is not None
    res = residual is not None
    args = [x]
    specs = [pl.BlockSpec((tm, k), lambda i: (i, 0))]
    if norm:
        args.append(norm_g.reshape(1, k).astype(F32))
        specs.append(pl.BlockSpec((1, k), lambda i: (0, 0)))
    args.append(w.astype(BF16))
    specs.append(pl.BlockSpec((k, n), lambda i: (0, 0)))
    if res:
        args.append(residual)
        specs.append(pl.BlockSpec((tm, n), lambda i: (i, 0)))
    outs = pl.pallas_call(
        functools.partial(_mm_body, norm=norm, residual=res, groups=groups),
        grid=(m // tm,),
        in_specs=specs,
        out_specs=[pl.BlockSpec((tm, gw), lambda i: (i, 0)) for gw in groups],
        out_shape=[jax.ShapeDtypeStruct((m, gw), F32) for gw in groups],
        compiler_params=pltpu.CompilerParams(dimension_semantics=("arbitrary",),
                                             vmem_limit_bytes=VMEM_LIMIT_BYTES),
        name="fused_matmul",
    )(*args)
    return outs[0] if single else outs


EV_GROUPS = (SSD_INNER, SSD_CONV_CH, NSA_HEADS * NSA_HEAD_DIM, 6 * NSA_KV_WIDTH, LANE)
SMALL_DT = 0
SMALL_GATE = SSD_HEADS
QB = NSA_QBLOCK


def even_weight(w_in):
    z, xbc, dtr, q, kvs, g = split_cols(w_in, EV_SPLITS)
    pad = jnp.zeros((w_in.shape[0], LANE - SSD_HEADS - 3 * NSA_HEADS), w_in.dtype)
    return jnp.concatenate([z, xbc, q, kvs, dtr, g, pad], axis=1)


def _bucket_tile(rel):
    exact = REL_BUCKETS // 2
    n = jnp.maximum(rel, 0)
    nf = jnp.maximum(n, exact).astype(F32)
    large = exact + (jnp.log(nf / exact) / math.log(REL_MAX_DIST / exact) * (REL_BUCKETS - exact)).astype(jnp.int32)
    return jnp.where(n < exact, n, jnp.minimum(large, REL_BUCKETS - 1))


def _bias_tiles_body(table_ref, o_ref, *, base_step, base_off, kmul, qmul):
    t = pl.program_id(0)
    shape = o_ref.shape[2:]
    qi = lax.broadcasted_iota(jnp.int32, shape, 0)
    ki = lax.broadcasted_iota(jnp.int32, shape, 1)
    bucket = _bucket_tile(t * base_step + base_off + qmul * qi - kmul * ki)
    for hd in range(NSA_HEADS):
        acc = jnp.zeros(shape, F32)
        for bk in range(REL_BUCKETS):
            acc = jnp.where(bucket == bk, table_ref[bk, hd], acc)
        o_ref[0, hd] = acc


def bias_tiles(table, n_tiles, base_step, base_off, kmul, qmul=1, rows=QB):
    return pl.pallas_call(
        functools.partial(_bias_tiles_body, base_step=base_step, base_off=base_off, kmul=kmul, qmul=qmul),
        grid=(n_tiles,),
        in_specs=[pl.BlockSpec(memory_space=pltpu.SMEM)],
        out_specs=pl.BlockSpec((1, NSA_HEADS, rows, LANE), lambda t: (t, 0, 0, 0)),
        out_shape=jax.ShapeDtypeStruct((n_tiles, NSA_HEADS, rows, LANE), F32),
        compiler_params=pltpu.CompilerParams(dimension_semantics=("arbitrary",)),
        name="t5_bias_tiles",
    )(table.astype(F32))


def _compress_body(kv_ref, w_ref, o_ref):
    nseg = o_ref.shape[1]
    first = jnp.zeros(o_ref.shape[1:], F32)
    second = jnp.zeros(o_ref.shape[1:], F32)
    for r in range(NSA_CMP_STRIDE):
        rows = kv_ref[pl.ds(r, nseg, stride=NSA_CMP_STRIDE), :]
        first = first + rows * w_ref[r:r + 1, :]
        second = second + rows * w_ref[NSA_CMP_STRIDE + r:NSA_CMP_STRIDE + r + 1, :]
    out = first + pltpu.roll(second, nseg - 1, axis=0)
    row = lax.broadcasted_iota(jnp.int32, out.shape, 0)
    o_ref[0] = jnp.where(row < nseg - 1, out, 0.0)


def nsa_compress_prompt(kv, w_ck, w_cv, b, l):
    nseg = l // NSA_CMP_STRIDE
    w = jnp.concatenate([w_ck, w_ck, w_cv, w_cv], axis=1).astype(F32)
    return pl.pallas_call(
        _compress_body,
        grid=(b, 2),
        in_specs=[pl.BlockSpec((l, NSA_KV_WIDTH), lambda i, j: (i, j)),
                  pl.BlockSpec((NSA_CMP_BLOCK, NSA_KV_WIDTH), lambda i, j: (0, j))],
        out_specs=pl.BlockSpec((1, nseg, NSA_KV_WIDTH), lambda i, j: (i, 0, j)),
        out_shape=jax.ShapeDtypeStruct((b, nseg, 2 * NSA_KV_WIDTH), F32),
        compiler_params=pltpu.CompilerParams(dimension_semantics=("arbitrary", "arbitrary")),
        name="nsa_compress",
    )(kv, w)


def _nt_dot(a, b):
    return lax.dot_general(a, b, (((1,), (1,)), ((), ())), preferred_element_type=F32)


def _importance(p_group_sum, ovt):
    return jnp.dot(p_group_sum.astype(BF16), ovt, preferred_element_type=F32)


def _vt_body(kv_ref, o_ref):
    for w, col in enumerate((3 * LANE, 5 * LANE)):
        o_ref[0, w, 0] = kv_ref[:, col:col + LANE].T.astype(BF16)


def nsa_value_transpose(kv, b, l):
    nkt = l // QB
    return pl.pallas_call(
        _vt_body,
        grid=(b, nkt),
        in_specs=[pl.BlockSpec((QB, 6 * NSA_KV_WIDTH), lambda bi, kt: (bi * nkt + kt, 0))],
        out_specs=pl.BlockSpec((1, 2, 1, LANE, QB), lambda bi, kt: (bi, 0, kt, 0, 0)),
        out_shape=jax.ShapeDtypeStruct((b, 2, nkt, LANE, QB), BF16),
        compiler_params=pltpu.CompilerParams(dimension_semantics=("arbitrary", "arbitrary")),
        name="nsa_value_transpose",
    )(kv)


def _nsa_prompt_t_body(q_ref, kv_ref, vt_ref, sm_ref, cmp_ref, cmpb_ref, toeb_ref, ov_ref, exp_ref, o_ref,
                       selm_ref, acc_ref, *, n_kt):
    i = pl.program_id(1)
    n_sel = 2 * n_kt
    krow = lax.broadcasted_iota(jnp.int32, (QB, LANE), 0)
    qlane = lax.broadcasted_iota(jnp.int32, (QB, LANE), 1)
    low = qlane < NSA_HEAD_DIM
    gates_t = jax.nn.sigmoid(sm_ref[...]).T
    qpos = i * QB + qlane
    heads = range(NSA_GROUP)
    all_heads = range(NSA_HEADS)

    def attend(qs, kcol, w, n_tiles, tile_of, use_sel, window):
        acc_ref[...] = jnp.zeros_like(acc_ref)

        def body(t, carry):
            m, l = carry
            kt = tile_of(t)
            d = i - kt
            k0 = pl.multiple_of(kt * QB, QB)
            k_t = kv_ref[pl.ds(k0, QB), kcol:kcol + LANE].astype(BF16)
            vt_t = vt_ref[0, w, kt]
            rel = d * QB + qlane - krow
            vis = rel >= 0
            if window:
                vis = vis & (rel < NSA_WINDOW)
            msk = [vis & (selm_ref[h, kt] > 0.5) if use_sel else vis for h in range(NSA_KV_HEADS)]
            bias = jnp.minimum(d, 2)
            s = [jnp.where(msk[hd // NSA_GROUP], _nt_dot(k_t, qs[hd]) + toeb_ref[bias, hd], NEG) for hd in all_heads]
            m_new = [jnp.maximum(m[hd], jnp.max(s[hd], axis=0, keepdims=True)) for hd in all_heads]
            alpha = [jnp.exp(m[hd] - m_new[hd]) for hd in all_heads]
            p = [jnp.exp(s[hd] - m_new[hd][0:1]) for hd in all_heads]
            l_new = [alpha[hd] * l[hd] + jnp.sum(p[hd], axis=0, keepdims=True) for hd in all_heads]
            pv = [jnp.dot(vt_t, p[hd].astype(BF16), preferred_element_type=F32) for hd in all_heads]
            for hd in all_heads:
                acc_ref[hd] = acc_ref[hd] * alpha[hd][0:1] + pv[hd]
            return tuple(m_new), tuple(l_new)

        m0 = tuple(jnp.full((8, LANE), NEG, F32) for _ in all_heads)
        l0 = tuple(jnp.zeros((8, LANE), F32) for _ in all_heads)
        _, l = lax.fori_loop(0, n_tiles, body, (m0, l0))
        return [acc_ref[hd] / l[hd][0:1] for hd in all_heads]

    qs, o_c = [], []
    for h in range(NSA_KV_HEADS):
        in_half = low if h == 0 else jnp.logical_not(low)
        qh = []
        for g in heads:
            hd = h * NSA_GROUP + g
            t = q_ref[:, (hd // 2) * LANE:(hd // 2) * LANE + LANE]
            if hd % 2 != h:
                t = pltpu.roll(t, NSA_HEAD_DIM, axis=1)
            qh.append((jnp.where(in_half, t, 0.0) * NSA_SCALE).astype(BF16))
        qs.extend(qh)

        kc = cmp_ref[0, :, 0:LANE].astype(BF16)
        vc_t = cmp_ref[0, :, LANE:2 * LANE].T.astype(BF16)
        mc = (qpos - (NSA_CMP_STRIDE * krow + NSA_CMP_BLOCK - 1)) >= 0
        mcf = mc.astype(F32)
        s = [jnp.where(mc, _nt_dot(kc, qh[g]) + cmpb_ref[0, h * NSA_GROUP + g], NEG) for g in heads]
        e = [jnp.exp(s[g] - jnp.max(s[g], axis=0, keepdims=True)) for g in heads]
        p = [e[g] / jnp.sum(e[g], axis=0, keepdims=True) * mcf for g in heads]
        o_c.extend(jnp.dot(vc_t, p[g].astype(BF16), preferred_element_type=F32) for g in heads)
        p_sum = p[0]
        for g in heads[1:]:
            p_sum = p_sum + p[g]
        imp_t = jnp.dot(ov_ref[...], p_sum.astype(BF16), preferred_element_type=F32)[0:n_sel]

        jrow = lax.broadcasted_iota(jnp.int32, (n_sel, LANE), 0)
        cur = (i * QB + lax.broadcasted_iota(jnp.int32, (n_sel, LANE), 1)) // NSA_SEL_BLOCK
        valid = jrow <= cur
        forced = (jrow == 0) | (jrow == cur) | (jrow == cur - 1)
        score = jnp.where(valid, jnp.where(forced, NSA_FORCE, imp_t), -1.0)
        cnt = jnp.zeros((n_sel, LANE), jnp.int32)
        for jp in range(n_sel):
            other = score[jp:jp + 1, :]
            beats = (other > score) | ((other == score) & (jrow > jp))
            cnt = cnt + beats.astype(jnp.int32)
        sel_t = ((cnt < NSA_TOPK) & valid).astype(F32)
        sel_t = jnp.concatenate([sel_t, jnp.zeros((LANE - n_sel, LANE), F32)], axis=0).astype(BF16)
        for kt in range(n_kt):
            selm_ref[h, kt] = jnp.dot(exp_ref[kt * QB:(kt + 1) * QB, :], sel_t, preferred_element_type=F32)

    o_s = attend(qs, 2 * LANE, 0, i + 1, lambda t: t, True, False)
    o_w = attend(qs, 4 * LANE, 1, jnp.minimum(i, NSA_WINDOW // QB) + 1, lambda t: i - t, False, True)

    for h in range(NSA_KV_HEADS):
        outs = []
        for g in heads:
            hd = h * NSA_GROUP + g
            c = SMALL_GATE + hd
            out_t = (gates_t[c:c + 1] * o_c[hd] + gates_t[c + NSA_HEADS:c + NSA_HEADS + 1] * o_s[hd]
                     + gates_t[c + 2 * NSA_HEADS:c + 2 * NSA_HEADS + 1] * o_w[hd])
            outs.append(out_t.T)
        for pr in range(NSA_GROUP // 2):
            a, b2 = outs[2 * pr], outs[2 * pr + 1]
            if h == 0:
                b2 = pltpu.roll(b2, NSA_HEAD_DIM, axis=1)
            else:
                a = pltpu.roll(a, NSA_HEAD_DIM, axis=1)
            c0 = (h * NSA_GROUP + 2 * pr) * NSA_HEAD_DIM
            o_ref[:, c0:c0 + LANE] = jnp.where(low, a, b2)


def nsa_prompt_attention(q, kv, small, w_ck, w_cv, table, b, l):
    nqb = l // QB
    n_sel = l // NSA_SEL_BLOCK
    nseg = l // NSA_CMP_STRIDE
    assert nseg == LANE and n_sel <= LANE and l % QB == 0
    cmp = nsa_compress_prompt(kv, w_ck, w_cv, b, l)
    vt = nsa_value_transpose(kv, b, l)
    cmp_bias = bias_tiles(table, nqb, QB, -(NSA_CMP_BLOCK - 1), -1, qmul=-NSA_CMP_STRIDE)
    toe_bias = bias_tiles(table, 3, QB, 0, -1, qmul=-1)
    jj = np.arange(LANE)[None, :]
    cc = np.arange(LANE)[:, None]
    lo = np.maximum(jj * NSA_SEL_BLOCK, cc * NSA_CMP_STRIDE)
    hi = np.minimum(jj * NSA_SEL_BLOCK + NSA_SEL_BLOCK, cc * NSA_CMP_STRIDE + NSA_CMP_BLOCK)
    ovt = np.where((jj < n_sel) & (cc < nseg - 1), np.maximum(hi - lo, 0) / NSA_CMP_STRIDE, 0.0)
    expand = (np.arange(l)[None, :] // NSA_SEL_BLOCK == np.arange(LANE)[:, None]).astype(np.float32)
    return pl.pallas_call(
        functools.partial(_nsa_prompt_t_body, n_kt=nqb),
        grid=(b, nqb),
        in_specs=[pl.BlockSpec((QB, NSA_HEADS * NSA_HEAD_DIM), lambda bi, i: (bi * nqb + i, 0)),
                  pl.BlockSpec((l, 6 * NSA_KV_WIDTH), lambda bi, i: (bi, 0)),
                  pl.BlockSpec((1, 2, nqb, LANE, QB), lambda bi, i: (bi, 0, 0, 0, 0)),
                  pl.BlockSpec((QB, LANE), lambda bi, i: (bi * nqb + i, 0)),
                  pl.BlockSpec((1, nseg, 2 * NSA_KV_WIDTH), lambda bi, i: (bi, 0, 0)),
                  pl.BlockSpec((1, NSA_HEADS, QB, LANE), lambda bi, i: (i, 0, 0, 0)),
                  pl.BlockSpec((3, NSA_HEADS, QB, LANE), lambda bi, i: (0, 0, 0, 0)),
                  pl.BlockSpec((LANE, LANE), lambda bi, i: (0, 0)),
                  pl.BlockSpec((l, LANE), lambda bi, i: (0, 0))],
        out_specs=pl.BlockSpec((QB, NSA_HEADS * NSA_HEAD_DIM), lambda bi, i: (bi * nqb + i, 0)),
        out_shape=jax.ShapeDtypeStruct((b * l, NSA_HEADS * NSA_HEAD_DIM), F32),
        scratch_shapes=[pltpu.VMEM((NSA_KV_HEADS, nqb, QB, QB), F32), pltpu.VMEM((NSA_HEADS, LANE, QB), F32)],
        compiler_params=pltpu.CompilerParams(dimension_semantics=("arbitrary", "arbitrary"),
                                             vmem_limit_bytes=VMEM_LIMIT_BYTES),
        name="nsa_prompt",
    )(q, kv, vt, small, cmp, cmp_bias, toe_bias, jnp.asarray(ovt.T, BF16), jnp.asarray(expand.T, BF16))


PAGES_PER_STEP = 16
NSEG_PAGE = PAGE_SIZE // NSA_CMP_STRIDE
SELF_RANK = 2


def _bf(x):
    return x.astype(BF16).astype(F32)


def _bias_rows(table, n_tiles, step, off, kmul):
    t = bias_tiles(table, n_tiles, step, off, kmul, qmul=0, rows=8)[:, :, 0, :]
    return t.transpose(1, 0, 2).reshape(NSA_KV_HEADS, NSA_GROUP, n_tiles * LANE)


def _nsa_sample_cmp_body(pt_ref, k_hbm, v_hbm, qp_ref, wk_ref, wv_ref, bias_ref, ovt_ref, oc_ref, idx_ref,
                         kbuf, vbuf, ksem, vsem, fk, sk, fv, sv, *, base, n_pages):
    slot = _paged_prefetch((k_hbm, v_hbm), (kbuf, vbuf), (ksem, vsem), pt_ref, base, n_pages)
    nseg = n_pages * NSEG_PAGE

    def compress_page(p, carry):
        for buf, w_ref, f_ref, s_ref in ((kbuf, wk_ref, fk, sk), (vbuf, wv_ref, fv, sv)):
            f = jnp.zeros((NSEG_PAGE, LANE), F32)
            s = jnp.zeros((NSEG_PAGE, LANE), F32)
            for r in range(NSA_CMP_STRIDE):
                rows = buf[slot, p, pl.ds(r, NSEG_PAGE, stride=NSA_CMP_STRIDE), :]
                f = f + rows * w_ref[r:r + 1, :]
                s = s + rows * w_ref[NSA_CMP_STRIDE + r:NSA_CMP_STRIDE + r + 1, :]
            row0 = pl.multiple_of(p * NSEG_PAGE, NSEG_PAGE)
            f_ref[pl.ds(row0, NSEG_PAGE), :] = f
            s_ref[pl.ds(row0, NSEG_PAGE), :] = s
        return carry

    lax.fori_loop(0, n_pages, compress_page, 0, unroll=4)

    def attend_and_select():
        rowi = lax.broadcasted_iota(jnp.int32, (nseg, LANE), 0)
        kc = jnp.where(rowi < nseg - 1, fk[...] + pltpu.roll(sk[...], nseg - 1, axis=0), 0.0).astype(BF16)
        vc = jnp.where(rowi < nseg - 1, fv[...] + pltpu.roll(sv[...], nseg - 1, axis=0), 0.0).astype(BF16)
        mc = lax.broadcasted_iota(jnp.int32, (NSA_GROUP, nseg), 1) < nseg - 1
        lane = lax.broadcasted_iota(jnp.int32, (8, LANE), 1)
        imps = []
        for h in range(NSA_KV_HEADS):
            qp = (qp_ref[0, h] * NSA_SCALE).astype(BF16)
            s = jnp.where(mc, _nt_dot(qp, kc) + bias_ref[h], NEG)
            e = jnp.exp(s - jnp.max(s, axis=-1, keepdims=True))
            p = e / jnp.sum(e, axis=-1, keepdims=True) * mc.astype(F32)
            pb = p.astype(BF16)
            oc_ref[0, h] = jnp.dot(pb, vc, preferred_element_type=F32)
            imps.append(_importance(jnp.sum(p, axis=0, keepdims=True), ovt_ref[...]))
        imp = jnp.concatenate(imps + [jnp.zeros((8 - NSA_KV_HEADS, LANE), F32)], axis=0)
        forced = (lane == 0) | (lane == LANE - 1)
        score = jnp.where(forced, NSA_FORCE, imp)
        cnt = (score < NSA_FORCE).astype(F32)
        for jp in range(LANE):
            col = score[:, jp:jp + 1]
            cnt = cnt + ((col > score) | ((col == score) & (lane > jp))).astype(F32)
        lanef = lane.astype(F32)
        idx = jnp.where(lane == SELF_RANK, float(LANE), 0.0)
        for k in range(NSA_TOPK):
            if k != SELF_RANK:
                v = jnp.sum(jnp.where(cnt == k, lanef, 0.0), axis=-1, keepdims=True)
                idx = jnp.where(lane == k, v, idx)
        idx_ref[0] = idx.astype(jnp.int32)

    attend_and_select()


def nsa_sample_cmp(qp, k_pool, v_pool, e, page_table, w_ck, w_cv, table):
    db, n_pages = page_table.shape
    n_pool = k_pool.shape[1]
    p0 = n_pages * PAGE_SIZE
    nseg = n_pages * NSEG_PAGE
    npg = PAGES_PER_STEP
    assert p0 // NSA_SEL_BLOCK == LANE and n_pages % npg == 0
    kp = k_pool.reshape(-1, PAGE_SIZE, NSA_KV_WIDTH)
    vp = v_pool.reshape(-1, PAGE_SIZE, NSA_KV_WIDTH)
    bias = _bias_rows(table, nseg // LANE, -LANE * NSA_CMP_STRIDE, p0 - (NSA_CMP_BLOCK - 1), NSA_CMP_STRIDE)
    jj = np.arange(LANE)[None, :]
    cc = np.arange(nseg)[:, None]
    lo = np.maximum(jj * NSA_SEL_BLOCK, cc * NSA_CMP_STRIDE)
    hi = np.minimum(jj * NSA_SEL_BLOCK + NSA_SEL_BLOCK, cc * NSA_CMP_STRIDE + NSA_CMP_BLOCK)
    ovt = np.where(cc < nseg - 1, np.maximum(hi - lo, 0) / NSA_CMP_STRIDE, 0.0)

    const = lambda shape: pl.BlockSpec(shape, lambda b, pt: (0,) * len(shape))
    hbm = pl.BlockSpec(memory_space=pl.ANY)
    page_buf = pltpu.VMEM((2, n_pages, PAGE_SIZE, NSA_KV_WIDTH), F32)
    grid_spec = pltpu.PrefetchScalarGridSpec(
        num_scalar_prefetch=1,
        grid=(db,),
        in_specs=[hbm, hbm, pl.BlockSpec((1, NSA_KV_HEADS, NSA_GROUP, LANE), lambda b, pt: (b, 0, 0, 0)),
                  const((NSA_CMP_BLOCK, NSA_KV_WIDTH)), const((NSA_CMP_BLOCK, NSA_KV_WIDTH)),
                  const((NSA_KV_HEADS, NSA_GROUP, nseg)), const((nseg, LANE))],
        out_specs=[pl.BlockSpec((1, NSA_KV_HEADS, NSA_GROUP, LANE), lambda b, pt: (b, 0, 0, 0)),
                   pl.BlockSpec((1, 8, LANE), lambda b, pt: (b, 0, 0))],
        scratch_shapes=[page_buf, page_buf, pltpu.SemaphoreType.DMA((2,)), pltpu.SemaphoreType.DMA((2,))]
        + [pltpu.VMEM((nseg, LANE), F32)] * 4)
    tile2 = lambda w: jnp.concatenate([w, w], axis=1).astype(F32)
    return pl.pallas_call(
        functools.partial(_nsa_sample_cmp_body, base=e * n_pool, n_pages=n_pages),
        grid_spec=grid_spec,
        out_shape=[jax.ShapeDtypeStruct((db, NSA_KV_HEADS, NSA_GROUP, LANE), F32),
                   jax.ShapeDtypeStruct((db, 8, LANE), jnp.int32)],
        compiler_params=pltpu.CompilerParams(dimension_semantics=("arbitrary",), vmem_limit_bytes=VMEM_LIMIT_BYTES),
        name="nsa_sample_cmp",
    )(page_table, kp, vp, qp, tile2(w_ck), tile2(w_cv), bias, jnp.asarray(ovt, BF16))


def _selected_block_copies(pt_ref, idx_ref, pools, bufs, sems, b, slot, base):
    half = NSA_SEL_BLOCK
    bpp = PAGE_SIZE // NSA_SEL_BLOCK
    copies = []
    for ent in range(NSA_KV_HEADS * NSA_TOPK):
        if ent % NSA_TOPK == SELF_RANK:
            continue
        j = jnp.minimum(idx_ref[b, ent], LANE - 1)
        page = pt_ref[b, j // bpp] + base
        rows = pl.ds(pl.multiple_of((j % bpp) * half, half), half)
        for pool_ref, buf_ref, sem_ref in zip(pools, bufs, sems):
            copies.append(pltpu.make_async_copy(pool_ref.at[page, rows], buf_ref.at[slot, ent], sem_ref.at[slot]))
    return copies


def _nsa_sample_attn_body(pt_ref, idx_ref, k_hbm, v_hbm, qp_ref, new_ref, kwin_ref, vwin_ref, oc_ref, gate_ref,
                          bnear_ref, bwin_ref, bends_ref, o_ref, kwo_ref, vwo_ref, kb, vb, ksem, vsem, *, nbuf, base):
    b = pl.program_id(0)
    cur = b % 2
    copies = functools.partial(_selected_block_copies, pt_ref, idx_ref, (k_hbm, v_hbm), (kb, vb), (ksem, vsem))

    @pl.when(b == 0)
    def _():
        for c in copies(b, cur, base):
            c.start()

    @pl.when(b + 1 < pl.num_programs(0))
    def _():
        for c in copies(b + 1, 1 - cur, base):
            c.start()

    for c in copies(b, cur, base):
        c.wait()
    new = new_ref[0]
    ks_new, vs_new, kw_new, vw_new = (new[:, c * LANE:(c + 1) * LANE] for c in (2, 3, 4, 5))
    kwin = kwin_ref[0]
    vwin = vwin_ref[0]
    wlane = lax.broadcasted_iota(jnp.int32, (NSA_GROUP, nbuf), 1)
    half = NSA_SEL_BLOCK
    for h in range(NSA_KV_HEADS):
        qp = (qp_ref[0, h] * NSA_SCALE).astype(BF16)
        qpf = qp.astype(F32)
        b_self = bends_ref[h][:, 0:1]
        b_far = bends_ref[h][:, 1:2]
        near = bnear_ref[h]
        scores, slots = [], []
        for k in range(NSA_TOPK):
            if k == SELF_RANK:
                continue
            j = idx_ref[b, h * NSA_TOPK + k]
            bias = jnp.where(j == LANE - 1, near[:, half:], jnp.where(j == LANE - 2, near[:, :half], b_far))
            scores.append(_nt_dot(qp, kb[cur, h * NSA_TOPK + k].astype(BF16)) + bias)
            slots.append(h * NSA_TOPK + k)
        s_self = jnp.sum(qpf * _bf(ks_new), axis=-1, keepdims=True) + b_self
        m = s_self
        for s in scores:
            m = jnp.maximum(m, jnp.max(s, axis=-1, keepdims=True))
        p_self = jnp.exp(s_self - m)
        l = p_self
        acc = _bf(p_self) * _bf(vs_new)
        for s, ent in zip(scores, slots):
            p = jnp.exp(s - m)
            l = l + jnp.sum(p, axis=-1, keepdims=True)
            acc = acc + jnp.dot(p.astype(BF16), vb[cur, ent].astype(BF16), preferred_element_type=F32)
        o_s = acc / l
        s = jnp.where(wlane >= 1, _nt_dot(qp, kwin.astype(BF16)) + bwin_ref[h], NEG)
        s_self = jnp.sum(qpf * _bf(kw_new), axis=-1, keepdims=True) + b_self
        m = jnp.maximum(jnp.max(s, axis=-1, keepdims=True), s_self)
        p = jnp.exp(s - m)
        p_self = jnp.exp(s_self - m)
        l = jnp.sum(p, axis=-1, keepdims=True) + p_self
        o_w = (jnp.dot(p.astype(BF16), vwin.astype(BF16), preferred_element_type=F32)
               + _bf(p_self) * _bf(vw_new)) / l
        gates = jax.nn.sigmoid(gate_ref[0, h])
        o_h = gates[:, 0:1] * oc_ref[0, h] + gates[:, 1:2] * o_s + gates[:, 2:3] * o_w
        o_rot = pltpu.roll(o_h, NSA_HEAD_DIM, axis=1)
        low = lax.broadcasted_iota(jnp.int32, (1, LANE), 1) < NSA_HEAD_DIM
        for pr in range(NSA_GROUP // 2):
            a = (o_h if h == 0 else o_rot)[2 * pr:2 * pr + 1]
            b2 = (o_rot if h == 0 else o_h)[2 * pr + 1:2 * pr + 2]
            c0 = (h * NSA_GROUP + 2 * pr) * NSA_HEAD_DIM
            o_ref[0, :, c0:c0 + LANE] = jnp.where(low, a, b2)
    rowi = lax.broadcasted_iota(jnp.int32, (nbuf, LANE), 0)
    kwo_ref[0] = jnp.where(rowi == nbuf - 1, kw_new, pltpu.roll(kwin, nbuf - 1, axis=0))
    vwo_ref[0] = jnp.where(rowi == nbuf - 1, vw_new, pltpu.roll(vwin, nbuf - 1, axis=0))


def nsa_sample_attn(qp, kv_new, idx, o_c, gates_raw, k_pool, v_pool, e, k_win, v_win, page_table, table):
    db, n_pages = page_table.shape
    n_pool = k_pool.shape[1]
    p0 = n_pages * PAGE_SIZE
    nbuf = k_win.shape[1]
    nb = NSA_KV_HEADS * NSA_TOPK
    bpp = PAGE_SIZE // NSA_SEL_BLOCK
    assert nbuf == NSA_WINDOW and nbuf % LANE == 0 and p0 >= nbuf
    kp = k_pool.reshape(-1, PAGE_SIZE, NSA_KV_WIDTH)
    vp = v_pool.reshape(-1, PAGE_SIZE, NSA_KV_WIDTH)
    bnear = _bias_rows(table, 1, 0, 2 * NSA_SEL_BLOCK, 1)
    bwin = _bias_rows(table, nbuf // LANE, -LANE, nbuf, 1)
    tb = table.astype(F32).reshape(REL_BUCKETS, NSA_KV_HEADS, NSA_GROUP)
    bends = jnp.pad(jnp.stack([tb[0], tb[REL_BUCKETS - 1]], axis=-1), ((0, 0), (0, 0), (0, LANE - 2)))

    per_b = lambda shape: pl.BlockSpec((1,) + shape, lambda b, pt, ix: (b,) + (0,) * len(shape))
    const = lambda shape: pl.BlockSpec(shape, lambda b, pt, ix: (0,) * len(shape))
    hbm = pl.BlockSpec(memory_space=pl.ANY)
    head_shape = (NSA_KV_HEADS, NSA_GROUP, LANE)
    blk_buf = pltpu.VMEM((2, nb, NSA_SEL_BLOCK, NSA_KV_WIDTH), F32)
    grid_spec = pltpu.PrefetchScalarGridSpec(
        num_scalar_prefetch=2,
        grid=(db,),
        in_specs=[hbm, hbm,
                  per_b(head_shape), per_b((1, 6 * NSA_KV_WIDTH)), per_b((nbuf, NSA_KV_WIDTH)), per_b((nbuf, NSA_KV_WIDTH)),
                  per_b(head_shape), per_b(head_shape), const(head_shape),
                  const((NSA_KV_HEADS, NSA_GROUP, nbuf)), const(head_shape)],
        out_specs=[per_b((1, NSA_HEADS * NSA_HEAD_DIM)), per_b((nbuf, NSA_KV_WIDTH)), per_b((nbuf, NSA_KV_WIDTH))],
        scratch_shapes=[blk_buf, blk_buf, pltpu.SemaphoreType.DMA((2,)), pltpu.SemaphoreType.DMA((2,))])
    return pl.pallas_call(
        functools.partial(_nsa_sample_attn_body, nbuf=nbuf, base=e * n_pool),
        grid_spec=grid_spec,
        out_shape=[jax.ShapeDtypeStruct((db, 1, NSA_HEADS * NSA_HEAD_DIM), F32),
                   jax.ShapeDtypeStruct((db, nbuf, NSA_KV_WIDTH), F32),
                   jax.ShapeDtypeStruct((db, nbuf, NSA_KV_WIDTH), F32)],
        compiler_params=pltpu.CompilerParams(dimension_semantics=("arbitrary",)),
        name="nsa_sample_attn",
    )(page_table, idx, kp, vp, qp, kv_new.reshape(db, 1, -1),
      k_win.reshape(db, nbuf, NSA_KV_WIDTH), v_win.reshape(db, nbuf, NSA_KV_WIDTH), o_c, gates_raw,
      bnear, bwin, bends)


def nsa_sample_mixer(q, kv, small, e, k_cmp_pool, v_cmp_pool, k_slc_pool, v_slc_pool, k_win, v_win,
                     page_table, w_ck, w_cv, table):
    db = q.shape[0]
    qh = q.reshape(db, NSA_KV_HEADS, NSA_GROUP, NSA_HEAD_DIM)
    zeros = jnp.zeros_like(qh[:, 0])
    qp = jnp.stack([jnp.concatenate([qh[:, 0], zeros], axis=-1), jnp.concatenate([zeros, qh[:, 1]], axis=-1)], axis=1)
    o_c, idx = nsa_sample_cmp(qp, k_cmp_pool, v_cmp_pool, e, page_table, w_ck, w_cv, table)
    idx = idx[:, :NSA_KV_HEADS, :NSA_TOPK].reshape(db, NSA_KV_HEADS * NSA_TOPK)
    g = small[:, SMALL_GATE:SMALL_GATE + 3 * NSA_HEADS].reshape(db, 3, NSA_KV_HEADS, NSA_GROUP).transpose(0, 2, 3, 1)
    g = jnp.pad(g, ((0, 0), (0, 0), (0, 0), (0, LANE - 3)))
    y, kw_n, vw_n = nsa_sample_attn(qp, kv, idx, o_c, g, k_slc_pool, v_slc_pool, e, k_win, v_win, page_table, table)
    shape = (db, -1, NSA_KV_HEADS, NSA_HEAD_DIM)
    return y.reshape(db, NSA_HEADS * NSA_HEAD_DIM), kw_n.reshape(shape), vw_n.reshape(shape)


def _bmm_body(a_ref, b_ref, o_ref):
    o_ref[0] = jnp.dot(a_ref[0].astype(BF16), b_ref[0].astype(BF16), preferred_element_type=F32)


def batched_matmul(a, b):
    hh, m, k = a.shape
    n = b.shape[2]
    return pl.pallas_call(
        _bmm_body,
        grid=(hh,),
        in_specs=[pl.BlockSpec((1, m, k), lambda i: (i, 0, 0)), pl.BlockSpec((1, k, n), lambda i: (i, 0, 0))],
        out_specs=pl.BlockSpec((1, m, n), lambda i: (i, 0, 0)),
        out_shape=jax.ShapeDtypeStruct((hh, m, n), F32),
        compiler_params=pltpu.CompilerParams(dimension_semantics=("arbitrary",)),
        name="batched_matmul",
    )(a, b)


def _page_copy(pool_ref, buf_ref, sem_ref, page, slot, p):
    return pltpu.make_async_copy(pool_ref.at[page], buf_ref.at[slot, p], sem_ref.at[slot])


def _start_pages(pools, bufs, sems, pt_ref, b, slot, base, n_pages):
    def issue(p, carry):
        page = pt_ref[b, p] + base
        for pool_ref, buf_ref, sem_ref in zip(pools, bufs, sems):
            _page_copy(pool_ref, buf_ref, sem_ref, page, slot, p).start()
        return carry
    lax.fori_loop(0, n_pages, issue, 0)


def _wait_pages(pools, bufs, sems, slot, n_pages):
    def wait(p, carry):
        for pool_ref, buf_ref, sem_ref in zip(pools, bufs, sems):
            _page_copy(pool_ref, buf_ref, sem_ref, 0, slot, p).wait()
        return carry
    lax.fori_loop(0, n_pages, wait, 0)


def _paged_prefetch(pools, bufs, sems, pt_ref, base, n_pages):
    b = pl.program_id(0)
    slot = b % 2

    @pl.when(b == 0)
    def _():
        _start_pages(pools, bufs, sems, pt_ref, b, slot, base, n_pages)

    @pl.when(b + 1 < pl.num_programs(0))
    def _():
        _start_pages(pools, bufs, sems, pt_ref, b + 1, 1 - slot, base, n_pages)

    _wait_pages(pools, bufs, sems, slot, n_pages)
    return slot


def _mla_sample_body(pt_ref, lat_hbm, kr_hbm, qlat_ref, qrope_ref, cnew_ref, krnew_ref, o_ref,
                     cbuf, rbuf, csem, rsem, *, base, n_pages):
    slot = _paged_prefetch((lat_hbm, kr_hbm), (cbuf, rbuf), (csem, rsem), pt_ref, base, n_pages)
    npg = PAGES_PER_STEP
    ql = qlat_ref[0].astype(BF16)
    qr = qrope_ref[0].astype(BF16)
    chunks, scores = [], []
    for ch in range(n_pages // npg):
        c = cbuf[slot, pl.ds(ch * npg, npg)].reshape(npg * PAGE_SIZE, MLA_KV_RANK).astype(BF16)
        r = rbuf[slot, pl.ds(ch * npg, npg)].reshape(npg * PAGE_SIZE, MLA_ROPE).astype(BF16)
        scores.append((_nt_dot(ql, c) + _nt_dot(qr, r)) * MLA_SCALE)
    cn = _bf(cnew_ref[0])
    kn = _bf(krnew_ref[0])
    s_self = (jnp.sum(ql.astype(F32) * cn, axis=-1, keepdims=True)
              + jnp.sum(qr.astype(F32) * kn, axis=-1, keepdims=True)) * MLA_SCALE
    m = s_self
    for s in scores:
        m = jnp.maximum(m, jnp.max(s, axis=-1, keepdims=True))
    p_self = jnp.exp(s_self - m)
    l = p_self
    acc = _bf(p_self) * cn
    for ch, s in enumerate(scores):
        p = jnp.exp(s - m)
        l = l + jnp.sum(p, axis=-1, keepdims=True)
        c = cbuf[slot, pl.ds(ch * npg, npg)].reshape(npg * PAGE_SIZE, MLA_KV_RANK).astype(BF16)
        acc = acc + jnp.dot(p.astype(BF16), c, preferred_element_type=F32)
    o_ref[0] = acc / l


def mla_sample_attention(q_lat, q_rope, c_new, kr_new, lat_pool, kr_pool, o, page_table):
    db, n_pages = page_table.shape
    n_pool = lat_pool.shape[1]
    npg = PAGES_PER_STEP
    assert n_pages % npg == 0
    cp = lat_pool.reshape(-1, PAGE_SIZE, MLA_KV_RANK)
    rp = kr_pool.reshape(-1, PAGE_SIZE, MLA_ROPE)

    per_b = lambda shape: pl.BlockSpec((1,) + shape, lambda b, pt: (b,) + (0,) * len(shape))
    hbm = pl.BlockSpec(memory_space=pl.ANY)
    grid_spec = pltpu.PrefetchScalarGridSpec(
        num_scalar_prefetch=1,
        grid=(db,),
        in_specs=[hbm, hbm, per_b((MLA_HEADS, MLA_KV_RANK)), per_b((MLA_HEADS, MLA_ROPE)),
                  per_b((1, MLA_KV_RANK)), per_b((1, MLA_ROPE))],
        out_specs=per_b((MLA_HEADS, MLA_KV_RANK)),
        scratch_shapes=[pltpu.VMEM((2, n_pages, PAGE_SIZE, MLA_KV_RANK), F32),
                        pltpu.VMEM((2, n_pages, PAGE_SIZE, MLA_ROPE), F32),
                        pltpu.SemaphoreType.DMA((2,)), pltpu.SemaphoreType.DMA((2,))])
    return pl.pallas_call(
        functools.partial(_mla_sample_body, base=o * n_pool, n_pages=n_pages),
        grid_spec=grid_spec,
        out_shape=jax.ShapeDtypeStruct((db, MLA_HEADS, MLA_KV_RANK), F32),
        compiler_params=pltpu.CompilerParams(dimension_semantics=("arbitrary",), vmem_limit_bytes=VMEM_LIMIT_BYTES),
        name="mla_sample",
    )(page_table, cp, rp, q_lat, q_rope, c_new.reshape(db, 1, -1), kr_new.reshape(db, 1, -1))


def mla_sample_mixer(q_nope, q_rope, c, k_rope, lat_pool, kr_pool, o_idx, page_table, w_kvb):
    db = c.shape[0]
    w = w_kvb.reshape(MLA_KV_RANK, MLA_HEADS, MLA_NOPE + MLA_V).transpose(1, 0, 2)
    q_lat = batched_matmul(q_nope.transpose(1, 0, 2), w[..., :MLA_NOPE].transpose(0, 2, 1)).transpose(1, 0, 2)
    o_lat = mla_sample_attention(q_lat, q_rope, c, k_rope, lat_pool, kr_pool, o_idx, page_table)
    out = batched_matmul(o_lat.transpose(1, 0, 2), w[..., MLA_NOPE:])
    return out.transpose(1, 0, 2).reshape(db, MLA_HEADS * MLA_V)


OD_GROUPS = (GLA_HEADS * GLA_DK, GLA_HEADS * GLA_DK, GLA_HEADS * GLA_DV, GLA_HEADS * GLA_DV, MLA_Q_RANK, MLA_KV_RANK, LANE)
MLA_PREP_ROWS = 128
MLA_TQ = 256
MLA_HEAD_GROUP = 8


def odd_weight(w_in):
    gq, gk, gv, g1, r, qa, kva, kr = split_cols(w_in, OD_SPLITS)
    z = lambda n: jnp.zeros((w_in.shape[0], n), w_in.dtype)
    small = jnp.concatenate([g1, z(MLA_NOPE - GLA_GATE_RANK), kr, z(LANE - MLA_NOPE - MLA_ROPE)], axis=1)
    return jnp.concatenate([gq, gk, gv, r, qa, kva, small], axis=1)


def _head_blocks(w, widths, keep):
    k = w.shape[0]
    w = w.reshape(k, MLA_HEADS, sum(widths))[:, :, keep[0]:keep[1]]
    return jnp.pad(w, ((0, 0), (0, 0), (0, LANE - (keep[1] - keep[0])))).reshape(k, MLA_HEADS * LANE)


def rope_tables(pos):
    half = MLA_ROPE // 2
    freqs = ROPE_THETA ** (-jnp.arange(half, dtype=F32) / half)
    ang = pos.astype(F32)[:, None] * freqs
    cos, sin = jnp.cos(ang), jnp.sin(ang)
    n = pos.shape[0]
    z = lambda w: jnp.zeros((n, w), F32)
    tail = LANE - MLA_NOPE - MLA_ROPE
    c = jnp.concatenate([jnp.ones((n, MLA_NOPE), F32), cos, cos, z(tail)], axis=1)
    s1 = jnp.concatenate([z(MLA_NOPE), -sin, z(half), z(tail)], axis=1)
    s2 = jnp.concatenate([z(MLA_NOPE), z(half), sin, z(tail)], axis=1)
    return c, s1, s2


def _rope_block(x, c, s1, s2):
    half = MLA_ROPE // 2
    return x * c + pltpu.roll(x, LANE - half, axis=1) * s1 + pltpu.roll(x, half, axis=1) * s2


def _mla_prep_body(qa_ref, kva_ref, sm_ref, qg_ref, kg_ref, wq_ref, wk_ref, wv_ref, c_ref, s1_ref, s2_ref,
                   q_out, c_out, k_out, v_out, kr_out):
    norm = lambda x, g: x * lax.rsqrt(jnp.mean(x * x, axis=-1, keepdims=True) + EPS) * g
    c, s1, s2 = c_ref[...], s1_ref[...], s2_ref[...]
    lane = lax.broadcasted_iota(jnp.int32, c.shape, 1)
    q = jnp.dot(norm(qa_ref[...], qg_ref[...]).astype(BF16), wq_ref[...], preferred_element_type=F32)
    lat = norm(kva_ref[...], kg_ref[...])
    c_out[...] = lat
    latb = lat.astype(BF16)
    kn = jnp.dot(latb, wk_ref[...], preferred_element_type=F32)
    v_out[...] = jnp.dot(latb, wv_ref[...], preferred_element_type=F32).astype(BF16)
    kr = jnp.where(lane >= MLA_NOPE, _rope_block(sm_ref[...], c, s1, s2), 0.0)
    kr_out[...] = kr
    for h in range(MLA_HEADS):
        cols = slice(h * LANE, (h + 1) * LANE)
        q_out[:, cols] = _rope_block(q[:, cols], c, s1, s2)
        k_out[:, cols] = (kn[:, cols] + kr).astype(BF16)


def mla_prep(qa, kva, small, q_norm, w_qb, kv_norm, w_kvb, tables, pos_block):
    m = qa.shape[0]
    tm = MLA_PREP_ROWS
    wq = _head_blocks(w_qb, (MLA_NOPE, MLA_ROPE), (0, MLA_NOPE + MLA_ROPE)).astype(BF16)
    wk = _head_blocks(w_kvb, (MLA_NOPE, MLA_V), (0, MLA_NOPE)).astype(BF16)
    wv = _head_blocks(w_kvb, (MLA_NOPE, MLA_V), (MLA_NOPE, MLA_NOPE + MLA_V)).astype(BF16)
    wide = MLA_HEADS * LANE
    rows = lambda w: pl.BlockSpec((tm, w), lambda i: (i, 0))
    const = lambda a: pl.BlockSpec(a.shape, lambda i: (0, 0))
    tab = pl.BlockSpec((tm, LANE), lambda i: (pos_block(i), 0))
    qg = q_norm.reshape(1, -1).astype(F32)
    kg = kv_norm.reshape(1, -1).astype(F32)
    return pl.pallas_call(
        _mla_prep_body,
        grid=(m // tm,),
        in_specs=[rows(MLA_Q_RANK), rows(MLA_KV_RANK), rows(LANE), const(qg), const(kg), const(wq), const(wk), const(wv),
                  tab, tab, tab],
        out_specs=[rows(wide), rows(MLA_KV_RANK), rows(wide), rows(wide), rows(LANE)],
        out_shape=[jax.ShapeDtypeStruct((m, wide), F32), jax.ShapeDtypeStruct((m, MLA_KV_RANK), F32),
                   jax.ShapeDtypeStruct((m, wide), BF16), jax.ShapeDtypeStruct((m, wide), BF16),
                   jax.ShapeDtypeStruct((m, LANE), F32)],
        compiler_params=pltpu.CompilerParams(dimension_semantics=("arbitrary",), vmem_limit_bytes=VMEM_LIMIT_BYTES),
        name="mla_prep",
    )(qa, kva, small, qg, kg, wq, wk, wv, *tables)


def _mla_prompt_body(q_ref, k_ref, v_ref, o_ref):
    tq = q_ref.shape[0]
    qi = pl.program_id(1)
    row = lax.broadcasted_iota(jnp.int32, (tq, tq), 0)
    col = lax.broadcasted_iota(jnp.int32, (tq, tq), 1)
    low = lax.broadcasted_iota(jnp.int32, (tq, LANE), 1) < MLA_V
    outs = []
    for h0 in range(0, MLA_HEADS, MLA_HEAD_GROUP):
        heads = range(h0, h0 + MLA_HEAD_GROUP)
        hcols = [slice(h * LANE, (h + 1) * LANE) for h in heads]
        qs = [(q_ref[:, c] * MLA_SCALE).astype(BF16) for c in hcols]

        def body(kt, carry):
            k0 = pl.multiple_of(kt * tq, tq)
            vis = (kt - qi) * tq + col <= row
            ss = [jnp.where(vis, _nt_dot(q, k_ref[pl.ds(k0, tq), c]), NEG) for q, c in zip(qs, hcols)]
            m_new = [jnp.maximum(st[0], jnp.max(s, axis=-1, keepdims=True)) for st, s in zip(carry, ss)]
            alpha = [jnp.exp(st[0] - mn) for st, mn in zip(carry, m_new)]
            ps = [jnp.exp(s - mn) for s, mn in zip(ss, m_new)]
            l_new = [a * st[1] + jnp.sum(p, axis=-1, keepdims=True) for a, st, p in zip(alpha, carry, ps)]
            pv = [jnp.dot(p.astype(BF16), v_ref[pl.ds(k0, tq), c], preferred_element_type=F32)
                  for p, c in zip(ps, hcols)]
            return tuple((mn, ln, a * st[2] + x) for mn, ln, a, st, x in zip(m_new, l_new, alpha, carry, pv))

        init = tuple((jnp.full((tq, 1), NEG, F32), jnp.zeros((tq, 1), F32), jnp.zeros((tq, LANE), F32))
                     for _ in heads)
        final = lax.fori_loop(0, qi + 1, body, init)
        outs.extend(acc / l for _, l, acc in final)
    for pr in range(MLA_HEADS // 2):
        o_ref[:, pr * LANE:(pr + 1) * LANE] = jnp.where(low, outs[2 * pr], pltpu.roll(outs[2 * pr + 1], MLA_V, axis=1))


def mla_prompt_attention(q_rot, k_full, v_pad, b, l):
    tq = MLA_TQ
    nq = l // tq
    wide = MLA_HEADS * LANE
    return pl.pallas_call(
        _mla_prompt_body,
        grid=(b, nq),
        in_specs=[pl.BlockSpec((tq, wide), lambda bi, i: (bi * nq + i, 0)),
                  pl.BlockSpec((l, wide), lambda bi, i: (bi, 0)),
                  pl.BlockSpec((l, wide), lambda bi, i: (bi, 0))],
        out_specs=pl.BlockSpec((tq, MLA_HEADS * MLA_V), lambda bi, i: (bi * nq + i, 0)),
        out_shape=jax.ShapeDtypeStruct((b * l, MLA_HEADS * MLA_V), F32),
        compiler_params=pltpu.CompilerParams(dimension_semantics=("arbitrary", "arbitrary"),
                                             vmem_limit_bytes=VMEM_LIMIT_BYTES),
        name="mla_prompt",
    )(q_rot, k_full, v_pad)


SSD_PAIRS = SSD_HEADS // 2
PAIR_PER_GROUP = SSD_PAIRS // SSD_GROUPS
CONV_PAD = 8


def _exact_dot(a, b):
    return jnp.dot(a, b, preferred_element_type=F32, precision=lax.Precision.HIGHEST)


def _softplus(x):
    u = jnp.exp(-jnp.abs(x))
    w = 1.0 + u
    return jnp.maximum(x, 0.0) + jnp.where(w == 1.0, u, jnp.log(w) * (u / (w - 1.0)))


def _silu(x):
    return x * jax.nn.sigmoid(x)


def _group_rmsnorm(y, g):
    gw = SSD_INNER // SSD_GROUPS
    outs = []
    for gi in range(SSD_GROUPS):
        yg = y[:, gi * gw:(gi + 1) * gw]
        outs.append(yg * lax.rsqrt(jnp.mean(yg * yg, axis=-1, keepdims=True) + EPS))
    return jnp.concatenate(outs, axis=1) * g


def _ssd_prompt_body(z_ref, xbc_ref, sm_ref, cw_ref, cb_ref, dtb_ref, alog_ref, d_ref, g_ref, ex_ref,
                     y_ref, st_ref, cv_ref, buf, state):
    c = pl.program_id(1)
    nc = pl.num_programs(1)
    t = SSD_CHUNK

    @pl.when(c == 0)
    def _():
        buf[0:CONV_PAD, :] = jnp.zeros((CONV_PAD, SSD_CONV_CH), F32)
        state[...] = jnp.zeros_like(state)

    buf[CONV_PAD:CONV_PAD + t, :] = xbc_ref[...]
    conv = cb_ref[...] + cw_ref[SSD_CONV - 1:SSD_CONV, :] * xbc_ref[...]
    for k in range(SSD_CONV - 1):
        conv = conv + cw_ref[k:k + 1, :] * buf[pl.ds(CONV_PAD - (SSD_CONV - 1) + k, t), :]
    buf[0:CONV_PAD, :] = buf[t:t + CONV_PAD, :]
    xc = _silu(conv)
    xs = xc[:, :SSD_INNER]
    bm = xc[:, SSD_INNER:SSD_INNER + SSD_GROUPS * SSD_STATE].astype(BF16)
    cm = xc[:, SSD_INNER + SSD_GROUPS * SSD_STATE:].astype(BF16)

    dt = _softplus(sm_ref[...] + dtb_ref[...])
    da = dt * (-jnp.exp(alog_ref[...]))
    row = lax.broadcasted_iota(jnp.int32, (t, t), 0)
    col = lax.broadcasted_iota(jnp.int32, (t, t), 1)
    tril = row >= col
    a_cs = _exact_dot(tril.astype(F32), da)
    a_cs_t = a_cs.T
    ex = ex_ref[...]
    dt_x = _exact_dot(dt, ex)
    acs_x = _exact_dot(a_cs, ex)
    last_x = acs_x[t - 1:t, :]
    xd = xs * dt_x
    xdw = xd * jnp.exp(last_x - acs_x)
    xdb = xd.astype(BF16)
    e_in = jnp.exp(acs_x)
    low = lax.broadcasted_iota(jnp.int32, (t, LANE), 1) < SSD_HEADDIM
    zero = jnp.zeros((t, LANE), BF16)

    y_parts = []
    for j in range(SSD_PAIRS):
        g = j // PAIR_PER_GROUP
        cg = cm[:, g * SSD_STATE:(g + 1) * SSD_STATE]
        bg = bm[:, g * SSD_STATE:(g + 1) * SSD_STATE]
        cb = _nt_dot(cg, bg)
        cols = slice(j * LANE, (j + 1) * LANE)
        xp = xdb[:, cols]
        y = jnp.zeros((t, LANE), F32)
        for hh in range(2):
            h = 2 * j + hh
            decay = jnp.where(tril, jnp.exp(a_cs[:, h:h + 1] - a_cs_t[h:h + 1, :]), 0.0)
            xh = jnp.where(low if hh == 0 else jnp.logical_not(low), xp, zero)
            y = y + jnp.dot((cb * decay).astype(BF16), xh, preferred_element_type=F32)
        s_old = state[j]
        y = y + _nt_dot(cg, s_old.astype(BF16)) * e_in[:, cols]
        dec_rows = jnp.broadcast_to(jnp.exp(last_x[:, cols]), (LANE, LANE)).T
        state[j] = dec_rows * s_old + jnp.dot(xdw[:, cols].T.astype(BF16), bg, preferred_element_type=F32)
        y_parts.append(y)
    y = jnp.concatenate(y_parts, axis=1) + xs * d_ref[...]
    y_ref[...] = _group_rmsnorm(y * _silu(z_ref[...]), g_ref[...])

    @pl.when(c == nc - 1)
    def _():
        st_ref[0] = state[...]
        cv_ref[0] = xbc_ref[t - (SSD_CONV - 1):t, :]


def _head_expand():
    return jnp.asarray(np.arange(LANE)[:, None] == (np.arange(SSD_INNER)[None, :] // SSD_HEADDIM), F32)


def _ssd_params(conv_w, conv_b, dt_bias, a_log, d_skip, norm_g):
    pad = lambda v: jnp.pad(v.astype(F32), (0, LANE - SSD_HEADS)).reshape(1, LANE)
    return (conv_w.astype(F32), conv_b.astype(F32).reshape(1, -1), pad(dt_bias), pad(a_log),
            jnp.repeat(d_skip.astype(F32), SSD_HEADDIM).reshape(1, -1), norm_g.astype(F32).reshape(1, -1), _head_expand())


def ssd_prompt(z, xbc, small, b, l, conv_w, conv_b, dt_bias, a_log, d_skip, norm_g):
    t = SSD_CHUNK
    nc = l // t
    params = _ssd_params(conv_w, conv_b, dt_bias, a_log, d_skip, norm_g)
    rows = lambda w: pl.BlockSpec((t, w), lambda bi, c: (bi * nc + c, 0))
    const = lambda a: pl.BlockSpec(a.shape, lambda bi, c: (0,) * a.ndim)
    y, st, cv = pl.pallas_call(
        _ssd_prompt_body,
        grid=(b, nc),
        in_specs=[rows(SSD_INNER), rows(SSD_CONV_CH), rows(LANE)] + [const(p) for p in params],
        out_specs=[rows(SSD_INNER),
                   pl.BlockSpec((1, SSD_PAIRS, LANE, SSD_STATE), lambda bi, c: (bi, 0, 0, 0)),
                   pl.BlockSpec((1, SSD_CONV - 1, SSD_CONV_CH), lambda bi, c: (bi, 0, 0))],
        out_shape=[jax.ShapeDtypeStruct((b * l, SSD_INNER), F32),
                   jax.ShapeDtypeStruct((b, SSD_PAIRS, LANE, SSD_STATE), F32),
                   jax.ShapeDtypeStruct((b, SSD_CONV - 1, SSD_CONV_CH), F32)],
        scratch_shapes=[pltpu.VMEM((CONV_PAD + t, SSD_CONV_CH), F32), pltpu.VMEM((SSD_PAIRS, LANE, SSD_STATE), F32)],
        compiler_params=pltpu.CompilerParams(dimension_semantics=("arbitrary", "arbitrary"),
                                             vmem_limit_bytes=VMEM_LIMIT_BYTES),
        name="ssd_prompt",
    )(z, xbc, small, *params)
    return y, st.reshape(b, SSD_HEADS, SSD_HEADDIM, SSD_STATE), cv


def _ssd_sample_body(z_ref, xbc_ref, sm_ref, cs_ref, st_ref, cw_ref, cb_ref, dtb_ref, alog_ref, d_ref, g_ref, ex_ref,
                     y_ref, sto_ref, cvo_ref):
    x_new = xbc_ref[0]
    cs = cs_ref[0]
    conv = cb_ref[...] + cw_ref[SSD_CONV - 1:SSD_CONV, :] * x_new
    for k in range(SSD_CONV - 1):
        conv = conv + cw_ref[k:k + 1, :] * cs[k:k + 1, :]
    cvo_ref[0] = jnp.concatenate([cs[1:], x_new], axis=0)
    xc = _silu(conv)
    xs = xc[:, :SSD_INNER]
    bm = xc[:, SSD_INNER:SSD_INNER + SSD_GROUPS * SSD_STATE]
    cm = xc[:, SSD_INNER + SSD_GROUPS * SSD_STATE:]
    dt = jnp.broadcast_to(_softplus(sm_ref[0] + dtb_ref[...]), (8, LANE))
    da = dt * (-jnp.exp(alog_ref[...]))
    ex = ex_ref[...]
    xd = xs * _exact_dot(dt, ex)[0:1]
    dec_x = jnp.exp(_exact_dot(da, ex)[0:1])
    y_parts = []
    for j in range(SSD_PAIRS):
        g = j // PAIR_PER_GROUP
        cols = slice(j * LANE, (j + 1) * LANE)
        bg = bm[:, g * SSD_STATE:(g + 1) * SSD_STATE]
        cg = jnp.broadcast_to(cm[:, g * SSD_STATE:(g + 1) * SSD_STATE], (8, SSD_STATE)).astype(BF16)
        dec_rows = jnp.broadcast_to(dec_x[:, cols], (LANE, LANE)).T
        xd_rows = jnp.broadcast_to(xd[:, cols], (LANE, LANE)).T
        s_new = dec_rows * st_ref[0, j] + xd_rows * bg
        sto_ref[0, j] = s_new
        y_parts.append(_nt_dot(cg, s_new.astype(BF16))[0:1])
    y = jnp.concatenate(y_parts, axis=1) + xs * d_ref[...]
    y_ref[0] = _group_rmsnorm(y * _silu(z_ref[0]), g_ref[...])


def ssd_sample(z, xbc, small, conv_state, ssm_state, conv_w, conv_b, dt_bias, a_log, d_skip, norm_g):
    db = z.shape[0]
    params = _ssd_params(conv_w, conv_b, dt_bias, a_log, d_skip, norm_g)
    per_b = lambda shape: pl.BlockSpec((1,) + shape, lambda i: (i,) + (0,) * len(shape))
    const = lambda a: pl.BlockSpec(a.shape, lambda i: (0,) * a.ndim)
    st_shape = (SSD_PAIRS, LANE, SSD_STATE)
    y, st, cv = pl.pallas_call(
        _ssd_sample_body,
        grid=(db,),
        in_specs=[per_b((1, SSD_INNER)), per_b((1, SSD_CONV_CH)), per_b((1, LANE)), per_b((SSD_CONV - 1, SSD_CONV_CH)),
                  per_b(st_shape)] + [const(p) for p in params],
        out_specs=[per_b((1, SSD_INNER)), per_b(st_shape), per_b((SSD_CONV - 1, SSD_CONV_CH))],
        out_shape=[jax.ShapeDtypeStruct((db, 1, SSD_INNER), F32), jax.ShapeDtypeStruct((db,) + st_shape, F32),
                   jax.ShapeDtypeStruct((db, SSD_CONV - 1, SSD_CONV_CH), F32)],
        compiler_params=pltpu.CompilerParams(dimension_semantics=("arbitrary",)),
        name="ssd_sample",
    )(z.reshape(db, 1, -1), xbc.reshape(db, 1, -1), small.reshape(db, 1, -1), conv_state,
      ssm_state.reshape((db,) + st_shape), *params)
    return y.reshape(db, SSD_INNER), st.reshape(db, SSD_HEADS, SSD_HEADDIM, SSD_STATE), cv


GLA_BLOCK = 128
GLA_INNER = GLA_HEADS * GLA_DV


def _log_gate(small, w2_ref, bg_ref):
    logits = jnp.dot(small.astype(BF16), w2_ref[...], preferred_element_type=F32) + bg_ref[...]
    return -_softplus(-logits) / GLA_TAU


def _head_rmsnorm_gate(o_heads, g_ref, r):
    outs = [o * lax.rsqrt(jnp.mean(o * o, axis=-1, keepdims=True) + EPS) for o in o_heads]
    return jnp.concatenate(outs, axis=1) * g_ref[...] * _silu(r)


def _gla_prompt_body(q_ref, k_ref, v_ref, r_ref, sm_ref, w2_ref, bg_ref, g_ref, y_ref, st_ref, state):
    blk = pl.program_id(1)
    t = GLA_BLOCK

    @pl.when(blk == 0)
    def _():
        state[...] = jnp.zeros_like(state)

    la = _log_gate(sm_ref[...], w2_ref, bg_ref)
    row = lax.broadcasted_iota(jnp.int32, (t, t), 0)
    col = lax.broadcasted_iota(jnp.int32, (t, t), 1)
    same = (row // GLA_CHUNK) == (col // GLA_CHUNK)
    causal = same & (row >= col)
    bc = _exact_dot(causal.astype(F32), la)
    is_last = col == (row // GLA_CHUNK) * GLA_CHUNK + GLA_CHUNK - 1
    bl = _exact_dot(is_last.astype(F32), bc)
    qt = q_ref[...] * (GLA_DK ** -0.5) * jnp.exp(bc)
    kt = k_ref[...] * jnp.exp(-bc)
    kd = k_ref[...] * jnp.exp(bl - bc)
    heads = range(GLA_HEADS)
    kcs = [slice(h * GLA_DK, (h + 1) * GLA_DK) for h in heads]
    vbs = [v_ref[:, h * GLA_DV:(h + 1) * GLA_DV].astype(BF16) for h in heads]
    qbs = [qt[:, kc].astype(BF16) for kc in kcs]
    o_intra = []
    for h in heads:
        att = jnp.where(causal, _nt_dot(qbs[h], kt[:, kcs[h]].astype(BF16)), 0.0)
        o_intra.append(jnp.dot(att.astype(BF16), vbs[h], preferred_element_type=F32))
    kd_t = [kd[:, kc].T for kc in kcs]
    dec_t = [jnp.exp(bl[:, kc]).T for kc in kcs]
    ss = [state[h] for h in heads]
    inter = [[] for _ in heads]
    for c in range(t // GLA_CHUNK):
        r0 = c * GLA_CHUNK
        in_chunk = (col // GLA_CHUNK) == c
        for h in heads:
            inter[h].append(jnp.dot(qbs[h][r0:r0 + GLA_CHUNK], ss[h].astype(BF16), preferred_element_type=F32))
        upd = [jnp.dot(jnp.where(in_chunk, kd_t[h], 0.0).astype(BF16), vbs[h], preferred_element_type=F32)
               for h in heads]
        ss = [dec_t[h][:, r0:r0 + 1] * ss[h] + upd[h] for h in heads]
    for h in heads:
        state[h] = ss[h]
    o_heads = [o_intra[h] + jnp.concatenate(inter[h], axis=0) for h in heads]
    y_ref[...] = _head_rmsnorm_gate(o_heads, g_ref, r_ref[...])

    @pl.when(blk == pl.num_programs(1) - 1)
    def _():
        st_ref[0] = state[...]


def _gla_params(w_g2, b_g, norm_g):
    w2 = jnp.pad(w_g2, ((0, LANE - GLA_GATE_RANK), (0, 0))).astype(BF16)
    return w2, b_g.astype(F32).reshape(1, -1), jnp.tile(norm_g.astype(F32), GLA_HEADS).reshape(1, -1)


def gla_prompt(gq, gk, gv, r, small, b, l, w_g2, b_g, norm_g):
    t = GLA_BLOCK
    nb = l // t
    params = _gla_params(w_g2, b_g, norm_g)
    rows = lambda w: pl.BlockSpec((t, w), lambda bi, i: (bi * nb + i, 0))
    const = lambda a: pl.BlockSpec(a.shape, lambda bi, i: (0,) * a.ndim)
    st_shape = (GLA_HEADS, GLA_DK, GLA_DV)
    return pl.pallas_call(
        _gla_prompt_body,
        grid=(b, nb),
        in_specs=[rows(GLA_HEADS * GLA_DK), rows(GLA_HEADS * GLA_DK), rows(GLA_INNER), rows(GLA_INNER), rows(LANE)]
        + [const(p) for p in params],
        out_specs=[rows(GLA_INNER), pl.BlockSpec((1,) + st_shape, lambda bi, i: (bi, 0, 0, 0))],
        out_shape=[jax.ShapeDtypeStruct((b * l, GLA_INNER), F32), jax.ShapeDtypeStruct((b,) + st_shape, F32)],
        scratch_shapes=[pltpu.VMEM(st_shape, F32)],
        compiler_params=pltpu.CompilerParams(dimension_semantics=("arbitrary", "arbitrary"),
                                             vmem_limit_bytes=VMEM_LIMIT_BYTES),
        name="gla_prompt",
    )(gq, gk, gv, r, small, *params)


def _gla_sample_body(q_ref, k_ref, v_ref, r_ref, sm_ref, st_ref, w2_ref, bg_ref, g_ref, y_ref, sto_ref):
    la = _log_gate(jnp.broadcast_to(sm_ref[0], (8, LANE)), w2_ref, bg_ref)[0:1]
    a = jnp.exp(la)
    qt = q_ref[0] * (GLA_DK ** -0.5) * a
    kt = k_ref[0] * jnp.exp(-la)
    o_heads = []
    for h in range(GLA_HEADS):
        kc = slice(h * GLA_DK, (h + 1) * GLA_DK)
        vb = _bf(v_ref[0, :, h * GLA_DV:(h + 1) * GLA_DV])
        qb = _bf(qt[:, kc])
        att = jnp.sum(qb * _bf(kt[:, kc]), axis=-1, keepdims=True)
        s_old = st_ref[0, h]
        o_inter = jnp.dot(jnp.broadcast_to(qb, (8, GLA_DK)).astype(BF16), s_old.astype(BF16),
                          preferred_element_type=F32)[0:1]
        a_rows = jnp.broadcast_to(a[:, kc], (GLA_DK, GLA_DK)).T[:, 0:1]
        k_rows = jnp.broadcast_to(k_ref[0, :, kc], (GLA_DK, GLA_DK)).T[:, 0:1]
        sto_ref[0, h] = a_rows * s_old + k_rows * v_ref[0, :, h * GLA_DV:(h + 1) * GLA_DV]
        o_heads.append(_bf(att) * vb + o_inter)
    y_ref[0] = _head_rmsnorm_gate(o_heads, g_ref, r_ref[0])


def gla_sample(gq, gk, gv, r, small, state, w_g2, b_g, norm_g):
    db = gq.shape[0]
    params = _gla_params(w_g2, b_g, norm_g)
    per_b = lambda shape: pl.BlockSpec((1,) + shape, lambda i: (i,) + (0,) * len(shape))
    const = lambda a: pl.BlockSpec(a.shape, lambda i: (0,) * a.ndim)
    st_shape = (GLA_HEADS, GLA_DK, GLA_DV)
    row3 = lambda x: x.reshape(db, 1, -1)
    y, st = pl.pallas_call(
        _gla_sample_body,
        grid=(db,),
        in_specs=[per_b((1, GLA_HEADS * GLA_DK)), per_b((1, GLA_HEADS * GLA_DK)), per_b((1, GLA_INNER)),
                  per_b((1, GLA_INNER)), per_b((1, LANE)), per_b(st_shape)] + [const(p) for p in params],
        out_specs=[per_b((1, GLA_INNER)), per_b(st_shape)],
        out_shape=[jax.ShapeDtypeStruct((db, 1, GLA_INNER), F32), jax.ShapeDtypeStruct((db,) + st_shape, F32)],
        compiler_params=pltpu.CompilerParams(dimension_semantics=("arbitrary",)),
        name="gla_sample",
    )(row3(gq), row3(gk), row3(gv), row3(r), row3(small), state, *params)
    return y.reshape(db, GLA_INNER), st


def _ffn_body(x_ref, g_ref, wg_ref, wu_ref, wd_ref, fg_ref, o_ref, *, chunks, final):
    x = x_ref[...]
    h = (x * lax.rsqrt(jnp.mean(x * x, axis=-1, keepdims=True) + EPS) * g_ref[...]).astype(BF16)
    acc = x
    for c0, cw in chunks:
        a = jnp.dot(h, wg_ref[:, c0:c0 + cw], preferred_element_type=F32)
        u = jnp.dot(h, wu_ref[:, c0:c0 + cw], preferred_element_type=F32)
        act = (a * jax.nn.sigmoid(a) * u).astype(BF16)
        acc = acc + jnp.dot(act, wd_ref[c0:c0 + cw, :], preferred_element_type=F32)
    if final:
        acc = acc * lax.rsqrt(jnp.mean(acc * acc, axis=-1, keepdims=True) + EPS) * fg_ref[...]
    o_ref[...] = acc


def swiglu_ffn(x, g, w_gate, w_up, w_down, final_g=None):
    m, d = x.shape
    hdim = w_gate.shape[1]
    tm = _row_tile(m)
    final = final_g is not None
    fg = (final_g if final else g).reshape(1, d).astype(F32)
    wspec = lambda shape: pl.BlockSpec(shape, lambda i: (0, 0), pipeline_mode=pl.Buffered(1))
    return pl.pallas_call(
        functools.partial(_ffn_body, chunks=_col_chunks(hdim, 256), final=final),
        grid=(m // tm,),
        in_specs=[pl.BlockSpec((tm, d), lambda i: (i, 0)),
                  pl.BlockSpec((1, d), lambda i: (0, 0)),
                  wspec((d, hdim)), wspec((d, hdim)), wspec((hdim, d)),
                  pl.BlockSpec((1, d), lambda i: (0, 0))],
        out_specs=pl.BlockSpec((tm, d), lambda i: (i, 0)),
        out_shape=jax.ShapeDtypeStruct((m, d), F32),
        compiler_params=pltpu.CompilerParams(dimension_semantics=("arbitrary",),
                                             vmem_limit_bytes=VMEM_LIMIT_BYTES),
        name="swiglu_ffn",
    )(x, g.reshape(1, d).astype(F32), w_gate.astype(BF16), w_up.astype(BF16), w_down.astype(BF16), fg)


def split_cols(h, sizes):
    return jnp.split(h, [int(i) for i in np.cumsum(sizes)[:-1]], axis=-1)


def kernel(x_prompt, x_sample, state_ssm, state_conv, cache_nsa_k_cmp, cache_nsa_v_cmp, cache_nsa_k_slc, cache_nsa_v_slc, cache_nsa_k_win, cache_nsa_v_win, state_gla, cache_mla_latent, cache_mla_krope, page_table, rel_bias, ev_norm, ev_w_in, ssd_conv_w, ssd_conv_b, ssd_dt_bias, ssd_a_log, ssd_d, ssd_norm, nsa_w_cmp_k, nsa_w_cmp_v, ev_w_out, od_norm, od_w_in, gla_w_gate2, gla_b_gate, gla_norm, mla_q_norm, mla_w_qb, mla_kv_norm, mla_w_kvb, od_w_out, ffn_norm, ffn_w_gate, ffn_w_up, ffn_w_down, final_norm):
    b, l = x_prompt.shape[:2]
    db, s = x_sample.shape[:2]
    depth = ffn_norm.shape[0]
    p0 = page_table.shape[1] * PAGE_SIZE
    pos_p = jnp.arange(l)
    pos_s = p0 + jnp.arange(s)
    names = ('ssm', 'conv', 'k_cmp', 'v_cmp', 'k_slc', 'v_slc', 'k_win', 'v_win', 'gla', 'lat', 'krope')
    newp = {n: [] for n in names}
    news = {n: [] for n in names}
    np_rows = b * l
    x = jnp.concatenate([x_prompt.reshape(np_rows, D_MODEL), x_sample.reshape(db * s, D_MODEL)], axis=0)
    for li in range(depth):
        if li % 2 == 0:
            e = li // 2
            ssd_w = (ssd_conv_w[e], ssd_conv_b[e], ssd_dt_bias[e], ssd_a_log[e], ssd_d[e], ssd_norm[e])
            parts = fused_matmul(x, even_weight(ev_w_in[e]), norm_g=ev_norm[e], groups=EV_GROUPS)
            z_all, xbc_all, _, kv_all, small_all = parts
            kv_heads = lambda r0, nb, nl: tuple(
                t.reshape(nb, nl, NSA_KV_HEADS, NSA_HEAD_DIM)
                for t in split_cols(kv_all[r0:r0 + nb * nl], (NSA_KV_WIDTH,) * 6))
            y_ssd, ssm_n, conv_n = ssd_prompt(z_all, xbc_all, small_all, b, l, *ssd_w)
            y_nsa = nsa_prompt_attention(parts[2], parts[3], parts[4], nsa_w_cmp_k[e], nsa_w_cmp_v[e], rel_bias, b, l)
            mix_p = jnp.concatenate([y_ssd, y_nsa], axis=-1)
            nw = min(NSA_WINDOW, l)
            kvs = kv_heads(0, b, l)
            for n, t in zip(names[:8], (ssm_n, conv_n, kvs[0], kvs[1], kvs[2], kvs[3],
                                        kvs[4][:, -nw:], kvs[5][:, -nw:])):
                newp[n].append(t)
            kvs = kv_heads(np_rows, db, s)
            y_ssd, ssm_n, conv_n = ssd_sample(z_all[np_rows:], xbc_all[np_rows:], small_all[np_rows:],
                                              state_conv[e], state_ssm[e], *ssd_w)
            y_nsa, kw_n, vw_n = nsa_sample_mixer(parts[2][np_rows:], parts[3][np_rows:], parts[4][np_rows:], e,
                                                 cache_nsa_k_cmp, cache_nsa_v_cmp, cache_nsa_k_slc, cache_nsa_v_slc,
                                                 cache_nsa_k_win[e], cache_nsa_v_win[e], page_table,
                                                 nsa_w_cmp_k[e], nsa_w_cmp_v[e], rel_bias)
            mix_s = jnp.concatenate([y_ssd, y_nsa], axis=-1)
            for n, t in zip(names[:8], (ssm_n, conv_n, kvs[0], kvs[1], kvs[2], kvs[3], kw_n, vw_n)):
                news[n].append(t)
            x = fused_matmul(jnp.concatenate([mix_p, mix_s], axis=0), ev_w_out[e], residual=x)
        else:
            o = li // 2
            gla_w = (gla_w_gate2[o], gla_b_gate[o], gla_norm[o])
            gq, gk, gv, r, qa, kva, small = fused_matmul(x, odd_weight(od_w_in[o]), norm_g=od_norm[o],
                                                         groups=OD_GROUPS)
            assert db * s == MLA_PREP_ROWS and l % MLA_PREP_ROWS == 0
            nblk = l // MLA_PREP_ROWS
            tables = tuple(jnp.concatenate(t, axis=0)
                           for t in zip(rope_tables(pos_p), rope_tables(jnp.tile(pos_s, db))))
            q_rot, lat, k_full, v_pad, kr_rot = mla_prep(
                qa, kva, small, mla_q_norm[o], mla_w_qb[o], mla_kv_norm[o], mla_w_kvb[o], tables,
                lambda i: jnp.where(i < b * nblk, i % nblk, nblk))
            kr_rot = kr_rot[:, MLA_NOPE:MLA_NOPE + MLA_ROPE]
            assert s == 1
            y_gla, gla_n = gla_prompt(gq, gk, gv, r, small, b, l, *gla_w)
            y_mla = mla_prompt_attention(q_rot, k_full, v_pad, b, l)
            mix_p = jnp.concatenate([y_gla, y_mla], axis=-1)
            newp['gla'].append(gla_n)
            newp['lat'].append(lat[:np_rows].reshape(b, l, -1))
            newp['krope'].append(kr_rot[:np_rows].reshape(b, l, -1))
            y_gla, gla_n = gla_sample(gq[np_rows:], gk[np_rows:], gv[np_rows:], r[np_rows:], small[np_rows:],
                                      state_gla[o], *gla_w)
            q_s = q_rot[np_rows:].reshape(db * s, MLA_HEADS, LANE)
            y_mla = mla_sample_mixer(q_s[..., :MLA_NOPE], q_s[..., MLA_NOPE:MLA_NOPE + MLA_ROPE], lat[np_rows:],
                                     kr_rot[np_rows:], cache_mla_latent, cache_mla_krope, o, page_table, mla_w_kvb[o])
            mix_s = jnp.concatenate([y_gla, y_mla], axis=-1)
            news['gla'].append(gla_n)
            news['lat'].append(lat[np_rows:].reshape(db, s, -1))
            news['krope'].append(kr_rot[np_rows:].reshape(db, s, -1))
            x = fused_matmul(jnp.concatenate([mix_p, mix_s], axis=0), od_w_out[o], residual=x)
        x = swiglu_ffn(x, ffn_norm[li], ffn_w_gate[li], ffn_w_up[li], ffn_w_down[li],
                       final_g=final_norm if li == depth - 1 else None)
    y_prompt = x[:np_rows].reshape(b, l, D_MODEL)
    y_sample = x[np_rows:].reshape(db, s, D_MODEL)
    st = lambda d, n: jnp.stack(d[n])
    return (y_prompt, y_sample,
            st(newp, 'ssm'), st(news, 'ssm'), st(newp, 'conv'), st(news, 'conv'),
            st(newp, 'k_cmp'), st(news, 'k_cmp'), st(newp, 'v_cmp'), st(news, 'v_cmp'),
            st(newp, 'k_slc'), st(news, 'k_slc'), st(newp, 'v_slc'), st(news, 'v_slc'),
            st(newp, 'k_win'), st(news, 'k_win'), st(newp, 'v_win'), st(news, 'v_win'),
            st(newp, 'gla'), st(news, 'gla'), st(newp, 'lat'), st(news, 'lat'),
            st(newp, 'krope'), st(news, 'krope'))
```
